```python
import math
import jax, jax.numpy as jnp
from jax import lax
import numpy as np

D_MODEL = 1024
BATCH = 4
SEQ = 8192
DEPTH = 2
DEC_BATCH = 32
DEC_SEQ = 8
PAST_LEN = 16384
PAGE_SIZE = 128

HEAD_DIM = 64
H_A = D_MODEL // (2 * HEAD_DIM)
H_B = D_MODEL // (2 * HEAD_DIM)
H_C = D_MODEL // HEAD_DIM
ROT_DIM = HEAD_DIM // 4
ROPE_THETA = 500000.0
RET_THETA = 10000.0
MOBA_BLOCK = 256
MOBA_TOPK = 3
MOBA_Q_CHUNK = 64
RET_CHUNK = 128
DILATED = ((128, 1), (512, 4), (2048, 16))
W_MAX = max(w for w, _ in DILATED)
DIL_Q_CHUNK = 64
D_FF = ((8 * D_MODEL + 3 * 256 - 1) // (3 * 256)) * 256
N_AB = (DEPTH + 1) // 2
N_C = DEPTH // 2
EPS = 1e-6
NEG_INF = -1e30

kernel_name = 'moba_retnet_longnet_hybrid_step'


def rms_norm(x, w):
    x32 = x.astype(jnp.float32)
    y = x32 * lax.rsqrt(jnp.mean(x32 * x32, axis=-1, keepdims=True) + EPS)
    return (y * w.astype(jnp.float32)).astype(x.dtype)


def rope(x, pos, rot_dim, theta):
    half = rot_dim // 2
    inv = theta ** (-jnp.arange(half, dtype=jnp.float32) / half)
    ang = pos.astype(jnp.float32)[:, None] * inv[None, :]
    cos = jnp.cos(ang)[:, None, :].astype(x.dtype)
    sin = jnp.sin(ang)[:, None, :].astype(x.dtype)
    x1 = x[..., :half]
    x2 = x[..., half:rot_dim]
    return jnp.concatenate([x1 * cos - x2 * sin, x1 * sin + x2 * cos, x[..., rot_dim:]], axis=-1)


def moba_attention(q, k, v, q_pos):
    B, L, H, dh = k.shape
    n_blk = max(-(-L // MOBA_BLOCK), MOBA_TOPK)
    pad = n_blk * MOBA_BLOCK - L
    k = jnp.pad(k, ((0, 0), (0, pad), (0, 0), (0, 0)))
    v = jnp.pad(v, ((0, 0), (0, pad), (0, 0), (0, 0)))
    kb = k.reshape(B, n_blk, MOBA_BLOCK, H, dh).transpose(0, 3, 1, 2, 4)
    vb = v.reshape(B, n_blk, MOBA_BLOCK, H, dh).transpose(0, 3, 1, 2, 4)
    k_mean = jnp.mean(kb.astype(jnp.float32), axis=3)
    Tq = q.shape[1]
    qc = math.gcd(Tq, MOBA_Q_CHUNK)
    nc = Tq // qc
    q_chunks = q.reshape(B, nc, qc, H, dh).transpose(1, 0, 2, 3, 4)
    pos_chunks = q_pos.reshape(nc, qc)
    b_idx = jnp.arange(B)[:, None, None, None]
    h_idx = jnp.arange(H)[None, :, None, None]
    blk_ids = jnp.arange(n_blk)
    in_blk = jnp.arange(MOBA_BLOCK)
    scale = dh ** -0.5

    def one_chunk(args):
        qq, pp = args
        own = pp // MOBA_BLOCK
        s_blk = jnp.einsum('bqhd,bhnd->bhqn', qq.astype(jnp.float32), k_mean)
        fully_past = blk_ids[None, :] < own[:, None]
        s_blk = jnp.where(fully_past[None, None], s_blk, NEG_INF)
        _, top_idx = lax.top_k(s_blk, MOBA_TOPK)
        own_b = jnp.broadcast_to(own[None, None, :, None], (B, H, qc, 1)).astype(top_idx.dtype)
        idx = jnp.concatenate([top_idx, own_b], axis=-1)
        k_sel = kb[b_idx, h_idx, idx]
        v_sel = vb[b_idx, h_idx, idx]
        s = jnp.einsum('bqhd,bhqnkd->bhqnk', qq, k_sel).astype(jnp.float32) * scale
        key_pos = idx[..., None] * MOBA_BLOCK + in_blk
        sel_ok = jnp.concatenate([jnp.arange(MOBA_TOPK)[None, :] < own[:, None],
                                  jnp.ones((qc, 1), dtype=bool)], axis=-1)
        ok = sel_ok[None, None, :, :, None] & (key_pos <= pp[None, None, :, None, None])
        s = jnp.where(ok, s, NEG_INF)
        p = jax.nn.softmax(s.reshape(B, H, qc, -1), axis=-1).reshape(s.shape)
        return jnp.einsum('bhqnk,bhqnkd->bqhd', p.astype(v_sel.dtype), v_sel)

    out = lax.map(one_chunk, (q_chunks, pos_chunks))
    return out.transpose(1, 0, 2, 3, 4).reshape(B, Tq, H, dh)


def retention(q, k, v, s0):
    B, T, H, dk = q.shape
    dv = v.shape[-1]
    L = math.gcd(T, RET_CHUNK)
    nc = T // L
    log_g = jnp.log1p(-jnp.exp2(-5.0 - jnp.arange(H, dtype=jnp.float32)))
    i = jnp.arange(L, dtype=jnp.float32)
    diff = i[:, None] - i[None, :]
    causal = diff >= 0
    decay = jnp.where(causal[None], jnp.exp(jnp.where(causal, diff, 0.0)[None] * log_g[:, None, None]), 0.0).astype(q.dtype)
    q_dec = jnp.exp((i + 1.0)[:, None] * log_g[None, :]).astype(q.dtype)[None, :, :, None]
    k_dec = jnp.exp((L - 1.0 - i)[:, None] * log_g[None, :]).astype(q.dtype)[None, :, :, None]
    c_dec = jnp.exp(L * log_g).astype(q.dtype)[None, :, None, None]

    def split(x):
        return x.reshape(B, nc, L, H, x.shape[-1]).transpose(1, 0, 2, 3, 4)

    def step(state, xs):
        qc, kc, vc = xs
        attn = jnp.einsum('bihd,bjhd->bhij', qc, kc) * decay[None]
        inner = jnp.einsum('bhij,bjhe->bihe', attn, vc)
        cross = jnp.einsum('bihd,bhde->bihe', qc, state) * q_dec
        new = state * c_dec + jnp.einsum('bjhd,bjhe->bhde', kc * k_dec, vc)
        return new.astype(state.dtype), (inner + cross).astype(vc.dtype)

    s_final, out = lax.scan(step, s0, (split(q), split(k), split(v)))
    return out.transpose(1, 0, 2, 3, 4).reshape(B, T, H, dv), s_final


def head_norm(o, gain):
    o32 = o.astype(jnp.float32)
    mu = jnp.mean(o32, axis=-1, keepdims=True)
    var = jnp.mean(jnp.square(o32 - mu), axis=-1, keepdims=True)
    y = (o32 - mu) * lax.rsqrt(var + EPS) * gain.astype(jnp.float32).reshape(o.shape[2], o.shape[3])
    return y.astype(o.dtype)


def dilated_attention(q, k_ext, v_ext, first_valid):
    B, T, H, dh = q.shape
    qc = math.gcd(T, DIL_Q_CHUNK)
    nc = T // qc
    q_chunks = q.reshape(B, nc, qc, H, dh).transpose(1, 0, 2, 3, 4)
    starts = jnp.arange(nc, dtype=jnp.int32) * qc
    scale = dh ** -0.5

    def one_chunk(args):
        qq, start = args
        kk = lax.dynamic_slice_in_dim(k_ext, start, qc + W_MAX, axis=1)
        vv = lax.dynamic_slice_in_dim(v_ext, start, qc + W_MAX, axis=1)
        outs, lses = [], []
        for window, dil in DILATED:
            local = (np.arange(qc)[:, None] + W_MAX - np.arange(window // dil + 1)[None, :] * dil).astype(np.int32)
            ks = kk[:, local]
            vs = vv[:, local]
            s = jnp.einsum('bqhd,bqmhd->bhqm', qq, ks).astype(jnp.float32) * scale
            ok = (start + local) >= first_valid
            s = jnp.where(ok[None, None], s, NEG_INF)
            lse = jax.nn.logsumexp(s, axis=-1)
            p = jnp.exp(s - lse[..., None])
            outs.append(jnp.einsum('bhqm,bqmhd->bqhd', p.astype(vs.dtype), vs))
            lses.append(lse)
        wts = jax.nn.softmax(jnp.stack(lses), axis=0)
        wts = wts.transpose(0, 1, 3, 2)[..., None]
        return jnp.sum(jnp.stack(outs) * wts.astype(outs[0].dtype), axis=0)

    out = lax.map(one_chunk, (q_chunks, starts))
    return out.transpose(1, 0, 2, 3, 4).reshape(B, T, H, dh)


def ab_mixer(h, pos, k_past, v_past, s0, w_in, w_out, gn_w):
    B, T, _ = h.shape
    wa = H_A * HEAD_DIM
    wb = H_B * HEAD_DIM
    proj = h @ w_in
    qa, ka, va, qb, kb, vb, gb = jnp.split(proj, [wa, 2 * wa, 3 * wa, 3 * wa + wb, 3 * wa + 2 * wb, 3 * wa + 3 * wb], axis=-1)
    qa = rope(qa.reshape(B, T, H_A, HEAD_DIM), pos, ROT_DIM, ROPE_THETA)
    ka = rope(ka.reshape(B, T, H_A, HEAD_DIM), pos, ROT_DIM, ROPE_THETA)
    va = va.reshape(B, T, H_A, HEAD_DIM)
    if k_past is None:
        k_all, v_all = ka, va
    else:
        k_all = jnp.concatenate([k_past, ka], axis=1)
        v_all = jnp.concatenate([v_past, va], axis=1)
    oa = moba_attention(qa, k_all, v_all, pos)
    qb = rope(qb.reshape(B, T, H_B, HEAD_DIM), pos, HEAD_DIM, RET_THETA)
    kb = rope(kb.reshape(B, T, H_B, HEAD_DIM), pos, HEAD_DIM, RET_THETA) * (HEAD_DIM ** -0.5)
    ob, s_new = retention(qb, kb, vb.reshape(B, T, H_B, HEAD_DIM), s0)
    ob = head_norm(ob, gn_w).reshape(B, T, wb) * jax.nn.silu(gb)
    out = jnp.concatenate([oa.reshape(B, T, wa), ob], axis=-1) @ w_out
    return out, ka, va, s_new


def c_mixer(h, pos, k_buf, v_buf, w_in, w_out):
    B, T, _ = h.shape
    q, k, v = jnp.split(h @ w_in, 3, axis=-1)
    q = rope(q.reshape(B, T, H_C, HEAD_DIM), pos, ROT_DIM, ROPE_THETA)
    k = rope(k.reshape(B, T, H_C, HEAD_DIM), pos, ROT_DIM, ROPE_THETA)
    v = v.reshape(B, T, H_C, HEAD_DIM)
    if k_buf is None:
        ctx_k, ctx_v, n_prev = k, v, 0
    else:
        ctx_k = jnp.concatenate([k_buf, k], axis=1)
        ctx_v = jnp.concatenate([v_buf, v], axis=1)
        n_prev = k_buf.shape[1]
    lead = W_MAX - n_prev
    k_ext = jnp.pad(ctx_k, ((0, 0), (lead, 0), (0, 0), (0, 0)))
    v_ext = jnp.pad(ctx_v, ((0, 0), (lead, 0), (0, 0), (0, 0)))
    o = dilated_attention(q, k_ext, v_ext, lead)
    keep = min(W_MAX, n_prev + T)
    return o.reshape(B, T, H_C * HEAD_DIM) @ w_out, ctx_k[:, -keep:], ctx_v[:, -keep:]


def swiglu_ffn(h, w_gate, w_up, w_down):
    return (jax.nn.silu(h @ w_gate) * (h @ w_up)) @ w_down


def setup_inputs(seed: int = 0) -> dict:
    key = jax.random.key(seed)
    ks = jax.random.split(key, 24)
    f32 = jnp.float32
    n_pages = PAST_LEN // PAGE_SIZE
    n_used = DEC_BATCH * n_pages
    n_pool = n_used + n_used // 4
    wbuf = min(W_MAX, PAST_LEN)
    d_ab_in = 3 * H_A * HEAD_DIM + 4 * H_B * HEAD_DIM
    d_ab_out = H_A * HEAD_DIM + H_B * HEAD_DIM
    d_c = H_C * HEAD_DIM

    def nrm(k, shape, scale=1.0):
        return jax.random.normal(k, shape, f32) * scale

    return {
        'x_prompt': nrm(ks[0], (BATCH, SEQ, D_MODEL)),
        'x_sample': nrm(ks[1], (DEC_BATCH, DEC_SEQ, D_MODEL)),
        'cache_k_a': nrm(ks[2], (N_AB, n_pool, PAGE_SIZE, H_A, HEAD_DIM)),
        'cache_v_a': nrm(ks[3], (N_AB, n_pool, PAGE_SIZE, H_A, HEAD_DIM)),
        'page_table': jax.random.permutation(ks[4], n_pool)[:n_used].reshape(DEC_BATCH, n_pages).astype(jnp.int32),
        'state_ret': nrm(ks[5], (N_AB, DEC_BATCH, H_B, HEAD_DIM, HEAD_DIM), 0.1),
        'cache_win_k': nrm(ks[6], (N_C, DEC_BATCH, wbuf, H_C, HEAD_DIM)),
        'cache_win_v': nrm(ks[7], (N_C, DEC_BATCH, wbuf, H_C, HEAD_DIM)),
        'norm_mix': 1.0 + nrm(ks[8], (DEPTH, D_MODEL), 0.02),
        'norm_ffn': 1.0 + nrm(ks[9], (DEPTH, D_MODEL), 0.02),
        'norm_final': 1.0 + nrm(ks[10], (D_MODEL,), 0.02),
        'w_in_ab': nrm(ks[11], (N_AB, D_MODEL, d_ab_in), D_MODEL ** -0.5),
        'w_out_ab': nrm(ks[12], (N_AB, d_ab_out, D_MODEL), d_ab_out ** -0.5),
        'ret_gn_w': 1.0 + nrm(ks[13], (N_AB, H_B * HEAD_DIM), 0.02),
        'w_in_c': nrm(ks[14], (N_C, D_MODEL, 3 * d_c), D_MODEL ** -0.5),
        'w_out_c': nrm(ks[15], (N_C, d_c, D_MODEL), d_c ** -0.5),
        'ffn_w_gate': nrm(ks[16], (DEPTH, D_MODEL, D_FF), D_MODEL ** -0.5),
        'ffn_w_up': nrm(ks[17], (DEPTH, D_MODEL, D_FF), D_MODEL ** -0.5),
        'ffn_w_down': nrm(ks[18], (DEPTH, D_FF, D_MODEL), D_FF ** -0.5),
    }


def reference(x_prompt, x_sample, cache_k_a, cache_v_a, page_table, state_ret, cache_win_k, cache_win_v,
              norm_mix, norm_ffn, norm_final, w_in_ab, w_out_ab, ret_gn_w, w_in_c, w_out_c,
              ffn_w_gate, ffn_w_up, ffn_w_down):
    n_seq, n_pages = page_table.shape
    past_len = n_pages * cache_k_a.shape[2]
    t_p = x_prompt.shape[1]
    t_s = x_sample.shape[1]
    pos_p = jnp.arange(t_p, dtype=jnp.int32)
    pos_s = past_len + jnp.arange(t_s, dtype=jnp.int32)
    xp, xs = x_prompt, x_sample
    ka_p, va_p, ret_p, kc_p, vc_p = [], [], [], [], []
    ka_s, va_s, ret_s, kc_s, vc_s = [], [], [], [], []
    for l in range(DEPTH):
        i = l // 2
        hp = rms_norm(xp, norm_mix[l])
        hs = rms_norm(xs, norm_mix[l])
        if l % 2 == 0:
            k_past = cache_k_a[i][page_table].reshape(n_seq, past_len, H_A, HEAD_DIM)
            v_past = cache_v_a[i][page_table].reshape(n_seq, past_len, H_A, HEAD_DIM)
            s0 = jnp.zeros((xp.shape[0], H_B, HEAD_DIM, HEAD_DIM), state_ret.dtype)
            op, kp, vp, sp = ab_mixer(hp, pos_p, None, None, s0, w_in_ab[i], w_out_ab[i], ret_gn_w[i])
            os_, kss, vss, sss = ab_mixer(hs, pos_s, k_past, v_past, state_ret[i], w_in_ab[i], w_out_ab[i], ret_gn_w[i])
            ka_p.append(kp); va_p.append(vp); ret_p.append(sp)
            ka_s.append(kss); va_s.append(vss); ret_s.append(sss)
        else:
            op, kp, vp = c_mixer(hp, pos_p, None, None, w_in_c[i], w_out_c[i])
            os_, kss, vss = c_mixer(hs, pos_s, cache_win_k[i], cache_win_v[i], w_in_c[i], w_out_c[i])
            kc_p.append(kp); vc_p.append(vp)
            kc_s.append(kss); vc_s.append(vss)
        xp = xp + op
        xs = xs + os_
        xp = xp + swiglu_ffn(rms_norm(xp, norm_ffn[l]), ffn_w_gate[l], ffn_w_up[l], ffn_w_down[l])
        xs = xs + swiglu_ffn(rms_norm(xs, norm_ffn[l]), ffn_w_gate[l], ffn_w_up[l], ffn_w_down[l])
    y_prompt = rms_norm(xp, norm_final)
    y_sample = rms_norm(xs, norm_final)
    return (y_prompt, y_sample,
            jnp.stack(ka_p), jnp.stack(va_p), jnp.stack(ret_p), jnp.stack(kc_p), jnp.stack(vc_p),
            jnp.stack(ka_s), jnp.stack(va_s), jnp.stack(ret_s), jnp.stack(kc_s), jnp.stack(vc_s))
```

```python
import functools
import math

import jax
import jax.numpy as jnp
import numpy as np
from jax import lax
from jax.experimental import pallas as pl
from jax.experimental.pallas import tpu as pltpu

F32 = jnp.float32
BF16 = jnp.bfloat16

HEAD_DIM = 64
LANES = 128
ROT_DIM = HEAD_DIM // 4
ROPE_THETA = 500000.0
RET_THETA = 10000.0
MOBA_BLOCK = 256
MOBA_TOPK = 3
RET_CHUNK = 128
DILATED = ((128, 1), (512, 4), (2048, 16))
W_MAX = max(w for w, _ in DILATED)
EPS = 1e-6
NEG_INF = -1e30
ATT_SCALE = HEAD_DIM ** -0.5
VMEM_LIMIT = 56 * 1024 * 1024

_NT = (((1,), (1,)), ((), ()))
_TN = (((0,), (0,)), ((), ()))


def _cparams(*sem):
    return pltpu.CompilerParams(dimension_semantics=sem, vmem_limit_bytes=VMEM_LIMIT)


def _rms(x, w):
    ms = jnp.mean(x * x, axis=-1, keepdims=True)
    return x * lax.rsqrt(ms + EPS) * w


SEG = 512


def _proj_kernel(x_ref, nw_ref, w_ref, ca_ref, na_ref, pa_ref, cb_ref, nb_ref, pb_ref, *out_refs, segs):
    xn = _rms(x_ref[...], nw_ref[...]).astype(BF16)
    for s, (kind, oi, col, post) in enumerate(segs):
        acc = jnp.dot(xn, w_ref[:, s * SEG:(s + 1) * SEG], preferred_element_type=F32)
        if kind is None:
            out_refs[oi][:, col:col + SEG] = acc
            continue
        c_ref, n_ref, p_ref, shift = (ca_ref, na_ref, pa_ref, ROT_DIM // 2) if kind == "a" else (
            cb_ref, nb_ref, pb_ref, HEAD_DIM // 2)
        for c in range(SEG // LANES):
            a = acc[:, c * LANES:(c + 1) * LANES]
            r = a * c_ref[...] + pltpu.roll(a, LANES - shift, 1) * n_ref[...] + pltpu.roll(a, shift, 1) * p_ref[...]
            if post != 1.0:
                r = r * post
            out_refs[oi][:, col + c * LANES:col + (c + 1) * LANES] = r


def _rope_tables(pos, rot_dim, theta):
    half = rot_dim // 2
    inv = theta ** (-jnp.arange(half, dtype=F32) / half)
    ang = pos.astype(F32)[:, None] * inv[None, :]
    cos, sin = jnp.cos(ang), jnp.sin(ang)
    lane = np.arange(LANES) % HEAD_DIM
    idx = lane % half
    cos_t = jnp.where(lane < rot_dim, cos[:, idx], 1.0)
    neg_t = jnp.where(lane < half, -sin[:, idx], 0.0)
    pos_t = jnp.where((lane >= half) & (lane < rot_dim), sin[:, idx], 0.0)
    return cos_t.astype(F32), neg_t.astype(F32), pos_t.astype(F32)


def _project(x, nw, w_bf, tabs_a, tabs_b, segs, out_widths, tm):
    n, d = x.shape
    t_tab = tabs_a[0].shape[0]
    nt = t_tab // tm
    tab_spec = pl.BlockSpec((tm, LANES), lambda i: (i % nt, 0))
    return pl.pallas_call(
        functools.partial(_proj_kernel, segs=segs),
        grid=(n // tm,),
        in_specs=[pl.BlockSpec((tm, d), lambda i: (i, 0)),
                  pl.BlockSpec((1, d), lambda i: (0, 0)),
                  pl.BlockSpec(w_bf.shape, lambda i: (0, 0))] + [tab_spec] * 6,
        out_specs=[pl.BlockSpec((tm, w), lambda i: (i, 0)) for w in out_widths],
        out_shape=[jax.ShapeDtypeStruct((n, w), F32) for w in out_widths],
        compiler_params=_cparams("parallel"),
        name="rms_proj_rope",
    )(x, nw.reshape(1, d), w_bf, *tabs_a, *tabs_b)


def _outproj_kernel(*refs, nparts):
    parts, w_ref, res_ref, o_ref = refs[:nparts], refs[nparts], refs[nparts + 1], refs[nparts + 2]
    a = jnp.concatenate([p[...].astype(BF16) for p in parts], axis=1) if nparts > 1 else parts[0][...].astype(BF16)
    o_ref[...] = res_ref[...] + jnp.dot(a, w_ref[...], preferred_element_type=F32)


def _outproj(parts, w_bf, res, tm):
    n, d = res.shape
    return pl.pallas_call(
        functools.partial(_outproj_kernel, nparts=len(parts)),
        grid=(n // tm,),
        in_specs=[pl.BlockSpec((tm, p.shape[1]), lambda i: (i, 0)) for p in parts]
        + [pl.BlockSpec(w_bf.shape, lambda i: (0, 0)), pl.BlockSpec((tm, d), lambda i: (i, 0))],
        out_specs=pl.BlockSpec((tm, d), lambda i: (i, 0)),
        out_shape=jax.ShapeDtypeStruct((n, d), F32),
        compiler_params=_cparams("parallel"),
        name="outproj_residual",
    )(*parts, w_bf, res)


def _ffn_kernel(x_ref, nw_ref, wg_ref, wu_ref, wd_ref, fw_ref, o_ref, *, nchunk, final):
    x = x_ref[...]
    xn = _rms(x, nw_ref[...]).astype(BF16)
    dff = wg_ref.shape[1]
    tf = dff // nchunk
    acc = x
    for c in range(nchunk):
        g = jnp.dot(xn, wg_ref[:, c * tf:(c + 1) * tf], preferred_element_type=F32)
        u = jnp.dot(xn, wu_ref[:, c * tf:(c + 1) * tf], preferred_element_type=F32)
        h = (g * jax.nn.sigmoid(g) * u).astype(BF16)
        acc = acc + jnp.dot(h, wd_ref[c * tf:(c + 1) * tf, :], preferred_element_type=F32)
    if final:
        acc = _rms(acc, fw_ref[...])
    o_ref[...] = acc


def _ffn(x, nw, wg, wu, wd, fw, final, tm):
    n, d = x.shape
    dff = wg.shape[1]
    const = lambda i: (0, 0)
    return pl.pallas_call(
        functools.partial(_ffn_kernel, nchunk=2, final=final),
        grid=(n // tm,),
        in_specs=[pl.BlockSpec((tm, d), lambda i: (i, 0)), pl.BlockSpec((1, d), const),
                  pl.BlockSpec((d, dff), const), pl.BlockSpec((d, dff), const), pl.BlockSpec((dff, d), const),
                  pl.BlockSpec((1, d), const)],
        out_specs=pl.BlockSpec((tm, d), lambda i: (i, 0)),
        out_shape=jax.ShapeDtypeStruct((n, d), F32),
        compiler_params=_cparams("parallel"),
        name="rms_swiglu_residual",
    )(x, nw.reshape(1, d), wg, wu, wd, fw.reshape(1, d))


def _stage_kv(k_ref, v_ref, kb_ref, vt_ref, nblk, blk, kmean_ref=None):
    def body(n, c):
        r = pl.multiple_of(n * blk, blk)
        kblk = k_ref[pl.ds(r, blk), :]
        if kmean_ref is not None:
            kmean_ref[pl.ds(n, 1), :] = jnp.sum(kblk, axis=0, keepdims=True) * (1.0 / blk)
        kb_ref[n] = kblk.astype(BF16)
        vt_ref[n] = v_ref[pl.ds(r, blk), :].T.astype(BF16)
        return c
    lax.fori_loop(0, nblk, body, 0)


def _online_step(carry, s, vt, mult=None):
    m, l, acc = carry
    m_new = jnp.maximum(m, jnp.max(s, axis=0, keepdims=True))
    alpha = jnp.exp(m - m_new)
    p = jnp.exp(s - m_new)
    if mult is not None:
        p = p * mult
    l = alpha * l + jnp.sum(p, axis=0, keepdims=True)
    acc = alpha * acc + jnp.dot(vt, p.astype(BF16), preferred_element_type=F32)
    return m_new, l, acc


def _first_step(s, vt, mult=None):
    m = jnp.max(s, axis=0, keepdims=True)
    p = jnp.exp(s - m)
    if mult is not None:
        p = p * mult
    return m, jnp.sum(p, axis=0, keepdims=True), jnp.dot(vt, p.astype(BF16), preferred_element_type=F32)


def _moba_prompt_kernel(q_ref, k_ref, v_ref, o_ref, kb_ref, vt_ref, kmean_ref, bias_ref, *, nblk):
    blk = MOBA_BLOCK
    qt = pl.program_id(2)

    @pl.when(qt == 0)
    def _():
        _stage_kv(k_ref, v_ref, kb_ref, vt_ref, nblk, blk, kmean_ref)

    q_t = q_ref[...].T
    feat = lax.broadcasted_iota(jnp.int32, (LANES, blk), 0)
    blk_id = lax.broadcasted_iota(jnp.int32, (nblk, blk), 0)
    blk_f = blk_id.astype(F32)
    key_i = lax.broadcasted_iota(jnp.int32, (blk, blk), 0)
    qry_i = lax.broadcasted_iota(jnp.int32, (blk, blk), 1)
    outs = []
    for hh in range(2):
        q_h = jnp.where(feat // HEAD_DIM == hh, q_t, 0.0)
        sc = jnp.dot(kmean_ref[...], q_h, preferred_element_type=F32, precision=lax.Precision.HIGHEST)
        sc = jnp.where(blk_id < qt, sc, NEG_INF)
        sel = jnp.zeros((nblk, blk), F32)
        for r in range(MOBA_TOPK):
            mx = jnp.max(sc, axis=0, keepdims=True)
            first = jnp.min(jnp.where(sc == mx, blk_f, float(nblk)), axis=0, keepdims=True)
            pick = blk_f == first
            sel = jnp.maximum(sel, jnp.where(pick, jnp.where(r < qt, 1.0, 0.0), 0.0))
            sc = jnp.where(pick, -jnp.inf, sc)
        bias_ref[hh] = jnp.where(sel > 0.0, 0.0, NEG_INF)

        qb = (q_h * ATT_SCALE).astype(BF16)
        s = jnp.dot(kb_ref[qt], qb, preferred_element_type=F32)
        s = jnp.where(key_i <= qry_i, s, NEG_INF)
        carry = _first_step(s, vt_ref[qt])

        def body(n, carry, hh=hh, qb=qb):
            s = jnp.dot(kb_ref[n], qb, preferred_element_type=F32) + bias_ref[hh, pl.ds(n, 1), :]
            return _online_step(carry, s, vt_ref[n])

        m, l, acc = lax.fori_loop(0, qt, body, carry)
        outs.append(acc / l)
    o_t = jnp.where(feat // HEAD_DIM == 0, outs[0], outs[1])
    o_ref[...] = o_t.T


def _moba_prompt(q, k, v, batch, seq):
    n, width = q.shape
    nblk = seq // MOBA_BLOCK
    npair = width // LANES
    blk = MOBA_BLOCK
    kv_spec = pl.BlockSpec((seq, LANES), lambda b, p, t: (b, p))
    q_spec = pl.BlockSpec((blk, LANES), lambda b, p, t: (b * nblk + t, p))
    return pl.pallas_call(
        functools.partial(_moba_prompt_kernel, nblk=nblk),
        grid=(batch, npair, nblk),
        in_specs=[q_spec, kv_spec, kv_spec],
        out_specs=q_spec,
        out_shape=jax.ShapeDtypeStruct((n, width), F32),
        scratch_shapes=[pltpu.VMEM((nblk, blk, LANES), BF16), pltpu.VMEM((nblk, LANES, blk), BF16),
                        pltpu.VMEM((nblk, LANES), F32), pltpu.VMEM((2, nblk, blk), F32)],
        compiler_params=_cparams("parallel", "parallel", "arbitrary"),
        name="moba_prompt",
    )(q, k, v)


def _dilated_multiplicity(delta):
    c = np.zeros(delta.shape, np.int32)
    for window, dil in DILATED:
        c += ((delta >= 0) & (delta <= window) & (delta % dil == 0)).astype(np.int32)
    return c


DIL_TILE = 256
DIL_NBACK = W_MAX // DIL_TILE


def _dilated_prompt_tables():
    j = np.arange(DIL_NBACK + 1)[:, None, None]
    ki = np.arange(DIL_TILE)[None, :, None]
    qi = np.arange(DIL_TILE)[None, None, :]
    c = _dilated_multiplicity(DIL_TILE * j + qi - ki)
    return np.where(c > 0, 0.0, NEG_INF).astype(np.float32), c.astype(np.float32)


def _dilated_prompt_kernel(q_ref, k_ref, v_ref, bias_ref, mult_ref, o_ref, kb_ref, vt_ref, *, nblk):
    blk = DIL_TILE
    qt = pl.program_id(2)

    @pl.when(qt == 0)
    def _():
        _stage_kv(k_ref, v_ref, kb_ref, vt_ref, nblk, blk)

    q_t = q_ref[...].T
    feat = lax.broadcasted_iota(jnp.int32, (LANES, blk), 0)
    outs = []
    for hh in range(2):
        qb = (jnp.where(feat // HEAD_DIM == hh, q_t, 0.0) * ATT_SCALE).astype(BF16)
        s = jnp.dot(kb_ref[qt], qb, preferred_element_type=F32) + bias_ref[0]
        carry = _first_step(s, vt_ref[qt], mult_ref[0])

        def body(j, carry, qb=qb):
            n = qt - j
            s = jnp.dot(kb_ref[n], qb, preferred_element_type=F32) + bias_ref[j]
            return _online_step(carry, s, vt_ref[n], mult_ref[j])

        m, l, acc = lax.fori_loop(1, jnp.minimum(qt, DIL_NBACK) + 1, body, carry)
        outs.append(acc / l)
    o_t = jnp.where(feat // HEAD_DIM == 0, outs[0], outs[1])
    o_ref[...] = o_t.T


def _dilated_prompt(q, k, v, batch, seq):
    n, width = q.shape
    blk = DIL_TILE
    nblk = seq // blk
    npair = width // LANES
    bias, mult = _dilated_prompt_tables()
    kv_spec = pl.BlockSpec((seq, LANES), lambda b, p, t: (b, p))
    q_spec = pl.BlockSpec((blk, LANES), lambda b, p, t: (b * nblk + t, p))
    tab_spec = pl.BlockSpec(bias.shape, lambda b, p, t: (0, 0, 0))
    return pl.pallas_call(
        functools.partial(_dilated_prompt_kernel, nblk=nblk),
        grid=(batch, npair, nblk),
        in_specs=[q_spec, kv_spec, kv_spec, tab_spec, tab_spec],
        out_specs=q_spec,
        out_shape=jax.ShapeDtypeStruct((n, width), F32),
        scratch_shapes=[pltpu.VMEM((nblk, blk, LANES), BF16), pltpu.VMEM((nblk, LANES, blk), BF16)],
        compiler_params=_cparams("parallel", "parallel", "arbitrary"),
        name="dilated_prompt",
    )(q, k, v, jnp.asarray(bias), jnp.asarray(mult))


RET_ROWS = 128


def _retention_tables(n_heads, chunk):
    log_g = jnp.log1p(-jnp.exp2(-5.0 - jnp.arange(n_heads, dtype=F32)))
    i = jnp.arange(chunk, dtype=F32)
    diff = i[:, None] - i[None, :]
    causal = diff >= 0
    decay = jnp.where(causal[None], jnp.exp(jnp.where(causal, diff, 0.0)[None] * log_g[:, None, None]), 0.0)
    q_dec = jnp.exp((i + 1.0)[:, None] * log_g[None, :])
    k_dec = jnp.exp((chunk - 1.0 - i)[:, None] * log_g[None, :])
    c_dec = jnp.exp(chunk * log_g)
    pad = RET_ROWS - chunk
    npair = n_heads // 2
    decay = jnp.pad(decay, ((0, 0), (0, pad), (0, pad))).reshape(npair, 2 * RET_ROWS, RET_ROWS)
    expand = lambda t: jnp.pad(jnp.repeat(t, HEAD_DIM, axis=1), ((0, pad), (0, 0))).reshape(
        RET_ROWS, npair, LANES).transpose(1, 0, 2)
    q_tab, k_tab = expand(q_dec), expand(k_dec)
    same_head = (np.arange(LANES)[:, None] // HEAD_DIM) == (np.arange(LANES)[None, :] // HEAD_DIM)
    c_tab = jnp.where(same_head[None], jnp.repeat(c_dec, HEAD_DIM).reshape(npair, LANES, 1), 0.0)
    return decay.astype(F32), q_tab.astype(F32), k_tab.astype(F32), c_tab.astype(F32)


def _retention_kernel(q_ref, k_ref, v_ref, g_ref, gn_ref, s0_ref, dec_ref, qd_ref, kd_ref, cd_ref,
                      o_ref, sout_ref, st_ref, *, chunk, nchunk):
    c = pl.program_id(2)

    @pl.when(c == 0)
    def _():
        st_ref[...] = s0_ref[...]

    pad = RET_ROWS - chunk
    lane = lax.broadcasted_iota(jnp.int32, (RET_ROWS, LANES), 1)
    head0 = lane < HEAD_DIM
    same_head = (lax.broadcasted_iota(jnp.int32, (LANES, LANES), 0) // HEAD_DIM) == (
        lax.broadcasted_iota(jnp.int32, (LANES, LANES), 1) // HEAD_DIM)

    def rows(ref, r):
        x = ref[pl.ds(r, chunk), :]
        return x if pad == 0 else jnp.concatenate([x, jnp.zeros((pad, LANES), F32)], axis=0)

    def seg_mean(x):
        s0 = jnp.sum(jnp.where(head0, x, 0.0), axis=-1, keepdims=True)
        s1 = jnp.sum(jnp.where(head0, 0.0, x), axis=-1, keepdims=True)
        return jnp.where(head0, s0, s1) * (1.0 / HEAD_DIM)

    def one_chunk(j, carry):
        r = pl.multiple_of(j * chunk, chunk)
        q, k, v, g = rows(q_ref, r), rows(k_ref, r), rows(v_ref, r), rows(g_ref, r)
        state = st_ref[...]
        kb, vb = k.astype(BF16), v.astype(BF16)
        q2 = jnp.concatenate([jnp.where(head0, q, 0.0), jnp.where(head0, 0.0, q)], axis=0).astype(BF16)
        attn = lax.dot_general(q2, kb, _NT, preferred_element_type=F32) * dec_ref[...]
        inn = jnp.dot(attn.astype(BF16), vb, preferred_element_type=F32)
        inner = jnp.where(head0, inn[:RET_ROWS], inn[RET_ROWS:])
        cross = jnp.dot(q.astype(BF16), state.astype(BF16), preferred_element_type=F32) * qd_ref[...]
        upd = lax.dot_general((k * kd_ref[...]).astype(BF16), vb, _TN, preferred_element_type=F32)
        st_ref[...] = state * cd_ref[...] + jnp.where(same_head, upd, 0.0)
        o = inner + cross
        mu = seg_mean(o)
        var = seg_mean(jnp.square(o - mu))
        y = (o - mu) * lax.rsqrt(var + EPS) * gn_ref[...]
        y = y * (g * jax.nn.sigmoid(g))
        o_ref[pl.ds(r, chunk), :] = y[:chunk]
        return carry

    lax.fori_loop(0, nchunk, one_chunk, 0)

    @pl.when(c == pl.num_programs(2) - 1)
    def _():
        sout_ref[...] = st_ref[...]


def _retention(q, k, v, g, gn_w, s0, batch, seq, chunk, nchunk):
    n, width = q.shape
    n_heads = width // HEAD_DIM
    npair = width // LANES
    rows = chunk * nchunk
    steps = seq // rows
    dec, qd, kd, cd = _retention_tables(n_heads, chunk)
    eye2 = jnp.eye(2, dtype=F32)
    s0p = s0.reshape(batch, npair, 2, HEAD_DIM, HEAD_DIM)
    s0p = jnp.einsum('bpide,ij->bpidje', s0p, eye2).reshape(batch, npair, LANES, LANES)
    row_spec = pl.BlockSpec((rows, LANES), lambda b, p, c: (b * steps + c, p))
    st_spec = pl.BlockSpec((None, None, LANES, LANES), lambda b, p, c: (b, p, 0, 0))
    tab = lambda t: pl.BlockSpec((None,) + t.shape[1:], lambda b, p, c: (p, 0, 0))
    out, s_new = pl.pallas_call(
        functools.partial(_retention_kernel, chunk=chunk, nchunk=nchunk),
        grid=(batch, npair, steps),
        in_specs=[row_spec, row_spec, row_spec, row_spec, pl.BlockSpec((1, LANES), lambda b, p, c: (0, p)),
                  st_spec, tab(dec), tab(qd), tab(kd), tab(cd)],
        out_specs=[row_spec, st_spec],
        out_shape=[jax.ShapeDtypeStruct((n, width), F32),
                   jax.ShapeDtypeStruct((batch, npair, LANES, LANES), F32)],
        scratch_shapes=[pltpu.VMEM((LANES, LANES), F32)],
        compiler_params=_cparams("parallel", "parallel", "arbitrary"),
        name="retention",
    )(q, k, v, g, gn_w.reshape(1, width), s0p, dec, qd, kd, cd)
    s_new = s_new.reshape(batch, npair, 2, HEAD_DIM, 2, HEAD_DIM)
    s_new = jnp.stack([s_new[:, :, 0, :, 0, :], s_new[:, :, 1, :, 1, :]], axis=2)
    return out, s_new.reshape(batch, n_heads, HEAD_DIM, HEAD_DIM)


def _block_diag_queries(q, n_heads):
    t = q.shape[0]
    rep = jnp.concatenate([q] * n_heads, axis=0)
    rows = lax.broadcasted_iota(jnp.int32, rep.shape, 0) // t
    cols = lax.broadcasted_iota(jnp.int32, rep.shape, 1) // HEAD_DIM
    return jnp.where(rows == cols, rep, 0.0)


def _scaled_bf16(q):
    return (q * ATT_SCALE).astype(BF16)


def _take_diag(o, n_heads, t):
    cols = lax.broadcasted_iota(jnp.int32, (t, o.shape[1]), 1) // HEAD_DIM
    out = jnp.zeros((t, o.shape[1]), F32)
    for h in range(n_heads):
        out = jnp.where(cols == h, o[h * t:(h + 1) * t, :], out)
    return out


def _pad_rows(x, rows):
    return jnp.concatenate([x, jnp.zeros((rows - x.shape[0], x.shape[1]), x.dtype)], axis=0)


MOBA_PAGES_PER_STEP = 16


def _moba_decode_kernel(pt_ref, q_ref, kn_ref, vn_ref, *rest, n_heads, t_new, pages_per_blk, nblk):
    npg = MOBA_PAGES_PER_STEP
    kp, vp = rest[:npg], rest[npg:2 * npg]
    o_ref, qbd_ref, qf_ref, m_ref, l_ref, sc_ref, oblk_ref = rest[2 * npg:]
    g = pl.program_id(1)
    rows = n_heads * t_new
    lane = lax.broadcasted_iota(jnp.int32, (rows, LANES), 1)
    blk_keys = pages_per_blk * kp[0].shape[0]

    @pl.when(g == 0)
    def _():
        qf_ref[...] = _block_diag_queries(q_ref[...], n_heads)
        qbd_ref[...] = _scaled_bf16(qf_ref[...])
        m_ref[...] = jnp.zeros_like(m_ref)
        l_ref[...] = jnp.zeros_like(l_ref)
        sc_ref[...] = jnp.full_like(sc_ref, -jnp.inf)

    qbd = qbd_ref[...]
    for jb in range(npg // pages_per_blk):
        n = g * (npg // pages_per_blk) + jb
        pages = range(jb * pages_per_blk, (jb + 1) * pages_per_blk)
        s = jnp.concatenate([lax.dot_general(qbd, kp[i][...].astype(BF16), _NT, preferred_element_type=F32)
                             for i in pages], axis=1)
        mb = jnp.max(s, axis=-1, keepdims=True)
        p = jnp.exp(s - mb)
        pk = kp[0].shape[0]
        pv = sum(jnp.dot(p[:, a * pk:(a + 1) * pk].astype(BF16), vp[i][...].astype(BF16),
                         preferred_element_type=F32) for a, i in enumerate(pages))
        oblk_ref[n] = pv
        m_ref[...] = jnp.where(lane == n, mb, m_ref[...])
        l_ref[...] = jnp.where(lane == n, jnp.sum(p, axis=-1, keepdims=True), l_ref[...])
        kmean = sum(jnp.sum(kp[i][...], axis=0, keepdims=True) for i in pages) * (1.0 / blk_keys)
        sc_ref[...] = jnp.where(lane == n, jnp.sum(qf_ref[...] * kmean, axis=-1, keepdims=True), sc_ref[...])

    @pl.when(g == pl.num_programs(1) - 1)
    def _():
        s = lax.dot_general(qbd, _pad_rows(kn_ref[...], LANES).astype(BF16), _NT, preferred_element_type=F32)
        qry = lax.broadcasted_iota(jnp.int32, (rows, LANES), 0) % t_new
        s = jnp.where(lane <= qry, s, NEG_INF)
        m_own = jnp.max(s, axis=-1, keepdims=True)
        p = jnp.exp(s - m_own)
        l_own = jnp.sum(p, axis=-1, keepdims=True)
        o_own = jnp.dot(p.astype(BF16), _pad_rows(vn_ref[...], LANES).astype(BF16), preferred_element_type=F32)
        sc = sc_ref[...]
        lane_f = lane.astype(F32)
        sel = lane < 0
        for _ in range(MOBA_TOPK):
            mx = jnp.max(sc, axis=-1, keepdims=True)
            first = jnp.min(jnp.where(sc == mx, lane_f, float(LANES)), axis=-1, keepdims=True)
            pick = lane_f == first
            sel = sel | pick
            sc = jnp.where(pick, -jnp.inf, sc)
        m_all = m_ref[...]
        m_fin = jnp.maximum(jnp.max(jnp.where(sel, m_all, -jnp.inf), axis=-1, keepdims=True), m_own)
        w = jnp.where(sel, jnp.exp(m_all - m_fin), 0.0)
        w_own = jnp.exp(m_own - m_fin)
        l_fin = jnp.sum(w * l_ref[...], axis=-1, keepdims=True) + w_own * l_own

        def body(n, acc):
            col = jnp.sum(jnp.where(lane == n, w, 0.0), axis=-1, keepdims=True)
            return acc + col * oblk_ref[n]

        acc = lax.fori_loop(0, nblk, body, w_own * o_own)
        o_ref[...] = _take_diag(acc / l_fin, n_heads, t_new)


def _moba_decode(q, k_new, v_new, cache_k, cache_v, page_table, t_new):
    n, width = q.shape
    n_heads = width // HEAD_DIM
    n_seq, n_pages = page_table.shape
    page = cache_k.shape[1]
    pages_per_blk = MOBA_BLOCK // page
    nblk = n_pages // pages_per_blk
    npg = MOBA_PAGES_PER_STEP
    assert n_pages % npg == 0 and npg % pages_per_blk == 0 and MOBA_TOPK <= nblk <= LANES
    assert n_pages * page == nblk * MOBA_BLOCK and t_new <= page
    rows = n_heads * t_new
    tok_spec = pl.BlockSpec((t_new, width), lambda b, g, pt: (b, 0))
    page_specs = [pl.BlockSpec((None, page, width), functools.partial(
        lambda b, g, pt, i: (pt[b, g * npg + i], 0, 0), i=i)) for i in range(npg)]
    grid_spec = pltpu.PrefetchScalarGridSpec(
        num_scalar_prefetch=1,
        grid=(n_seq, n_pages // npg),
        in_specs=[tok_spec, tok_spec, tok_spec] + page_specs + page_specs,
        out_specs=tok_spec,
        scratch_shapes=[pltpu.VMEM((rows, width), BF16), pltpu.VMEM((rows, width), F32),
                        pltpu.VMEM((rows, LANES), F32), pltpu.VMEM((rows, LANES), F32),
                        pltpu.VMEM((rows, LANES), F32), pltpu.VMEM((nblk, rows, width), F32)],
    )
    return pl.pallas_call(
        functools.partial(_moba_decode_kernel, n_heads=n_heads, t_new=t_new,
                          pages_per_blk=pages_per_blk, nblk=nblk),
        grid_spec=grid_spec,
        out_shape=jax.ShapeDtypeStruct((n, width), F32),
        compiler_params=_cparams("parallel", "arbitrary"),
        name="moba_decode",
    )(page_table, q, k_new, v_new, *([cache_k] * npg), *([cache_v] * npg))


DIL_DEC_TILE = 512


def _dilated_decode_tables(n_heads, t_new, n_prev):
    qi = (np.arange(n_heads * t_new) % t_new)[:, None]
    c_old = _dilated_multiplicity(n_prev + qi - np.arange(n_prev)[None, :])
    c_new = _dilated_multiplicity(qi - np.arange(LANES)[None, :])
    c_new = np.where(np.arange(LANES)[None, :] < t_new, c_new, 0)
    f = lambda c: (np.where(c > 0, 0.0, NEG_INF).astype(np.float32), c.astype(np.float32))
    return f(c_old) + f(c_new)


def _dilated_decode_kernel(q_ref, kn_ref, vn_ref, kc_ref, vc_ref, bo_ref, mo_ref, bn_ref, mn_ref,
                           o_ref, qbd_ref, m_ref, l_ref, acc_ref, *, n_heads, t_new):
    kt = pl.program_id(1)

    @pl.when(kt == 0)
    def _():
        qbd_ref[...] = _scaled_bf16(_block_diag_queries(q_ref[...], n_heads))
        m_ref[...] = jnp.full_like(m_ref, -jnp.inf)
        l_ref[...] = jnp.zeros_like(l_ref)
        acc_ref[...] = jnp.zeros_like(acc_ref)

    def step(k, v, bias, mult):
        s = lax.dot_general(qbd_ref[...], k.astype(BF16), _NT, preferred_element_type=F32) + bias
        m_old = m_ref[...]
        m_new = jnp.maximum(m_old, jnp.max(s, axis=-1, keepdims=True))
        alpha = jnp.exp(m_old - m_new)
        p = jnp.exp(s - m_new) * mult
        l_ref[...] = alpha * l_ref[...] + jnp.sum(p, axis=-1, keepdims=True)
        acc_ref[...] = alpha * acc_ref[...] + jnp.dot(p.astype(BF16), v.astype(BF16), preferred_element_type=F32)
        m_ref[...] = m_new

    @pl.when(kt == 0)
    def _():
        step(_pad_rows(kn_ref[...], LANES), _pad_rows(vn_ref[...], LANES), bn_ref[...], mn_ref[...])

    step(kc_ref[...], vc_ref[...], bo_ref[...], mo_ref[...])

    @pl.when(kt == pl.num_programs(1) - 1)
    def _():
        o_ref[...] = _take_diag(acc_ref[...] / l_ref[...], n_heads, t_new)


def _dilated_decode(q, k_new, v_new, win_k, win_v, t_new):
    n, width = q.shape
    n_heads = width // HEAD_DIM
    n_seq, n_prev, _ = win_k.shape
    assert n_prev == W_MAX and n_prev % DIL_DEC_TILE == 0
    rows = n_heads * t_new
    bo, mo, bn, mn = _dilated_decode_tables(n_heads, t_new, n_prev)
    tok_spec = pl.BlockSpec((t_new, width), lambda b, t: (b, 0))
    win_spec = pl.BlockSpec((None, DIL_DEC_TILE, width), lambda b, t: (b, t, 0))
    old_tab = pl.BlockSpec((rows, DIL_DEC_TILE), lambda b, t: (0, t))
    new_tab = pl.BlockSpec((rows, LANES), lambda b, t: (0, 0))
    return pl.pallas_call(
        functools.partial(_dilated_decode_kernel, n_heads=n_heads, t_new=t_new),
        grid=(n_seq, n_prev // DIL_DEC_TILE),
        in_specs=[tok_spec, tok_spec, tok_spec, win_spec, win_spec, old_tab, old_tab, new_tab, new_tab],
        out_specs=tok_spec,
        out_shape=jax.ShapeDtypeStruct((n, width), F32),
        scratch_shapes=[pltpu.VMEM((rows, width), BF16), pltpu.VMEM((rows, 1), F32),
                        pltpu.VMEM((rows, 1), F32), pltpu.VMEM((rows, width), F32)],
        compiler_params=_cparams("parallel", "arbitrary"),
        name="dilated_decode",
    )(q, k_new, v_new, win_k, win_v, jnp.asarray(bo), jnp.asarray(mo), jnp.asarray(bn), jnp.asarray(mn))


_AB_SEGS = (("a", 0, 0, 1.0), ("a", 1, 0, 1.0), (None, 2, 0, 1.0), ("b", 3, 0, 1.0),
            ("b", 4, 0, ATT_SCALE), (None, 5, 0, 1.0), (None, 6, 0, 1.0))
_C_SEGS = tuple(("a" if o < 2 else None, o, c, 1.0) for o in range(3) for c in (0, SEG))


def _row_tile(n):
    return 512 if n % 512 == 0 else n


def kernel(x_prompt, x_sample, cache_k_a, cache_v_a, page_table, state_ret, cache_win_k, cache_win_v,
           norm_mix, norm_ffn, norm_final, w_in_ab, w_out_ab, ret_gn_w, w_in_c, w_out_c,
           ffn_w_gate, ffn_w_up, ffn_w_down):
    bp, tp, d = x_prompt.shape
    bs, ts, _ = x_sample.shape
    n_pool, page = cache_k_a.shape[1], cache_k_a.shape[2]
    past_len = page_table.shape[1] * page
    h_a = cache_k_a.shape[3]
    wa = h_a * HEAD_DIM
    wb = w_in_ab.shape[2] - 3 * wa
    assert wb == 4 * wa and w_in_ab.shape[0] == 1 and w_in_c.shape[0] == 1 and norm_mix.shape[0] == 2
    h_b = wa // HEAD_DIM
    h_c = w_in_c.shape[2] // (3 * HEAD_DIM)

    pos_p = jnp.arange(tp, dtype=jnp.int32)
    pos_s = past_len + jnp.arange(ts, dtype=jnp.int32)
    tile_s = lambda tabs: tuple(jnp.tile(t, (bs, 1)) for t in tabs)
    tabs = {
        "p": (_rope_tables(pos_p, ROT_DIM, ROPE_THETA), _rope_tables(pos_p, HEAD_DIM, RET_THETA)),
        "s": (tile_s(_rope_tables(pos_s, ROT_DIM, ROPE_THETA)), tile_s(_rope_tables(pos_s, HEAD_DIM, RET_THETA))),
    }
    bf = lambda w: w.astype(BF16)
    w_in_ab_bf, w_out_ab_bf, w_in_c_bf, w_out_c_bf = bf(w_in_ab[0]), bf(w_out_ab[0]), bf(w_in_c[0]), bf(w_out_c[0])
    wg, wu, wd = bf(ffn_w_gate), bf(ffn_w_up), bf(ffn_w_down)

    xp = x_prompt.reshape(bp * tp, d)
    xs = x_sample.reshape(bs * ts, d)
    tm_p, tm_s = _row_tile(bp * tp), _row_tile(bs * ts)
    assert tp % tm_p == 0

    ab_widths = (wa,) * 7
    qa_p, ka_p, va_p, qb_p, kb_p, vb_p, gb_p = _project(
        xp, norm_mix[0], w_in_ab_bf, *tabs["p"], _AB_SEGS, ab_widths, tm_p)
    qa_s, ka_s, va_s, qb_s, kb_s, vb_s, gb_s = _project(
        xs, norm_mix[0], w_in_ab_bf, *tabs["s"], _AB_SEGS, ab_widths, tm_s)

    oa_p = _moba_prompt(qa_p, ka_p, va_p, bp, tp)
    oa_s = _moba_decode(qa_s, ka_s, va_s, cache_k_a[0].reshape(n_pool, page, wa),
                        cache_v_a[0].reshape(n_pool, page, wa), page_table, ts)

    chunk_p = math.gcd(tp, RET_CHUNK)
    ob_p, ret_p = _retention(qb_p, kb_p, vb_p, gb_p, ret_gn_w[0],
                             jnp.zeros((bp, h_b, HEAD_DIM, HEAD_DIM), F32), bp, tp, chunk_p,
                             math.gcd(tp // chunk_p, 8))
    ob_s, ret_s = _retention(qb_s, kb_s, vb_s, gb_s, ret_gn_w[0], state_ret[0], bs, ts,
                             math.gcd(ts, RET_CHUNK), 1)

    xp = _outproj([oa_p, ob_p], w_out_ab_bf, xp, tm_p)
    xs = _outproj([oa_s, ob_s], w_out_ab_bf, xs, tm_s)
    xp = _ffn(xp, norm_ffn[0], wg[0], wu[0], wd[0], norm_final, False, tm_p)
    xs = _ffn(xs, norm_ffn[0], wg[0], wu[0], wd[0], norm_final, False, tm_s)

    wc = h_c * HEAD_DIM
    q_p, k_p, v_p = _project(xp, norm_mix[1], w_in_c_bf, *tabs["p"], _C_SEGS, (wc,) * 3, tm_p)
    q_s, k_s, v_s = _project(xs, norm_mix[1], w_in_c_bf, *tabs["s"], _C_SEGS, (wc,) * 3, tm_s)
    n_prev = cache_win_k.shape[2]
    o_p = _dilated_prompt(q_p, k_p, v_p, bp, tp)
    o_s = _dilated_decode(q_s, k_s, v_s, cache_win_k[0].reshape(bs, n_prev, wc),
                          cache_win_v[0].reshape(bs, n_prev, wc), ts)
    xp = _outproj([o_p], w_out_c_bf, xp, tm_p)
    xs = _outproj([o_s], w_out_c_bf, xs, tm_s)
    y_p = _ffn(xp, norm_ffn[1], wg[1], wu[1], wd[1], norm_final, True, tm_p)
    y_s = _ffn(xs, norm_ffn[1], wg[1], wu[1], wd[1], norm_final, True, tm_s)

    keep_p = min(W_MAX, tp)
    keep_s = min(W_MAX, n_prev + ts)
    win = lambda old, new: jnp.concatenate([old, new.reshape(bs, ts, h_c, HEAD_DIM)], axis=1)[:, -keep_s:][None]
    return (
        y_p.reshape(bp, tp, d), y_s.reshape(bs, ts, d),
        ka_p.reshape(1, bp, tp, h_a, HEAD_DIM), va_p.reshape(1, bp, tp, h_a, HEAD_DIM), ret_p[None],
        k_p.reshape(bp, tp, h_c, HEAD_DIM)[:, -keep_p:][None], v_p.reshape(bp, tp, h_c, HEAD_DIM)[:, -keep_p:][None],
        ka_s.reshape(1, bs, ts, h_a, HEAD_DIM), va_s.reshape(1, bs, ts, h_a, HEAD_DIM), ret_s[None],
        win(cache_win_k[0], k_s), win(cache_win_v[0], v_s),
    )
```

```python
import functools
import math

import jax
import jax.numpy as jnp
import numpy as np
from jax import lax
from jax.experimental import pallas as pl
from jax.experimental.pallas import tpu as pltpu

F32 = jnp.float32
BF16 = jnp.bfloat16

HEAD_DIM = 64
LANES = 128
ROT_DIM = HEAD_DIM // 4
ROPE_THETA = 500000.0
RET_THETA = 10000.0
MOBA_BLOCK = 256
MOBA_TOPK = 3
RET_CHUNK = 128
DILATED = ((128, 1), (512, 4), (2048, 16))
W_MAX = max(w for w, _ in DILATED)
EPS = 1e-6
NEG_INF = -1e30
ATT_SCALE = HEAD_DIM ** -0.5
VMEM_LIMIT = 56 * 1024 * 1024

_NT = (((1,), (1,)), ((), ()))
_TN = (((0,), (0,)), ((), ()))


def _cparams(*sem):
    return pltpu.CompilerParams(dimension_semantics=sem, vmem_limit_bytes=VMEM_LIMIT)


def _rms(x, w):
    ms = jnp.mean(x * x, axis=-1, keepdims=True)
    return x * lax.rsqrt(ms + EPS) * w


SEG = 512


def _proj_kernel(x_ref, nw_ref, w_ref, ca_ref, na_ref, pa_ref, cb_ref, nb_ref, pb_ref, *out_refs, segs, transposed):
    xn = _rms(x_ref[...], nw_ref[...]).astype(BF16)
    for s, (kind, oi, col, post) in enumerate(segs):
        acc = jnp.dot(xn, w_ref[:, s * SEG:(s + 1) * SEG], preferred_element_type=F32)
        for c in range(SEG // LANES):
            r = acc[:, c * LANES:(c + 1) * LANES]
            if kind is not None:
                c_ref, n_ref, p_ref, shift = (ca_ref, na_ref, pa_ref, ROT_DIM // 2) if kind == "a" else (
                    cb_ref, nb_ref, pb_ref, HEAD_DIM // 2)
                r = (r * c_ref[...] + pltpu.roll(r, LANES - shift, 1) * n_ref[...]
                     + pltpu.roll(r, shift, 1) * p_ref[...])
                if post != 1.0:
                    r = r * post
            lo = col + c * LANES
            if oi in transposed:
                out_refs[oi][lo:lo + LANES, :] = r.T
            else:
                out_refs[oi][:, lo:lo + LANES] = r


def _rope_tables(pos, rot_dim, theta):
    half = rot_dim // 2
    inv = theta ** (-jnp.arange(half, dtype=F32) / half)
    ang = pos.astype(F32)[:, None] * inv[None, :]
    cos, sin = jnp.cos(ang), jnp.sin(ang)
    lane = np.arange(LANES) % HEAD_DIM
    idx = lane % half
    cos_t = jnp.where(lane < rot_dim, cos[:, idx], 1.0)
    neg_t = jnp.where(lane < half, -sin[:, idx], 0.0)
    pos_t = jnp.where((lane >= half) & (lane < rot_dim), sin[:, idx], 0.0)
    return cos_t.astype(F32), neg_t.astype(F32), pos_t.astype(F32)


def _project(x, nw, w_bf, tabs_a, tabs_b, segs, out_widths, tm, transposed=()):
    n, d = x.shape
    t_tab = tabs_a[0].shape[0]
    nt = t_tab // tm
    tab_spec = pl.BlockSpec((tm, LANES), lambda i: (i % nt, 0))
    out_specs, out_shape = [], []
    for oi, w in enumerate(out_widths):
        if oi in transposed:
            out_specs.append(pl.BlockSpec((None, w, tm), lambda i: (i // nt, 0, i % nt)))
            out_shape.append(jax.ShapeDtypeStruct((n // t_tab, w, t_tab), F32))
        else:
            out_specs.append(pl.BlockSpec((tm, w), lambda i: (i, 0)))
            out_shape.append(jax.ShapeDtypeStruct((n, w), F32))
    return pl.pallas_call(
        functools.partial(_proj_kernel, segs=segs, transposed=tuple(transposed)),
        grid=(n // tm,),
        in_specs=[pl.BlockSpec((tm, d), lambda i: (i, 0)),
                  pl.BlockSpec((1, d), lambda i: (0, 0)),
                  pl.BlockSpec(w_bf.shape, lambda i: (0, 0))] + [tab_spec] * 6,
        out_specs=out_specs,
        out_shape=out_shape,
        compiler_params=_cparams("parallel"),
        name="rms_proj_rope",
    )(x, nw.reshape(1, d), w_bf, *tabs_a, *tabs_b)


def _outproj_kernel(*refs, nparts):
    parts, w_ref, res_ref, o_ref = refs[:nparts], refs[nparts], refs[nparts + 1], refs[nparts + 2]
    a = jnp.concatenate([p[...].astype(BF16) for p in parts], axis=1) if nparts > 1 else parts[0][...].astype(BF16)
    o_ref[...] = res_ref[...] + jnp.dot(a, w_ref[...], preferred_element_type=F32)


def _outproj(parts, w_bf, res, tm):
    n, d = res.shape
    return pl.pallas_call(
        functools.partial(_outproj_kernel, nparts=len(parts)),
        grid=(n // tm,),
        in_specs=[pl.BlockSpec((tm, p.shape[1]), lambda i: (i, 0)) for p in parts]
        + [pl.BlockSpec(w_bf.shape, lambda i: (0, 0)), pl.BlockSpec((tm, d), lambda i: (i, 0))],
        out_specs=pl.BlockSpec((tm, d), lambda i: (i, 0)),
        out_shape=jax.ShapeDtypeStruct((n, d), F32),
        compiler_params=_cparams("parallel"),
        name="outproj_residual",
    )(*parts, w_bf, res)


def _ffn_kernel(x_ref, nw_ref, wg_ref, wu_ref, wd_ref, fw_ref, o_ref, *, nchunk, final):
    x = x_ref[...]
    xn = _rms(x, nw_ref[...]).astype(BF16)
    dff = wg_ref.shape[1]
    tf = dff // nchunk
    acc = x
    for c in range(nchunk):
        g = jnp.dot(xn, wg_ref[:, c * tf:(c + 1) * tf], preferred_element_type=F32)
        u = jnp.dot(xn, wu_ref[:, c * tf:(c + 1) * tf], preferred_element_type=F32)
        h = (g * jax.nn.sigmoid(g) * u).astype(BF16)
        acc = acc + jnp.dot(h, wd_ref[c * tf:(c + 1) * tf, :], preferred_element_type=F32)
    if final:
        acc = _rms(acc, fw_ref[...])
    o_ref[...] = acc


def _ffn(x, nw, wg, wu, wd, fw, final, tm):
    n, d = x.shape
    dff = wg.shape[1]
    const = lambda i: (0, 0)
    return pl.pallas_call(
        functools.partial(_ffn_kernel, nchunk=2, final=final),
        grid=(n // tm,),
        in_specs=[pl.BlockSpec((tm, d), lambda i: (i, 0)), pl.BlockSpec((1, d), const),
                  pl.BlockSpec((d, dff), const), pl.BlockSpec((d, dff), const), pl.BlockSpec((dff, d), const),
                  pl.BlockSpec((1, d), const)],
        out_specs=pl.BlockSpec((tm, d), lambda i: (i, 0)),
        out_shape=jax.ShapeDtypeStruct((n, d), F32),
        compiler_params=_cparams("parallel"),
        name="rms_swiglu_residual",
    )(x, nw.reshape(1, d), wg, wu, wd, fw.reshape(1, d))


ATT_TILE = 256
ATT_GROUP = 4


def _pv(vts, p):
    pb = p.astype(BF16)
    nk = vts[0].shape[1]
    return sum(jnp.dot(vt, pb[i * nk:(i + 1) * nk], preferred_element_type=F32) for i, vt in enumerate(vts))


def _online_step(carry, s, vts, mult=None):
    m, l, acc = carry
    m_new = jnp.maximum(m, jnp.max(s, axis=0, keepdims=True))
    alpha = jnp.exp(m - m_new)
    p = jnp.exp(s - m_new)
    if mult is not None:
        p = p * mult
    return m_new, alpha * l + jnp.sum(p, axis=0, keepdims=True), alpha * acc + _pv(vts, p)


def _first_step(s, vts, mult=None):
    m = jnp.max(s, axis=0, keepdims=True)
    p = jnp.exp(s - m)
    if mult is not None:
        p = p * mult
    return m, jnp.sum(p, axis=0, keepdims=True), _pv(vts, p)


def _head_queries(q_t, feat):
    return [jnp.where(feat // HEAD_DIM == hh, q_t, 0.0) for hh in range(2)]


def _scaled_bf16(q):
    return (q * ATT_SCALE).astype(BF16)


def _moba_prompt_kernel(q_ref, kt_ref, vt_ref, o_ref, kb_ref, vb_ref, kmean_ref, bias_ref, *, nblk):
    blk, grp = ATT_TILE, ATT_GROUP
    qt = pl.program_id(2)

    @pl.when(qt == 0)
    def _():
        for n in range(nblk):
            g, j = divmod(n, grp)
            kblk = kt_ref[:, n * blk:(n + 1) * blk].T
            kmean_ref[n:n + 1, :] = jnp.sum(kblk, axis=0, keepdims=True) * (1.0 / blk)
            kb_ref[g, j * blk:(j + 1) * blk, :] = kblk.astype(BF16)
            vb_ref[n] = vt_ref[:, n * blk:(n + 1) * blk].astype(BF16)

    feat = lax.broadcasted_iota(jnp.int32, (LANES, blk), 0)
    blk_id = lax.broadcasted_iota(jnp.int32, (nblk, blk), 0)
    blk_f = blk_id.astype(F32)
    q_heads = _head_queries(q_ref[...].T, feat)
    for hh, q_h in enumerate(q_heads):
        sc = jnp.dot(kmean_ref[...], q_h, preferred_element_type=F32, precision=lax.Precision.HIGHEST)
        sc = jnp.where(blk_id < qt, sc, NEG_INF)
        sel = jnp.zeros((nblk, blk), F32)
        for r in range(MOBA_TOPK):
            mx = jnp.max(sc, axis=0, keepdims=True)
            first = jnp.min(jnp.where(sc == mx, blk_f, float(nblk)), axis=0, keepdims=True)
            pick = blk_f == first
            sel = jnp.maximum(sel, jnp.where(pick, jnp.where(r < qt, 1.0, 0.0), 0.0))
            sc = jnp.where(pick, -jnp.inf, sc)
        bias_ref[hh] = jnp.where((sel > 0.0) | (blk_id == qt), 0.0, NEG_INF)
    qbs = [_scaled_bf16(q_h) for q_h in q_heads]

    def scores(g, hh):
        s = jnp.dot(kb_ref[g], qbs[hh], preferred_element_type=F32)
        return jnp.concatenate([s[j * blk:(j + 1) * blk] + bias_ref[hh, pl.ds(g * grp + j, 1), :]
                                for j in range(grp)], axis=0)

    def values(g):
        return [vb_ref[g * grp + j] for j in range(grp)]

    g_own = qt // grp
    key_i = lax.broadcasted_iota(jnp.int32, (grp * blk, blk), 0)
    qry_i = lax.broadcasted_iota(jnp.int32, (grp * blk, blk), 1)
    causal = jnp.where((key_i // blk != qt % grp) | (key_i % blk <= qry_i), 0.0, NEG_INF)
    carry = ()
    for hh in range(2):
        carry += _first_step(scores(g_own, hh) + causal, values(g_own))

    def body(g, carry):
        out = ()
        for hh in range(2):
            out += _online_step(carry[3 * hh:3 * hh + 3], scores(g, hh), values(g))
        return out

    carry = lax.fori_loop(0, g_own, body, carry)
    o_t = jnp.where(feat // HEAD_DIM == 0, carry[2] / carry[1], carry[5] / carry[4])
    o_ref[...] = o_t.T


def _moba_prompt(q, kt, vt, batch, seq):
    n, width = q.shape
    blk, grp = ATT_TILE, ATT_GROUP
    assert blk == MOBA_BLOCK and seq % (blk * grp) == 0
    nblk = seq // blk
    npair = width // LANES
    kv_spec = pl.BlockSpec((None, LANES, seq), lambda b, p, t: (b, p, 0))
    q_spec = pl.BlockSpec((blk, LANES), lambda b, p, t: (b * nblk + t, p))
    return pl.pallas_call(
        functools.partial(_moba_prompt_kernel, nblk=nblk),
        grid=(batch, npair, nblk),
        in_specs=[q_spec, kv_spec, kv_spec],
        out_specs=q_spec,
        out_shape=jax.ShapeDtypeStruct((n, width), F32),
        scratch_shapes=[pltpu.VMEM((nblk // grp, grp * blk, LANES), BF16), pltpu.VMEM((nblk, LANES, blk), BF16),
                        pltpu.VMEM((nblk, LANES), F32), pltpu.VMEM((2, nblk, blk), F32)],
        compiler_params=_cparams("parallel", "parallel", "arbitrary"),
        name="moba_prompt",
    )(q, kt, vt)


def _dilated_multiplicity(delta):
    c = np.zeros(delta.shape, np.int32)
    for window, dil in DILATED:
        c += ((delta >= 0) & (delta <= window) & (delta % dil == 0)).astype(np.int32)
    return c


DIL_NBACK = W_MAX // ATT_TILE


def _dilated_prompt_tables():
    ki = np.arange(ATT_TILE)[:, None]
    qi = np.arange(ATT_TILE)[None, :]
    tile = lambda j: _dilated_multiplicity(ATT_TILE * j + qi - ki)
    f = lambda c: (np.where(c > 0, 0.0, NEG_INF).astype(np.float32), c.astype(np.float32))
    groups = [np.concatenate([tile(DIL_NBACK - g * ATT_GROUP - i) for i in range(ATT_GROUP)], axis=0)
              for g in range(DIL_NBACK // ATT_GROUP)]
    bias_g, mult_g = f(np.stack(groups))
    return f(tile(0)) + (bias_g, mult_g)


def _dilated_prompt_kernel(q_ref, kt_ref, vt_ref, b0_ref, m0_ref, bg_ref, mg_ref, o_ref,
                           kb_ref, vb_ref, tab_ref, *, nblk):
    blk, grp, nback = ATT_TILE, ATT_GROUP, DIL_NBACK
    qt = pl.program_id(2)

    @pl.when(qt == 0)
    def _():
        kb_ref[0:nback * blk, :] = jnp.zeros((nback * blk, LANES), BF16)
        for n in range(nback):
            vb_ref[n] = jnp.zeros((LANES, blk), BF16)
        for n in range(nblk):
            kb_ref[(n + nback) * blk:(n + nback + 1) * blk, :] = kt_ref[:, n * blk:(n + 1) * blk].T.astype(BF16)
            vb_ref[n + nback] = vt_ref[:, n * blk:(n + 1) * blk].astype(BF16)

    @pl.when(qt <= nback)
    def _():
        row = lax.broadcasted_iota(jnp.int32, (grp * blk, blk), 0)
        for gi in range(nback // grp):
            first_key = (qt - nback + gi * grp) * blk
            tab_ref[gi] = jnp.where(first_key + row >= 0, bg_ref[gi], NEG_INF)

    feat = lax.broadcasted_iota(jnp.int32, (LANES, blk), 0)
    qbs = [_scaled_bf16(q_h) for q_h in _head_queries(q_ref[...].T, feat)]
    outs = []
    for hh in range(2):
        own = pl.multiple_of((qt + nback) * blk, blk)
        s = jnp.dot(kb_ref[pl.ds(own, blk), :], qbs[hh], preferred_element_type=F32) + b0_ref[...]
        carry = _first_step(s, [vb_ref[qt + nback]], m0_ref[...])
        for gi in range(nback // grp):
            start = pl.multiple_of((qt + gi * grp) * blk, blk)
            s = jnp.dot(kb_ref[pl.ds(start, grp * blk), :], qbs[hh], preferred_element_type=F32) + tab_ref[gi]
            carry = _online_step(carry, s, [vb_ref[qt + gi * grp + j] for j in range(grp)], mg_ref[gi])
        outs.append(carry[2] / carry[1])
    o_t = jnp.where(feat // HEAD_DIM == 0, outs[0], outs[1])
    o_ref[...] = o_t.T


def _dilated_prompt(q, kt, vt, batch, seq):
    n, width = q.shape
    blk, grp, nback = ATT_TILE, ATT_GROUP, DIL_NBACK
    assert seq % blk == 0 and nback % grp == 0
    nblk = seq // blk
    npair = width // LANES
    tabs = [jnp.asarray(t) for t in _dilated_prompt_tables()]
    kv_spec = pl.BlockSpec((None, LANES, seq), lambda b, p, t: (b, p, 0))
    q_spec = pl.BlockSpec((blk, LANES), lambda b, p, t: (b * nblk + t, p))
    tab_specs = [pl.BlockSpec(t.shape, functools.partial(lambda b, p, t, nd: (0,) * nd, nd=t.ndim)) for t in tabs]
    return pl.pallas_call(
        functools.partial(_dilated_prompt_kernel, nblk=nblk),
        grid=(batch, npair, nblk),
        in_specs=[q_spec, kv_spec, kv_spec] + tab_specs,
        out_specs=q_spec,
        out_shape=jax.ShapeDtypeStruct((n, width), F32),
        scratch_shapes=[pltpu.VMEM(((nblk + nback) * blk, LANES), BF16), pltpu.VMEM((nblk + nback, LANES, blk), BF16),
                        pltpu.VMEM((nback // grp, grp * blk, blk), F32)],
        compiler_params=_cparams("parallel", "parallel", "arbitrary"),
        name="dilated_prompt",
    )(q, kt, vt, *tabs)


RET_ROWS = 128


def _retention_tables(n_heads, chunk):
    log_g = jnp.log1p(-jnp.exp2(-5.0 - jnp.arange(n_heads, dtype=F32)))
    i = jnp.arange(chunk, dtype=F32)
    diff = i[:, None] - i[None, :]
    causal = diff >= 0
    decay = jnp.where(causal[None], jnp.exp(jnp.where(causal, diff, 0.0)[None] * log_g[:, None, None]), 0.0)
    q_dec = jnp.exp((i + 1.0)[:, None] * log_g[None, :])
    k_dec = jnp.exp((chunk - 1.0 - i)[:, None] * log_g[None, :])
    c_dec = jnp.exp(chunk * log_g)
    pad = RET_ROWS - chunk
    npair = n_heads // 2
    decay = jnp.pad(decay, ((0, 0), (0, pad), (0, pad))).reshape(npair, 2 * RET_ROWS, RET_ROWS)
    expand = lambda t: jnp.pad(jnp.repeat(t, HEAD_DIM, axis=1), ((0, pad), (0, 0))).reshape(
        RET_ROWS, npair, LANES).transpose(1, 0, 2)
    q_tab, k_tab = expand(q_dec), expand(k_dec)
    same_head = (np.arange(LANES)[:, None] // HEAD_DIM) == (np.arange(LANES)[None, :] // HEAD_DIM)
    c_tab = jnp.where(same_head[None], jnp.repeat(c_dec, HEAD_DIM).reshape(npair, LANES, 1), 0.0)
    return decay.astype(F32), q_tab.astype(F32), k_tab.astype(F32), c_tab.astype(F32)


def _retention_kernel(q_ref, k_ref, v_ref, g_ref, gn_ref, s0_ref, dec_ref, qd_ref, kd_ref, cd_ref,
                      o_ref, sout_ref, st_ref, *, chunk, nchunk):
    c = pl.program_id(2)

    @pl.when(c == 0)
    def _():
        st_ref[...] = s0_ref[...]

    pad = RET_ROWS - chunk
    lane = lax.broadcasted_iota(jnp.int32, (RET_ROWS, LANES), 1)
    head0 = lane < HEAD_DIM
    same_head = (lax.broadcasted_iota(jnp.int32, (LANES, LANES), 0) // HEAD_DIM) == (
        lax.broadcasted_iota(jnp.int32, (LANES, LANES), 1) // HEAD_DIM)

    def rows(ref, r):
        x = ref[pl.ds(r, chunk), :]
        return x if pad == 0 else jnp.concatenate([x, jnp.zeros((pad, LANES), F32)], axis=0)

    def seg_mean(x):
        s0 = jnp.sum(jnp.where(head0, x, 0.0), axis=-1, keepdims=True)
        s1 = jnp.sum(jnp.where(head0, 0.0, x), axis=-1, keepdims=True)
        return jnp.where(head0, s0, s1) * (1.0 / HEAD_DIM)

    def one_chunk(j, carry):
        r = pl.multiple_of(j * chunk, chunk)
        q, k, v, g = rows(q_ref, r), rows(k_ref, r), rows(v_ref, r), rows(g_ref, r)
        state = st_ref[...]
        kb, vb = k.astype(BF16), v.astype(BF16)
        q2 = jnp.concatenate([jnp.where(head0, q, 0.0), jnp.where(head0, 0.0, q)], axis=0).astype(BF16)
        attn = lax.dot_general(q2, kb, _NT, preferred_element_type=F32) * dec_ref[...]
        inn = jnp.dot(attn.astype(BF16), vb, preferred_element_type=F32)
        inner = jnp.where(head0, inn[:RET_ROWS], inn[RET_ROWS:])
        cross = jnp.dot(q.astype(BF16), state.astype(BF16), preferred_element_type=F32) * qd_ref[...]
        upd = lax.dot_general((k * kd_ref[...]).astype(BF16), vb, _TN, preferred_element_type=F32)
        st_ref[...] = state * cd_ref[...] + jnp.where(same_head, upd, 0.0)
        o = inner + cross
        mu = seg_mean(o)
        var = seg_mean(jnp.square(o - mu))
        y = (o - mu) * lax.rsqrt(var + EPS) * gn_ref[...]
        y = y * (g * jax.nn.sigmoid(g))
        o_ref[pl.ds(r, chunk), :] = y[:chunk]
        return carry

    lax.fori_loop(0, nchunk, one_chunk, 0)

    @pl.when(c == pl.num_programs(2) - 1)
    def _():
        sout_ref[...] = st_ref[...]


def _retention(q, k, v, g, gn_w, s0, batch, seq, chunk, nchunk):
    n, width = q.shape
    n_heads = width // HEAD_DIM
    npair = width // LANES
    rows = chunk * nchunk
    steps = seq // rows
    dec, qd, kd, cd = _retention_tables(n_heads, chunk)
    eye2 = jnp.eye(2, dtype=F32)
    s0p = s0.reshape(batch, npair, 2, HEAD_DIM, HEAD_DIM)
    s0p = jnp.einsum('bpide,ij->bpidje', s0p, eye2).reshape(batch, npair, LANES, LANES)
    row_spec = pl.BlockSpec((rows, LANES), lambda b, p, c: (b * steps + c, p))
    st_spec = pl.BlockSpec((None, None, LANES, LANES), lambda b, p, c: (b, p, 0, 0))
    tab = lambda t: pl.BlockSpec((None,) + t.shape[1:], lambda b, p, c: (p, 0, 0))
    out, s_new = pl.pallas_call(
        functools.partial(_retention_kernel, chunk=chunk, nchunk=nchunk),
        grid=(batch, npair, steps),
        in_specs=[row_spec, row_spec, row_spec, row_spec, pl.BlockSpec((1, LANES), lambda b, p, c: (0, p)),
                  st_spec, tab(dec), tab(qd), tab(kd), tab(cd)],
        out_specs=[row_spec, st_spec],
        out_shape=[jax.ShapeDtypeStruct((n, width), F32),
                   jax.ShapeDtypeStruct((batch, npair, LANES, LANES), F32)],
        scratch_shapes=[pltpu.VMEM((LANES, LANES), F32)],
        compiler_params=_cparams("parallel", "parallel", "arbitrary"),
        name="retention",
    )(q, k, v, g, gn_w.reshape(1, width), s0p, dec, qd, kd, cd)
    s_new = s_new.reshape(batch, npair, 2, HEAD_DIM, 2, HEAD_DIM)
    s_new = jnp.stack([s_new[:, :, 0, :, 0, :], s_new[:, :, 1, :, 1, :]], axis=2)
    return out, s_new.reshape(batch, n_heads, HEAD_DIM, HEAD_DIM)


def _block_diag_queries(q, n_heads):
    t = q.shape[0]
    rep = jnp.concatenate([q] * n_heads, axis=0)
    rows = lax.broadcasted_iota(jnp.int32, rep.shape, 0) // t
    cols = lax.broadcasted_iota(jnp.int32, rep.shape, 1) // HEAD_DIM
    return jnp.where(rows == cols, rep, 0.0)


def _take_diag(o, n_heads, t):
    cols = lax.broadcasted_iota(jnp.int32, (t, o.shape[1]), 1) // HEAD_DIM
    out = jnp.zeros((t, o.shape[1]), F32)
    for h in range(n_heads):
        out = jnp.where(cols == h, o[h * t:(h + 1) * t, :], out)
    return out


def _pad_rows(x, rows):
    return jnp.concatenate([x, jnp.zeros((rows - x.shape[0], x.shape[1]), x.dtype)], axis=0)


MOBA_PAGES_PER_STEP = 16


def _moba_decode_kernel(pt_ref, q_ref, kn_ref, vn_ref, *rest, n_heads, t_new, pages_per_blk, nblk):
    npg = MOBA_PAGES_PER_STEP
    kp, vp = rest[:npg], rest[npg:2 * npg]
    o_ref, qbd_ref, qft_ref, m_ref, l_ref, sc_ref, oblk_ref = rest[2 * npg:]
    g = pl.program_id(1)
    rows = n_heads * t_new
    lane = lax.broadcasted_iota(jnp.int32, (rows, LANES), 1)
    page = kp[0].shape[1]
    blk_keys = pages_per_blk * page

    @pl.when(g == 0)
    def _():
        qf = _block_diag_queries(q_ref[...], n_heads)
        qbd_ref[...] = _scaled_bf16(qf)
        qft_ref[...] = _pad_rows(qf, LANES).T
        m_ref[...] = jnp.zeros_like(m_ref)
        l_ref[...] = jnp.zeros_like(l_ref)
        sc_ref[...] = jnp.full_like(sc_ref, -jnp.inf)

    qbd = qbd_ref[...]
    for jb in range(npg // pages_per_blk):
        n = g * (npg // pages_per_blk) + jb
        pages = range(jb * pages_per_blk, (jb + 1) * pages_per_blk)
        kts = [kp[i][...] for i in pages]
        s = jnp.concatenate([jnp.dot(qbd, kt.astype(BF16), preferred_element_type=F32) for kt in kts], axis=1)
        mb = jnp.max(s, axis=-1, keepdims=True)
        p = jnp.exp(s - mb)
        oblk_ref[n] = sum(lax.dot_general(p[:, a * page:(a + 1) * page].astype(BF16), vp[i][...].astype(BF16),
                                          _NT, preferred_element_type=F32) for a, i in enumerate(pages))
        m_ref[...] = jnp.where(lane == n, mb, m_ref[...])
        l_ref[...] = jnp.where(lane == n, jnp.sum(p, axis=-1, keepdims=True), l_ref[...])
        kmean = jnp.sum(sum(kts), axis=-1, keepdims=True) * (1.0 / blk_keys)
        sc_ref[pl.ds(n, 1), :] = jnp.sum(qft_ref[...] * kmean, axis=0, keepdims=True)

    @pl.when(g == pl.num_programs(1) - 1)
    def _():
        s = lax.dot_general(qbd, _pad_rows(kn_ref[...], LANES).astype(BF16), _NT, preferred_element_type=F32)
        qry = lax.broadcasted_iota(jnp.int32, (rows, LANES), 0) % t_new
        s = jnp.where(lane <= qry, s, NEG_INF)
        m_own = jnp.max(s, axis=-1, keepdims=True)
        p = jnp.exp(s - m_own)
        l_own = jnp.sum(p, axis=-1, keepdims=True)
        o_own = jnp.dot(p.astype(BF16), _pad_rows(vn_ref[...], LANES).astype(BF16), preferred_element_type=F32)
        sc = sc_ref[...].T[:rows]
        lane_f = lane.astype(F32)
        sel = lane < 0
        for _ in range(MOBA_TOPK):
            mx = jnp.max(sc, axis=-1, keepdims=True)
            first = jnp.min(jnp.where(sc == mx, lane_f, float(LANES)), axis=-1, keepdims=True)
            pick = lane_f == first
            sel = sel | pick
            sc = jnp.where(pick, -jnp.inf, sc)
        m_all = m_ref[...]
        m_fin = jnp.maximum(jnp.max(jnp.where(sel, m_all, -jnp.inf), axis=-1, keepdims=True), m_own)
        w = jnp.where(sel, jnp.exp(m_all - m_fin), 0.0)
        w_own = jnp.exp(m_own - m_fin)
        l_fin = jnp.sum(w * l_ref[...], axis=-1, keepdims=True) + w_own * l_own

        def body(n, acc):
            col = jnp.sum(jnp.where(lane == n, w, 0.0), axis=-1, keepdims=True)
            return acc + col * oblk_ref[n]

        acc = lax.fori_loop(0, nblk, body, w_own * o_own)
        o_ref[...] = _take_diag(acc / l_fin, n_heads, t_new)


def _moba_decode(q, k_new, v_new, cache_kt, cache_vt, page_table, t_new):
    n, width = q.shape
    n_heads = width // HEAD_DIM
    n_seq, n_pages = page_table.shape
    page = cache_kt.shape[2]
    pages_per_blk = MOBA_BLOCK // page
    nblk = n_pages // pages_per_blk
    npg = MOBA_PAGES_PER_STEP
    rows = n_heads * t_new
    assert n_pages % npg == 0 and npg % pages_per_blk == 0 and MOBA_TOPK <= nblk <= LANES
    assert n_pages * page == nblk * MOBA_BLOCK and t_new <= page and page == LANES and rows <= LANES
    tok_spec = pl.BlockSpec((t_new, width), lambda b, g, pt: (b, 0))
    page_specs = [pl.BlockSpec((None, width, page), functools.partial(
        lambda b, g, pt, i: (pt[b, g * npg + i], 0, 0), i=i)) for i in range(npg)]
    grid_spec = pltpu.PrefetchScalarGridSpec(
        num_scalar_prefetch=1,
        grid=(n_seq, n_pages // npg),
        in_specs=[tok_spec, tok_spec, tok_spec] + page_specs + page_specs,
        out_specs=tok_spec,
        scratch_shapes=[pltpu.VMEM((rows, width), BF16), pltpu.VMEM((width, LANES), F32),
                        pltpu.VMEM((rows, LANES), F32), pltpu.VMEM((rows, LANES), F32),
                        pltpu.VMEM((LANES, LANES), F32), pltpu.VMEM((nblk, rows, width), F32)],
    )
    return pl.pallas_call(
        functools.partial(_moba_decode_kernel, n_heads=n_heads, t_new=t_new,
                          pages_per_blk=pages_per_blk, nblk=nblk),
        grid_spec=grid_spec,
        out_shape=jax.ShapeDtypeStruct((n, width), F32),
        compiler_params=_cparams("parallel", "arbitrary"),
        name="moba_decode",
    )(page_table, q, k_new, v_new, *([cache_kt] * npg), *([cache_vt] * npg))


DIL_DEC_TILE = 512


def _dilated_decode_tables(n_heads, t_new, n_prev):
    qi = (np.arange(n_heads * t_new) % t_new)[:, None]
    c_old = _dilated_multiplicity(n_prev + qi - np.arange(n_prev)[None, :])
    c_new = _dilated_multiplicity(qi - np.arange(LANES)[None, :])
    c_new = np.where(np.arange(LANES)[None, :] < t_new, c_new, 0)
    f = lambda c: (np.where(c > 0, 0.0, NEG_INF).astype(np.float32), c.astype(np.float32))
    return f(c_old) + f(c_new)


def _dilated_decode_kernel(q_ref, kn_ref, vn_ref, kc_ref, vc_ref, bo_ref, mo_ref, bn_ref, mn_ref,
                           o_ref, qbd_ref, m_ref, l_ref, acc_ref, *, n_heads, t_new):
    kt = pl.program_id(1)

    @pl.when(kt == 0)
    def _():
        qbd_ref[...] = _scaled_bf16(_block_diag_queries(q_ref[...], n_heads))
        m_ref[...] = jnp.full_like(m_ref, -jnp.inf)
        l_ref[...] = jnp.zeros_like(l_ref)
        acc_ref[...] = jnp.zeros_like(acc_ref)

    def step(s, pv, bias, mult):
        s = s + bias
        m_old = m_ref[...]
        m_new = jnp.maximum(m_old, jnp.max(s, axis=-1, keepdims=True))
        alpha = jnp.exp(m_old - m_new)
        p = jnp.exp(s - m_new) * mult
        l_ref[...] = alpha * l_ref[...] + jnp.sum(p, axis=-1, keepdims=True)
        acc_ref[...] = alpha * acc_ref[...] + pv(p.astype(BF16))
        m_ref[...] = m_new

    @pl.when(kt == 0)
    def _():
        k, v = _pad_rows(kn_ref[...], LANES).astype(BF16), _pad_rows(vn_ref[...], LANES).astype(BF16)
        step(lax.dot_general(qbd_ref[...], k, _NT, preferred_element_type=F32),
             lambda p: jnp.dot(p, v, preferred_element_type=F32), bn_ref[...], mn_ref[...])

    step(jnp.dot(qbd_ref[...], kc_ref[...].astype(BF16), preferred_element_type=F32),
         lambda p: lax.dot_general(p, vc_ref[...].astype(BF16), _NT, preferred_element_type=F32),
         bo_ref[...], mo_ref[...])

    @pl.when(kt == pl.num_programs(1) - 1)
    def _():
        o_ref[...] = _take_diag(acc_ref[...] / l_ref[...], n_heads, t_new)


def _dilated_decode(q, k_new, v_new, win_kt, win_vt, t_new):
    n, width = q.shape
    n_heads = width // HEAD_DIM
    n_seq, _, n_prev = win_kt.shape
    assert n_prev == W_MAX and n_prev % DIL_DEC_TILE == 0
    rows = n_heads * t_new
    bo, mo, bn, mn = _dilated_decode_tables(n_heads, t_new, n_prev)
    tok_spec = pl.BlockSpec((t_new, width), lambda b, t: (b, 0))
    win_spec = pl.BlockSpec((None, width, DIL_DEC_TILE), lambda b, t: (b, 0, t))
    old_tab = pl.BlockSpec((rows, DIL_DEC_TILE), lambda b, t: (0, t))
    new_tab = pl.BlockSpec((rows, LANES), lambda b, t: (0, 0))
    return pl.pallas_call(
        functools.partial(_dilated_decode_kernel, n_heads=n_heads, t_new=t_new),
        grid=(n_seq, n_prev // DIL_DEC_TILE),
        in_specs=[tok_spec, tok_spec, tok_spec, win_spec, win_spec, old_tab, old_tab, new_tab, new_tab],
        out_specs=tok_spec,
        out_shape=jax.ShapeDtypeStruct((n, width), F32),
        scratch_shapes=[pltpu.VMEM((rows, width), BF16), pltpu.VMEM((rows, 1), F32),
                        pltpu.VMEM((rows, 1), F32), pltpu.VMEM((rows, width), F32)],
        compiler_params=_cparams("parallel", "arbitrary"),
        name="dilated_decode",
    )(q, k_new, v_new, win_kt, win_vt, jnp.asarray(bo), jnp.asarray(mo), jnp.asarray(bn), jnp.asarray(mn))


_AB_SEGS = (("a", 0, 0, 1.0), ("a", 1, 0, 1.0), (None, 2, 0, 1.0), ("b", 3, 0, 1.0),
            ("b", 4, 0, ATT_SCALE), (None, 5, 0, 1.0), (None, 6, 0, 1.0))
_C_SEGS = tuple(("a" if o < 2 else None, o, c, 1.0) for o in range(3) for c in (0, SEG))
_KV_OUT = (1, 2)


def _row_tile(n):
    return 512 if n % 512 == 0 else n


def _feature_major(x):
    lead = x.shape[:-3]
    t, h, dh = x.shape[-3:]
    nl = len(lead)
    return x.transpose(*range(nl), nl + 1, nl + 2, nl).reshape(*lead, h * dh, t)


def _token_major(xt, n_heads):
    lead = xt.shape[:-2]
    t = xt.shape[-1]
    nl = len(lead)
    return xt.reshape(*lead, n_heads, HEAD_DIM, t).transpose(*range(nl), nl + 2, nl, nl + 1)


def kernel(x_prompt, x_sample, cache_k_a, cache_v_a, page_table, state_ret, cache_win_k, cache_win_v,
           norm_mix, norm_ffn, norm_final, w_in_ab, w_out_ab, ret_gn_w, w_in_c, w_out_c,
           ffn_w_gate, ffn_w_up, ffn_w_down):
    bp, tp, d = x_prompt.shape
    bs, ts, _ = x_sample.shape
    page = cache_k_a.shape[2]
    past_len = page_table.shape[1] * page
    h_a = cache_k_a.shape[3]
    wa = h_a * HEAD_DIM
    wb = w_in_ab.shape[2] - 3 * wa
    assert wb == 4 * wa and w_in_ab.shape[0] == 1 and w_in_c.shape[0] == 1 and norm_mix.shape[0] == 2
    h_b = wa // HEAD_DIM
    h_c = w_in_c.shape[2] // (3 * HEAD_DIM)

    pos_p = jnp.arange(tp, dtype=jnp.int32)
    pos_s = past_len + jnp.arange(ts, dtype=jnp.int32)
    tile_s = lambda tabs: tuple(jnp.tile(t, (bs, 1)) for t in tabs)
    tabs = {
        "p": (_rope_tables(pos_p, ROT_DIM, ROPE_THETA), _rope_tables(pos_p, HEAD_DIM, RET_THETA)),
        "s": (tile_s(_rope_tables(pos_s, ROT_DIM, ROPE_THETA)), tile_s(_rope_tables(pos_s, HEAD_DIM, RET_THETA))),
    }
    bf = lambda w: w.astype(BF16)
    w_in_ab_bf, w_out_ab_bf, w_in_c_bf, w_out_c_bf = bf(w_in_ab[0]), bf(w_out_ab[0]), bf(w_in_c[0]), bf(w_out_c[0])
    wg, wu, wd = bf(ffn_w_gate), bf(ffn_w_up), bf(ffn_w_down)

    xp = x_prompt.reshape(bp * tp, d)
    xs = x_sample.reshape(bs * ts, d)
    tm_p, tm_s = _row_tile(bp * tp), _row_tile(bs * ts)
    assert tp % tm_p == 0

    ab_widths = (wa,) * 7
    qa_p, kat_p, vat_p, qb_p, kb_p, vb_p, gb_p = _project(
        xp, norm_mix[0], w_in_ab_bf, *tabs["p"], _AB_SEGS, ab_widths, tm_p, _KV_OUT)
    qa_s, ka_s, va_s, qb_s, kb_s, vb_s, gb_s = _project(
        xs, norm_mix[0], w_in_ab_bf, *tabs["s"], _AB_SEGS, ab_widths, tm_s)

    oa_p = _moba_prompt(qa_p, kat_p, vat_p, bp, tp)
    oa_s = _moba_decode(qa_s, ka_s, va_s, _feature_major(cache_k_a[0]), _feature_major(cache_v_a[0]),
                        page_table, ts)

    chunk_p = math.gcd(tp, RET_CHUNK)
    ob_p, ret_p = _retention(qb_p, kb_p, vb_p, gb_p, ret_gn_w[0],
                             jnp.zeros((bp, h_b, HEAD_DIM, HEAD_DIM), F32), bp, tp, chunk_p,
                             math.gcd(tp // chunk_p, 8))
    ob_s, ret_s = _retention(qb_s, kb_s, vb_s, gb_s, ret_gn_w[0], state_ret[0], bs, ts,
                             math.gcd(ts, RET_CHUNK), 1)

    xp = _outproj([oa_p, ob_p], w_out_ab_bf, xp, tm_p)
    xs = _outproj([oa_s, ob_s], w_out_ab_bf, xs, tm_s)
    xp = _ffn(xp, norm_ffn[0], wg[0], wu[0], wd[0], norm_final, False, tm_p)
    xs = _ffn(xs, norm_ffn[0], wg[0], wu[0], wd[0], norm_final, False, tm_s)

    wc = h_c * HEAD_DIM
    q_p, kt_p, vt_p = _project(xp, norm_mix[1], w_in_c_bf, *tabs["p"], _C_SEGS, (wc,) * 3, tm_p, _KV_OUT)
    q_s, k_s, v_s = _project(xs, norm_mix[1], w_in_c_bf, *tabs["s"], _C_SEGS, (wc,) * 3, tm_s)
    n_prev = cache_win_k.shape[2]
    o_p = _dilated_prompt(q_p, kt_p, vt_p, bp, tp)
    o_s = _dilated_decode(q_s, k_s, v_s, _feature_major(cache_win_k[0]), _feature_major(cache_win_v[0]), ts)
    xp = _outproj([o_p], w_out_c_bf, xp, tm_p)
    xs = _outproj([o_s], w_out_c_bf, xs, tm_s)
    y_p = _ffn(xp, norm_ffn[1], wg[1], wu[1], wd[1], norm_final, True, tm_p)
    y_s = _ffn(xs, norm_ffn[1], wg[1], wu[1], wd[1], norm_final, True, tm_s)

    keep_p = min(W_MAX, tp)
    keep_s = min(W_MAX, n_prev + ts)
    win = lambda old, new: jnp.concatenate([old, new.reshape(bs, ts, h_c, HEAD_DIM)], axis=1)[:, -keep_s:][None]
    return (
        y_p.reshape(bp, tp, d), y_s.reshape(bs, ts, d),
        _token_major(kat_p, h_a)[None], _token_major(vat_p, h_a)[None], ret_p[None],
        _token_major(kt_p[:, :, tp - keep_p:], h_c)[None], _token_major(vt_p[:, :, tp - keep_p:], h_c)[None],
        ka_s.reshape(1, bs, ts, h_a, HEAD_DIM), va_s.reshape(1, bs, ts, h_a, HEAD_DIM), ret_s[None],
        win(cache_win_k[0], k_s), win(cache_win_v[0], v_s),
    )
```

```python
import functools
import math

import jax
import jax.numpy as jnp
import numpy as np
from jax import lax
from jax.experimental import pallas as pl
from jax.experimental.pallas import tpu as pltpu

F32 = jnp.float32
BF16 = jnp.bfloat16

HEAD_DIM = 64
LANES = 128
ROT_DIM = HEAD_DIM // 4
ROPE_THETA = 500000.0
RET_THETA = 10000.0
MOBA_BLOCK = 256
MOBA_TOPK = 3
RET_CHUNK = 128
DILATED = ((128, 1), (512, 4), (2048, 16))
W_MAX = max(w for w, _ in DILATED)
EPS = 1e-6
NEG_INF = -1e30
ATT_SCALE = HEAD_DIM ** -0.5
VMEM_LIMIT = 56 * 1024 * 1024

_NT = (((1,), (1,)), ((), ()))
_TN = (((0,), (0,)), ((), ()))


def _cparams(*sem):
    return pltpu.CompilerParams(dimension_semantics=sem, vmem_limit_bytes=VMEM_LIMIT)


def _rms(x, w):
    ms = jnp.mean(x * x, axis=-1, keepdims=True)
    return x * lax.rsqrt(ms + EPS) * w


SEG = 512


def _proj_kernel(x_ref, nw_ref, w_ref, ca_ref, na_ref, pa_ref, cb_ref, nb_ref, pb_ref, *out_refs, segs, transposed):
    xn = _rms(x_ref[...], nw_ref[...]).astype(BF16)
    for s, (kind, oi, col, post) in enumerate(segs):
        acc = jnp.dot(xn, w_ref[:, s * SEG:(s + 1) * SEG], preferred_element_type=F32)
        for c in range(SEG // LANES):
            r = acc[:, c * LANES:(c + 1) * LANES]
            if kind is not None:
                c_ref, n_ref, p_ref, shift = (ca_ref, na_ref, pa_ref, ROT_DIM // 2) if kind == "a" else (
                    cb_ref, nb_ref, pb_ref, HEAD_DIM // 2)
                r = (r * c_ref[...] + pltpu.roll(r, LANES - shift, 1) * n_ref[...]
                     + pltpu.roll(r, shift, 1) * p_ref[...])
                if post != 1.0:
                    r = r * post
            lo = col + c * LANES
            if oi in transposed:
                out_refs[oi][lo:lo + LANES, :] = r.T
            else:
                out_refs[oi][:, lo:lo + LANES] = r


def _rope_tables(pos, rot_dim, theta):
    half = rot_dim // 2
    inv = theta ** (-jnp.arange(half, dtype=F32) / half)
    ang = pos.astype(F32)[:, None] * inv[None, :]
    cos, sin = jnp.cos(ang), jnp.sin(ang)
    lane = np.arange(LANES) % HEAD_DIM
    idx = lane % half
    cos_t = jnp.where(lane < rot_dim, cos[:, idx], 1.0)
    neg_t = jnp.where(lane < half, -sin[:, idx], 0.0)
    pos_t = jnp.where((lane >= half) & (lane < rot_dim), sin[:, idx], 0.0)
    return cos_t.astype(F32), neg_t.astype(F32), pos_t.astype(F32)


def _project(x, nw, w_bf, tabs_a, tabs_b, segs, out_widths, tm, transposed=()):
    n, d = x.shape
    t_tab = tabs_a[0].shape[0]
    nt = t_tab // tm
    tab_spec = pl.BlockSpec((tm, LANES), lambda i: (i % nt, 0))
    out_specs, out_shape = [], []
    for oi, w in enumerate(out_widths):
        if oi in transposed:
            out_specs.append(pl.BlockSpec((None, w, tm), lambda i: (i // nt, 0, i % nt)))
            out_shape.append(jax.ShapeDtypeStruct((n // t_tab, w, t_tab), F32))
        else:
            out_specs.append(pl.BlockSpec((tm, w), lambda i: (i, 0)))
            out_shape.append(jax.ShapeDtypeStruct((n, w), F32))
    return pl.pallas_call(
        functools.partial(_proj_kernel, segs=segs, transposed=tuple(transposed)),
        grid=(n // tm,),
        in_specs=[pl.BlockSpec((tm, d), lambda i: (i, 0)),
                  pl.BlockSpec((1, d), lambda i: (0, 0)),
                  pl.BlockSpec(w_bf.shape, lambda i: (0, 0))] + [tab_spec] * 6,
        out_specs=out_specs,
        out_shape=out_shape,
        compiler_params=_cparams("parallel"),
        name="rms_proj_rope",
    )(x, nw.reshape(1, d), w_bf, *tabs_a, *tabs_b)


def _outproj_kernel(*refs, nparts):
    parts, w_ref, res_ref, o_ref = refs[:nparts], refs[nparts], refs[nparts + 1], refs[nparts + 2]
    a = jnp.concatenate([p[...].astype(BF16) for p in parts], axis=1) if nparts > 1 else parts[0][...].astype(BF16)
    o_ref[...] = res_ref[...] + jnp.dot(a, w_ref[...], preferred_element_type=F32)


def _outproj(parts, w_bf, res, tm):
    n, d = res.shape
    return pl.pallas_call(
        functools.partial(_outproj_kernel, nparts=len(parts)),
        grid=(n // tm,),
        in_specs=[pl.BlockSpec((tm, p.shape[1]), lambda i: (i, 0)) for p in parts]
        + [pl.BlockSpec(w_bf.shape, lambda i: (0, 0)), pl.BlockSpec((tm, d), lambda i: (i, 0))],
        out_specs=pl.BlockSpec((tm, d), lambda i: (i, 0)),
        out_shape=jax.ShapeDtypeStruct((n, d), F32),
        compiler_params=_cparams("parallel"),
        name="outproj_residual",
    )(*parts, w_bf, res)


def _ffn_kernel(x_ref, nw_ref, wg_ref, wu_ref, wd_ref, fw_ref, o_ref, *, nchunk, final):
    x = x_ref[...]
    xn = _rms(x, nw_ref[...]).astype(BF16)
    dff = wg_ref.shape[1]
    tf = dff // nchunk
    acc = x
    for c in range(nchunk):
        g = jnp.dot(xn, wg_ref[:, c * tf:(c + 1) * tf], preferred_element_type=F32)
        u = jnp.dot(xn, wu_ref[:, c * tf:(c + 1) * tf], preferred_element_type=F32)
        h = (g * jax.nn.sigmoid(g) * u).astype(BF16)
        acc = acc + jnp.dot(h, wd_ref[c * tf:(c + 1) * tf, :], preferred_element_type=F32)
    if final:
        acc = _rms(acc, fw_ref[...])
    o_ref[...] = acc


def _ffn(x, nw, wg, wu, wd, fw, final, tm):
    n, d = x.shape
    dff = wg.shape[1]
    const = lambda i: (0, 0)
    return pl.pallas_call(
        functools.partial(_ffn_kernel, nchunk=2, final=final),
        grid=(n // tm,),
        in_specs=[pl.BlockSpec((tm, d), lambda i: (i, 0)), pl.BlockSpec((1, d), const),
                  pl.BlockSpec((d, dff), const), pl.BlockSpec((d, dff), const), pl.BlockSpec((dff, d), const),
                  pl.BlockSpec((1, d), const)],
        out_specs=pl.BlockSpec((tm, d), lambda i: (i, 0)),
        out_shape=jax.ShapeDtypeStruct((n, d), F32),
        compiler_params=_cparams("parallel"),
        name="rms_swiglu_residual",
    )(x, nw.reshape(1, d), wg, wu, wd, fw.reshape(1, d))


ATT_TILE = 256
ATT_GROUP = 4
DIL_GROUP = 3
ATT_CHUNK = 64


def _unit_scores(kb, qb, bias_fn):
    s = jnp.dot(kb, qb, preferred_element_type=F32)
    chunks = [s[r:r + ATT_CHUNK] + bias_fn(r, r + ATT_CHUNK) for r in range(0, s.shape[0], ATT_CHUNK)]
    return chunks, jnp.max(functools.reduce(jnp.maximum, chunks), axis=0, keepdims=True)


def _unit_values(chunks, m_u, vts, mult_fn):
    ps, tot = [], None
    for i, c in enumerate(chunks):
        p = jnp.exp(c - m_u)
        if mult_fn is not None:
            p = p * mult_fn(i * ATT_CHUNK, (i + 1) * ATT_CHUNK)
        tot = p if tot is None else tot + p
        ps.append(p.astype(BF16))
    per = len(ps) // len(vts)
    pv = sum(jnp.dot(vt, jnp.concatenate(ps[b * per:(b + 1) * per], axis=0), preferred_element_type=F32)
             for b, vt in enumerate(vts))
    return jnp.sum(tot, axis=0, keepdims=True), pv


def _attend_units(units, prior):
    parts = [[], []] if prior is None else [[p] for p in prior]
    pending = None

    def finish(unit, chunks, m_u):
        hh, _, _, _, vt_fn, mult_fn = unit
        parts[hh].append((m_u,) + _unit_values(chunks, m_u, vt_fn(), mult_fn))

    for unit in units:
        scored = _unit_scores(unit[1](), unit[2], unit[3])
        if pending is not None:
            finish(*pending)
        pending = (unit,) + scored
    finish(*pending)
    merged = []
    for ps in parts:
        m = functools.reduce(jnp.maximum, [p[0] for p in ps])
        ws = [jnp.exp(p[0] - m) for p in ps]
        merged.append((m, sum(w * p[1] for w, p in zip(ws, ps)), sum(w * p[2] for w, p in zip(ws, ps))))
    return merged


def _head_queries(q_t, feat):
    return [jnp.where(feat // HEAD_DIM == hh, q_t, 0.0) for hh in range(2)]


def _scaled_bf16(q):
    return (q * ATT_SCALE).astype(BF16)


def _moba_prompt_kernel(q_ref, kt_ref, vt_ref, o_ref, kb_ref, vb_ref, kmean_ref, bias_ref, causal_ref, *, nblk):
    blk, grp = ATT_TILE, ATT_GROUP
    qt = pl.program_id(2)

    @pl.when(qt == 0)
    def _():
        for n in range(nblk):
            kblk = kt_ref[:, n * blk:(n + 1) * blk].T
            kmean_ref[n:n + 1, :] = jnp.sum(kblk, axis=0, keepdims=True) * (1.0 / blk)
            kb_ref[n * blk:(n + 1) * blk, :] = kblk.astype(BF16)
            vb_ref[n] = vt_ref[:, n * blk:(n + 1) * blk].astype(BF16)
        key_i = lax.broadcasted_iota(jnp.int32, (blk, blk), 0)
        qry_i = lax.broadcasted_iota(jnp.int32, (blk, blk), 1)
        causal_ref[...] = jnp.where(key_i <= qry_i, 0.0, NEG_INF)

    feat = lax.broadcasted_iota(jnp.int32, (LANES, blk), 0)
    blk_id = lax.broadcasted_iota(jnp.int32, (nblk, blk), 0)
    blk_f = blk_id.astype(F32)
    q_heads = _head_queries(q_ref[...].T, feat)
    for hh, q_h in enumerate(q_heads):
        sc = jnp.dot(kmean_ref[...], q_h, preferred_element_type=F32, precision=lax.Precision.HIGHEST)
        sc = jnp.where(blk_id < qt, sc, NEG_INF)
        sel = jnp.zeros((nblk, blk), F32)
        for r in range(MOBA_TOPK):
            mx = jnp.max(sc, axis=0, keepdims=True)
            first = jnp.min(jnp.where(sc == mx, blk_f, float(nblk)), axis=0, keepdims=True)
            pick = blk_f == first
            sel = jnp.maximum(sel, jnp.where(pick, jnp.where(r < qt, 1.0, 0.0), 0.0))
            sc = jnp.where(pick, -jnp.inf, sc)
        bias_ref[hh] = jnp.where((sel > 0.0) | (blk_id == qt), 0.0, NEG_INF)
    qbs = [_scaled_bf16(q_h) for q_h in q_heads]

    def units(g, with_own):
        out = []
        for hh in range(2):
            def bias_fn(r0, r1, hh=hh):
                n, r = g * grp + r0 // blk, r0 % blk
                row = bias_ref[hh, pl.ds(n, 1), :]
                return row + jnp.where(n == qt, causal_ref[r:r + (r1 - r0), :], 0.0) if with_own else row

            out.append((hh, lambda: kb_ref[pl.ds(pl.multiple_of(g * grp * blk, blk), grp * blk), :], qbs[hh],
                        bias_fn, lambda: [vb_ref[g * grp + b] for b in range(grp)], None))
        return out

    g_own = qt // grp
    merged = _attend_units(units(g_own, True), None)

    def body(g, carry):
        merged = _attend_units(units(g, False), [carry[:3], carry[3:]])
        return merged[0] + merged[1]

    carry = lax.fori_loop(0, g_own, body, merged[0] + merged[1])
    o_t = jnp.where(feat // HEAD_DIM == 0, carry[2] / carry[1], carry[5] / carry[4])
    o_ref[...] = o_t.T


def _moba_prompt(q, kt, vt, batch, seq):
    n, width = q.shape
    blk, grp = ATT_TILE, ATT_GROUP
    assert blk == MOBA_BLOCK and seq % (blk * grp) == 0
    nblk = seq // blk
    npair = width // LANES
    kv_spec = pl.BlockSpec((None, LANES, seq), lambda b, p, t: (b, p, 0))
    q_spec = pl.BlockSpec((blk, LANES), lambda b, p, t: (b * nblk + t, p))
    return pl.pallas_call(
        functools.partial(_moba_prompt_kernel, nblk=nblk),
        grid=(batch, npair, nblk),
        in_specs=[q_spec, kv_spec, kv_spec],
        out_specs=q_spec,
        out_shape=jax.ShapeDtypeStruct((n, width), F32),
        scratch_shapes=[pltpu.VMEM((nblk * blk, LANES), BF16), pltpu.VMEM((nblk, LANES, blk), BF16),
                        pltpu.VMEM((nblk, LANES), F32), pltpu.VMEM((2, nblk, blk), F32),
                        pltpu.VMEM((blk, blk), F32)],
        compiler_params=_cparams("parallel", "parallel", "arbitrary"),
        name="moba_prompt",
    )(q, kt, vt)


def _dilated_multiplicity(delta):
    c = np.zeros(delta.shape, np.int32)
    for window, dil in DILATED:
        c += ((delta >= 0) & (delta <= window) & (delta % dil == 0)).astype(np.int32)
    return c


DIL_NBACK = W_MAX // ATT_TILE


def _dilated_prompt_tables():
    ki = np.arange(ATT_TILE)[:, None]
    qi = np.arange(ATT_TILE)[None, :]
    c = np.stack([_dilated_multiplicity(ATT_TILE * (DIL_NBACK - i) + qi - ki) for i in range(DIL_NBACK + 1)])
    return np.where(c > 0, 0.0, NEG_INF).astype(np.float32), c.astype(np.float32)


def _dilated_prompt_kernel(q_ref, kt_ref, vt_ref, bias_ref, mult_ref, o_ref, kb_ref, vb_ref, tab_ref, *, nblk):
    blk, grp, nback = ATT_TILE, DIL_GROUP, DIL_NBACK
    qt = pl.program_id(2)

    @pl.when(qt == 0)
    def _():
        kb_ref[0:nback * blk, :] = jnp.zeros((nback * blk, LANES), BF16)
        for n in range(nback):
            vb_ref[n] = jnp.zeros((LANES, blk), BF16)
        for n in range(nblk):
            kb_ref[(n + nback) * blk:(n + nback + 1) * blk, :] = kt_ref[:, n * blk:(n + 1) * blk].T.astype(BF16)
            vb_ref[n + nback] = vt_ref[:, n * blk:(n + 1) * blk].astype(BF16)

    @pl.when(qt <= nback)
    def _():
        for i in range(nback + 1):
            tab_ref[i] = jnp.where(qt - nback + i >= 0, bias_ref[i], NEG_INF)

    feat = lax.broadcasted_iota(jnp.int32, (LANES, blk), 0)
    qbs = [_scaled_bf16(q_h) for q_h in _head_queries(q_ref[...].T, feat)]
    units = []
    for i0 in range(0, nback + 1, grp):
        for hh in range(2):
            units.append((
                hh, lambda i0=i0: kb_ref[pl.ds(pl.multiple_of((qt + i0) * blk, blk), grp * blk), :], qbs[hh],
                lambda r0, r1, i0=i0: tab_ref[i0 + r0 // blk, r0 % blk:r0 % blk + (r1 - r0), :],
                lambda i0=i0: [vb_ref[qt + i0 + b] for b in range(grp)],
                lambda r0, r1, i0=i0: mult_ref[i0 + r0 // blk, r0 % blk:r0 % blk + (r1 - r0), :]))
    (_, l0, a0), (_, l1, a1) = _attend_units(units, None)
    o_t = jnp.where(feat // HEAD_DIM == 0, a0 / l0, a1 / l1)
    o_ref[...] = o_t.T


def _dilated_prompt(q, kt, vt, batch, seq):
    n, width = q.shape
    blk, grp, nback = ATT_TILE, DIL_GROUP, DIL_NBACK
    assert seq % blk == 0 and (nback + 1) % grp == 0
    nblk = seq // blk
    npair = width // LANES
    tabs = [jnp.asarray(t) for t in _dilated_prompt_tables()]
    kv_spec = pl.BlockSpec((None, LANES, seq), lambda b, p, t: (b, p, 0))
    q_spec = pl.BlockSpec((blk, LANES), lambda b, p, t: (b * nblk + t, p))
    tab_specs = [pl.BlockSpec(t.shape, functools.partial(lambda b, p, t, nd: (0,) * nd, nd=t.ndim)) for t in tabs]
    return pl.pallas_call(
        functools.partial(_dilated_prompt_kernel, nblk=nblk),
        grid=(batch, npair, nblk),
        in_specs=[q_spec, kv_spec, kv_spec] + tab_specs,
        out_specs=q_spec,
        out_shape=jax.ShapeDtypeStruct((n, width), F32),
        scratch_shapes=[pltpu.VMEM(((nblk + nback) * blk, LANES), BF16), pltpu.VMEM((nblk + nback, LANES, blk), BF16),
                        pltpu.VMEM((nback + 1, blk, blk), F32)],
        compiler_params=_cparams("parallel", "parallel", "arbitrary"),
        name="dilated_prompt",
    )(q, kt, vt, *tabs)


RET_ROWS = 128


def _retention_tables(n_heads, chunk):
    log_g = jnp.log1p(-jnp.exp2(-5.0 - jnp.arange(n_heads, dtype=F32)))
    i = jnp.arange(chunk, dtype=F32)
    diff = i[:, None] - i[None, :]
    causal = diff >= 0
    decay = jnp.where(causal[None], jnp.exp(jnp.where(causal, diff, 0.0)[None] * log_g[:, None, None]), 0.0)
    q_dec = jnp.exp((i + 1.0)[:, None] * log_g[None, :])
    k_dec = jnp.exp((chunk - 1.0 - i)[:, None] * log_g[None, :])
    c_dec = jnp.exp(chunk * log_g)
    pad = RET_ROWS - chunk
    npair = n_heads // 2
    decay = jnp.pad(decay, ((0, 0), (0, pad), (0, pad))).reshape(npair, 2 * RET_ROWS, RET_ROWS)
    expand = lambda t: jnp.pad(jnp.repeat(t, HEAD_DIM, axis=1), ((0, pad), (0, 0))).reshape(
        RET_ROWS, npair, LANES).transpose(1, 0, 2)
    q_tab, k_tab = expand(q_dec), expand(k_dec)
    same_head = (np.arange(LANES)[:, None] // HEAD_DIM) == (np.arange(LANES)[None, :] // HEAD_DIM)
    c_tab = jnp.where(same_head[None], jnp.repeat(c_dec, HEAD_DIM).reshape(npair, LANES, 1), 0.0)
    return decay.astype(F32), q_tab.astype(F32), k_tab.astype(F32), c_tab.astype(F32)


def _retention_kernel(q_ref, k_ref, v_ref, g_ref, gn_ref, s0_ref, dec_ref, qd_ref, kd_ref, cd_ref,
                      o_ref, sout_ref, st_ref, *, chunk, nchunk):
    c = pl.program_id(2)

    @pl.when(c == 0)
    def _():
        st_ref[...] = s0_ref[...]

    pad = RET_ROWS - chunk
    lane = lax.broadcasted_iota(jnp.int32, (RET_ROWS, LANES), 1)
    head0 = lane < HEAD_DIM
    same_head = (lax.broadcasted_iota(jnp.int32, (LANES, LANES), 0) // HEAD_DIM) == (
        lax.broadcasted_iota(jnp.int32, (LANES, LANES), 1) // HEAD_DIM)

    def rows(ref, r):
        x = ref[pl.ds(r, chunk), :]
        return x if pad == 0 else jnp.concatenate([x, jnp.zeros((pad, LANES), F32)], axis=0)

    def seg_mean(x):
        s0 = jnp.sum(jnp.where(head0, x, 0.0), axis=-1, keepdims=True)
        s1 = jnp.sum(jnp.where(head0, 0.0, x), axis=-1, keepdims=True)
        return jnp.where(head0, s0, s1) * (1.0 / HEAD_DIM)

    def one_chunk(j, state):
        r = j * chunk
        q, k, v, g = rows(q_ref, r), rows(k_ref, r), rows(v_ref, r), rows(g_ref, r)
        kb, vb = k.astype(BF16), v.astype(BF16)
        q2 = jnp.concatenate([jnp.where(head0, q, 0.0), jnp.where(head0, 0.0, q)], axis=0).astype(BF16)
        attn = lax.dot_general(q2, kb, _NT, preferred_element_type=F32) * dec_ref[...]
        inn = jnp.dot(attn.astype(BF16), vb, preferred_element_type=F32)
        inner = jnp.where(head0, inn[:RET_ROWS], inn[RET_ROWS:])
        cross = jnp.dot(q.astype(BF16), state.astype(BF16), preferred_element_type=F32) * qd_ref[...]
        upd = lax.dot_general((k * kd_ref[...]).astype(BF16), vb, _TN, preferred_element_type=F32)
        o = inner + cross
        mu = seg_mean(o)
        var = seg_mean(jnp.square(o - mu))
        y = (o - mu) * lax.rsqrt(var + EPS) * gn_ref[...]
        y = y * (g * jax.nn.sigmoid(g))
        o_ref[r:r + chunk, :] = y[:chunk]
        return state * cd_ref[...] + jnp.where(same_head, upd, 0.0)

    state = st_ref[...]
    for j in range(nchunk):
        state = one_chunk(j, state)
    st_ref[...] = state

    @pl.when(c == pl.num_programs(2) - 1)
    def _():
        sout_ref[...] = st_ref[...]


def _retention(q, k, v, g, gn_w, s0, batch, seq, chunk, nchunk):
    n, width = q.shape
    n_heads = width // HEAD_DIM
    npair = width // LANES
    rows = chunk * nchunk
    steps = seq // rows
    dec, qd, kd, cd = _retention_tables(n_heads, chunk)
    eye2 = jnp.eye(2, dtype=F32)
    s0p = s0.reshape(batch, npair, 2, HEAD_DIM, HEAD_DIM)
    s0p = jnp.einsum('bpide,ij->bpidje', s0p, eye2).reshape(batch, npair, LANES, LANES)
    row_spec = pl.BlockSpec((rows, LANES), lambda b, p, c: (b * steps + c, p))
    st_spec = pl.BlockSpec((None, None, LANES, LANES), lambda b, p, c: (b, p, 0, 0))
    tab = lambda t: pl.BlockSpec((None,) + t.shape[1:], lambda b, p, c: (p, 0, 0))
    out, s_new = pl.pallas_call(
        functools.partial(_retention_kernel, chunk=chunk, nchunk=nchunk),
        grid=(batch, npair, steps),
        in_specs=[row_spec, row_spec, row_spec, row_spec, pl.BlockSpec((1, LANES), lambda b, p, c: (0, p)),
                  st_spec, tab(dec), tab(qd), tab(kd), tab(cd)],
        out_specs=[row_spec, st_spec],
        out_shape=[jax.ShapeDtypeStruct((n, width), F32),
                   jax.ShapeDtypeStruct((batch, npair, LANES, LANES), F32)],
        scratch_shapes=[pltpu.VMEM((LANES, LANES), F32)],
        compiler_params=_cparams("parallel", "parallel", "arbitrary"),
        name="retention",
    )(q, k, v, g, gn_w.reshape(1, width), s0p, dec, qd, kd, cd)
    s_new = s_new.reshape(batch, npair, 2, HEAD_DIM, 2, HEAD_DIM)
    s_new = jnp.stack([s_new[:, :, 0, :, 0, :], s_new[:, :, 1, :, 1, :]], axis=2)
    return out, s_new.reshape(batch, n_heads, HEAD_DIM, HEAD_DIM)


def _block_diag_queries(q, n_heads):
    t = q.shape[0]
    rep = jnp.concatenate([q] * n_heads, axis=0)
    rows = lax.broadcasted_iota(jnp.int32, rep.shape, 0) // t
    cols = lax.broadcasted_iota(jnp.int32, rep.shape, 1) // HEAD_DIM
    return jnp.where(rows == cols, rep, 0.0)


def _take_diag(o, n_heads, t):
    cols = lax.broadcasted_iota(jnp.int32, (t, o.shape[1]), 1) // HEAD_DIM
    out = jnp.zeros((t, o.shape[1]), F32)
    for h in range(n_heads):
        out = jnp.where(cols == h, o[h * t:(h + 1) * t, :], out)
    return out


def _pad_rows(x, rows):
    return jnp.concatenate([x, jnp.zeros((rows - x.shape[0], x.shape[1]), x.dtype)], axis=0)


MOBA_PAGES_PER_STEP = 16


def _moba_decode_kernel(pt_ref, q_ref, kn_ref, vn_ref, *rest, n_heads, t_new, pages_per_blk, nblk):
    npg = MOBA_PAGES_PER_STEP
    kp, vp = rest[:npg], rest[npg:2 * npg]
    o_ref, qbd_ref, m_ref, l_ref, sc_ref, oblk_ref = rest[2 * npg:]
    g = pl.program_id(1)
    rows = n_heads * t_new
    lane = lax.broadcasted_iota(jnp.int32, (rows, LANES), 1)
    page = kp[0].shape[1]
    blk_keys = pages_per_blk * page
    step_blks = npg // pages_per_blk

    @pl.when(g == 0)
    def _():
        qbd_ref[...] = _scaled_bf16(_block_diag_queries(q_ref[...], n_heads))
        m_ref[...] = jnp.zeros_like(m_ref)
        l_ref[...] = jnp.zeros_like(l_ref)
        sc_ref[...] = jnp.full_like(sc_ref, -jnp.inf)

    kt_all = jnp.concatenate([kp[i][...].astype(BF16) for i in range(npg)], axis=1)
    s_all = jnp.dot(qbd_ref[...], kt_all, preferred_element_type=F32)
    m_new, l_new, sc_new = m_ref[...], l_ref[...], sc_ref[...]
    for jb in range(step_blks):
        n = g * step_blks + jb
        s = s_all[:, jb * blk_keys:(jb + 1) * blk_keys]
        mb = jnp.max(s, axis=-1, keepdims=True)
        p = jnp.exp(s - mb)
        oblk_ref[n] = sum(lax.dot_general(p[:, a * page:(a + 1) * page].astype(BF16),
                                          vp[jb * pages_per_blk + a][...].astype(BF16),
                                          _NT, preferred_element_type=F32) for a in range(pages_per_blk))
        here = lane == n
        m_new = jnp.where(here, mb, m_new)
        l_new = jnp.where(here, jnp.sum(p, axis=-1, keepdims=True), l_new)
        sc_new = jnp.where(here, jnp.sum(s, axis=-1, keepdims=True) * (1.0 / blk_keys), sc_new)
    m_ref[...], l_ref[...], sc_ref[...] = m_new, l_new, sc_new

    @pl.when(g == pl.num_programs(1) - 1)
    def _():
        s = lax.dot_general(qbd_ref[...], _pad_rows(kn_ref[...], LANES).astype(BF16), _NT,
                            preferred_element_type=F32)
        qry = lax.broadcasted_iota(jnp.int32, (rows, LANES), 0) % t_new
        s = jnp.where(lane <= qry, s, NEG_INF)
        m_own = jnp.max(s, axis=-1, keepdims=True)
        p = jnp.exp(s - m_own)
        l_own = jnp.sum(p, axis=-1, keepdims=True)
        o_own = jnp.dot(p.astype(BF16), _pad_rows(vn_ref[...], LANES).astype(BF16), preferred_element_type=F32)
        sc = sc_ref[...]
        lane_f = lane.astype(F32)
        sel = lane < 0
        for _ in range(MOBA_TOPK):
            mx = jnp.max(sc, axis=-1, keepdims=True)
            first = jnp.min(jnp.where(sc == mx, lane_f, float(LANES)), axis=-1, keepdims=True)
            pick = lane_f == first
            sel = sel | pick
            sc = jnp.where(pick, -jnp.inf, sc)
        m_all = m_ref[...]
        m_fin = jnp.maximum(jnp.max(jnp.where(sel, m_all, -jnp.inf), axis=-1, keepdims=True), m_own)
        w = jnp.where(sel, jnp.exp(m_all - m_fin), 0.0)
        w_own = jnp.exp(m_own - m_fin)
        l_fin = jnp.sum(w * l_ref[...], axis=-1, keepdims=True) + w_own * l_own

        def body(n, acc):
            col = jnp.sum(jnp.where(lane == n, w, 0.0), axis=-1, keepdims=True)
            return acc + col * oblk_ref[n]

        acc = lax.fori_loop(0, nblk, body, w_own * o_own)
        o_ref[...] = _take_diag(acc / l_fin, n_heads, t_new)


def _moba_decode(q, k_new, v_new, cache_kt, cache_vt, page_table, t_new):
    n, width = q.shape
    n_heads = width // HEAD_DIM
    n_seq, n_pages = page_table.shape
    page = cache_kt.shape[2]
    pages_per_blk = MOBA_BLOCK // page
    nblk = n_pages // pages_per_blk
    npg = MOBA_PAGES_PER_STEP
    rows = n_heads * t_new
    assert n_pages % npg == 0 and npg % pages_per_blk == 0 and MOBA_TOPK <= nblk <= LANES
    assert n_pages * page == nblk * MOBA_BLOCK and t_new <= page and page == LANES and rows <= LANES
    tok_spec = pl.BlockSpec((t_new, width), lambda b, g, pt: (b, 0))
    page_specs = [pl.BlockSpec((None, width, page), functools.partial(
        lambda b, g, pt, i: (pt[b, g * npg + i], 0, 0), i=i)) for i in range(npg)]
    grid_spec = pltpu.PrefetchScalarGridSpec(
        num_scalar_prefetch=1,
        grid=(n_seq, n_pages // npg),
        in_specs=[tok_spec, tok_spec, tok_spec] + page_specs + page_specs,
        out_specs=tok_spec,
        scratch_shapes=[pltpu.VMEM((rows, width), BF16), pltpu.VMEM((rows, LANES), F32),
                        pltpu.VMEM((rows, LANES), F32), pltpu.VMEM((rows, LANES), F32),
                        pltpu.VMEM((nblk, rows, width), F32)],
    )
    return pl.pallas_call(
        functools.partial(_moba_decode_kernel, n_heads=n_heads, t_new=t_new,
                          pages_per_blk=pages_per_blk, nblk=nblk),
        grid_spec=grid_spec,
        out_shape=jax.ShapeDtypeStruct((n, width), F32),
        compiler_params=_cparams("parallel", "arbitrary"),
        name="moba_decode",
    )(page_table, q, k_new, v_new, *([cache_kt] * npg), *([cache_vt] * npg))


DIL_DEC_TILE = 512


def _dilated_decode_tables(n_heads, t_new, n_prev):
    qi = (np.arange(n_heads * t_new) % t_new)[:, None]
    c_old = _dilated_multiplicity(n_prev + qi - np.arange(n_prev)[None, :])
    c_new = _dilated_multiplicity(qi - np.arange(LANES)[None, :])
    c_new = np.where(np.arange(LANES)[None, :] < t_new, c_new, 0)
    f = lambda c: (np.where(c > 0, 0.0, NEG_INF).astype(np.float32), c.astype(np.float32))
    return f(c_old) + f(c_new)


def _dilated_decode_kernel(q_ref, kn_ref, vn_ref, kc_ref, vc_ref, bo_ref, mo_ref, bn_ref, mn_ref,
                           o_ref, qbd_ref, m_ref, l_ref, acc_ref, *, n_heads, t_new):
    kt = pl.program_id(1)

    @pl.when(kt == 0)
    def _():
        qbd_ref[...] = _scaled_bf16(_block_diag_queries(q_ref[...], n_heads))
        m_ref[...] = jnp.full_like(m_ref, -jnp.inf)
        l_ref[...] = jnp.zeros_like(l_ref)
        acc_ref[...] = jnp.zeros_like(acc_ref)

    def step(s, pv, bias, mult):
        s = s + bias
        m_old = m_ref[...]
        m_new = jnp.maximum(m_old, jnp.max(s, axis=-1, keepdims=True))
        alpha = jnp.exp(m_old - m_new)
        p = jnp.exp(s - m_new) * mult
        l_ref[...] = alpha * l_ref[...] + jnp.sum(p, axis=-1, keepdims=True)
        acc_ref[...] = alpha * acc_ref[...] + pv(p.astype(BF16))
        m_ref[...] = m_new

    @pl.when(kt == 0)
    def _():
        k, v = _pad_rows(kn_ref[...], LANES).astype(BF16), _pad_rows(vn_ref[...], LANES).astype(BF16)
        step(lax.dot_general(qbd_ref[...], k, _NT, preferred_element_type=F32),
             lambda p: jnp.dot(p, v, preferred_element_type=F32), bn_ref[...], mn_ref[...])

    step(jnp.dot(qbd_ref[...], kc_ref[...].astype(BF16), preferred_element_type=F32),
         lambda p: lax.dot_general(p, vc_ref[...].astype(BF16), _NT, preferred_element_type=F32),
         bo_ref[...], mo_ref[...])

    @pl.when(kt == pl.num_programs(1) - 1)
    def _():
        o_ref[...] = _take_diag(acc_ref[...] / l_ref[...], n_heads, t_new)


def _dilated_decode(q, k_new, v_new, win_kt, win_vt, t_new):
    n, width = q.shape
    n_heads = width // HEAD_DIM
    n_seq, _, n_prev = win_kt.shape
    assert n_prev == W_MAX and n_prev % DIL_DEC_TILE == 0
    rows = n_heads * t_new
    bo, mo, bn, mn = _dilated_decode_tables(n_heads, t_new, n_prev)
    tok_spec = pl.BlockSpec((t_new, width), lambda b, t: (b, 0))
    win_spec = pl.BlockSpec((None, width, DIL_DEC_TILE), lambda b, t: (b, 0, t))
    old_tab = pl.BlockSpec((rows, DIL_DEC_TILE), lambda b, t: (0, t))
    new_tab = pl.BlockSpec((rows, LANES), lambda b, t: (0, 0))
    return pl.pallas_call(
        functools.partial(_dilated_decode_kernel, n_heads=n_heads, t_new=t_new),
        grid=(n_seq, n_prev // DIL_DEC_TILE),
        in_specs=[tok_spec, tok_spec, tok_spec, win_spec, win_spec, old_tab, old_tab, new_tab, new_tab],
        out_specs=tok_spec,
        out_shape=jax.ShapeDtypeStruct((n, width), F32),
        scratch_shapes=[pltpu.VMEM((rows, width), BF16), pltpu.VMEM((rows, 1), F32),
                        pltpu.VMEM((rows, 1), F32), pltpu.VMEM((rows, width), F32)],
        compiler_params=_cparams("parallel", "arbitrary"),
        name="dilated_decode",
    )(q, k_new, v_new, win_kt, win_vt, jnp.asarray(bo), jnp.asarray(mo), jnp.asarray(bn), jnp.asarray(mn))


_AB_SEGS = (("a", 0, 0, 1.0), ("a", 1, 0, 1.0), (None, 2, 0, 1.0), ("b", 3, 0, 1.0),
            ("b", 4, 0, ATT_SCALE), (None, 5, 0, 1.0), (None, 6, 0, 1.0))
_C_SEGS = tuple(("a" if o < 2 else None, o, c, 1.0) for o in range(3) for c in (0, SEG))
_KV_OUT = (1, 2)


def _row_tile(n):
    return 512 if n % 512 == 0 else n


def _feature_major(x):
    lead = x.shape[:-3]
    t, h, dh = x.shape[-3:]
    nl = len(lead)
    return x.transpose(*range(nl), nl + 1, nl + 2, nl).reshape(*lead, h * dh, t)


def _token_major(xt, n_heads):
    lead = xt.shape[:-2]
    t = xt.shape[-1]
    nl = len(lead)
    return xt.reshape(*lead, n_heads, HEAD_DIM, t).transpose(*range(nl), nl + 2, nl, nl + 1)


def kernel(x_prompt, x_sample, cache_k_a, cache_v_a, page_table, state_ret, cache_win_k, cache_win_v,
           norm_mix, norm_ffn, norm_final, w_in_ab, w_out_ab, ret_gn_w, w_in_c, w_out_c,
           ffn_w_gate, ffn_w_up, ffn_w_down):
    bp, tp, d = x_prompt.shape
    bs, ts, _ = x_sample.shape
    page = cache_k_a.shape[2]
    past_len = page_table.shape[1] * page
    h_a = cache_k_a.shape[3]
    wa = h_a * HEAD_DIM
    wb = w_in_ab.shape[2] - 3 * wa
    assert wb == 4 * wa and w_in_ab.shape[0] == 1 and w_in_c.shape[0] == 1 and norm_mix.shape[0] == 2
    h_b = wa // HEAD_DIM
    h_c = w_in_c.shape[2] // (3 * HEAD_DIM)

    pos_p = jnp.arange(tp, dtype=jnp.int32)
    pos_s = past_len + jnp.arange(ts, dtype=jnp.int32)
    tile_s = lambda tabs: tuple(jnp.tile(t, (bs, 1)) for t in tabs)
    tabs = {
        "p": (_rope_tables(pos_p, ROT_DIM, ROPE_THETA), _rope_tables(pos_p, HEAD_DIM, RET_THETA)),
        "s": (tile_s(_rope_tables(pos_s, ROT_DIM, ROPE_THETA)), tile_s(_rope_tables(pos_s, HEAD_DIM, RET_THETA))),
    }
    bf = lambda w: w.astype(BF16)
    w_in_ab_bf, w_out_ab_bf, w_in_c_bf, w_out_c_bf = bf(w_in_ab[0]), bf(w_out_ab[0]), bf(w_in_c[0]), bf(w_out_c[0])
    wg, wu, wd = bf(ffn_w_gate), bf(ffn_w_up), bf(ffn_w_down)

    xp = x_prompt.reshape(bp * tp, d)
    xs = x_sample.reshape(bs * ts, d)
    tm_p, tm_s = _row_tile(bp * tp), _row_tile(bs * ts)
    assert tp % tm_p == 0

    ab_widths = (wa,) * 7
    qa_p, kat_p, vat_p, qb_p, kb_p, vb_p, gb_p = _project(
        xp, norm_mix[0], w_in_ab_bf, *tabs["p"], _AB_SEGS, ab_widths, tm_p, _KV_OUT)
    qa_s, ka_s, va_s, qb_s, kb_s, vb_s, gb_s = _project(
        xs, norm_mix[0], w_in_ab_bf, *tabs["s"], _AB_SEGS, ab_widths, tm_s)

    oa_p = _moba_prompt(qa_p, kat_p, vat_p, bp, tp)
    oa_s = _moba_decode(qa_s, ka_s, va_s, _feature_major(cache_k_a[0]), _feature_major(cache_v_a[0]),
                        page_table, ts)

    chunk_p = math.gcd(tp, RET_CHUNK)
    ob_p, ret_p = _retention(qb_p, kb_p, vb_p, gb_p, ret_gn_w[0],
                             jnp.zeros((bp, h_b, HEAD_DIM, HEAD_DIM), F32), bp, tp, chunk_p,
                             math.gcd(tp // chunk_p, 8))
    ob_s, ret_s = _retention(qb_s, kb_s, vb_s, gb_s, ret_gn_w[0], state_ret[0], bs, ts,
                             math.gcd(ts, RET_CHUNK), 1)

    xp = _outproj([oa_p, ob_p], w_out_ab_bf, xp, tm_p)
    xs = _outproj([oa_s, ob_s], w_out_ab_bf, xs, tm_s)
    xp = _ffn(xp, norm_ffn[0], wg[0], wu[0], wd[0], norm_final, False, tm_p)
    xs = _ffn(xs, norm_ffn[0], wg[0], wu[0], wd[0], norm_final, False, tm_s)

    wc = h_c * HEAD_DIM
    q_p, kt_p, vt_p = _project(xp, norm_mix[1], w_in_c_bf, *tabs["p"], _C_SEGS, (wc,) * 3, tm_p, _KV_OUT)
    q_s, k_s, v_s = _project(xs, norm_mix[1], w_in_c_bf, *tabs["s"], _C_SEGS, (wc,) * 3, tm_s)
    n_prev = cache_win_k.shape[2]
    o_p = _dilated_prompt(q_p, kt_p, vt_p, bp, tp)
    o_s = _dilated_decode(q_s, k_s, v_s, _feature_major(cache_win_k[0]), _feature_major(cache_win_v[0]), ts)
    xp = _outproj([o_p], w_out_c_bf, xp, tm_p)
    xs = _outproj([o_s], w_out_c_bf, xs, tm_s)
    y_p = _ffn(xp, norm_ffn[1], wg[1], wu[1], wd[1], norm_final, True, tm_p)
    y_s = _ffn(xs, norm_ffn[1], wg[1], wu[1], wd[1], norm_final, True, tm_s)

    keep_p = min(W_MAX, tp)
    keep_s = min(W_MAX, n_prev + ts)
    win = lambda old, new: jnp.concatenate([old, new.reshape(bs, ts, h_c, HEAD_DIM)], axis=1)[:, -keep_s:][None]
    return (
        y_p.reshape(bp, tp, d), y_s.reshape(bs, ts, d),
        _token_major(kat_p, h_a)[None], _token_major(vat_p, h_a)[None], ret_p[None],
        _token_major(kt_p[:, :, tp - keep_p:], h_c)[None], _token_major(vt_p[:, :, tp - keep_p:], h_c)[None],
        ka_s.reshape(1, bs, ts, h_a, HEAD_DIM), va_s.reshape(1, bs, ts, h_a, HEAD_DIM), ret_s[None],
        win(cache_win_k[0], k_s), win(cache_win_v[0], v_s),
    )
```

```python
import functools
import math

import jax
import jax.numpy as jnp
import numpy as np
from jax import lax
from jax.experimental import pallas as pl
from jax.experimental.pallas import tpu as pltpu

F32 = jnp.float32
BF16 = jnp.bfloat16

HEAD_DIM = 64
LANES = 128
ROT_DIM = HEAD_DIM // 4
ROPE_THETA = 500000.0
RET_THETA = 10000.0
MOBA_BLOCK = 256
MOBA_TOPK = 3
RET_CHUNK = 128
DILATED = ((128, 1), (512, 4), (2048, 16))
W_MAX = max(w for w, _ in DILATED)
EPS = 1e-6
NEG_INF = -1e30
ATT_SCALE = HEAD_DIM ** -0.5
VMEM_LIMIT = 56 * 1024 * 1024

_NT = (((1,), (1,)), ((), ()))
_TN = (((0,), (0,)), ((), ()))


def _cparams(*sem):
    return pltpu.CompilerParams(dimension_semantics=sem, vmem_limit_bytes=VMEM_LIMIT)


def _rms(x, w):
    ms = jnp.mean(x * x, axis=-1, keepdims=True)
    return x * lax.rsqrt(ms + EPS) * w


SEG = 512


def _proj_kernel(x_ref, nw_ref, w_ref, ca_ref, na_ref, pa_ref, cb_ref, nb_ref, pb_ref, *out_refs, segs, transposed):
    xn = _rms(x_ref[...], nw_ref[...]).astype(BF16)
    for s, (kind, oi, col, post) in enumerate(segs):
        acc = jnp.dot(xn, w_ref[:, s * SEG:(s + 1) * SEG], preferred_element_type=F32)
        for c in range(SEG // LANES):
            r = acc[:, c * LANES:(c + 1) * LANES]
            if kind is not None:
                c_ref, n_ref, p_ref, shift = (ca_ref, na_ref, pa_ref, ROT_DIM // 2) if kind == "a" else (
                    cb_ref, nb_ref, pb_ref, HEAD_DIM // 2)
                r = (r * c_ref[...] + pltpu.roll(r, LANES - shift, 1) * n_ref[...]
                     + pltpu.roll(r, shift, 1) * p_ref[...])
                if post != 1.0:
                    r = r * post
            lo = col + c * LANES
            if oi in transposed:
                out_refs[oi][lo:lo + LANES, :] = r.T
            else:
                out_refs[oi][:, lo:lo + LANES] = r


def _rope_tables(pos, rot_dim, theta):
    half = rot_dim // 2
    inv = theta ** (-jnp.arange(half, dtype=F32) / half)
    ang = pos.astype(F32)[:, None] * inv[None, :]
    cos, sin = jnp.cos(ang), jnp.sin(ang)
    lane = np.arange(LANES) % HEAD_DIM
    idx = lane % half
    cos_t = jnp.where(lane < rot_dim, cos[:, idx], 1.0)
    neg_t = jnp.where(lane < half, -sin[:, idx], 0.0)
    pos_t = jnp.where((lane >= half) & (lane < rot_dim), sin[:, idx], 0.0)
    return cos_t.astype(F32), neg_t.astype(F32), pos_t.astype(F32)


def _project(x, nw, w_bf, tabs_a, tabs_b, segs, out_widths, tm, transposed=()):
    n, d = x.shape
    t_tab = tabs_a[0].shape[0]
    nt = t_tab // tm
    tab_spec = pl.BlockSpec((tm, LANES), lambda i: (i % nt, 0))
    out_specs, out_shape = [], []
    for oi, w in enumerate(out_widths):
        if oi in transposed:
            out_specs.append(pl.BlockSpec((None, w, tm), lambda i: (i // nt, 0, i % nt)))
            out_shape.append(jax.ShapeDtypeStruct((n // t_tab, w, t_tab), F32))
        else:
            out_specs.append(pl.BlockSpec((tm, w), lambda i: (i, 0)))
            out_shape.append(jax.ShapeDtypeStruct((n, w), F32))
    return pl.pallas_call(
        functools.partial(_proj_kernel, segs=segs, transposed=tuple(transposed)),
        grid=(n // tm,),
        in_specs=[pl.BlockSpec((tm, d), lambda i: (i, 0)),
                  pl.BlockSpec((1, d), lambda i: (0, 0)),
                  pl.BlockSpec(w_bf.shape, lambda i: (0, 0))] + [tab_spec] * 6,
        out_specs=out_specs,
        out_shape=out_shape,
        compiler_params=_cparams("parallel"),
        name="rms_proj_rope",
    )(x, nw.reshape(1, d), w_bf, *tabs_a, *tabs_b)


FFN_CHUNKS = 2


def _mix_ffn_kernel(*refs, nparts, final):
    parts = refs[:nparts]
    wo_ref, x_ref, nw_ref, wg_ref, wu_ref, wd_ref, fw_ref, o_ref = refs[nparts:]
    a = jnp.concatenate([p[...].astype(BF16) for p in parts], axis=1) if nparts > 1 else parts[0][...].astype(BF16)
    x = x_ref[...] + jnp.dot(a, wo_ref[...], preferred_element_type=F32)
    xn = _rms(x, nw_ref[...]).astype(BF16)
    tf = wg_ref.shape[1] // FFN_CHUNKS
    acc = x
    for c in range(FFN_CHUNKS):
        g = jnp.dot(xn, wg_ref[:, c * tf:(c + 1) * tf], preferred_element_type=F32)
        u = jnp.dot(xn, wu_ref[:, c * tf:(c + 1) * tf], preferred_element_type=F32)
        h = (g * jax.nn.sigmoid(g) * u).astype(BF16)
        acc = acc + jnp.dot(h, wd_ref[c * tf:(c + 1) * tf, :], preferred_element_type=F32)
    if final:
        acc = _rms(acc, fw_ref[...])
    o_ref[...] = acc


def _mix_ffn(parts, wo, x, nw, wg, wu, wd, fw, final, tm):
    n, d = x.shape
    dff = wg.shape[1]
    const = lambda i: (0, 0)
    row = lambda w: pl.BlockSpec((tm, w), lambda i: (i, 0))
    return pl.pallas_call(
        functools.partial(_mix_ffn_kernel, nparts=len(parts), final=final),
        grid=(n // tm,),
        in_specs=[row(p.shape[1]) for p in parts]
        + [pl.BlockSpec(wo.shape, const), row(d), pl.BlockSpec((1, d), const),
           pl.BlockSpec((d, dff), const), pl.BlockSpec((d, dff), const), pl.BlockSpec((dff, d), const),
           pl.BlockSpec((1, d), const)],
        out_specs=row(d),
        out_shape=jax.ShapeDtypeStruct((n, d), F32),
        compiler_params=_cparams("parallel"),
        name="outproj_swiglu",
    )(*parts, wo, x, nw.reshape(1, d), wg, wu, wd, fw.reshape(1, d))


ATT_TILE = 256
ATT_GROUP = 4
DIL_GROUP = 3
ATT_CHUNK = 64


def _unit_scores(kb, qb, bias_fn):
    s = jnp.dot(kb, qb, preferred_element_type=F32)
    chunks = [s[r:r + ATT_CHUNK] + bias_fn(r, r + ATT_CHUNK) for r in range(0, s.shape[0], ATT_CHUNK)]
    return chunks, jnp.max(functools.reduce(jnp.maximum, chunks), axis=0, keepdims=True)


def _unit_values(chunks, m_u, vts, mult_fn):
    ps, tot = [], None
    for i, c in enumerate(chunks):
        p = jnp.exp(c - m_u)
        if mult_fn is not None:
            p = p * mult_fn(i * ATT_CHUNK, (i + 1) * ATT_CHUNK)
        tot = p if tot is None else tot + p
        ps.append(p.astype(BF16))
    per = len(ps) // len(vts)
    pv = sum(jnp.dot(vt, jnp.concatenate(ps[b * per:(b + 1) * per], axis=0), preferred_element_type=F32)
             for b, vt in enumerate(vts))
    return jnp.sum(tot, axis=0, keepdims=True), pv


def _attend_units(units, prior):
    parts = [[], []] if prior is None else [[p] for p in prior]
    pending = None

    def finish(unit, chunks, m_u):
        hh, _, _, _, vt_fn, mult_fn = unit
        parts[hh].append((m_u,) + _unit_values(chunks, m_u, vt_fn(), mult_fn))

    for unit in units:
        scored = _unit_scores(unit[1](), unit[2], unit[3])
        if pending is not None:
            finish(*pending)
        pending = (unit,) + scored
    finish(*pending)
    merged = []
    for ps in parts:
        m = functools.reduce(jnp.maximum, [p[0] for p in ps])
        ws = [jnp.exp(p[0] - m) for p in ps]
        merged.append((m, sum(w * p[1] for w, p in zip(ws, ps)), sum(w * p[2] for w, p in zip(ws, ps))))
    return merged


def _head_queries(q_t, feat):
    return [jnp.where(feat // HEAD_DIM == hh, q_t, 0.0) for hh in range(2)]


def _scaled_bf16(q):
    return (q * ATT_SCALE).astype(BF16)


def _moba_prompt_kernel(q_ref, kt_ref, vt_ref, o_ref, kb_ref, vb_ref, kmean_ref, bias_ref, causal_ref, *, nblk):
    blk, grp = ATT_TILE, ATT_GROUP
    qt = pl.program_id(2)

    @pl.when(qt == 0)
    def _():
        for n in range(nblk):
            kblk = kt_ref[:, n * blk:(n + 1) * blk].T
            kmean_ref[n:n + 1, :] = jnp.sum(kblk, axis=0, keepdims=True) * (1.0 / blk)
            kb_ref[n * blk:(n + 1) * blk, :] = kblk.astype(BF16)
            vb_ref[n] = vt_ref[:, n * blk:(n + 1) * blk].astype(BF16)
        key_i = lax.broadcasted_iota(jnp.int32, (blk, blk), 0)
        qry_i = lax.broadcasted_iota(jnp.int32, (blk, blk), 1)
        causal_ref[...] = jnp.where(key_i <= qry_i, 0.0, NEG_INF)

    feat = lax.broadcasted_iota(jnp.int32, (LANES, blk), 0)
    blk_id = lax.broadcasted_iota(jnp.int32, (nblk, blk), 0)
    blk_f = blk_id.astype(F32)
    q_heads = _head_queries(q_ref[...].T, feat)
    for hh, q_h in enumerate(q_heads):
        sc = jnp.dot(kmean_ref[...], q_h, preferred_element_type=F32, precision=lax.Precision.HIGHEST)
        sc = jnp.where(blk_id < qt, sc, NEG_INF)
        sel = jnp.zeros((nblk, blk), F32)
        for r in range(MOBA_TOPK):
            mx = jnp.max(sc, axis=0, keepdims=True)
            first = jnp.min(jnp.where(sc == mx, blk_f, float(nblk)), axis=0, keepdims=True)
            pick = blk_f == first
            sel = jnp.maximum(sel, jnp.where(pick, jnp.where(r < qt, 1.0, 0.0), 0.0))
            sc = jnp.where(pick, -jnp.inf, sc)
        bias_ref[hh] = jnp.where((sel > 0.0) | (blk_id == qt), 0.0, NEG_INF)
    qbs = [_scaled_bf16(q_h) for q_h in q_heads]

    def units(g, with_own):
        out = []
        for hh in range(2):
            def bias_fn(r0, r1, hh=hh):
                n, r = g * grp + r0 // blk, r0 % blk
                row = bias_ref[hh, pl.ds(n, 1), :]
                return row + jnp.where(n == qt, causal_ref[r:r + (r1 - r0), :], 0.0) if with_own else row

            out.append((hh, lambda: kb_ref[pl.ds(pl.multiple_of(g * grp * blk, blk), grp * blk), :], qbs[hh],
                        bias_fn, lambda: [vb_ref[g * grp + b] for b in range(grp)], None))
        return out

    g_own = qt // grp
    merged = _attend_units(units(g_own, True), None)

    def body(g, carry):
        merged = _attend_units(units(g, False), [carry[:3], carry[3:]])
        return merged[0] + merged[1]

    carry = lax.fori_loop(0, g_own, body, merged[0] + merged[1])
    o_t = jnp.where(feat // HEAD_DIM == 0, carry[2] / carry[1], carry[5] / carry[4])
    o_ref[...] = o_t.T


def _moba_prompt(q, kt, vt, batch, seq):
    n, width = q.shape
    blk, grp = ATT_TILE, ATT_GROUP
    assert blk == MOBA_BLOCK and seq % (blk * grp) == 0
    nblk = seq // blk
    npair = width // LANES
    kv_spec = pl.BlockSpec((None, LANES, seq), lambda b, p, t: (b, p, 0))
    q_spec = pl.BlockSpec((blk, LANES), lambda b, p, t: (b * nblk + t, p))
    return pl.pallas_call(
        functools.partial(_moba_prompt_kernel, nblk=nblk),
        grid=(batch, npair, nblk),
        in_specs=[q_spec, kv_spec, kv_spec],
        out_specs=q_spec,
        out_shape=jax.ShapeDtypeStruct((n, width), F32),
        scratch_shapes=[pltpu.VMEM((nblk * blk, LANES), BF16), pltpu.VMEM((nblk, LANES, blk), BF16),
                        pltpu.VMEM((nblk, LANES), F32), pltpu.VMEM((2, nblk, blk), F32),
                        pltpu.VMEM((blk, blk), F32)],
        compiler_params=_cparams("parallel", "parallel", "arbitrary"),
        name="moba_prompt",
    )(q, kt, vt)


def _dilated_multiplicity(delta):
    c = np.zeros(delta.shape, np.int32)
    for window, dil in DILATED:
        c += ((delta >= 0) & (delta <= window) & (delta % dil == 0)).astype(np.int32)
    return c


DIL_NBACK = W_MAX // ATT_TILE


def _dilated_prompt_tables():
    ki = np.arange(ATT_TILE)[:, None]
    qi = np.arange(ATT_TILE)[None, :]
    c = np.stack([_dilated_multiplicity(ATT_TILE * (DIL_NBACK - i) + qi - ki) for i in range(DIL_NBACK + 1)])
    return np.where(c > 0, 0.0, NEG_INF).astype(np.float32), c.astype(np.float32)


def _dilated_prompt_kernel(q_ref, kt_ref, vt_ref, bias_ref, mult_ref, o_ref, kb_ref, vb_ref, tab_ref, *, nblk):
    blk, grp, nback = ATT_TILE, DIL_GROUP, DIL_NBACK
    qt = pl.program_id(2)

    @pl.when(qt == 0)
    def _():
        kb_ref[0:nback * blk, :] = jnp.zeros((nback * blk, LANES), BF16)
        for n in range(nback):
            vb_ref[n] = jnp.zeros((LANES, blk), BF16)
        for n in range(nblk):
            kb_ref[(n + nback) * blk:(n + nback + 1) * blk, :] = kt_ref[:, n * blk:(n + 1) * blk].T.astype(BF16)
            vb_ref[n + nback] = vt_ref[:, n * blk:(n + 1) * blk].astype(BF16)

    @pl.when(qt <= nback)
    def _():
        for i in range(nback + 1):
            tab_ref[i] = jnp.where(qt - nback + i >= 0, bias_ref[i], NEG_INF)

    def attend(first_tile):
        feat = lax.broadcasted_iota(jnp.int32, (LANES, blk), 0)
        qbs = [_scaled_bf16(q_h) for q_h in _head_queries(q_ref[...].T, feat)]
        units = []
        for i0 in range(first_tile, nback + 1, grp):
            for hh in range(2):
                units.append((
                    hh, lambda i0=i0: kb_ref[pl.ds(pl.multiple_of((qt + i0) * blk, blk), grp * blk), :], qbs[hh],
                    lambda r0, r1, i0=i0: tab_ref[i0 + r0 // blk, r0 % blk:r0 % blk + (r1 - r0), :],
                    lambda i0=i0: [vb_ref[qt + i0 + b] for b in range(grp)],
                    lambda r0, r1, i0=i0: mult_ref[i0 + r0 // blk, r0 % blk:r0 % blk + (r1 - r0), :]))
        (_, l0, a0), (_, l1, a1) = _attend_units(units, None)
        o_t = jnp.where(feat // HEAD_DIM == 0, a0 / l0, a1 / l1)
        o_ref[...] = o_t.T

    firsts = list(range(0, nback + 1, grp))
    for k, first_tile in enumerate(firsts):
        lo = nback - first_tile - grp + 1
        cond = qt >= lo if k == 0 else (qt >= lo) & (qt < lo + grp)
        pl.when(cond)(functools.partial(attend, first_tile))


def _dilated_prompt(q, kt, vt, batch, seq):
    n, width = q.shape
    blk, grp, nback = ATT_TILE, DIL_GROUP, DIL_NBACK
    assert seq % blk == 0 and (nback + 1) % grp == 0
    nblk = seq // blk
    npair = width // LANES
    tabs = [jnp.asarray(t) for t in _dilated_prompt_tables()]
    kv_spec = pl.BlockSpec((None, LANES, seq), lambda b, p, t: (b, p, 0))
    q_spec = pl.BlockSpec((blk, LANES), lambda b, p, t: (b * nblk + t, p))
    tab_specs = [pl.BlockSpec(t.shape, functools.partial(lambda b, p, t, nd: (0,) * nd, nd=t.ndim)) for t in tabs]
    return pl.pallas_call(
        functools.partial(_dilated_prompt_kernel, nblk=nblk),
        grid=(batch, npair, nblk),
        in_specs=[q_spec, kv_spec, kv_spec] + tab_specs,
        out_specs=q_spec,
        out_shape=jax.ShapeDtypeStruct((n, width), F32),
        scratch_shapes=[pltpu.VMEM(((nblk + nback) * blk, LANES), BF16), pltpu.VMEM((nblk + nback, LANES, blk), BF16),
                        pltpu.VMEM((nback + 1, blk, blk), F32)],
        compiler_params=_cparams("parallel", "parallel", "arbitrary"),
        name="dilated_prompt",
    )(q, kt, vt, *tabs)


RET_ROWS = 128


def _retention_tables(n_heads, chunk):
    log_g = jnp.log1p(-jnp.exp2(-5.0 - jnp.arange(n_heads, dtype=F32)))
    i = jnp.arange(chunk, dtype=F32)
    diff = i[:, None] - i[None, :]
    causal = diff >= 0
    decay = jnp.where(causal[None], jnp.exp(jnp.where(causal, diff, 0.0)[None] * log_g[:, None, None]), 0.0)
    q_dec = jnp.exp((i + 1.0)[:, None] * log_g[None, :])
    k_dec = jnp.exp((chunk - 1.0 - i)[:, None] * log_g[None, :])
    c_dec = jnp.exp(chunk * log_g)
    pad = RET_ROWS - chunk
    npair = n_heads // 2
    decay = jnp.pad(decay, ((0, 0), (0, pad), (0, pad))).reshape(npair, 2 * RET_ROWS, RET_ROWS)
    expand = lambda t: jnp.pad(jnp.repeat(t, HEAD_DIM, axis=1), ((0, pad), (0, 0))).reshape(
        RET_ROWS, npair, LANES).transpose(1, 0, 2)
    q_tab, k_tab = expand(q_dec), expand(k_dec)
    same_head = (np.arange(LANES)[:, None] // HEAD_DIM) == (np.arange(LANES)[None, :] // HEAD_DIM)
    c_tab = jnp.where(same_head[None], jnp.repeat(c_dec, HEAD_DIM).reshape(npair, LANES, 1), 0.0)
    return decay.astype(F32), q_tab.astype(F32), k_tab.astype(F32), c_tab.astype(F32)


def _retention_kernel(q_ref, k_ref, v_ref, g_ref, gn_ref, s0_ref, dec_ref, qd_ref, kd_ref, cd_ref,
                      o_ref, sout_ref, st_ref, *, chunk, nchunk):
    c = pl.program_id(2)

    @pl.when(c == 0)
    def _():
        st_ref[...] = s0_ref[...]

    pad = RET_ROWS - chunk
    lane = lax.broadcasted_iota(jnp.int32, (RET_ROWS, LANES), 1)
    head0 = lane < HEAD_DIM
    same_head = (lax.broadcasted_iota(jnp.int32, (LANES, LANES), 0) // HEAD_DIM) == (
        lax.broadcasted_iota(jnp.int32, (LANES, LANES), 1) // HEAD_DIM)

    def rows(ref, r):
        x = ref[pl.ds(r, chunk), :]
        return x if pad == 0 else jnp.concatenate([x, jnp.zeros((pad, LANES), F32)], axis=0)

    def seg_mean(x):
        s0 = jnp.sum(jnp.where(head0, x, 0.0), axis=-1, keepdims=True)
        s1 = jnp.sum(jnp.where(head0, 0.0, x), axis=-1, keepdims=True)
        return jnp.where(head0, s0, s1) * (1.0 / HEAD_DIM)

    def one_chunk(j, state):
        r = j * chunk
        q, k, v, g = rows(q_ref, r), rows(k_ref, r), rows(v_ref, r), rows(g_ref, r)
        kb, vb = k.astype(BF16), v.astype(BF16)
        q2 = jnp.concatenate([jnp.where(head0, q, 0.0), jnp.where(head0, 0.0, q)], axis=0).astype(BF16)
        attn = lax.dot_general(q2, kb, _NT, preferred_element_type=F32) * dec_ref[...]
        inn = jnp.dot(attn.astype(BF16), vb, preferred_element_type=F32)
        inner = jnp.where(head0, inn[:RET_ROWS], inn[RET_ROWS:])
        cross = jnp.dot(q.astype(BF16), state.astype(BF16), preferred_element_type=F32) * qd_ref[...]
        upd = lax.dot_general((k * kd_ref[...]).astype(BF16), vb, _TN, preferred_element_type=F32)
        o = inner + cross
        mu = seg_mean(o)
        var = seg_mean(jnp.square(o - mu))
        y = (o - mu) * lax.rsqrt(var + EPS) * gn_ref[...]
        y = y * (g * jax.nn.sigmoid(g))
        o_ref[r:r + chunk, :] = y[:chunk]
        return state * cd_ref[...] + jnp.where(same_head, upd, 0.0)

    state = st_ref[...]
    for j in range(nchunk):
        state = one_chunk(j, state)
    st_ref[...] = state

    @pl.when(c == pl.num_programs(2) - 1)
    def _():
        sout_ref[...] = st_ref[...]


def _retention(q, k, v, g, gn_w, s0, batch, seq, chunk, nchunk):
    n, width = q.shape
    n_heads = width // HEAD_DIM
    npair = width // LANES
    rows = chunk * nchunk
    steps = seq // rows
    dec, qd, kd, cd = _retention_tables(n_heads, chunk)
    eye2 = jnp.eye(2, dtype=F32)
    s0p = s0.reshape(batch, npair, 2, HEAD_DIM, HEAD_DIM)
    s0p = jnp.einsum('bpide,ij->bpidje', s0p, eye2).reshape(batch, npair, LANES, LANES)
    row_spec = pl.BlockSpec((rows, LANES), lambda b, p, c: (b * steps + c, p))
    st_spec = pl.BlockSpec((None, None, LANES, LANES), lambda b, p, c: (b, p, 0, 0))
    tab = lambda t: pl.BlockSpec((None,) + t.shape[1:], lambda b, p, c: (p, 0, 0))
    out, s_new = pl.pallas_call(
        functools.partial(_retention_kernel, chunk=chunk, nchunk=nchunk),
        grid=(batch, npair, steps),
        in_specs=[row_spec, row_spec, row_spec, row_spec, pl.BlockSpec((1, LANES), lambda b, p, c: (0, p)),
                  st_spec, tab(dec), tab(qd), tab(kd), tab(cd)],
        out_specs=[row_spec, st_spec],
        out_shape=[jax.ShapeDtypeStruct((n, width), F32),
                   jax.ShapeDtypeStruct((batch, npair, LANES, LANES), F32)],
        scratch_shapes=[pltpu.VMEM((LANES, LANES), F32)],
        compiler_params=_cparams("parallel", "parallel", "arbitrary"),
        name="retention",
    )(q, k, v, g, gn_w.reshape(1, width), s0p, dec, qd, kd, cd)
    s_new = s_new.reshape(batch, npair, 2, HEAD_DIM, 2, HEAD_DIM)
    s_new = jnp.stack([s_new[:, :, 0, :, 0, :], s_new[:, :, 1, :, 1, :]], axis=2)
    return out, s_new.reshape(batch, n_heads, HEAD_DIM, HEAD_DIM)


def _block_diag_queries(q, n_heads):
    t = q.shape[0]
    rep = jnp.concatenate([q] * n_heads, axis=0)
    rows = lax.broadcasted_iota(jnp.int32, rep.shape, 0) // t
    cols = lax.broadcasted_iota(jnp.int32, rep.shape, 1) // HEAD_DIM
    return jnp.where(rows == cols, rep, 0.0)


def _take_diag(o, n_heads, t):
    cols = lax.broadcasted_iota(jnp.int32, (t, o.shape[1]), 1) // HEAD_DIM
    out = jnp.zeros((t, o.shape[1]), F32)
    for h in range(n_heads):
        out = jnp.where(cols == h, o[h * t:(h + 1) * t, :], out)
    return out


def _pad_rows(x, rows):
    return jnp.concatenate([x, jnp.zeros((rows - x.shape[0], x.shape[1]), x.dtype)], axis=0)


MOBA_PAGES_PER_STEP = 32


def _moba_decode_kernel(pt_ref, q_ref, kn_ref, vn_ref, *rest, n_heads, t_new, pages_per_blk, nblk):
    npg = MOBA_PAGES_PER_STEP
    kp, vp = rest[:npg], rest[npg:2 * npg]
    o_ref, qbd_ref, m_ref, l_ref, sc_ref, oblk_ref = rest[2 * npg:]
    g = pl.program_id(1)
    rows = n_heads * t_new
    lane = lax.broadcasted_iota(jnp.int32, (rows, LANES), 1)
    page = kp[0].shape[1]
    blk_keys = pages_per_blk * page
    step_blks = npg // pages_per_blk

    @pl.when(g == 0)
    def _():
        qbd_ref[...] = _scaled_bf16(_block_diag_queries(q_ref[...], n_heads))
        m_ref[...] = jnp.zeros_like(m_ref)
        l_ref[...] = jnp.zeros_like(l_ref)
        sc_ref[...] = jnp.full_like(sc_ref, -jnp.inf)

    kt_all = jnp.concatenate([kp[i][...].astype(BF16) for i in range(npg)], axis=1)
    s_all = jnp.dot(qbd_ref[...], kt_all, preferred_element_type=F32)
    m_new, l_new, sc_new = m_ref[...], l_ref[...], sc_ref[...]
    for jb in range(step_blks):
        n = g * step_blks + jb
        s = s_all[:, jb * blk_keys:(jb + 1) * blk_keys]
        mb = jnp.max(s, axis=-1, keepdims=True)
        p = jnp.exp(s - mb)
        oblk_ref[n] = sum(lax.dot_general(p[:, a * page:(a + 1) * page].astype(BF16),
                                          vp[jb * pages_per_blk + a][...].astype(BF16),
                                          _NT, preferred_element_type=F32) for a in range(pages_per_blk))
        here = lane == n
        m_new = jnp.where(here, mb, m_new)
        l_new = jnp.where(here, jnp.sum(p, axis=-1, keepdims=True), l_new)
        sc_new = jnp.where(here, jnp.sum(s, axis=-1, keepdims=True) * (1.0 / blk_keys), sc_new)
    m_ref[...], l_ref[...], sc_ref[...] = m_new, l_new, sc_new

    @pl.when(g == pl.num_programs(1) - 1)
    def _():
        s = lax.dot_general(qbd_ref[...], _pad_rows(kn_ref[...], LANES).astype(BF16), _NT,
                            preferred_element_type=F32)
        qry = lax.broadcasted_iota(jnp.int32, (rows, LANES), 0) % t_new
        s = jnp.where(lane <= qry, s, NEG_INF)
        m_own = jnp.max(s, axis=-1, keepdims=True)
        p = jnp.exp(s - m_own)
        l_own = jnp.sum(p, axis=-1, keepdims=True)
        o_own = jnp.dot(p.astype(BF16), _pad_rows(vn_ref[...], LANES).astype(BF16), preferred_element_type=F32)
        sc = sc_ref[...]
        lane_f = lane.astype(F32)
        sel = lane < 0
        for _ in range(MOBA_TOPK):
            mx = jnp.max(sc, axis=-1, keepdims=True)
            first = jnp.min(jnp.where(sc == mx, lane_f, float(LANES)), axis=-1, keepdims=True)
            pick = lane_f == first
            sel = sel | pick
            sc = jnp.where(pick, -jnp.inf, sc)
        m_all = m_ref[...]
        m_fin = jnp.maximum(jnp.max(jnp.where(sel, m_all, -jnp.inf), axis=-1, keepdims=True), m_own)
        w = jnp.where(sel, jnp.exp(m_all - m_fin), 0.0)
        w_own = jnp.exp(m_own - m_fin)
        l_fin = jnp.sum(w * l_ref[...], axis=-1, keepdims=True) + w_own * l_own

        def body(n, acc):
            col = jnp.sum(jnp.where(lane == n, w, 0.0), axis=-1, keepdims=True)
            return acc + col * oblk_ref[n]

        acc = lax.fori_loop(0, nblk, body, w_own * o_own)
        o_ref[...] = _take_diag(acc / l_fin, n_heads, t_new)


def _moba_decode(q, k_new, v_new, cache_kt, cache_vt, page_table, t_new):
    n, width = q.shape
    n_heads = width // HEAD_DIM
    n_seq, n_pages = page_table.shape
    page = cache_kt.shape[2]
    pages_per_blk = MOBA_BLOCK // page
    nblk = n_pages // pages_per_blk
    npg = MOBA_PAGES_PER_STEP
    rows = n_heads * t_new
    assert n_pages % npg == 0 and npg % pages_per_blk == 0 and MOBA_TOPK <= nblk <= LANES
    assert n_pages * page == nblk * MOBA_BLOCK and t_new <= page and page == LANES and rows <= LANES
    tok_spec = pl.BlockSpec((t_new, width), lambda b, g, pt: (b, 0))
    page_specs = [pl.BlockSpec((None, width, page), functools.partial(
        lambda b, g, pt, i: (pt[b, g * npg + i], 0, 0), i=i)) for i in range(npg)]
    grid_spec = pltpu.PrefetchScalarGridSpec(
        num_scalar_prefetch=1,
        grid=(n_seq, n_pages // npg),
        in_specs=[tok_spec, tok_spec, tok_spec] + page_specs + page_specs,
        out_specs=tok_spec,
        scratch_shapes=[pltpu.VMEM((rows, width), BF16), pltpu.VMEM((rows, LANES), F32),
                        pltpu.VMEM((rows, LANES), F32), pltpu.VMEM((rows, LANES), F32),
                        pltpu.VMEM((nblk, rows, width), F32)],
    )
    return pl.pallas_call(
        functools.partial(_moba_decode_kernel, n_heads=n_heads, t_new=t_new,
                          pages_per_blk=pages_per_blk, nblk=nblk),
        grid_spec=grid_spec,
        out_shape=jax.ShapeDtypeStruct((n, width), F32),
        compiler_params=_cparams("parallel", "arbitrary"),
        name="moba_decode",
    )(page_table, q, k_new, v_new, *([cache_kt] * npg), *([cache_vt] * npg))


DIL_DEC_TILE = 512


def _dilated_decode_tables(n_heads, t_new, n_prev):
    qi = (np.arange(n_heads * t_new) % t_new)[:, None]
    c_old = _dilated_multiplicity(n_prev + qi - np.arange(n_prev)[None, :])
    c_new = _dilated_multiplicity(qi - np.arange(LANES)[None, :])
    c_new = np.where(np.arange(LANES)[None, :] < t_new, c_new, 0)
    f = lambda c: (np.where(c > 0, 0.0, NEG_INF).astype(np.float32), c.astype(np.float32))
    return f(c_old) + f(c_new)


def _shifted_window_tile(cur_ref, nxt_ref, new_ref, out_ref, tail_ref, is_last, t_new):
    lane = lax.broadcasted_iota(jnp.int32, (cur_ref.shape[0], LANES), 1)
    nblk = cur_ref.shape[1] // LANES

    @pl.when(is_last)
    def _():
        tail_ref[...] = _pad_rows(new_ref[...], LANES).T

    @pl.when(jnp.logical_not(is_last))
    def _():
        tail_ref[...] = nxt_ref[...]

    rolled = [pltpu.roll(cur_ref[:, c * LANES:(c + 1) * LANES], LANES - t_new, 1) for c in range(nblk)]
    rolled.append(pltpu.roll(tail_ref[...], LANES - t_new, 1))
    for c in range(nblk):
        out_ref[:, c * LANES:(c + 1) * LANES] = jnp.where(lane < LANES - t_new, rolled[c], rolled[c + 1])


def _dilated_decode_kernel(q_ref, kn_ref, vn_ref, kc_ref, vc_ref, kx_ref, vx_ref, bo_ref, mo_ref, bn_ref, mn_ref,
                           o_ref, ko_ref, vo_ref, qbd_ref, m_ref, l_ref, acc_ref, tail_ref, *, n_heads, t_new):
    kt = pl.program_id(1)
    is_last = kt == pl.num_programs(1) - 1
    _shifted_window_tile(kc_ref, kx_ref, kn_ref, ko_ref, tail_ref, is_last, t_new)
    _shifted_window_tile(vc_ref, vx_ref, vn_ref, vo_ref, tail_ref, is_last, t_new)

    @pl.when(kt == 0)
    def _():
        qbd_ref[...] = _scaled_bf16(_block_diag_queries(q_ref[...], n_heads))
        m_ref[...] = jnp.full_like(m_ref, -jnp.inf)
        l_ref[...] = jnp.zeros_like(l_ref)
        acc_ref[...] = jnp.zeros_like(acc_ref)

    def step(s, pv, bias, mult):
        s = s + bias
        m_old = m_ref[...]
        m_new = jnp.maximum(m_old, jnp.max(s, axis=-1, keepdims=True))
        alpha = jnp.exp(m_old - m_new)
        p = jnp.exp(s - m_new) * mult
        l_ref[...] = alpha * l_ref[...] + jnp.sum(p, axis=-1, keepdims=True)
        acc_ref[...] = alpha * acc_ref[...] + pv(p.astype(BF16))
        m_ref[...] = m_new

    @pl.when(kt == 0)
    def _():
        k, v = _pad_rows(kn_ref[...], LANES).astype(BF16), _pad_rows(vn_ref[...], LANES).astype(BF16)
        step(lax.dot_general(qbd_ref[...], k, _NT, preferred_element_type=F32),
             lambda p: jnp.dot(p, v, preferred_element_type=F32), bn_ref[...], mn_ref[...])

    step(jnp.dot(qbd_ref[...], kc_ref[...].astype(BF16), preferred_element_type=F32),
         lambda p: lax.dot_general(p, vc_ref[...].astype(BF16), _NT, preferred_element_type=F32),
         bo_ref[...], mo_ref[...])

    @pl.when(kt == pl.num_programs(1) - 1)
    def _():
        o_ref[...] = _take_diag(acc_ref[...] / l_ref[...], n_heads, t_new)


def _dilated_decode(q, k_new, v_new, win_kt, win_vt, t_new):
    n, width = q.shape
    n_heads = width // HEAD_DIM
    n_seq, _, n_prev = win_kt.shape
    tile = DIL_DEC_TILE
    assert n_prev == W_MAX and n_prev % tile == 0 and t_new <= LANES
    rows = n_heads * t_new
    bo, mo, bn, mn = _dilated_decode_tables(n_heads, t_new, n_prev)
    tok_spec = pl.BlockSpec((t_new, width), lambda b, t: (b, 0))
    win_spec = pl.BlockSpec((None, width, tile), lambda b, t: (b, 0, t))
    last_lane_blk = n_prev // LANES - 1
    nxt_spec = pl.BlockSpec((None, width, LANES),
                            lambda b, t: (b, 0, jnp.minimum((t + 1) * (tile // LANES), last_lane_blk)))
    old_tab = pl.BlockSpec((rows, tile), lambda b, t: (0, t))
    new_tab = pl.BlockSpec((rows, LANES), lambda b, t: (0, 0))
    win_shape = jax.ShapeDtypeStruct(win_kt.shape, F32)
    return pl.pallas_call(
        functools.partial(_dilated_decode_kernel, n_heads=n_heads, t_new=t_new),
        grid=(n_seq, n_prev // tile),
        in_specs=[tok_spec, tok_spec, tok_spec, win_spec, win_spec, nxt_spec, nxt_spec,
                  old_tab, old_tab, new_tab, new_tab],
        out_specs=[tok_spec, win_spec, win_spec],
        out_shape=[jax.ShapeDtypeStruct((n, width), F32), win_shape, win_shape],
        scratch_shapes=[pltpu.VMEM((rows, width), BF16), pltpu.VMEM((rows, 1), F32),
                        pltpu.VMEM((rows, 1), F32), pltpu.VMEM((rows, width), F32),
                        pltpu.VMEM((width, LANES), F32)],
        compiler_params=_cparams("parallel", "arbitrary"),
        name="dilated_decode",
    )(q, k_new, v_new, win_kt, win_vt, win_kt, win_vt,
      jnp.asarray(bo), jnp.asarray(mo), jnp.asarray(bn), jnp.asarray(mn))


_AB_SEGS = (("a", 0, 0, 1.0), ("a", 1, 0, 1.0), (None, 2, 0, 1.0), ("b", 3, 0, 1.0),
            ("b", 4, 0, ATT_SCALE), (None, 5, 0, 1.0), (None, 6, 0, 1.0))
_C_SEGS = tuple(("a" if o < 2 else None, o, c, 1.0) for o in range(3) for c in (0, SEG))
_KV_OUT = (1, 2)


def _row_tile(n):
    return 512 if n % 512 == 0 else n


def _feature_major(x):
    lead = x.shape[:-3]
    t, h, dh = x.shape[-3:]
    nl = len(lead)
    return x.transpose(*range(nl), nl + 1, nl + 2, nl).reshape(*lead, h * dh, t)


def _token_major(xt, n_heads):
    lead = xt.shape[:-2]
    t = xt.shape[-1]
    nl = len(lead)
    return xt.reshape(*lead, n_heads, HEAD_DIM, t).transpose(*range(nl), nl + 2, nl, nl + 1)


def kernel(x_prompt, x_sample, cache_k_a, cache_v_a, page_table, state_ret, cache_win_k, cache_win_v,
           norm_mix, norm_ffn, norm_final, w_in_ab, w_out_ab, ret_gn_w, w_in_c, w_out_c,
           ffn_w_gate, ffn_w_up, ffn_w_down):
    bp, tp, d = x_prompt.shape
    bs, ts, _ = x_sample.shape
    page = cache_k_a.shape[2]
    past_len = page_table.shape[1] * page
    h_a = cache_k_a.shape[3]
    wa = h_a * HEAD_DIM
    wb = w_in_ab.shape[2] - 3 * wa
    assert wb == 4 * wa and w_in_ab.shape[0] == 1 and w_in_c.shape[0] == 1 and norm_mix.shape[0] == 2
    h_b = wa // HEAD_DIM
    h_c = w_in_c.shape[2] // (3 * HEAD_DIM)

    pos_p = jnp.arange(tp, dtype=jnp.int32)
    pos_s = past_len + jnp.arange(ts, dtype=jnp.int32)
    tile_s = lambda tabs: tuple(jnp.tile(t, (bs, 1)) for t in tabs)
    tabs = {
        "p": (_rope_tables(pos_p, ROT_DIM, ROPE_THETA), _rope_tables(pos_p, HEAD_DIM, RET_THETA)),
        "s": (tile_s(_rope_tables(pos_s, ROT_DIM, ROPE_THETA)), tile_s(_rope_tables(pos_s, HEAD_DIM, RET_THETA))),
    }
    bf = lambda w: w.astype(BF16)
    w_in_ab_bf, w_out_ab_bf, w_in_c_bf, w_out_c_bf = bf(w_in_ab[0]), bf(w_out_ab[0]), bf(w_in_c[0]), bf(w_out_c[0])
    wg, wu, wd = bf(ffn_w_gate), bf(ffn_w_up), bf(ffn_w_down)

    xp = x_prompt.reshape(bp * tp, d)
    xs = x_sample.reshape(bs * ts, d)
    tm_p, tm_s = _row_tile(bp * tp), _row_tile(bs * ts)
    assert tp % tm_p == 0

    ab_widths = (wa,) * 7
    qa_p, kat_p, vat_p, qb_p, kb_p, vb_p, gb_p = _project(
        xp, norm_mix[0], w_in_ab_bf, *tabs["p"], _AB_SEGS, ab_widths, tm_p, _KV_OUT)
    qa_s, ka_s, va_s, qb_s, kb_s, vb_s, gb_s = _project(
        xs, norm_mix[0], w_in_ab_bf, *tabs["s"], _AB_SEGS, ab_widths, tm_s)

    oa_p = _moba_prompt(qa_p, kat_p, vat_p, bp, tp)
    oa_s = _moba_decode(qa_s, ka_s, va_s, _feature_major(cache_k_a[0]), _feature_major(cache_v_a[0]),
                        page_table, ts)

    chunk_p = math.gcd(tp, RET_CHUNK)
    ob_p, ret_p = _retention(qb_p, kb_p, vb_p, gb_p, ret_gn_w[0],
                             jnp.zeros((bp, h_b, HEAD_DIM, HEAD_DIM), F32), bp, tp, chunk_p,
                             math.gcd(tp // chunk_p, 8))
    ob_s, ret_s = _retention(qb_s, kb_s, vb_s, gb_s, ret_gn_w[0], state_ret[0], bs, ts,
                             math.gcd(ts, RET_CHUNK), 1)

    xp = _mix_ffn([oa_p, ob_p], w_out_ab_bf, xp, norm_ffn[0], wg[0], wu[0], wd[0], norm_final, False, tm_p)
    xs = _mix_ffn([oa_s, ob_s], w_out_ab_bf, xs, norm_ffn[0], wg[0], wu[0], wd[0], norm_final, False, tm_s)

    wc = h_c * HEAD_DIM
    q_p, kt_p, vt_p = _project(xp, norm_mix[1], w_in_c_bf, *tabs["p"], _C_SEGS, (wc,) * 3, tm_p, _KV_OUT)
    q_s, k_s, v_s = _project(xs, norm_mix[1], w_in_c_bf, *tabs["s"], _C_SEGS, (wc,) * 3, tm_s)
    n_prev = cache_win_k.shape[2]
    o_p = _dilated_prompt(q_p, kt_p, vt_p, bp, tp)
    o_s, win_kt, win_vt = _dilated_decode(q_s, k_s, v_s, _feature_major(cache_win_k[0]),
                                          _feature_major(cache_win_v[0]), ts)
    y_p = _mix_ffn([o_p], w_out_c_bf, xp, norm_ffn[1], wg[1], wu[1], wd[1], norm_final, True, tm_p)
    y_s = _mix_ffn([o_s], w_out_c_bf, xs, norm_ffn[1], wg[1], wu[1], wd[1], norm_final, True, tm_s)

    keep_p = min(W_MAX, tp)
    return (
        y_p.reshape(bp, tp, d), y_s.reshape(bs, ts, d),
        _token_major(kat_p, h_a)[None], _token_major(vat_p, h_a)[None], ret_p[None],
        _token_major(kt_p[:, :, tp - keep_p:], h_c)[None], _token_major(vt_p[:, :, tp - keep_p:], h_c)[None],
        ka_s.reshape(1, bs, ts, h_a, HEAD_DIM), va_s.reshape(1, bs, ts, h_a, HEAD_DIM), ret_s[None],
        _token_major(win_kt, h_c)[None], _token_major(win_vt, h_c)[None],
    )
```

```python
import functools
import math

import jax
import jax.numpy as jnp
import numpy as np
from jax import lax
from jax.experimental import pallas as pl
from jax.experimental.pallas import tpu as pltpu

F32 = jnp.float32
BF16 = jnp.bfloat16

HEAD_DIM = 64
LANES = 128
ROT_DIM = HEAD_DIM // 4
ROPE_THETA = 500000.0
RET_THETA = 10000.0
MOBA_BLOCK = 256
MOBA_TOPK = 3
RET_CHUNK = 128
DILATED = ((128, 1), (512, 4), (2048, 16))
W_MAX = max(w for w, _ in DILATED)
EPS = 1e-6
NEG_INF = -1e30
ATT_SCALE = HEAD_DIM ** -0.5
VMEM_LIMIT = 56 * 1024 * 1024

_NT = (((1,), (1,)), ((), ()))
_TN = (((0,), (0,)), ((), ()))


def _cparams(*sem):
    return pltpu.CompilerParams(dimension_semantics=sem, vmem_limit_bytes=VMEM_LIMIT)


def _rms(x, w):
    ms = jnp.mean(x * x, axis=-1, keepdims=True)
    return x * lax.rsqrt(ms + EPS) * w


SEG = 512


def _proj_kernel(x_ref, nw_ref, w_ref, ca_ref, na_ref, pa_ref, cb_ref, nb_ref, pb_ref, *out_refs, segs, transposed):
    xn = _rms(x_ref[...], nw_ref[...]).astype(BF16)
    for s, (kind, dests, col, post) in enumerate(segs):
        acc = jnp.dot(xn, w_ref[:, s * SEG:(s + 1) * SEG], preferred_element_type=F32)
        for c in range(SEG // LANES):
            r = acc[:, c * LANES:(c + 1) * LANES]
            if kind is not None:
                c_ref, n_ref, p_ref, shift = (ca_ref, na_ref, pa_ref, ROT_DIM // 2) if kind == "a" else (
                    cb_ref, nb_ref, pb_ref, HEAD_DIM // 2)
                r = (r * c_ref[...] + pltpu.roll(r, LANES - shift, 1) * n_ref[...]
                     + pltpu.roll(r, shift, 1) * p_ref[...])
                if post != 1.0:
                    r = r * post
            lo = col + c * LANES
            for oi in dests:
                if oi in transposed:
                    out_refs[oi][lo:lo + LANES, :] = r.T
                else:
                    out_refs[oi][:, lo:lo + LANES] = r


def _rope_tables(pos, rot_dim, theta):
    half = rot_dim // 2
    inv = theta ** (-jnp.arange(half, dtype=F32) / half)
    ang = pos.astype(F32)[:, None] * inv[None, :]
    cos, sin = jnp.cos(ang), jnp.sin(ang)
    lane = np.arange(LANES) % HEAD_DIM
    idx = lane % half
    cos_t = jnp.where(lane < rot_dim, cos[:, idx], 1.0)
    neg_t = jnp.where(lane < half, -sin[:, idx], 0.0)
    pos_t = jnp.where((lane >= half) & (lane < rot_dim), sin[:, idx], 0.0)
    return cos_t.astype(F32), neg_t.astype(F32), pos_t.astype(F32)


def _project(x, nw, w_bf, tabs_a, tabs_b, segs, out_widths, tm, transposed=()):
    n, d = x.shape
    t_tab = tabs_a[0].shape[0]
    nt = t_tab // tm
    tab_spec = pl.BlockSpec((tm, LANES), lambda i: (i % nt, 0))
    out_specs, out_shape = [], []
    for oi, w in enumerate(out_widths):
        if oi in transposed:
            out_specs.append(pl.BlockSpec((None, w, tm), lambda i: (i // nt, 0, i % nt)))
            out_shape.append(jax.ShapeDtypeStruct((n // t_tab, w, t_tab), F32))
        else:
            out_specs.append(pl.BlockSpec((tm, w), lambda i: (i, 0)))
            out_shape.append(jax.ShapeDtypeStruct((n, w), F32))
    return pl.pallas_call(
        functools.partial(_proj_kernel, segs=segs, transposed=tuple(transposed)),
        grid=(n // tm,),
        in_specs=[pl.BlockSpec((tm, d), lambda i: (i, 0)),
                  pl.BlockSpec((1, d), lambda i: (0, 0)),
                  pl.BlockSpec(w_bf.shape, lambda i: (0, 0))] + [tab_spec] * 6,
        out_specs=out_specs,
        out_shape=out_shape,
        compiler_params=_cparams("parallel"),
        name="rms_proj_rope",
    )(x, nw.reshape(1, d), w_bf, *tabs_a, *tabs_b)


FFN_CHUNKS = 2


def _mix_ffn_kernel(*refs, nparts, final):
    parts = refs[:nparts]
    wo_ref, x_ref, nw_ref, wg_ref, wu_ref, wd_ref, fw_ref, o_ref = refs[nparts:]
    a = jnp.concatenate([p[...].astype(BF16) for p in parts], axis=1) if nparts > 1 else parts[0][...].astype(BF16)
    x = x_ref[...] + jnp.dot(a, wo_ref[...], preferred_element_type=F32)
    xn = _rms(x, nw_ref[...]).astype(BF16)
    tf = wg_ref.shape[1] // FFN_CHUNKS
    acc = x
    for c in range(FFN_CHUNKS):
        g = jnp.dot(xn, wg_ref[:, c * tf:(c + 1) * tf], preferred_element_type=F32)
        u = jnp.dot(xn, wu_ref[:, c * tf:(c + 1) * tf], preferred_element_type=F32)
        h = (g * jax.nn.sigmoid(g) * u).astype(BF16)
        acc = acc + jnp.dot(h, wd_ref[c * tf:(c + 1) * tf, :], preferred_element_type=F32)
    if final:
        acc = _rms(acc, fw_ref[...])
    o_ref[...] = acc


def _mix_ffn(parts, wo, x, nw, wg, wu, wd, fw, final, tm):
    n, d = x.shape
    dff = wg.shape[1]
    const = lambda i: (0, 0)
    row = lambda w: pl.BlockSpec((tm, w), lambda i: (i, 0))
    return pl.pallas_call(
        functools.partial(_mix_ffn_kernel, nparts=len(parts), final=final),
        grid=(n // tm,),
        in_specs=[row(p.shape[1]) for p in parts]
        + [pl.BlockSpec(wo.shape, const), row(d), pl.BlockSpec((1, d), const),
           pl.BlockSpec((d, dff), const), pl.BlockSpec((d, dff), const), pl.BlockSpec((dff, d), const),
           pl.BlockSpec((1, d), const)],
        out_specs=row(d),
        out_shape=jax.ShapeDtypeStruct((n, d), F32),
        compiler_params=_cparams("parallel"),
        name="outproj_swiglu",
    )(*parts, wo, x, nw.reshape(1, d), wg, wu, wd, fw.reshape(1, d))


ATT_TILE = 256
ATT_GROUP = 4
MOBA_SUBTILES = 2
ATT_CHUNK = 64


def _unit_scores(kb, qb, bias_fn):
    s = jnp.dot(kb, qb, preferred_element_type=F32)
    chunks = [s[r:r + ATT_CHUNK] + bias_fn(r, r + ATT_CHUNK) for r in range(0, s.shape[0], ATT_CHUNK)]
    return chunks, jnp.max(functools.reduce(jnp.maximum, chunks), axis=0, keepdims=True)


def _unit_values(chunks, m_u, vts, mult_fn):
    ps, tot = [], None
    for i, c in enumerate(chunks):
        p = jnp.exp(c - m_u)
        if mult_fn is not None:
            p = p * mult_fn(i * ATT_CHUNK, (i + 1) * ATT_CHUNK)
        tot = p if tot is None else tot + p
        ps.append(p.astype(BF16))
    per = len(ps) // len(vts)
    pv = sum(jnp.dot(vt, jnp.concatenate(ps[b * per:(b + 1) * per], axis=0), preferred_element_type=F32)
             for b, vt in enumerate(vts))
    return jnp.sum(tot, axis=0, keepdims=True), pv


def _attend_unit(kb, q2, bias_fn, vts, mult_fn=None):
    chunks, m_u = _unit_scores(kb, q2, bias_fn)
    return (m_u,) + _unit_values(chunks, m_u, vts, mult_fn)


def _merge(parts):
    m = functools.reduce(jnp.maximum, [p[0] for p in parts])
    ws = [jnp.exp(p[0] - m) for p in parts]
    return m, sum(w * p[1] for w, p in zip(ws, parts)), sum(w * p[2] for w, p in zip(ws, parts))


def _head_queries(q_t, feat):
    return [jnp.where(feat // HEAD_DIM == hh, q_t, 0.0) for hh in range(2)]


def _scaled_bf16(q):
    return (q * ATT_SCALE).astype(BF16)


def _both(x):
    return jnp.concatenate([x, x], axis=1)


def _pick_heads(x, feat):
    nq = x.shape[1] // 2
    return jnp.where(feat // HEAD_DIM == 0, x[:, :nq], x[:, nq:])


def _moba_prompt_kernel(q_ref, kt_ref, vt_ref, o_ref, kb_ref, vb_ref, kmean_ref, bias_ref, causal_ref, *, nblk):
    blk, grp = ATT_TILE, ATT_GROUP
    nsub = MOBA_SUBTILES
    step = pl.program_id(2)

    @pl.when(step == 0)
    def _():
        for n in range(nblk):
            kblk = kt_ref[:, n * blk:(n + 1) * blk].T
            kmean_ref[n:n + 1, :] = jnp.sum(kblk, axis=0, keepdims=True) * (1.0 / blk)
            kb_ref[n * blk:(n + 1) * blk, :] = kblk.astype(BF16)
            vb_ref[n] = vt_ref[:, n * blk:(n + 1) * blk].astype(BF16)
        key_i = lax.broadcasted_iota(jnp.int32, (blk, 2 * blk), 0)
        qry_i = lax.broadcasted_iota(jnp.int32, (blk, 2 * blk), 1) % blk
        causal_ref[...] = jnp.where(key_i <= qry_i, 0.0, NEG_INF)

    feat = lax.broadcasted_iota(jnp.int32, (LANES, blk), 0)
    blk_id = lax.broadcasted_iota(jnp.int32, (nblk, 2 * blk), 0)
    blk_f = blk_id.astype(F32)
    qb2s = []
    for sub in range(nsub):
        qt = step * nsub + sub
        q2 = jnp.concatenate(_head_queries(q_ref[sub * blk:(sub + 1) * blk, :].T, feat), axis=1)
        sc = jnp.dot(kmean_ref[...], q2, preferred_element_type=F32, precision=lax.Precision.HIGHEST)
        sc = jnp.where(blk_id < qt, sc, NEG_INF)
        sel = jnp.zeros((nblk, 2 * blk), F32)
        for r in range(MOBA_TOPK):
            mx = jnp.max(sc, axis=0, keepdims=True)
            first = jnp.min(jnp.where(sc == mx, blk_f, float(nblk)), axis=0, keepdims=True)
            pick = blk_f == first
            sel = jnp.maximum(sel, jnp.where(pick, jnp.where(r < qt, 1.0, 0.0), 0.0))
            sc = jnp.where(pick, -jnp.inf, sc)
        bias_ref[sub] = jnp.where((sel > 0.0) | (blk_id == qt), 0.0, NEG_INF)
        qb2s.append(_scaled_bf16(q2))

    def group(sub, g, with_own):
        def bias_fn(r0, r1):
            n, r = g * grp + r0 // blk, r0 % blk
            row = bias_ref[sub, pl.ds(n, 1), :]
            if not with_own:
                return row
            return row + jnp.where(n == step * nsub + sub, causal_ref[r:r + (r1 - r0), :], 0.0)

        kb = kb_ref[pl.ds(pl.multiple_of(g * grp * blk, blk), grp * blk), :]
        return _attend_unit(kb, qb2s[sub], bias_fn, [vb_ref[g * grp + b] for b in range(grp)])

    g_own = step * nsub // grp

    def body(g, carry):
        return sum((_merge([carry[3 * sub:3 * sub + 3], group(sub, g, False)]) for sub in range(nsub)), ())

    carry = lax.fori_loop(0, g_own, body, sum((group(sub, g_own, True) for sub in range(nsub)), ()))
    for sub in range(nsub):
        o_ref[sub * blk:(sub + 1) * blk, :] = _pick_heads(carry[3 * sub + 2] / carry[3 * sub + 1], feat).T


def _moba_prompt(q, kt, vt, batch, seq):
    n, width = q.shape
    blk, grp, nsub = ATT_TILE, ATT_GROUP, MOBA_SUBTILES
    assert blk == MOBA_BLOCK and seq % (blk * grp) == 0 and grp % nsub == 0
    nblk = seq // blk
    steps = nblk // nsub
    npair = width // LANES
    kv_spec = pl.BlockSpec((None, LANES, seq), lambda b, p, t: (b, p, 0))
    q_spec = pl.BlockSpec((nsub * blk, LANES), lambda b, p, t: (b * steps + t, p))
    return pl.pallas_call(
        functools.partial(_moba_prompt_kernel, nblk=nblk),
        grid=(batch, npair, steps),
        in_specs=[q_spec, kv_spec, kv_spec],
        out_specs=q_spec,
        out_shape=jax.ShapeDtypeStruct((n, width), F32),
        scratch_shapes=[pltpu.VMEM((nblk * blk, LANES), BF16), pltpu.VMEM((nblk, LANES, blk), BF16),
                        pltpu.VMEM((nblk, LANES), F32), pltpu.VMEM((nsub, nblk, 2 * blk), F32),
                        pltpu.VMEM((blk, 2 * blk), F32)],
        compiler_params=_cparams("parallel", "parallel", "arbitrary"),
        name="moba_prompt",
    )(q, kt, vt)


def _dilated_multiplicity(delta):
    c = np.zeros(delta.shape, np.int32)
    for window, dil in DILATED:
        c += ((delta >= 0) & (delta <= window) & (delta % dil == 0)).astype(np.int32)
    return c


DIL_NEAR = 2
DIL_NEAR_SUBTILES = 4
DIL_FAR_STRIDE = DILATED[-1][1]
DIL_FAR_REACH = DILATED[-1][0] // DIL_FAR_STRIDE
assert DILATED[-2][0] <= ATT_TILE * DIL_NEAR and ATT_TILE % DIL_FAR_STRIDE == 0 and DIL_FAR_REACH <= ATT_TILE


def _mask_bias(valid):
    return np.where(valid, 0.0, NEG_INF).astype(np.float32)


def _dilated_near_tables():
    ki = np.arange(ATT_TILE)[:, None]
    qi = np.arange(ATT_TILE)[None, :]
    c = np.stack([_dilated_multiplicity(ATT_TILE * (DIL_NEAR - i) + qi - ki) for i in range(DIL_NEAR + 1)])
    return _mask_bias(c > 0), c.astype(np.float32)


def _dilated_far_tables():
    per_tile = ATT_TILE // DIL_FAR_STRIDE

    def table(k0, nk, q0):
        ka = k0 + np.arange(nk)[:, None]
        qa = q0 + np.arange(ATT_TILE)[None, :]
        return _mask_bias((qa - ka <= DIL_FAR_REACH) & (ka // per_tile < qa // per_tile - DIL_NEAR))

    return table(0, ATT_TILE, 0), table(ATT_TILE - DIL_FAR_REACH, ATT_TILE + DIL_FAR_REACH, ATT_TILE)


def _dilated_far_kernel(q_ref, k_ref, v_ref, b0_ref, b1_ref, o_ref, lse_ref, *, seq):
    stride, blk, reach = DIL_FAR_STRIDE, ATT_TILE, DIL_FAR_REACH
    ntile = seq // stride // blk
    feat = lax.broadcasted_iota(jnp.int32, (LANES, blk), 0)

    def one_class(cls, carry):
        for tau in range(ntile):
            k0, nk, tab = (0, blk, b0_ref) if tau == 0 else (tau * blk - reach, blk + reach, b1_ref)
            rows_q = pl.ds(cls + stride * blk * tau, blk, stride=stride)
            rows_k = pl.ds(cls + stride * k0, nk, stride=stride)
            qb2 = _scaled_bf16(jnp.concatenate(_head_queries(q_ref[rows_q, :].T, feat), axis=1))
            kb = k_ref[rows_k, :].astype(BF16)
            vt = v_ref[rows_k, :].T.astype(BF16)
            m_u, l_u, pv = _attend_unit(kb, qb2, lambda r0, r1: _both(tab[r0:r1, :]), [vt])
            o_ref[rows_q, :] = _pick_heads(pv / l_u, feat).T
            lse_ref[rows_q, :] = _pick_heads(jnp.broadcast_to(m_u + jnp.log(l_u), pv.shape), feat).T
        return carry

    lax.fori_loop(0, stride, one_class, 0)


def _dilated_near_kernel(q_ref, kt_ref, vt_ref, far_ref, lse_ref, bias_ref, mult_ref, o_ref,
                         kb_ref, vb_ref, tab_ref, *, nblk):
    blk, near, nsub = ATT_TILE, DIL_NEAR, DIL_NEAR_SUBTILES
    step = pl.program_id(2)

    @pl.when(step == 0)
    def _():
        kb_ref[0:near * blk, :] = jnp.zeros((near * blk, LANES), BF16)
        for n in range(near):
            vb_ref[n] = jnp.zeros((LANES, blk), BF16)
        for n in range(nblk):
            kb_ref[(n + near) * blk:(n + near + 1) * blk, :] = kt_ref[:, n * blk:(n + 1) * blk].T.astype(BF16)
            vb_ref[n + near] = vt_ref[:, n * blk:(n + 1) * blk].astype(BF16)

    @pl.when(step * nsub < near + nsub)
    def _():
        for sub in range(nsub):
            for i in range(near + 1):
                tab_ref[sub, i] = jnp.where(step * nsub + sub - near + i >= 0, bias_ref[i], NEG_INF)

    feat = lax.broadcasted_iota(jnp.int32, (LANES, blk), 0)
    for sub in range(nsub):
        qt = step * nsub + sub
        rows = slice(sub * blk, (sub + 1) * blk)
        qb2 = _scaled_bf16(jnp.concatenate(_head_queries(q_ref[rows, :].T, feat), axis=1))
        m2, l2, a2 = _attend_unit(
            kb_ref[pl.ds(pl.multiple_of(qt * blk, blk), (near + 1) * blk), :], qb2,
            lambda r0, r1: _both(tab_ref[sub, r0 // blk, r0 % blk:r0 % blk + (r1 - r0), :]),
            [vb_ref[qt + i] for i in range(near + 1)],
            lambda r0, r1: _both(mult_ref[r0 // blk, r0 % blk:r0 % blk + (r1 - r0), :]))
        m_n = _pick_heads(jnp.broadcast_to(m2, a2.shape), feat)
        l_n = _pick_heads(jnp.broadcast_to(l2, a2.shape), feat)
        a_n = _pick_heads(a2, feat)
        lse_f = lse_ref[rows, :].T
        m = jnp.maximum(m_n, lse_f)
        w_n, w_f = jnp.exp(m_n - m), jnp.exp(lse_f - m)
        o_ref[rows, :] = ((a_n * w_n + far_ref[rows, :].T * w_f) / (l_n * w_n + w_f)).T


def _dilated_prompt(q, k, v, kt, vt, batch, seq):
    n, width = q.shape
    blk, near, stride = ATT_TILE, DIL_NEAR, DIL_FAR_STRIDE
    assert seq % (blk * stride) == 0
    nblk = seq // blk
    npair = width // LANES
    const = lambda nd: (lambda *_: (0,) * nd)
    full = lambda t: pl.BlockSpec(t.shape, const(t.ndim))

    far_tabs = [jnp.asarray(t) for t in _dilated_far_tables()]
    seq_spec = pl.BlockSpec((seq, LANES), lambda b, p: (b, p))
    o_far, lse_far = pl.pallas_call(
        functools.partial(_dilated_far_kernel, seq=seq),
        grid=(batch, npair),
        in_specs=[seq_spec, seq_spec, seq_spec] + [full(t) for t in far_tabs],
        out_specs=[seq_spec, seq_spec],
        out_shape=[jax.ShapeDtypeStruct((n, width), F32)] * 2,
        compiler_params=_cparams("parallel", "parallel"),
        name="dilated_far",
    )(q, k, v, *far_tabs)

    near_tabs = [jnp.asarray(t) for t in _dilated_near_tables()]
    kv_spec = pl.BlockSpec((None, LANES, seq), lambda b, p, t: (b, p, 0))
    nsub = DIL_NEAR_SUBTILES
    assert nblk % nsub == 0
    steps = nblk // nsub
    q_spec = pl.BlockSpec((nsub * blk, LANES), lambda b, p, t: (b * steps + t, p))
    return pl.pallas_call(
        functools.partial(_dilated_near_kernel, nblk=nblk),
        grid=(batch, npair, steps),
        in_specs=[q_spec, kv_spec, kv_spec, q_spec, q_spec] + [full(t) for t in near_tabs],
        out_specs=q_spec,
        out_shape=jax.ShapeDtypeStruct((n, width), F32),
        scratch_shapes=[pltpu.VMEM(((nblk + near) * blk, LANES), BF16), pltpu.VMEM((nblk + near, LANES, blk), BF16),
                        pltpu.VMEM((nsub, near + 1, blk, blk), F32)],
        compiler_params=_cparams("parallel", "parallel", "arbitrary"),
        name="dilated_near",
    )(q, kt, vt, o_far, lse_far, *near_tabs)


RET_ROWS = 128


def _retention_tables(n_heads, chunk):
    log_g = jnp.log1p(-jnp.exp2(-5.0 - jnp.arange(n_heads, dtype=F32)))
    i = jnp.arange(chunk, dtype=F32)
    diff = i[:, None] - i[None, :]
    causal = diff >= 0
    decay = jnp.where(causal[None], jnp.exp(jnp.where(causal, diff, 0.0)[None] * log_g[:, None, None]), 0.0)
    q_dec = jnp.exp((i + 1.0)[:, None] * log_g[None, :])
    k_dec = jnp.exp((chunk - 1.0 - i)[:, None] * log_g[None, :])
    c_dec = jnp.exp(chunk * log_g)
    pad = RET_ROWS - chunk
    npair = n_heads // 2
    decay = jnp.pad(decay, ((0, 0), (0, pad), (0, pad))).reshape(npair, 2 * RET_ROWS, RET_ROWS)
    expand = lambda t: jnp.pad(jnp.repeat(t, HEAD_DIM, axis=1), ((0, pad), (0, 0))).reshape(
        RET_ROWS, npair, LANES).transpose(1, 0, 2)
    q_tab, k_tab = expand(q_dec), expand(k_dec)
    same_head = (np.arange(LANES)[:, None] // HEAD_DIM) == (np.arange(LANES)[None, :] // HEAD_DIM)
    c_tab = jnp.where(same_head[None], jnp.repeat(c_dec, HEAD_DIM).reshape(npair, LANES, 1), 0.0)
    return decay.astype(F32), q_tab.astype(F32), k_tab.astype(F32), c_tab.astype(F32)


def _retention_kernel(q_ref, k_ref, v_ref, g_ref, gn_ref, s0_ref, dec_ref, qd_ref, kd_ref, cd_ref,
                      o_ref, sout_ref, st_ref, *, chunk, nchunk):
    c = pl.program_id(2)

    @pl.when(c == 0)
    def _():
        st_ref[...] = s0_ref[...]

    pad = RET_ROWS - chunk
    lane = lax.broadcasted_iota(jnp.int32, (RET_ROWS, LANES), 1)
    head0 = lane < HEAD_DIM
    same_head = (lax.broadcasted_iota(jnp.int32, (LANES, LANES), 0) // HEAD_DIM) == (
        lax.broadcasted_iota(jnp.int32, (LANES, LANES), 1) // HEAD_DIM)

    def rows(ref, r):
        x = ref[pl.ds(r, chunk), :]
        return x if pad == 0 else jnp.concatenate([x, jnp.zeros((pad, LANES), F32)], axis=0)

    def seg_mean(x):
        s0 = jnp.sum(jnp.where(head0, x, 0.0), axis=-1, keepdims=True)
        s1 = jnp.sum(jnp.where(head0, 0.0, x), axis=-1, keepdims=True)
        return jnp.where(head0, s0, s1) * (1.0 / HEAD_DIM)

    def one_chunk(j, state):
        r = j * chunk
        q, k, v, g = rows(q_ref, r), rows(k_ref, r), rows(v_ref, r), rows(g_ref, r)
        kb, vb = k.astype(BF16), v.astype(BF16)
        q2 = jnp.concatenate([jnp.where(head0, q, 0.0), jnp.where(head0, 0.0, q)], axis=0).astype(BF16)
        attn = lax.dot_general(q2, kb, _NT, preferred_element_type=F32) * dec_ref[...]
        inn = jnp.dot(attn.astype(BF16), vb, preferred_element_type=F32)
        inner = jnp.where(head0, inn[:RET_ROWS], inn[RET_ROWS:])
        cross = jnp.dot(q.astype(BF16), state.astype(BF16), preferred_element_type=F32) * qd_ref[...]
        upd = lax.dot_general((k * kd_ref[...]).astype(BF16), vb, _TN, preferred_element_type=F32)
        o = inner + cross
        mu = seg_mean(o)
        var = seg_mean(jnp.square(o - mu))
        y = (o - mu) * lax.rsqrt(var + EPS) * gn_ref[...]
        y = y * (g * jax.nn.sigmoid(g))
        o_ref[r:r + chunk, :] = y[:chunk]
        return state * cd_ref[...] + jnp.where(same_head, upd, 0.0)

    state = st_ref[...]
    for j in range(nchunk):
        state = one_chunk(j, state)
    st_ref[...] = state

    @pl.when(c == pl.num_programs(2) - 1)
    def _():
        sout_ref[...] = st_ref[...]


def _retention(q, k, v, g, gn_w, s0, batch, seq, chunk, nchunk):
    n, width = q.shape
    n_heads = width // HEAD_DIM
    npair = width // LANES
    rows = chunk * nchunk
    steps = seq // rows
    dec, qd, kd, cd = _retention_tables(n_heads, chunk)
    eye2 = jnp.eye(2, dtype=F32)
    s0p = s0.reshape(batch, npair, 2, HEAD_DIM, HEAD_DIM)
    s0p = jnp.einsum('bpide,ij->bpidje', s0p, eye2).reshape(batch, npair, LANES, LANES)
    row_spec = pl.BlockSpec((rows, LANES), lambda b, p, c: (b * steps + c, p))
    st_spec = pl.BlockSpec((None, None, LANES, LANES), lambda b, p, c: (b, p, 0, 0))
    tab = lambda t: pl.BlockSpec((None,) + t.shape[1:], lambda b, p, c: (p, 0, 0))
    out, s_new = pl.pallas_call(
        functools.partial(_retention_kernel, chunk=chunk, nchunk=nchunk),
        grid=(batch, npair, steps),
        in_specs=[row_spec, row_spec, row_spec, row_spec, pl.BlockSpec((1, LANES), lambda b, p, c: (0, p)),
                  st_spec, tab(dec), tab(qd), tab(kd), tab(cd)],
        out_specs=[row_spec, st_spec],
        out_shape=[jax.ShapeDtypeStruct((n, width), F32),
                   jax.ShapeDtypeStruct((batch, npair, LANES, LANES), F32)],
        scratch_shapes=[pltpu.VMEM((LANES, LANES), F32)],
        compiler_params=_cparams("parallel", "parallel", "arbitrary"),
        name="retention",
    )(q, k, v, g, gn_w.reshape(1, width), s0p, dec, qd, kd, cd)
    s_new = s_new.reshape(batch, npair, 2, HEAD_DIM, 2, HEAD_DIM)
    s_new = jnp.stack([s_new[:, :, 0, :, 0, :], s_new[:, :, 1, :, 1, :]], axis=2)
    return out, s_new.reshape(batch, n_heads, HEAD_DIM, HEAD_DIM)


def _block_diag_queries(q, n_heads):
    t = q.shape[0]
    rep = jnp.concatenate([q] * n_heads, axis=0)
    rows = lax.broadcasted_iota(jnp.int32, rep.shape, 0) // t
    cols = lax.broadcasted_iota(jnp.int32, rep.shape, 1) // HEAD_DIM
    return jnp.where(rows == cols, rep, 0.0)


def _take_diag(o, n_heads, t):
    cols = lax.broadcasted_iota(jnp.int32, (t, o.shape[1]), 1) // HEAD_DIM
    out = jnp.zeros((t, o.shape[1]), F32)
    for h in range(n_heads):
        out = jnp.where(cols == h, o[h * t:(h + 1) * t, :], out)
    return out


def _pad_rows(x, rows):
    return jnp.concatenate([x, jnp.zeros((rows - x.shape[0], x.shape[1]), x.dtype)], axis=0)


MOBA_PAGES_PER_STEP = 32


def _moba_decode_kernel(pt_ref, q_ref, kn_ref, vn_ref, *rest, n_heads, t_new, pages_per_blk, nblk):
    npg = MOBA_PAGES_PER_STEP
    kp, vp = rest[:npg], rest[npg:2 * npg]
    o_ref, qbd_ref, m_ref, l_ref, sc_ref, oblk_ref = rest[2 * npg:]
    g = pl.program_id(1)
    rows = n_heads * t_new
    lane = lax.broadcasted_iota(jnp.int32, (rows, LANES), 1)
    page = kp[0].shape[1]
    blk_keys = pages_per_blk * page
    step_blks = npg // pages_per_blk

    @pl.when(g == 0)
    def _():
        qbd_ref[...] = _scaled_bf16(_block_diag_queries(q_ref[...], n_heads))
        m_ref[...] = jnp.zeros_like(m_ref)
        l_ref[...] = jnp.zeros_like(l_ref)
        sc_ref[...] = jnp.full_like(sc_ref, -jnp.inf)

    kt_all = jnp.concatenate([kp[i][...].astype(BF16) for i in range(npg)], axis=1)
    s_all = jnp.dot(qbd_ref[...], kt_all, preferred_element_type=F32)
    m_new, l_new, sc_new = m_ref[...], l_ref[...], sc_ref[...]
    for jb in range(step_blks):
        n = g * step_blks + jb
        s = s_all[:, jb * blk_keys:(jb + 1) * blk_keys]
        mb = jnp.max(s, axis=-1, keepdims=True)
        p = jnp.exp(s - mb)
        oblk_ref[n] = sum(lax.dot_general(p[:, a * page:(a + 1) * page].astype(BF16),
                                          vp[jb * pages_per_blk + a][...].astype(BF16),
                                          _NT, preferred_element_type=F32) for a in range(pages_per_blk))
        here = lane == n
        m_new = jnp.where(here, mb, m_new)
        l_new = jnp.where(here, jnp.sum(p, axis=-1, keepdims=True), l_new)
        sc_new = jnp.where(here, jnp.sum(s, axis=-1, keepdims=True) * (1.0 / blk_keys), sc_new)
    m_ref[...], l_ref[...], sc_ref[...] = m_new, l_new, sc_new

    @pl.when(g == pl.num_programs(1) - 1)
    def _():
        s = lax.dot_general(qbd_ref[...], _pad_rows(kn_ref[...], LANES).astype(BF16), _NT,
                            preferred_element_type=F32)
        qry = lax.broadcasted_iota(jnp.int32, (rows, LANES), 0) % t_new
        s = jnp.where(lane <= qry, s, NEG_INF)
        m_own = jnp.max(s, axis=-1, keepdims=True)
        p = jnp.exp(s - m_own)
        l_own = jnp.sum(p, axis=-1, keepdims=True)
        o_own = jnp.dot(p.astype(BF16), _pad_rows(vn_ref[...], LANES).astype(BF16), preferred_element_type=F32)
        sc = sc_ref[...]
        lane_f = lane.astype(F32)
        sel = lane < 0
        for _ in range(MOBA_TOPK):
            mx = jnp.max(sc, axis=-1, keepdims=True)
            first = jnp.min(jnp.where(sc == mx, lane_f, float(LANES)), axis=-1, keepdims=True)
            pick = lane_f == first
            sel = sel | pick
            sc = jnp.where(pick, -jnp.inf, sc)
        m_all = m_ref[...]
        m_fin = jnp.maximum(jnp.max(jnp.where(sel, m_all, -jnp.inf), axis=-1, keepdims=True), m_own)
        w = jnp.where(sel, jnp.exp(m_all - m_fin), 0.0)
        w_own = jnp.exp(m_own - m_fin)
        l_fin = jnp.sum(w * l_ref[...], axis=-1, keepdims=True) + w_own * l_own

        def body(n, acc):
            col = jnp.sum(jnp.where(lane == n, w, 0.0), axis=-1, keepdims=True)
            return acc + col * oblk_ref[n]

        acc = lax.fori_loop(0, nblk, body, w_own * o_own)
        o_ref[...] = _take_diag(acc / l_fin, n_heads, t_new)


def _moba_decode(q, k_new, v_new, cache_kt, cache_vt, page_table, t_new):
    n, width = q.shape
    n_heads = width // HEAD_DIM
    n_seq, n_pages = page_table.shape
    page = cache_kt.shape[2]
    pages_per_blk = MOBA_BLOCK // page
    nblk = n_pages // pages_per_blk
    npg = MOBA_PAGES_PER_STEP
    rows = n_heads * t_new
    assert n_pages % npg == 0 and npg % pages_per_blk == 0 and MOBA_TOPK <= nblk <= LANES
    assert n_pages * page == nblk * MOBA_BLOCK and t_new <= page and page == LANES and rows <= LANES
    tok_spec = pl.BlockSpec((t_new, width), lambda b, g, pt: (b, 0))
    page_specs = [pl.BlockSpec((None, width, page), functools.partial(
        lambda b, g, pt, i: (pt[b, g * npg + i], 0, 0), i=i)) for i in range(npg)]
    grid_spec = pltpu.PrefetchScalarGridSpec(
        num_scalar_prefetch=1,
        grid=(n_seq, n_pages // npg),
        in_specs=[tok_spec, tok_spec, tok_spec] + page_specs + page_specs,
        out_specs=tok_spec,
        scratch_shapes=[pltpu.VMEM((rows, width), BF16), pltpu.VMEM((rows, LANES), F32),
                        pltpu.VMEM((rows, LANES), F32), pltpu.VMEM((rows, LANES), F32),
                        pltpu.VMEM((nblk, rows, width), F32)],
    )
    return pl.pallas_call(
        functools.partial(_moba_decode_kernel, n_heads=n_heads, t_new=t_new,
                          pages_per_blk=pages_per_blk, nblk=nblk),
        grid_spec=grid_spec,
        out_shape=jax.ShapeDtypeStruct((n, width), F32),
        compiler_params=_cparams("parallel", "arbitrary"),
        name="moba_decode",
    )(page_table, q, k_new, v_new, *([cache_kt] * npg), *([cache_vt] * npg))


DIL_DEC_TILE = 512


def _dilated_decode_tables(n_heads, t_new, n_prev):
    qi = (np.arange(n_heads * t_new) % t_new)[:, None]
    c_old = _dilated_multiplicity(n_prev + qi - np.arange(n_prev)[None, :])
    c_new = _dilated_multiplicity(qi - np.arange(LANES)[None, :])
    c_new = np.where(np.arange(LANES)[None, :] < t_new, c_new, 0)
    f = lambda c: (np.where(c > 0, 0.0, NEG_INF).astype(np.float32), c.astype(np.float32))
    return f(c_old) + f(c_new)


def _shifted_window_tile(cur_ref, nxt_ref, new_ref, out_ref, tail_ref, is_last, t_new):
    lane = lax.broadcasted_iota(jnp.int32, (cur_ref.shape[0], LANES), 1)
    nblk = cur_ref.shape[1] // LANES

    @pl.when(is_last)
    def _():
        tail_ref[...] = _pad_rows(new_ref[...], LANES).T

    @pl.when(jnp.logical_not(is_last))
    def _():
        tail_ref[...] = nxt_ref[...]

    rolled = [pltpu.roll(cur_ref[:, c * LANES:(c + 1) * LANES], LANES - t_new, 1) for c in range(nblk)]
    rolled.append(pltpu.roll(tail_ref[...], LANES - t_new, 1))
    for c in range(nblk):
        out_ref[:, c * LANES:(c + 1) * LANES] = jnp.where(lane < LANES - t_new, rolled[c], rolled[c + 1])


def _dilated_decode_kernel(q_ref, kn_ref, vn_ref, kc_ref, vc_ref, kx_ref, vx_ref, bo_ref, mo_ref, bn_ref, mn_ref,
                           o_ref, ko_ref, vo_ref, qbd_ref, m_ref, l_ref, acc_ref, tail_ref, *, n_heads, t_new):
    kt = pl.program_id(1)
    is_last = kt == pl.num_programs(1) - 1
    _shifted_window_tile(kc_ref, kx_ref, kn_ref, ko_ref, tail_ref, is_last, t_new)
    _shifted_window_tile(vc_ref, vx_ref, vn_ref, vo_ref, tail_ref, is_last, t_new)

    @pl.when(kt == 0)
    def _():
        qbd_ref[...] = _scaled_bf16(_block_diag_queries(q_ref[...], n_heads))
        m_ref[...] = jnp.full_like(m_ref, -jnp.inf)
        l_ref[...] = jnp.zeros_like(l_ref)
        acc_ref[...] = jnp.zeros_like(acc_ref)

    def step(s, pv, bias, mult):
        s = s + bias
        m_old = m_ref[...]
        m_new = jnp.maximum(m_old, jnp.max(s, axis=-1, keepdims=True))
        alpha = jnp.exp(m_old - m_new)
        p = jnp.exp(s - m_new) * mult
        l_ref[...] = alpha * l_ref[...] + jnp.sum(p, axis=-1, keepdims=True)
        acc_ref[...] = alpha * acc_ref[...] + pv(p.astype(BF16))
        m_ref[...] = m_new

    @pl.when(kt == 0)
    def _():
        k, v = _pad_rows(kn_ref[...], LANES).astype(BF16), _pad_rows(vn_ref[...], LANES).astype(BF16)
        step(lax.dot_general(qbd_ref[...], k, _NT, preferred_element_type=F32),
             lambda p: jnp.dot(p, v, preferred_element_type=F32), bn_ref[...], mn_ref[...])

    step(jnp.dot(qbd_ref[...], kc_ref[...].astype(BF16), preferred_element_type=F32),
         lambda p: lax.dot_general(p, vc_ref[...].astype(BF16), _NT, preferred_element_type=F32),
         bo_ref[...], mo_ref[...])

    @pl.when(kt == pl.num_programs(1) - 1)
    def _():
        o_ref[...] = _take_diag(acc_ref[...] / l_ref[...], n_heads, t_new)


def _dilated_decode(q, k_new, v_new, win_kt, win_vt, t_new):
    n, width = q.shape
    n_heads = width // HEAD_DIM
    n_seq, _, n_prev = win_kt.shape
    tile = DIL_DEC_TILE
    assert n_prev == W_MAX and n_prev % tile == 0 and t_new <= LANES
    rows = n_heads * t_new
    bo, mo, bn, mn = _dilated_decode_tables(n_heads, t_new, n_prev)
    tok_spec = pl.BlockSpec((t_new, width), lambda b, t: (b, 0))
    win_spec = pl.BlockSpec((None, width, tile), lambda b, t: (b, 0, t))
    last_lane_blk = n_prev // LANES - 1
    nxt_spec = pl.BlockSpec((None, width, LANES),
                            lambda b, t: (b, 0, jnp.minimum((t + 1) * (tile // LANES), last_lane_blk)))
    old_tab = pl.BlockSpec((rows, tile), lambda b, t: (0, t))
    new_tab = pl.BlockSpec((rows, LANES), lambda b, t: (0, 0))
    win_shape = jax.ShapeDtypeStruct(win_kt.shape, F32)
    return pl.pallas_call(
        functools.partial(_dilated_decode_kernel, n_heads=n_heads, t_new=t_new),
        grid=(n_seq, n_prev // tile),
        in_specs=[tok_spec, tok_spec, tok_spec, win_spec, win_spec, nxt_spec, nxt_spec,
                  old_tab, old_tab, new_tab, new_tab],
        out_specs=[tok_spec, win_spec, win_spec],
        out_shape=[jax.ShapeDtypeStruct((n, width), F32), win_shape, win_shape],
        scratch_shapes=[pltpu.VMEM((rows, width), BF16), pltpu.VMEM((rows, 1), F32),
                        pltpu.VMEM((rows, 1), F32), pltpu.VMEM((rows, width), F32),
                        pltpu.VMEM((width, LANES), F32)],
        compiler_params=_cparams("parallel", "arbitrary"),
        name="dilated_decode",
    )(q, k_new, v_new, win_kt, win_vt, win_kt, win_vt,
      jnp.asarray(bo), jnp.asarray(mo), jnp.asarray(bn), jnp.asarray(mn))


_AB_SEGS = (("a", (0,), 0, 1.0), ("a", (1,), 0, 1.0), (None, (2,), 0, 1.0), ("b", (3,), 0, 1.0),
            ("b", (4,), 0, ATT_SCALE), (None, (5,), 0, 1.0), (None, (6,), 0, 1.0))
_C_SEGS = tuple(("a" if o < 2 else None, (o,), c, 1.0) for o in range(3) for c in (0, SEG))
_C_SEGS_PROMPT = tuple((kind, dests if dests == (0,) else dests + (dests[0] + 2,), c, post)
                       for kind, dests, c, post in _C_SEGS)
_KV_OUT = (1, 2)


def _row_tile(n):
    return 512 if n % 512 == 0 else n


def _feature_major(x):
    lead = x.shape[:-3]
    t, h, dh = x.shape[-3:]
    nl = len(lead)
    return x.transpose(*range(nl), nl + 1, nl + 2, nl).reshape(*lead, h * dh, t)


def _token_major(xt, n_heads):
    lead = xt.shape[:-2]
    t = xt.shape[-1]
    nl = len(lead)
    return xt.reshape(*lead, n_heads, HEAD_DIM, t).transpose(*range(nl), nl + 2, nl, nl + 1)


def kernel(x_prompt, x_sample, cache_k_a, cache_v_a, page_table, state_ret, cache_win_k, cache_win_v,
           norm_mix, norm_ffn, norm_final, w_in_ab, w_out_ab, ret_gn_w, w_in_c, w_out_c,
           ffn_w_gate, ffn_w_up, ffn_w_down):
    bp, tp, d = x_prompt.shape
    bs, ts, _ = x_sample.shape
    page = cache_k_a.shape[2]
    past_len = page_table.shape[1] * page
    h_a = cache_k_a.shape[3]
    wa = h_a * HEAD_DIM
    wb = w_in_ab.shape[2] - 3 * wa
    assert wb == 4 * wa and w_in_ab.shape[0] == 1 and w_in_c.shape[0] == 1 and norm_mix.shape[0] == 2
    h_b = wa // HEAD_DIM
    h_c = w_in_c.shape[2] // (3 * HEAD_DIM)

    pos_p = jnp.arange(tp, dtype=jnp.int32)
    pos_s = past_len + jnp.arange(ts, dtype=jnp.int32)
    tile_s = lambda tabs: tuple(jnp.tile(t, (bs, 1)) for t in tabs)
    tabs = {
        "p": (_rope_tables(pos_p, ROT_DIM, ROPE_THETA), _rope_tables(pos_p, HEAD_DIM, RET_THETA)),
        "s": (tile_s(_rope_tables(pos_s, ROT_DIM, ROPE_THETA)), tile_s(_rope_tables(pos_s, HEAD_DIM, RET_THETA))),
    }
    bf = lambda w: w.astype(BF16)
    w_in_ab_bf, w_out_ab_bf, w_in_c_bf, w_out_c_bf = bf(w_in_ab[0]), bf(w_out_ab[0]), bf(w_in_c[0]), bf(w_out_c[0])
    wg, wu, wd = bf(ffn_w_gate), bf(ffn_w_up), bf(ffn_w_down)

    xp = x_prompt.reshape(bp * tp, d)
    xs = x_sample.reshape(bs * ts, d)
    tm_p, tm_s = _row_tile(bp * tp), _row_tile(bs * ts)
    assert tp % tm_p == 0

    ab_widths = (wa,) * 7
    qa_p, kat_p, vat_p, qb_p, kb_p, vb_p, gb_p = _project(
        xp, norm_mix[0], w_in_ab_bf, *tabs["p"], _AB_SEGS, ab_widths, tm_p, _KV_OUT)
    qa_s, ka_s, va_s, qb_s, kb_s, vb_s, gb_s = _project(
        xs, norm_mix[0], w_in_ab_bf, *tabs["s"], _AB_SEGS, ab_widths, tm_s)

    oa_p = _moba_prompt(qa_p, kat_p, vat_p, bp, tp)
    oa_s = _moba_decode(qa_s, ka_s, va_s, _feature_major(cache_k_a[0]), _feature_major(cache_v_a[0]),
                        page_table, ts)

    chunk_p = math.gcd(tp, RET_CHUNK)
    ob_p, ret_p = _retention(qb_p, kb_p, vb_p, gb_p, ret_gn_w[0],
                             jnp.zeros((bp, h_b, HEAD_DIM, HEAD_DIM), F32), bp, tp, chunk_p,
                             math.gcd(tp // chunk_p, 8))
    ob_s, ret_s = _retention(qb_s, kb_s, vb_s, gb_s, ret_gn_w[0], state_ret[0], bs, ts,
                             math.gcd(ts, RET_CHUNK), 1)

    xp = _mix_ffn([oa_p, ob_p], w_out_ab_bf, xp, norm_ffn[0], wg[0], wu[0], wd[0], norm_final, False, tm_p)
    xs = _mix_ffn([oa_s, ob_s], w_out_ab_bf, xs, norm_ffn[0], wg[0], wu[0], wd[0], norm_final, False, tm_s)

    wc = h_c * HEAD_DIM
    q_p, kt_p, vt_p, k_p, v_p = _project(xp, norm_mix[1], w_in_c_bf, *tabs["p"], _C_SEGS_PROMPT, (wc,) * 5,
                                         tm_p, _KV_OUT)
    q_s, k_s, v_s = _project(xs, norm_mix[1], w_in_c_bf, *tabs["s"], _C_SEGS, (wc,) * 3, tm_s)
    n_prev = cache_win_k.shape[2]
    o_p = _dilated_prompt(q_p, k_p, v_p, kt_p, vt_p, bp, tp)
    o_s, win_kt, win_vt = _dilated_decode(q_s, k_s, v_s, _feature_major(cache_win_k[0]),
                                          _feature_major(cache_win_v[0]), ts)
    y_p = _mix_ffn([o_p], w_out_c_bf, xp, norm_ffn[1], wg[1], wu[1], wd[1], norm_final, True, tm_p)
    y_s = _mix_ffn([o_s], w_out_c_bf, xs, norm_ffn[1], wg[1], wu[1], wd[1], norm_final, True, tm_s)

    keep_p = min(W_MAX, tp)
    return (
        y_p.reshape(bp, tp, d), y_s.reshape(bs, ts, d),
        _token_major(kat_p, h_a)[None], _token_major(vat_p, h_a)[None], ret_p[None],
        _token_major(kt_p[:, :, tp - keep_p:], h_c)[None], _token_major(vt_p[:, :, tp - keep_p:], h_c)[None],
        ka_s.reshape(1, bs, ts, h_a, HEAD_DIM), va_s.reshape(1, bs, ts, h_a, HEAD_DIM), ret_s[None],
        _token_major(win_kt, h_c)[None], _token_major(win_vt, h_c)[None],
    )
```

```python
import functools
import math

import jax
import jax.numpy as jnp
import numpy as np
from jax import lax
from jax.experimental import pallas as pl
from jax.experimental.pallas import tpu as pltpu

F32 = jnp.float32
BF16 = jnp.bfloat16

HEAD_DIM = 64
LANES = 128
ROT_DIM = HEAD_DIM // 4
ROPE_THETA = 500000.0
RET_THETA = 10000.0
MOBA_BLOCK = 256
MOBA_TOPK = 3
RET_CHUNK = 128
DILATED = ((128, 1), (512, 4), (2048, 16))
W_MAX = max(w for w, _ in DILATED)
EPS = 1e-6
NEG_INF = -1e30
ATT_SCALE = HEAD_DIM ** -0.5
VMEM_LIMIT = 56 * 1024 * 1024

_NT = (((1,), (1,)), ((), ()))
_TN = (((0,), (0,)), ((), ()))


def _cparams(*sem):
    return pltpu.CompilerParams(dimension_semantics=sem, vmem_limit_bytes=VMEM_LIMIT)


def _rms(x, w):
    ms = jnp.mean(x * x, axis=-1, keepdims=True)
    return x * lax.rsqrt(ms + EPS) * w


SEG = 512


def _proj_kernel(x_ref, nw_ref, w_ref, ca_ref, na_ref, pa_ref, cb_ref, nb_ref, pb_ref, *out_refs, segs, transposed):
    xn = _rms(x_ref[...], nw_ref[...]).astype(BF16)
    for s, (kind, dests, col, post) in enumerate(segs):
        acc = jnp.dot(xn, w_ref[:, s * SEG:(s + 1) * SEG], preferred_element_type=F32)
        for c in range(SEG // LANES):
            r = acc[:, c * LANES:(c + 1) * LANES]
            if kind is not None:
                c_ref, n_ref, p_ref, shift = (ca_ref, na_ref, pa_ref, ROT_DIM // 2) if kind == "a" else (
                    cb_ref, nb_ref, pb_ref, HEAD_DIM // 2)
                r = (r * c_ref[...] + pltpu.roll(r, LANES - shift, 1) * n_ref[...]
                     + pltpu.roll(r, shift, 1) * p_ref[...])
                if post != 1.0:
                    r = r * post
            lo = col + c * LANES
            for oi in dests:
                if oi in transposed:
                    out_refs[oi][lo:lo + LANES, :] = r.T
                else:
                    out_refs[oi][:, lo:lo + LANES] = r


def _rope_tables(pos, rot_dim, theta):
    half = rot_dim // 2
    inv = theta ** (-jnp.arange(half, dtype=F32) / half)
    ang = pos.astype(F32)[:, None] * inv[None, :]
    cos, sin = jnp.cos(ang), jnp.sin(ang)
    lane = np.arange(LANES) % HEAD_DIM
    idx = lane % half
    cos_t = jnp.where(lane < rot_dim, cos[:, idx], 1.0)
    neg_t = jnp.where(lane < half, -sin[:, idx], 0.0)
    pos_t = jnp.where((lane >= half) & (lane < rot_dim), sin[:, idx], 0.0)
    return cos_t.astype(F32), neg_t.astype(F32), pos_t.astype(F32)


def _project(x, nw, w_bf, tabs_a, tabs_b, segs, out_widths, tm, transposed=()):
    n, d = x.shape
    t_tab = tabs_a[0].shape[0]
    nt = t_tab // tm
    tab_spec = pl.BlockSpec((tm, LANES), lambda i: (i % nt, 0))
    out_specs, out_shape = [], []
    for oi, w in enumerate(out_widths):
        if oi in transposed:
            out_specs.append(pl.BlockSpec((None, w, tm), lambda i: (i // nt, 0, i % nt)))
            out_shape.append(jax.ShapeDtypeStruct((n // t_tab, w, t_tab), F32))
        else:
            out_specs.append(pl.BlockSpec((tm, w), lambda i: (i, 0)))
            out_shape.append(jax.ShapeDtypeStruct((n, w), F32))
    return pl.pallas_call(
        functools.partial(_proj_kernel, segs=segs, transposed=tuple(transposed)),
        grid=(n // tm,),
        in_specs=[pl.BlockSpec((tm, d), lambda i: (i, 0)),
                  pl.BlockSpec((1, d), lambda i: (0, 0)),
                  pl.BlockSpec(w_bf.shape, lambda i: (0, 0))] + [tab_spec] * 6,
        out_specs=out_specs,
        out_shape=out_shape,
        compiler_params=_cparams("parallel"),
        name="rms_proj_rope",
    )(x, nw.reshape(1, d), w_bf, *tabs_a, *tabs_b)


FFN_CHUNKS = 2


def _mix_ffn_kernel(*refs, nparts, final):
    parts = refs[:nparts]
    wo_ref, x_ref, nw_ref, wg_ref, wu_ref, wd_ref, fw_ref, o_ref = refs[nparts:]
    a = jnp.concatenate([p[...].astype(BF16) for p in parts], axis=1) if nparts > 1 else parts[0][...].astype(BF16)
    x = x_ref[...] + jnp.dot(a, wo_ref[...], preferred_element_type=F32)
    xn = _rms(x, nw_ref[...]).astype(BF16)
    tf = wg_ref.shape[1] // FFN_CHUNKS
    acc = x
    for c in range(FFN_CHUNKS):
        g = jnp.dot(xn, wg_ref[:, c * tf:(c + 1) * tf], preferred_element_type=F32)
        u = jnp.dot(xn, wu_ref[:, c * tf:(c + 1) * tf], preferred_element_type=F32)
        h = (g * jax.nn.sigmoid(g) * u).astype(BF16)
        acc = acc + jnp.dot(h, wd_ref[c * tf:(c + 1) * tf, :], preferred_element_type=F32)
    if final:
        acc = _rms(acc, fw_ref[...])
    o_ref[...] = acc


def _mix_ffn(parts, wo, x, nw, wg, wu, wd, fw, final, tm):
    n, d = x.shape
    dff = wg.shape[1]
    const = lambda i: (0, 0)
    row = lambda w: pl.BlockSpec((tm, w), lambda i: (i, 0))
    return pl.pallas_call(
        functools.partial(_mix_ffn_kernel, nparts=len(parts), final=final),
        grid=(n // tm,),
        in_specs=[row(p.shape[1]) for p in parts]
        + [pl.BlockSpec(wo.shape, const), row(d), pl.BlockSpec((1, d), const),
           pl.BlockSpec((d, dff), const), pl.BlockSpec((d, dff), const), pl.BlockSpec((dff, d), const),
           pl.BlockSpec((1, d), const)],
        out_specs=row(d),
        out_shape=jax.ShapeDtypeStruct((n, d), F32),
        compiler_params=_cparams("parallel"),
        name="outproj_swiglu",
    )(*parts, wo, x, nw.reshape(1, d), wg, wu, wd, fw.reshape(1, d))


ATT_TILE = 256
ATT_GROUP = 4
MOBA_SUBTILES = 4
ATT_CHUNK = 64


def _unit_scores(kb, qb, bias_fn):
    s = jnp.dot(kb, qb, preferred_element_type=F32)
    chunks = [s[r:r + ATT_CHUNK] + bias_fn(r, r + ATT_CHUNK) for r in range(0, s.shape[0], ATT_CHUNK)]
    return chunks, jnp.max(functools.reduce(jnp.maximum, chunks), axis=0, keepdims=True)


def _unit_values(chunks, m_u, vts, mult_fn):
    ps, tot = [], None
    for i, c in enumerate(chunks):
        p = jnp.exp(c - m_u)
        if mult_fn is not None:
            p = p * mult_fn(i * ATT_CHUNK, (i + 1) * ATT_CHUNK)
        tot = p if tot is None else tot + p
        ps.append(p.astype(BF16))
    per = len(ps) // len(vts)
    pv = sum(jnp.dot(vt, jnp.concatenate(ps[b * per:(b + 1) * per], axis=0), preferred_element_type=F32)
             for b, vt in enumerate(vts))
    return jnp.sum(tot, axis=0, keepdims=True), pv


def _attend_unit(kb, q2, bias_fn, vts, mult_fn=None):
    chunks, m_u = _unit_scores(kb, q2, bias_fn)
    return (m_u,) + _unit_values(chunks, m_u, vts, mult_fn)


def _merge(parts):
    m = functools.reduce(jnp.maximum, [p[0] for p in parts])
    ws = [jnp.exp(p[0] - m) for p in parts]
    return m, sum(w * p[1] for w, p in zip(ws, parts)), sum(w * p[2] for w, p in zip(ws, parts))


def _head_queries(q_t, feat):
    return [jnp.where(feat // HEAD_DIM == hh, q_t, 0.0) for hh in range(2)]


def _scaled_bf16(q):
    return (q * ATT_SCALE).astype(BF16)


def _both(x):
    return jnp.concatenate([x, x], axis=1)


def _pick_heads(x, feat):
    nq = x.shape[1] // 2
    return jnp.where(feat // HEAD_DIM == 0, x[:, :nq], x[:, nq:])


def _moba_prompt_kernel(q_ref, kt_ref, vt_ref, o_ref, kb_ref, vb_ref, kmean_ref, bias_ref, causal_ref, *, nblk):
    blk, grp = ATT_TILE, ATT_GROUP
    nsub = MOBA_SUBTILES
    step = pl.program_id(2)

    @pl.when(step == 0)
    def _():
        for n in range(nblk):
            kblk = kt_ref[:, n * blk:(n + 1) * blk].T
            kmean_ref[n:n + 1, :] = jnp.sum(kblk, axis=0, keepdims=True) * (1.0 / blk)
            kb_ref[n * blk:(n + 1) * blk, :] = kblk.astype(BF16)
            vb_ref[n] = vt_ref[:, n * blk:(n + 1) * blk].astype(BF16)
        key_i = lax.broadcasted_iota(jnp.int32, (blk, 2 * blk), 0)
        qry_i = lax.broadcasted_iota(jnp.int32, (blk, 2 * blk), 1) % blk
        causal_ref[...] = jnp.where(key_i <= qry_i, 0.0, NEG_INF)

    feat = lax.broadcasted_iota(jnp.int32, (LANES, blk), 0)
    blk_id = lax.broadcasted_iota(jnp.int32, (nblk, 2 * blk), 0)
    blk_f = blk_id.astype(F32)
    qb2s = []
    for sub in range(nsub):
        qt = step * nsub + sub
        q2 = jnp.concatenate(_head_queries(q_ref[sub * blk:(sub + 1) * blk, :].T, feat), axis=1)
        sc = jnp.dot(kmean_ref[...], q2, preferred_element_type=F32, precision=lax.Precision.HIGHEST)
        sc = jnp.where(blk_id < qt, sc, NEG_INF)
        sel = jnp.zeros((nblk, 2 * blk), F32)
        for r in range(MOBA_TOPK):
            mx = jnp.max(sc, axis=0, keepdims=True)
            first = jnp.min(jnp.where(sc == mx, blk_f, float(nblk)), axis=0, keepdims=True)
            pick = blk_f == first
            sel = jnp.maximum(sel, jnp.where(pick, jnp.where(r < qt, 1.0, 0.0), 0.0))
            sc = jnp.where(pick, -jnp.inf, sc)
        bias_ref[sub] = jnp.where((sel > 0.0) | (blk_id == qt), 0.0, NEG_INF)
        qb2s.append(_scaled_bf16(q2))

    def group(sub, g, with_own):
        def bias_fn(r0, r1):
            n, r = g * grp + r0 // blk, r0 % blk
            row = bias_ref[sub, pl.ds(n, 1), :]
            if not with_own:
                return row
            return row + jnp.where(n == step * nsub + sub, causal_ref[r:r + (r1 - r0), :], 0.0)

        kb = kb_ref[pl.ds(pl.multiple_of(g * grp * blk, blk), grp * blk), :]
        return _attend_unit(kb, qb2s[sub], bias_fn, [vb_ref[g * grp + b] for b in range(grp)])

    g_own = step * nsub // grp

    def body(g, carry):
        return sum((_merge([carry[3 * sub:3 * sub + 3], group(sub, g, False)]) for sub in range(nsub)), ())

    carry = lax.fori_loop(0, g_own, body, sum((group(sub, g_own, True) for sub in range(nsub)), ()))
    for sub in range(nsub):
        o_ref[sub * blk:(sub + 1) * blk, :] = _pick_heads(carry[3 * sub + 2] / carry[3 * sub + 1], feat).T


def _moba_prompt(q, kt, vt, batch, seq):
    n, width = q.shape
    blk, grp, nsub = ATT_TILE, ATT_GROUP, MOBA_SUBTILES
    assert blk == MOBA_BLOCK and seq % (blk * grp) == 0 and grp % nsub == 0
    nblk = seq // blk
    steps = nblk // nsub
    npair = width // LANES
    kv_spec = pl.BlockSpec((None, LANES, seq), lambda b, p, t: (b, p, 0))
    q_spec = pl.BlockSpec((nsub * blk, LANES), lambda b, p, t: (b * steps + t, p))
    return pl.pallas_call(
        functools.partial(_moba_prompt_kernel, nblk=nblk),
        grid=(batch, npair, steps),
        in_specs=[q_spec, kv_spec, kv_spec],
        out_specs=q_spec,
        out_shape=jax.ShapeDtypeStruct((n, width), F32),
        scratch_shapes=[pltpu.VMEM((nblk * blk, LANES), BF16), pltpu.VMEM((nblk, LANES, blk), BF16),
                        pltpu.VMEM((nblk, LANES), F32), pltpu.VMEM((nsub, nblk, 2 * blk), F32),
                        pltpu.VMEM((blk, 2 * blk), F32)],
        compiler_params=_cparams("parallel", "parallel", "arbitrary"),
        name="moba_prompt",
    )(q, kt, vt)


def _dilated_multiplicity(delta):
    c = np.zeros(delta.shape, np.int32)
    for window, dil in DILATED:
        c += ((delta >= 0) & (delta <= window) & (delta % dil == 0)).astype(np.int32)
    return c


DIL_NEAR = 2
DIL_FAR_CLASSES_PER_STEP = 2
assert DILATED[-1][1] % DIL_FAR_CLASSES_PER_STEP == 0
DIL_NEAR_SUBTILES = 4
DIL_FAR_STRIDE = DILATED[-1][1]
DIL_FAR_REACH = DILATED[-1][0] // DIL_FAR_STRIDE
assert DILATED[-2][0] <= ATT_TILE * DIL_NEAR and ATT_TILE % DIL_FAR_STRIDE == 0 and DIL_FAR_REACH <= ATT_TILE


def _mask_bias(valid):
    return np.where(valid, 0.0, NEG_INF).astype(np.float32)


def _dilated_near_tables():
    ki = np.arange(ATT_TILE)[:, None]
    qi = np.arange(ATT_TILE)[None, :]
    c = np.stack([_dilated_multiplicity(ATT_TILE * (DIL_NEAR - i) + qi - ki) for i in range(DIL_NEAR + 1)])
    return _mask_bias(c > 0), c.astype(np.float32)


def _dilated_far_tables():
    per_tile = ATT_TILE // DIL_FAR_STRIDE

    def table(k0, nk, q0):
        ka = k0 + np.arange(nk)[:, None]
        qa = q0 + np.arange(ATT_TILE)[None, :]
        return _mask_bias((qa - ka <= DIL_FAR_REACH) & (ka // per_tile < qa // per_tile - DIL_NEAR))

    return table(0, ATT_TILE, 0), table(ATT_TILE - DIL_FAR_REACH, ATT_TILE + DIL_FAR_REACH, ATT_TILE)


def _dilated_far_kernel(q_ref, k_ref, v_ref, b0_ref, b1_ref, o_ref, lse_ref, *, seq):
    stride, blk, reach = DIL_FAR_STRIDE, ATT_TILE, DIL_FAR_REACH
    ntile = seq // stride // blk
    feat = lax.broadcasted_iota(jnp.int32, (LANES, blk), 0)

    def one_class(cls):
        for tau in range(ntile):
            k0, nk, tab = (0, blk, b0_ref) if tau == 0 else (tau * blk - reach, blk + reach, b1_ref)
            rows_q = pl.ds(cls + stride * blk * tau, blk, stride=stride)
            rows_k = pl.ds(cls + stride * k0, nk, stride=stride)
            qb2 = _scaled_bf16(jnp.concatenate(_head_queries(q_ref[rows_q, :].T, feat), axis=1))
            kb = k_ref[rows_k, :].astype(BF16)
            vt = v_ref[rows_k, :].T.astype(BF16)
            m_u, l_u, pv = _attend_unit(kb, qb2, lambda r0, r1: _both(tab[r0:r1, :]), [vt])
            o_ref[rows_q, :] = _pick_heads(pv / l_u, feat).T
            lse_ref[rows_q, :] = _pick_heads(jnp.broadcast_to(m_u + jnp.log(l_u), pv.shape), feat).T

    def body(i, carry):
        for c in range(DIL_FAR_CLASSES_PER_STEP):
            one_class(i * DIL_FAR_CLASSES_PER_STEP + c)
        return carry

    lax.fori_loop(0, stride // DIL_FAR_CLASSES_PER_STEP, body, 0)


def _dilated_near_kernel(q_ref, kt_ref, vt_ref, far_ref, lse_ref, bias_ref, mult_ref, o_ref,
                         kb_ref, vb_ref, tab_ref, *, nblk):
    blk, near, nsub = ATT_TILE, DIL_NEAR, DIL_NEAR_SUBTILES
    step = pl.program_id(2)

    @pl.when(step == 0)
    def _():
        kb_ref[0:near * blk, :] = jnp.zeros((near * blk, LANES), BF16)
        for n in range(near):
            vb_ref[n] = jnp.zeros((LANES, blk), BF16)
        for n in range(nblk):
            kb_ref[(n + near) * blk:(n + near + 1) * blk, :] = kt_ref[:, n * blk:(n + 1) * blk].T.astype(BF16)
            vb_ref[n + near] = vt_ref[:, n * blk:(n + 1) * blk].astype(BF16)

    @pl.when(step * nsub < near + nsub)
    def _():
        for sub in range(nsub):
            for i in range(near + 1):
                tab_ref[sub, i] = jnp.where(step * nsub + sub - near + i >= 0, bias_ref[i], NEG_INF)

    feat = lax.broadcasted_iota(jnp.int32, (LANES, blk), 0)
    for sub in range(nsub):
        qt = step * nsub + sub
        rows = slice(sub * blk, (sub + 1) * blk)
        qb2 = _scaled_bf16(jnp.concatenate(_head_queries(q_ref[rows, :].T, feat), axis=1))
        m2, l2, a2 = _attend_unit(
            kb_ref[pl.ds(pl.multiple_of(qt * blk, blk), (near + 1) * blk), :], qb2,
            lambda r0, r1: _both(tab_ref[sub, r0 // blk, r0 % blk:r0 % blk + (r1 - r0), :]),
            [vb_ref[qt + i] for i in range(near + 1)],
            lambda r0, r1: _both(mult_ref[r0 // blk, r0 % blk:r0 % blk + (r1 - r0), :]))
        m_n = _pick_heads(jnp.broadcast_to(m2, a2.shape), feat)
        l_n = _pick_heads(jnp.broadcast_to(l2, a2.shape), feat)
        a_n = _pick_heads(a2, feat)
        lse_f = lse_ref[rows, :].T
        m = jnp.maximum(m_n, lse_f)
        w_n, w_f = jnp.exp(m_n - m), jnp.exp(lse_f - m)
        o_ref[rows, :] = ((a_n * w_n + far_ref[rows, :].T * w_f) / (l_n * w_n + w_f)).T


def _dilated_prompt(q, k, v, kt, vt, batch, seq):
    n, width = q.shape
    blk, near, stride = ATT_TILE, DIL_NEAR, DIL_FAR_STRIDE
    assert seq % (blk * stride) == 0
    nblk = seq // blk
    npair = width // LANES
    const = lambda nd: (lambda *_: (0,) * nd)
    full = lambda t: pl.BlockSpec(t.shape, const(t.ndim))

    far_tabs = [jnp.asarray(t) for t in _dilated_far_tables()]
    seq_spec = pl.BlockSpec((seq, LANES), lambda b, p: (b, p))
    o_far, lse_far = pl.pallas_call(
        functools.partial(_dilated_far_kernel, seq=seq),
        grid=(batch, npair),
        in_specs=[seq_spec, seq_spec, seq_spec] + [full(t) for t in far_tabs],
        out_specs=[seq_spec, seq_spec],
        out_shape=[jax.ShapeDtypeStruct((n, width), F32)] * 2,
        compiler_params=_cparams("parallel", "parallel"),
        name="dilated_far",
    )(q, k, v, *far_tabs)

    near_tabs = [jnp.asarray(t) for t in _dilated_near_tables()]
    kv_spec = pl.BlockSpec((None, LANES, seq), lambda b, p, t: (b, p, 0))
    nsub = DIL_NEAR_SUBTILES
    assert nblk % nsub == 0
    steps = nblk // nsub
    q_spec = pl.BlockSpec((nsub * blk, LANES), lambda b, p, t: (b * steps + t, p))
    return pl.pallas_call(
        functools.partial(_dilated_near_kernel, nblk=nblk),
        grid=(batch, npair, steps),
        in_specs=[q_spec, kv_spec, kv_spec, q_spec, q_spec] + [full(t) for t in near_tabs],
        out_specs=q_spec,
        out_shape=jax.ShapeDtypeStruct((n, width), F32),
        scratch_shapes=[pltpu.VMEM(((nblk + near) * blk, LANES), BF16), pltpu.VMEM((nblk + near, LANES, blk), BF16),
                        pltpu.VMEM((nsub, near + 1, blk, blk), F32)],
        compiler_params=_cparams("parallel", "parallel", "arbitrary"),
        name="dilated_near",
    )(q, kt, vt, o_far, lse_far, *near_tabs)


RET_ROWS = 128


def _retention_tables(n_heads, chunk):
    log_g = jnp.log1p(-jnp.exp2(-5.0 - jnp.arange(n_heads, dtype=F32)))
    i = jnp.arange(chunk, dtype=F32)
    diff = i[:, None] - i[None, :]
    causal = diff >= 0
    decay = jnp.where(causal[None], jnp.exp(jnp.where(causal, diff, 0.0)[None] * log_g[:, None, None]), 0.0)
    q_dec = jnp.exp((i + 1.0)[:, None] * log_g[None, :])
    k_dec = jnp.exp((chunk - 1.0 - i)[:, None] * log_g[None, :])
    c_dec = jnp.exp(chunk * log_g)
    pad = RET_ROWS - chunk
    npair = n_heads // 2
    decay = jnp.pad(decay, ((0, 0), (0, pad), (0, pad))).reshape(npair, 2 * RET_ROWS, RET_ROWS)
    expand = lambda t: jnp.pad(jnp.repeat(t, HEAD_DIM, axis=1), ((0, pad), (0, 0))).reshape(
        RET_ROWS, npair, LANES).transpose(1, 0, 2)
    q_tab, k_tab = expand(q_dec), expand(k_dec)
    same_head = (np.arange(LANES)[:, None] // HEAD_DIM) == (np.arange(LANES)[None, :] // HEAD_DIM)
    c_tab = jnp.where(same_head[None], jnp.repeat(c_dec, HEAD_DIM).reshape(npair, LANES, 1), 0.0)
    return decay.astype(F32), q_tab.astype(F32), k_tab.astype(F32), c_tab.astype(F32)


def _retention_kernel(q_ref, k_ref, v_ref, g_ref, gn_ref, s0_ref, dec_ref, qd_ref, kd_ref, cd_ref,
                      o_ref, sout_ref, st_ref, *, chunk, nchunk):
    c = pl.program_id(2)

    @pl.when(c == 0)
    def _():
        zero = jnp.zeros((HEAD_DIM, HEAD_DIM), F32)
        st_ref[...] = jnp.concatenate([jnp.concatenate([s0_ref[0], zero], axis=1),
                                       jnp.concatenate([zero, s0_ref[1]], axis=1)], axis=0)

    pad = RET_ROWS - chunk
    lane = lax.broadcasted_iota(jnp.int32, (RET_ROWS, LANES), 1)
    head0 = lane < HEAD_DIM
    same_head = (lax.broadcasted_iota(jnp.int32, (LANES, LANES), 0) // HEAD_DIM) == (
        lax.broadcasted_iota(jnp.int32, (LANES, LANES), 1) // HEAD_DIM)

    def rows(ref, r):
        x = ref[pl.ds(r, chunk), :]
        return x if pad == 0 else jnp.concatenate([x, jnp.zeros((pad, LANES), F32)], axis=0)

    def seg_mean(x):
        s0 = jnp.sum(jnp.where(head0, x, 0.0), axis=-1, keepdims=True)
        s1 = jnp.sum(jnp.where(head0, 0.0, x), axis=-1, keepdims=True)
        return jnp.where(head0, s0, s1) * (1.0 / HEAD_DIM)

    def one_chunk(j, state):
        r = j * chunk
        q, k, v, g = rows(q_ref, r), rows(k_ref, r), rows(v_ref, r), rows(g_ref, r)
        kb, vb = k.astype(BF16), v.astype(BF16)
        q2 = jnp.concatenate([jnp.where(head0, q, 0.0), jnp.where(head0, 0.0, q)], axis=0).astype(BF16)
        attn = lax.dot_general(q2, kb, _NT, preferred_element_type=F32) * dec_ref[...]
        inn = jnp.dot(attn.astype(BF16), vb, preferred_element_type=F32)
        inner = jnp.where(head0, inn[:RET_ROWS], inn[RET_ROWS:])
        cross = jnp.dot(q.astype(BF16), state.astype(BF16), preferred_element_type=F32) * qd_ref[...]
        upd = lax.dot_general((k * kd_ref[...]).astype(BF16), vb, _TN, preferred_element_type=F32)
        o = inner + cross
        mu = seg_mean(o)
        var = seg_mean(jnp.square(o - mu))
        y = (o - mu) * lax.rsqrt(var + EPS) * gn_ref[...]
        y = y * (g * jax.nn.sigmoid(g))
        o_ref[r:r + chunk, :] = y[:chunk]
        return state * cd_ref[...] + jnp.where(same_head, upd, 0.0)

    state = st_ref[...]
    for j in range(nchunk):
        state = one_chunk(j, state)
    st_ref[...] = state

    @pl.when(c == pl.num_programs(2) - 1)
    def _():
        sout_ref[0] = st_ref[0:HEAD_DIM, 0:HEAD_DIM]
        sout_ref[1] = st_ref[HEAD_DIM:, HEAD_DIM:]


def _retention(q, k, v, g, gn_w, s0, batch, seq, chunk, nchunk):
    n, width = q.shape
    n_heads = width // HEAD_DIM
    npair = width // LANES
    rows = chunk * nchunk
    steps = seq // rows
    dec, qd, kd, cd = _retention_tables(n_heads, chunk)
    row_spec = pl.BlockSpec((rows, LANES), lambda b, p, c: (b * steps + c, p))
    st_spec = pl.BlockSpec((None, 2, HEAD_DIM, HEAD_DIM), lambda b, p, c: (b, p, 0, 0))
    tab = lambda t: pl.BlockSpec((None,) + t.shape[1:], lambda b, p, c: (p, 0, 0))
    out, s_new = pl.pallas_call(
        functools.partial(_retention_kernel, chunk=chunk, nchunk=nchunk),
        grid=(batch, npair, steps),
        in_specs=[row_spec, row_spec, row_spec, row_spec, pl.BlockSpec((1, LANES), lambda b, p, c: (0, p)),
                  st_spec, tab(dec), tab(qd), tab(kd), tab(cd)],
        out_specs=[row_spec, st_spec],
        out_shape=[jax.ShapeDtypeStruct((n, width), F32),
                   jax.ShapeDtypeStruct((batch, n_heads, HEAD_DIM, HEAD_DIM), F32)],
        scratch_shapes=[pltpu.VMEM((LANES, LANES), F32)],
        compiler_params=_cparams("parallel", "parallel", "arbitrary"),
        name="retention",
    )(q, k, v, g, gn_w.reshape(1, width), s0, dec, qd, kd, cd)
    return out, s_new


def _block_diag_queries(q, n_heads):
    t = q.shape[0]
    rep = jnp.concatenate([q] * n_heads, axis=0)
    rows = lax.broadcasted_iota(jnp.int32, rep.shape, 0) // t
    cols = lax.broadcasted_iota(jnp.int32, rep.shape, 1) // HEAD_DIM
    return jnp.where(rows == cols, rep, 0.0)


def _take_diag(o, n_heads, t):
    cols = lax.broadcasted_iota(jnp.int32, (t, o.shape[1]), 1) // HEAD_DIM
    out = jnp.zeros((t, o.shape[1]), F32)
    for h in range(n_heads):
        out = jnp.where(cols == h, o[h * t:(h + 1) * t, :], out)
    return out


def _pad_rows(x, rows):
    return jnp.concatenate([x, jnp.zeros((rows - x.shape[0], x.shape[1]), x.dtype)], axis=0)


MOBA_PAGES_PER_STEP = 32


def _moba_decode_kernel(pt_ref, q_ref, kn_ref, vn_ref, *rest, n_heads, t_new, pages_per_blk, nblk):
    npg = MOBA_PAGES_PER_STEP
    kp, vp = rest[:npg], rest[npg:2 * npg]
    o_ref, qbd_ref, m_ref, l_ref, sc_ref, oblk_ref = rest[2 * npg:]
    g = pl.program_id(1)
    rows = n_heads * t_new
    lane = lax.broadcasted_iota(jnp.int32, (rows, LANES), 1)
    page = kp[0].shape[1]
    blk_keys = pages_per_blk * page
    step_blks = npg // pages_per_blk

    @pl.when(g == 0)
    def _():
        qbd_ref[...] = _scaled_bf16(_block_diag_queries(q_ref[...], n_heads))
        m_ref[...] = jnp.zeros_like(m_ref)
        l_ref[...] = jnp.zeros_like(l_ref)
        sc_ref[...] = jnp.full_like(sc_ref, -jnp.inf)

    kt_all = jnp.concatenate([kp[i][...].astype(BF16) for i in range(npg)], axis=1)
    s_all = jnp.dot(qbd_ref[...], kt_all, preferred_element_type=F32)
    m_new, l_new, sc_new = m_ref[...], l_ref[...], sc_ref[...]
    for jb in range(step_blks):
        n = g * step_blks + jb
        s = s_all[:, jb * blk_keys:(jb + 1) * blk_keys]
        mb = jnp.max(s, axis=-1, keepdims=True)
        p = jnp.exp(s - mb)
        oblk_ref[n] = sum(lax.dot_general(p[:, a * page:(a + 1) * page].astype(BF16),
                                          vp[jb * pages_per_blk + a][...].astype(BF16),
                                          _NT, preferred_element_type=F32) for a in range(pages_per_blk))
        here = lane == n
        m_new = jnp.where(here, mb, m_new)
        l_new = jnp.where(here, jnp.sum(p, axis=-1, keepdims=True), l_new)
        sc_new = jnp.where(here, jnp.sum(s, axis=-1, keepdims=True) * (1.0 / blk_keys), sc_new)
    m_ref[...], l_ref[...], sc_ref[...] = m_new, l_new, sc_new

    @pl.when(g == pl.num_programs(1) - 1)
    def _():
        s = lax.dot_general(qbd_ref[...], _pad_rows(kn_ref[...], LANES).astype(BF16), _NT,
                            preferred_element_type=F32)
        qry = lax.broadcasted_iota(jnp.int32, (rows, LANES), 0) % t_new
        s = jnp.where(lane <= qry, s, NEG_INF)
        m_own = jnp.max(s, axis=-1, keepdims=True)
        p = jnp.exp(s - m_own)
        l_own = jnp.sum(p, axis=-1, keepdims=True)
        o_own = jnp.dot(p.astype(BF16), _pad_rows(vn_ref[...], LANES).astype(BF16), preferred_element_type=F32)
        sc = sc_ref[...]
        lane_f = lane.astype(F32)
        sel = lane < 0
        for _ in range(MOBA_TOPK):
            mx = jnp.max(sc, axis=-1, keepdims=True)
            first = jnp.min(jnp.where(sc == mx, lane_f, float(LANES)), axis=-1, keepdims=True)
            pick = lane_f == first
            sel = sel | pick
            sc = jnp.where(pick, -jnp.inf, sc)
        m_all = m_ref[...]
        m_fin = jnp.maximum(jnp.max(jnp.where(sel, m_all, -jnp.inf), axis=-1, keepdims=True), m_own)
        w = jnp.where(sel, jnp.exp(m_all - m_fin), 0.0)
        w_own = jnp.exp(m_own - m_fin)
        l_fin = jnp.sum(w * l_ref[...], axis=-1, keepdims=True) + w_own * l_own

        def body(n, acc):
            col = jnp.sum(jnp.where(lane == n, w, 0.0), axis=-1, keepdims=True)
            return acc + col * oblk_ref[n]

        acc = lax.fori_loop(0, nblk, body, w_own * o_own)
        o_ref[...] = _take_diag(acc / l_fin, n_heads, t_new)


def _moba_decode(q, k_new, v_new, cache_kt, cache_vt, page_table, t_new):
    n, width = q.shape
    n_heads = width // HEAD_DIM
    n_seq, n_pages = page_table.shape
    page = cache_kt.shape[2]
    pages_per_blk = MOBA_BLOCK // page
    nblk = n_pages // pages_per_blk
    npg = MOBA_PAGES_PER_STEP
    rows = n_heads * t_new
    assert n_pages % npg == 0 and npg % pages_per_blk == 0 and MOBA_TOPK <= nblk <= LANES
    assert n_pages * page == nblk * MOBA_BLOCK and t_new <= page and page == LANES and rows <= LANES
    tok_spec = pl.BlockSpec((t_new, width), lambda b, g, pt: (b, 0))
    page_specs = [pl.BlockSpec((None, width, page), functools.partial(
        lambda b, g, pt, i: (pt[b, g * npg + i], 0, 0), i=i)) for i in range(npg)]
    grid_spec = pltpu.PrefetchScalarGridSpec(
        num_scalar_prefetch=1,
        grid=(n_seq, n_pages // npg),
        in_specs=[tok_spec, tok_spec, tok_spec] + page_specs + page_specs,
        out_specs=tok_spec,
        scratch_shapes=[pltpu.VMEM((rows, width), BF16), pltpu.VMEM((rows, LANES), F32),
                        pltpu.VMEM((rows, LANES), F32), pltpu.VMEM((rows, LANES), F32),
                        pltpu.VMEM((nblk, rows, width), F32)],
    )
    return pl.pallas_call(
        functools.partial(_moba_decode_kernel, n_heads=n_heads, t_new=t_new,
                          pages_per_blk=pages_per_blk, nblk=nblk),
        grid_spec=grid_spec,
        out_shape=jax.ShapeDtypeStruct((n, width), F32),
        compiler_params=_cparams("parallel", "arbitrary"),
        name="moba_decode",
    )(page_table, q, k_new, v_new, *([cache_kt] * npg), *([cache_vt] * npg))


DIL_DEC_TILE = 512


def _dilated_decode_tables(n_heads, t_new, n_prev):
    qi = (np.arange(n_heads * t_new) % t_new)[:, None]
    c_old = _dilated_multiplicity(n_prev + qi - np.arange(n_prev)[None, :])
    c_new = _dilated_multiplicity(qi - np.arange(LANES)[None, :])
    c_new = np.where(np.arange(LANES)[None, :] < t_new, c_new, 0)
    f = lambda c: (np.where(c > 0, 0.0, NEG_INF).astype(np.float32), c.astype(np.float32))
    return f(c_old) + f(c_new)


def _shifted_window_tile(cur_ref, nxt_ref, new_ref, out_ref, tail_ref, is_last, t_new):
    lane = lax.broadcasted_iota(jnp.int32, (cur_ref.shape[0], LANES), 1)
    nblk = cur_ref.shape[1] // LANES

    @pl.when(is_last)
    def _():
        tail_ref[...] = _pad_rows(new_ref[...], LANES).T

    @pl.when(jnp.logical_not(is_last))
    def _():
        tail_ref[...] = nxt_ref[...]

    rolled = [pltpu.roll(cur_ref[:, c * LANES:(c + 1) * LANES], LANES - t_new, 1) for c in range(nblk)]
    rolled.append(pltpu.roll(tail_ref[...], LANES - t_new, 1))
    for c in range(nblk):
        out_ref[:, c * LANES:(c + 1) * LANES] = jnp.where(lane < LANES - t_new, rolled[c], rolled[c + 1])


def _dilated_decode_kernel(q_ref, kn_ref, vn_ref, kc_ref, vc_ref, kx_ref, vx_ref, bo_ref, mo_ref, bn_ref, mn_ref,
                           o_ref, ko_ref, vo_ref, qbd_ref, m_ref, l_ref, acc_ref, tail_ref, *, n_heads, t_new):
    kt = pl.program_id(1)
    is_last = kt == pl.num_programs(1) - 1
    _shifted_window_tile(kc_ref, kx_ref, kn_ref, ko_ref, tail_ref, is_last, t_new)
    _shifted_window_tile(vc_ref, vx_ref, vn_ref, vo_ref, tail_ref, is_last, t_new)

    @pl.when(kt == 0)
    def _():
        qbd_ref[...] = _scaled_bf16(_block_diag_queries(q_ref[...], n_heads))
        m_ref[...] = jnp.full_like(m_ref, -jnp.inf)
        l_ref[...] = jnp.zeros_like(l_ref)
        acc_ref[...] = jnp.zeros_like(acc_ref)

    def step(s, pv, bias, mult):
        s = s + bias
        m_old = m_ref[...]
        m_new = jnp.maximum(m_old, jnp.max(s, axis=-1, keepdims=True))
        alpha = jnp.exp(m_old - m_new)
        p = jnp.exp(s - m_new) * mult
        l_ref[...] = alpha * l_ref[...] + jnp.sum(p, axis=-1, keepdims=True)
        acc_ref[...] = alpha * acc_ref[...] + pv(p.astype(BF16))
        m_ref[...] = m_new

    @pl.when(kt == 0)
    def _():
        k, v = _pad_rows(kn_ref[...], LANES).astype(BF16), _pad_rows(vn_ref[...], LANES).astype(BF16)
        step(lax.dot_general(qbd_ref[...], k, _NT, preferred_element_type=F32),
             lambda p: jnp.dot(p, v, preferred_element_type=F32), bn_ref[...], mn_ref[...])

    step(jnp.dot(qbd_ref[...], kc_ref[...].astype(BF16), preferred_element_type=F32),
         lambda p: lax.dot_general(p, vc_ref[...].astype(BF16), _NT, preferred_element_type=F32),
         bo_ref[...], mo_ref[...])

    @pl.when(kt == pl.num_programs(1) - 1)
    def _():
        o_ref[...] = _take_diag(acc_ref[...] / l_ref[...], n_heads, t_new)


def _dilated_decode(q, k_new, v_new, win_kt, win_vt, t_new):
    n, width = q.shape
    n_heads = width // HEAD_DIM
    n_seq, _, n_prev = win_kt.shape
    tile = DIL_DEC_TILE
    assert n_prev == W_MAX and n_prev % tile == 0 and t_new <= LANES
    rows = n_heads * t_new
    bo, mo, bn, mn = _dilated_decode_tables(n_heads, t_new, n_prev)
    tok_spec = pl.BlockSpec((t_new, width), lambda b, t: (b, 0))
    win_spec = pl.BlockSpec((None, width, tile), lambda b, t: (b, 0, t))
    last_lane_blk = n_prev // LANES - 1
    nxt_spec = pl.BlockSpec((None, width, LANES),
                            lambda b, t: (b, 0, jnp.minimum((t + 1) * (tile // LANES), last_lane_blk)))
    old_tab = pl.BlockSpec((rows, tile), lambda b, t: (0, t))
    new_tab = pl.BlockSpec((rows, LANES), lambda b, t: (0, 0))
    win_shape = jax.ShapeDtypeStruct(win_kt.shape, F32)
    return pl.pallas_call(
        functools.partial(_dilated_decode_kernel, n_heads=n_heads, t_new=t_new),
        grid=(n_seq, n_prev // tile),
        in_specs=[tok_spec, tok_spec, tok_spec, win_spec, win_spec, nxt_spec, nxt_spec,
                  old_tab, old_tab, new_tab, new_tab],
        out_specs=[tok_spec, win_spec, win_spec],
        out_shape=[jax.ShapeDtypeStruct((n, width), F32), win_shape, win_shape],
        scratch_shapes=[pltpu.VMEM((rows, width), BF16), pltpu.VMEM((rows, 1), F32),
                        pltpu.VMEM((rows, 1), F32), pltpu.VMEM((rows, width), F32),
                        pltpu.VMEM((width, LANES), F32)],
        compiler_params=_cparams("parallel", "arbitrary"),
        name="dilated_decode",
    )(q, k_new, v_new, win_kt, win_vt, win_kt, win_vt,
      jnp.asarray(bo), jnp.asarray(mo), jnp.asarray(bn), jnp.asarray(mn))


_AB_SEGS = (("a", (0,), 0, 1.0), ("a", (1,), 0, 1.0), (None, (2,), 0, 1.0), ("b", (3,), 0, 1.0),
            ("b", (4,), 0, ATT_SCALE), (None, (5,), 0, 1.0), (None, (6,), 0, 1.0))
_C_SEGS = tuple(("a" if o < 2 else None, (o,), c, 1.0) for o in range(3) for c in (0, SEG))
_C_SEGS_PROMPT = tuple((kind, dests if dests == (0,) else dests + (dests[0] + 2,), c, post)
                       for kind, dests, c, post in _C_SEGS)
_KV_OUT = (1, 2)


def _row_tile(n):
    return 512 if n % 512 == 0 else n


def _feature_major(x):
    lead = x.shape[:-3]
    t, h, dh = x.shape[-3:]
    nl = len(lead)
    return x.transpose(*range(nl), nl + 1, nl + 2, nl).reshape(*lead, h * dh, t)


def _token_major(xt, n_heads):
    lead = xt.shape[:-2]
    t = xt.shape[-1]
    nl = len(lead)
    return xt.reshape(*lead, n_heads, HEAD_DIM, t).transpose(*range(nl), nl + 2, nl, nl + 1)


def kernel(x_prompt, x_sample, cache_k_a, cache_v_a, page_table, state_ret, cache_win_k, cache_win_v,
           norm_mix, norm_ffn, norm_final, w_in_ab, w_out_ab, ret_gn_w, w_in_c, w_out_c,
           ffn_w_gate, ffn_w_up, ffn_w_down):
    bp, tp, d = x_prompt.shape
    bs, ts, _ = x_sample.shape
    page = cache_k_a.shape[2]
    past_len = page_table.shape[1] * page
    h_a = cache_k_a.shape[3]
    wa = h_a * HEAD_DIM
    wb = w_in_ab.shape[2] - 3 * wa
    assert wb == 4 * wa and w_in_ab.shape[0] == 1 and w_in_c.shape[0] == 1 and norm_mix.shape[0] == 2
    h_b = wa // HEAD_DIM
    h_c = w_in_c.shape[2] // (3 * HEAD_DIM)

    pos_p = jnp.arange(tp, dtype=jnp.int32)
    pos_s = past_len + jnp.arange(ts, dtype=jnp.int32)
    tile_s = lambda tabs: tuple(jnp.tile(t, (bs, 1)) for t in tabs)
    tabs = {
        "p": (_rope_tables(pos_p, ROT_DIM, ROPE_THETA), _rope_tables(pos_p, HEAD_DIM, RET_THETA)),
        "s": (tile_s(_rope_tables(pos_s, ROT_DIM, ROPE_THETA)), tile_s(_rope_tables(pos_s, HEAD_DIM, RET_THETA))),
    }
    bf = lambda w: w.astype(BF16)
    w_in_ab_bf, w_out_ab_bf, w_in_c_bf, w_out_c_bf = bf(w_in_ab[0]), bf(w_out_ab[0]), bf(w_in_c[0]), bf(w_out_c[0])
    wg, wu, wd = bf(ffn_w_gate), bf(ffn_w_up), bf(ffn_w_down)

    xp = x_prompt.reshape(bp * tp, d)
    xs = x_sample.reshape(bs * ts, d)
    tm_p, tm_s = _row_tile(bp * tp), _row_tile(bs * ts)
    assert tp % tm_p == 0

    ab_widths = (wa,) * 7
    qa_p, kat_p, vat_p, qb_p, kb_p, vb_p, gb_p = _project(
        xp, norm_mix[0], w_in_ab_bf, *tabs["p"], _AB_SEGS, ab_widths, tm_p, _KV_OUT)
    qa_s, ka_s, va_s, qb_s, kb_s, vb_s, gb_s = _project(
        xs, norm_mix[0], w_in_ab_bf, *tabs["s"], _AB_SEGS, ab_widths, tm_s)

    oa_p = _moba_prompt(qa_p, kat_p, vat_p, bp, tp)
    oa_s = _moba_decode(qa_s, ka_s, va_s, _feature_major(cache_k_a[0]), _feature_major(cache_v_a[0]),
                        page_table, ts)

    chunk_p = math.gcd(tp, RET_CHUNK)
    ob_p, ret_p = _retention(qb_p, kb_p, vb_p, gb_p, ret_gn_w[0],
                             jnp.zeros((bp, h_b, HEAD_DIM, HEAD_DIM), F32), bp, tp, chunk_p,
                             math.gcd(tp // chunk_p, 8))
    ob_s, ret_s = _retention(qb_s, kb_s, vb_s, gb_s, ret_gn_w[0], state_ret[0], bs, ts,
                             math.gcd(ts, RET_CHUNK), 1)

    xp = _mix_ffn([oa_p, ob_p], w_out_ab_bf, xp, norm_ffn[0], wg[0], wu[0], wd[0], norm_final, False, tm_p)
    xs = _mix_ffn([oa_s, ob_s], w_out_ab_bf, xs, norm_ffn[0], wg[0], wu[0], wd[0], norm_final, False, tm_s)

    wc = h_c * HEAD_DIM
    q_p, kt_p, vt_p, k_p, v_p = _project(xp, norm_mix[1], w_in_c_bf, *tabs["p"], _C_SEGS_PROMPT, (wc,) * 5,
                                         tm_p, _KV_OUT)
    q_s, k_s, v_s = _project(xs, norm_mix[1], w_in_c_bf, *tabs["s"], _C_SEGS, (wc,) * 3, tm_s)
    n_prev = cache_win_k.shape[2]
    o_p = _dilated_prompt(q_p, k_p, v_p, kt_p, vt_p, bp, tp)
    o_s, win_kt, win_vt = _dilated_decode(q_s, k_s, v_s, _feature_major(cache_win_k[0]),
                                          _feature_major(cache_win_v[0]), ts)
    y_p = _mix_ffn([o_p], w_out_c_bf, xp, norm_ffn[1], wg[1], wu[1], wd[1], norm_final, True, tm_p)
    y_s = _mix_ffn([o_s], w_out_c_bf, xs, norm_ffn[1], wg[1], wu[1], wd[1], norm_final, True, tm_s)

    keep_p = min(W_MAX, tp)
    return (
        y_p.reshape(bp, tp, d), y_s.reshape(bs, ts, d),
        _token_major(kat_p, h_a)[None], _token_major(vat_p, h_a)[None], ret_p[None],
        _token_major(kt_p[:, :, tp - keep_p:], h_c)[None], _token_major(vt_p[:, :, tp - keep_p:], h_c)[None],
        ka_s.reshape(1, bs, ts, h_a, HEAD_DIM), va_s.reshape(1, bs, ts, h_a, HEAD_DIM), ret_s[None],
        _token_major(win_kt, h_c)[None], _token_major(win_vt, h_c)[None],
    )
```

```python
import functools
import math

import jax
import jax.numpy as jnp
import numpy as np
from jax import lax
from jax.experimental import pallas as pl
from jax.experimental.pallas import tpu as pltpu

F32 = jnp.float32
BF16 = jnp.bfloat16

HEAD_DIM = 64
LANES = 128
ROT_DIM = HEAD_DIM // 4
ROPE_THETA = 500000.0
RET_THETA = 10000.0
MOBA_BLOCK = 256
MOBA_TOPK = 3
RET_CHUNK = 128
DILATED = ((128, 1), (512, 4), (2048, 16))
W_MAX = max(w for w, _ in DILATED)
EPS = 1e-6
NEG_INF = -1e30
ATT_SCALE = HEAD_DIM ** -0.5
VMEM_LIMIT = 56 * 1024 * 1024

_NT = (((1,), (1,)), ((), ()))
_TN = (((0,), (0,)), ((), ()))


def _cparams(*sem):
    return pltpu.CompilerParams(dimension_semantics=sem, vmem_limit_bytes=VMEM_LIMIT)


def _rms(x, w):
    ms = jnp.mean(x * x, axis=-1, keepdims=True)
    return x * lax.rsqrt(ms + EPS) * w


SEG = 512


def _proj_kernel(x_ref, nw_ref, w_ref, ca_ref, na_ref, pa_ref, cb_ref, nb_ref, pb_ref, *out_refs, segs, transposed):
    xn = _rms(x_ref[...], nw_ref[...]).astype(BF16)
    for s, (kind, dests, col, post) in enumerate(segs):
        acc = jnp.dot(xn, w_ref[:, s * SEG:(s + 1) * SEG], preferred_element_type=F32)
        for c in range(SEG // LANES):
            r = acc[:, c * LANES:(c + 1) * LANES]
            if kind is not None:
                c_ref, n_ref, p_ref, shift = (ca_ref, na_ref, pa_ref, ROT_DIM // 2) if kind == "a" else (
                    cb_ref, nb_ref, pb_ref, HEAD_DIM // 2)
                r = (r * c_ref[...] + pltpu.roll(r, LANES - shift, 1) * n_ref[...]
                     + pltpu.roll(r, shift, 1) * p_ref[...])
                if post != 1.0:
                    r = r * post
            lo = col + c * LANES
            for oi in dests:
                if oi in transposed:
                    out_refs[oi][lo:lo + LANES, :] = r.T
                else:
                    out_refs[oi][:, lo:lo + LANES] = r


def _rope_tables(pos, rot_dim, theta):
    half = rot_dim // 2
    inv = theta ** (-jnp.arange(half, dtype=F32) / half)
    ang = pos.astype(F32)[:, None] * inv[None, :]
    cos, sin = jnp.cos(ang), jnp.sin(ang)
    lane = np.arange(LANES) % HEAD_DIM
    idx = lane % half
    cos_t = jnp.where(lane < rot_dim, cos[:, idx], 1.0)
    neg_t = jnp.where(lane < half, -sin[:, idx], 0.0)
    pos_t = jnp.where((lane >= half) & (lane < rot_dim), sin[:, idx], 0.0)
    return cos_t.astype(F32), neg_t.astype(F32), pos_t.astype(F32)


def _project(x, nw, w_bf, tabs_a, tabs_b, segs, out_widths, tm, transposed=()):
    n, d = x.shape
    t_tab = tabs_a[0].shape[0]
    nt = t_tab // tm
    tab_spec = pl.BlockSpec((tm, LANES), lambda i: (i % nt, 0))
    out_specs, out_shape = [], []
    for oi, w in enumerate(out_widths):
        if oi in transposed:
            out_specs.append(pl.BlockSpec((None, w, tm), lambda i: (i // nt, 0, i % nt)))
            out_shape.append(jax.ShapeDtypeStruct((n // t_tab, w, t_tab), F32))
        else:
            out_specs.append(pl.BlockSpec((tm, w), lambda i: (i, 0)))
            out_shape.append(jax.ShapeDtypeStruct((n, w), F32))
    return pl.pallas_call(
        functools.partial(_proj_kernel, segs=segs, transposed=tuple(transposed)),
        grid=(n // tm,),
        in_specs=[pl.BlockSpec((tm, d), lambda i: (i, 0)),
                  pl.BlockSpec((1, d), lambda i: (0, 0)),
                  pl.BlockSpec(w_bf.shape, lambda i: (0, 0))] + [tab_spec] * 6,
        out_specs=out_specs,
        out_shape=out_shape,
        compiler_params=_cparams("parallel"),
        name="rms_proj_rope",
    )(x, nw.reshape(1, d), w_bf, *tabs_a, *tabs_b)


FFN_CHUNKS = 2


def _mix_ffn_kernel(*refs, nparts, final):
    parts = refs[:nparts]
    wo_ref, x_ref, nw_ref, wg_ref, wu_ref, wd_ref, fw_ref, o_ref = refs[nparts:]
    a = jnp.concatenate([p[...].astype(BF16) for p in parts], axis=1) if nparts > 1 else parts[0][...].astype(BF16)
    x = x_ref[...] + jnp.dot(a, wo_ref[...], preferred_element_type=F32)
    xn = _rms(x, nw_ref[...]).astype(BF16)
    tf = wg_ref.shape[1] // FFN_CHUNKS
    acc = x
    for c in range(FFN_CHUNKS):
        g = jnp.dot(xn, wg_ref[:, c * tf:(c + 1) * tf], preferred_element_type=F32)
        u = jnp.dot(xn, wu_ref[:, c * tf:(c + 1) * tf], preferred_element_type=F32)
        h = (g * jax.nn.sigmoid(g) * u).astype(BF16)
        acc = acc + jnp.dot(h, wd_ref[c * tf:(c + 1) * tf, :], preferred_element_type=F32)
    if final:
        acc = _rms(acc, fw_ref[...])
    o_ref[...] = acc


def _mix_ffn(parts, wo, x, nw, wg, wu, wd, fw, final, tm):
    n, d = x.shape
    dff = wg.shape[1]
    const = lambda i: (0, 0)
    row = lambda w: pl.BlockSpec((tm, w), lambda i: (i, 0))
    return pl.pallas_call(
        functools.partial(_mix_ffn_kernel, nparts=len(parts), final=final),
        grid=(n // tm,),
        in_specs=[row(p.shape[1]) for p in parts]
        + [pl.BlockSpec(wo.shape, const), row(d), pl.BlockSpec((1, d), const),
           pl.BlockSpec((d, dff), const), pl.BlockSpec((d, dff), const), pl.BlockSpec((dff, d), const),
           pl.BlockSpec((1, d), const)],
        out_specs=row(d),
        out_shape=jax.ShapeDtypeStruct((n, d), F32),
        compiler_params=_cparams("parallel"),
        name="outproj_swiglu",
    )(*parts, wo, x, nw.reshape(1, d), wg, wu, wd, fw.reshape(1, d))


ATT_TILE = 256
ATT_GROUP = 4
MOBA_SUBTILES = 4
ATT_CHUNK = 64


def _unit_scores(kb, qb, bias_fn):
    s = jnp.dot(kb, qb, preferred_element_type=F32)
    chunks = [s[r:r + ATT_CHUNK] + bias_fn(r, r + ATT_CHUNK) for r in range(0, s.shape[0], ATT_CHUNK)]
    return chunks, jnp.max(functools.reduce(jnp.maximum, chunks), axis=0, keepdims=True)


def _unit_values(chunks, m_u, vts, mult_fn):
    ps, tot = [], None
    for i, c in enumerate(chunks):
        p = jnp.exp(c - m_u)
        if mult_fn is not None:
            p = p * mult_fn(i * ATT_CHUNK, (i + 1) * ATT_CHUNK)
        tot = p if tot is None else tot + p
        ps.append(p.astype(BF16))
    per = len(ps) // len(vts)
    pv = sum(jnp.dot(vt, jnp.concatenate(ps[b * per:(b + 1) * per], axis=0), preferred_element_type=F32)
             for b, vt in enumerate(vts))
    return jnp.sum(tot, axis=0, keepdims=True), pv


def _attend_unit(kb, q2, bias_fn, vts, mult_fn=None):
    chunks, m_u = _unit_scores(kb, q2, bias_fn)
    return (m_u,) + _unit_values(chunks, m_u, vts, mult_fn)


def _merge(parts):
    m = functools.reduce(jnp.maximum, [p[0] for p in parts])
    ws = [jnp.exp(p[0] - m) for p in parts]
    return m, sum(w * p[1] for w, p in zip(ws, parts)), sum(w * p[2] for w, p in zip(ws, parts))


def _head_queries(q_t, feat):
    return [jnp.where(feat // HEAD_DIM == hh, q_t, 0.0) for hh in range(2)]


def _scaled_bf16(q):
    return (q * ATT_SCALE).astype(BF16)


def _both(x):
    return jnp.concatenate([x, x], axis=1)


def _pick_heads(x, feat):
    nq = x.shape[1] // 2
    return jnp.where(feat // HEAD_DIM == 0, x[:, :nq], x[:, nq:])


def _moba_prompt_step(step, q_ref, kt_ref, vt_ref, o_ref, kb_ref, vb_ref, kmean_ref, bias_ref, causal_ref, *, nblk):
    blk, grp = ATT_TILE, ATT_GROUP
    nsub = MOBA_SUBTILES

    @pl.when(step == 0)
    def _():
        for n in range(nblk):
            kblk = kt_ref[:, n * blk:(n + 1) * blk].T
            kmean_ref[n:n + 1, :] = jnp.sum(kblk, axis=0, keepdims=True) * (1.0 / blk)
            kb_ref[n * blk:(n + 1) * blk, :] = kblk.astype(BF16)
            vb_ref[n] = vt_ref[:, n * blk:(n + 1) * blk].astype(BF16)
        key_i = lax.broadcasted_iota(jnp.int32, (blk, 2 * blk), 0)
        qry_i = lax.broadcasted_iota(jnp.int32, (blk, 2 * blk), 1) % blk
        causal_ref[...] = jnp.where(key_i <= qry_i, 0.0, NEG_INF)

    feat = lax.broadcasted_iota(jnp.int32, (LANES, blk), 0)
    blk_id = lax.broadcasted_iota(jnp.int32, (nblk, 2 * blk), 0)
    blk_f = blk_id.astype(F32)
    qb2s = []
    for sub in range(nsub):
        qt = step * nsub + sub
        q2 = jnp.concatenate(_head_queries(q_ref[sub * blk:(sub + 1) * blk, :].T, feat), axis=1)
        sc = jnp.dot(kmean_ref[...], q2, preferred_element_type=F32, precision=lax.Precision.HIGHEST)
        sc = jnp.where(blk_id < qt, sc, NEG_INF)
        sel = jnp.zeros((nblk, 2 * blk), F32)
        for r in range(MOBA_TOPK):
            mx = jnp.max(sc, axis=0, keepdims=True)
            first = jnp.min(jnp.where(sc == mx, blk_f, float(nblk)), axis=0, keepdims=True)
            pick = blk_f == first
            sel = jnp.maximum(sel, jnp.where(pick, jnp.where(r < qt, 1.0, 0.0), 0.0))
            sc = jnp.where(pick, -jnp.inf, sc)
        bias_ref[sub] = jnp.where((sel > 0.0) | (blk_id == qt), 0.0, NEG_INF)
        qb2s.append(_scaled_bf16(q2))

    def group(sub, g, with_own):
        def bias_fn(r0, r1):
            n, r = g * grp + r0 // blk, r0 % blk
            row = bias_ref[sub, pl.ds(n, 1), :]
            if not with_own:
                return row
            return row + jnp.where(n == step * nsub + sub, causal_ref[r:r + (r1 - r0), :], 0.0)

        kb = kb_ref[pl.ds(pl.multiple_of(g * grp * blk, blk), grp * blk), :]
        return _attend_unit(kb, qb2s[sub], bias_fn, [vb_ref[g * grp + b] for b in range(grp)])

    g_own = step * nsub // grp

    def body(g, carry):
        return sum((_merge([carry[3 * sub:3 * sub + 3], group(sub, g, False)]) for sub in range(nsub)), ())

    carry = lax.fori_loop(0, g_own, body, sum((group(sub, g_own, True) for sub in range(nsub)), ()))
    for sub in range(nsub):
        o_ref[sub * blk:(sub + 1) * blk, :] = _pick_heads(carry[3 * sub + 2] / carry[3 * sub + 1], feat).T


def _dilated_multiplicity(delta):
    c = np.zeros(delta.shape, np.int32)
    for window, dil in DILATED:
        c += ((delta >= 0) & (delta <= window) & (delta % dil == 0)).astype(np.int32)
    return c


DIL_NEAR = 2
DIL_FAR_CLASSES_PER_STEP = 2
assert DILATED[-1][1] % DIL_FAR_CLASSES_PER_STEP == 0
DIL_NEAR_SUBTILES = 4
DIL_FAR_STRIDE = DILATED[-1][1]
DIL_FAR_REACH = DILATED[-1][0] // DIL_FAR_STRIDE
assert DILATED[-2][0] <= ATT_TILE * DIL_NEAR and ATT_TILE % DIL_FAR_STRIDE == 0 and DIL_FAR_REACH <= ATT_TILE


def _mask_bias(valid):
    return np.where(valid, 0.0, NEG_INF).astype(np.float32)


def _dilated_near_tables():
    ki = np.arange(ATT_TILE)[:, None]
    qi = np.arange(ATT_TILE)[None, :]
    c = np.stack([_dilated_multiplicity(ATT_TILE * (DIL_NEAR - i) + qi - ki) for i in range(DIL_NEAR + 1)])
    return _mask_bias(c > 0), c.astype(np.float32)


def _dilated_far_tables():
    per_tile = ATT_TILE // DIL_FAR_STRIDE

    def table(k0, nk, q0):
        ka = k0 + np.arange(nk)[:, None]
        qa = q0 + np.arange(ATT_TILE)[None, :]
        return _mask_bias((qa - ka <= DIL_FAR_REACH) & (ka // per_tile < qa // per_tile - DIL_NEAR))

    return table(0, ATT_TILE, 0), table(ATT_TILE - DIL_FAR_REACH, ATT_TILE + DIL_FAR_REACH, ATT_TILE)


def _dilated_far_kernel(q_ref, k_ref, v_ref, b0_ref, b1_ref, o_ref, lse_ref, *, seq):
    stride, blk, reach = DIL_FAR_STRIDE, ATT_TILE, DIL_FAR_REACH
    ntile = seq // stride // blk
    feat = lax.broadcasted_iota(jnp.int32, (LANES, blk), 0)

    def one_class(cls):
        for tau in range(ntile):
            k0, nk, tab = (0, blk, b0_ref) if tau == 0 else (tau * blk - reach, blk + reach, b1_ref)
            rows_q = pl.ds(cls + stride * blk * tau, blk, stride=stride)
            rows_k = pl.ds(cls + stride * k0, nk, stride=stride)
            qb2 = _scaled_bf16(jnp.concatenate(_head_queries(q_ref[rows_q, :].T, feat), axis=1))
            kb = k_ref[rows_k, :].astype(BF16)
            vt = v_ref[rows_k, :].T.astype(BF16)
            m_u, l_u, pv = _attend_unit(kb, qb2, lambda r0, r1: _both(tab[r0:r1, :]), [vt])
            o_ref[rows_q, :] = _pick_heads(pv / l_u, feat).T
            lse_ref[rows_q, :] = _pick_heads(jnp.broadcast_to(m_u + jnp.log(l_u), pv.shape), feat).T

    def body(i, carry):
        for c in range(DIL_FAR_CLASSES_PER_STEP):
            one_class(i * DIL_FAR_CLASSES_PER_STEP + c)
        return carry

    lax.fori_loop(0, stride // DIL_FAR_CLASSES_PER_STEP, body, 0)


def _dilated_near_kernel(q_ref, kt_ref, vt_ref, far_ref, lse_ref, bias_ref, mult_ref, o_ref,
                         kb_ref, vb_ref, tab_ref, *, nblk):
    blk, near, nsub = ATT_TILE, DIL_NEAR, DIL_NEAR_SUBTILES
    step = pl.program_id(2)

    @pl.when(step == 0)
    def _():
        kb_ref[0:near * blk, :] = jnp.zeros((near * blk, LANES), BF16)
        for n in range(near):
            vb_ref[n] = jnp.zeros((LANES, blk), BF16)
        for n in range(nblk):
            kb_ref[(n + near) * blk:(n + near + 1) * blk, :] = kt_ref[:, n * blk:(n + 1) * blk].T.astype(BF16)
            vb_ref[n + near] = vt_ref[:, n * blk:(n + 1) * blk].astype(BF16)

    @pl.when(step * nsub < near + nsub)
    def _():
        for sub in range(nsub):
            for i in range(near + 1):
                tab_ref[sub, i] = jnp.where(step * nsub + sub - near + i >= 0, bias_ref[i], NEG_INF)

    feat = lax.broadcasted_iota(jnp.int32, (LANES, blk), 0)
    for sub in range(nsub):
        qt = step * nsub + sub
        rows = slice(sub * blk, (sub + 1) * blk)
        qb2 = _scaled_bf16(jnp.concatenate(_head_queries(q_ref[rows, :].T, feat), axis=1))
        m2, l2, a2 = _attend_unit(
            kb_ref[pl.ds(pl.multiple_of(qt * blk, blk), (near + 1) * blk), :], qb2,
            lambda r0, r1: _both(tab_ref[sub, r0 // blk, r0 % blk:r0 % blk + (r1 - r0), :]),
            [vb_ref[qt + i] for i in range(near + 1)],
            lambda r0, r1: _both(mult_ref[r0 // blk, r0 % blk:r0 % blk + (r1 - r0), :]))
        m_n = _pick_heads(jnp.broadcast_to(m2, a2.shape), feat)
        l_n = _pick_heads(jnp.broadcast_to(l2, a2.shape), feat)
        a_n = _pick_heads(a2, feat)
        lse_f = lse_ref[rows, :].T
        m = jnp.maximum(m_n, lse_f)
        w_n, w_f = jnp.exp(m_n - m), jnp.exp(lse_f - m)
        o_ref[rows, :] = ((a_n * w_n + far_ref[rows, :].T * w_f) / (l_n * w_n + w_f)).T


def _dilated_prompt(q, k, v, kt, vt, batch, seq):
    n, width = q.shape
    blk, near, stride = ATT_TILE, DIL_NEAR, DIL_FAR_STRIDE
    assert seq % (blk * stride) == 0
    nblk = seq // blk
    npair = width // LANES
    const = lambda nd: (lambda *_: (0,) * nd)
    full = lambda t: pl.BlockSpec(t.shape, const(t.ndim))

    far_tabs = [jnp.asarray(t) for t in _dilated_far_tables()]
    seq_spec = pl.BlockSpec((seq, LANES), lambda b, p: (b, p))
    o_far, lse_far = pl.pallas_call(
        functools.partial(_dilated_far_kernel, seq=seq),
        grid=(batch, npair),
        in_specs=[seq_spec, seq_spec, seq_spec] + [full(t) for t in far_tabs],
        out_specs=[seq_spec, seq_spec],
        out_shape=[jax.ShapeDtypeStruct((n, width), F32)] * 2,
        compiler_params=_cparams("parallel", "parallel"),
        name="dilated_far",
    )(q, k, v, *far_tabs)

    near_tabs = [jnp.asarray(t) for t in _dilated_near_tables()]
    kv_spec = pl.BlockSpec((None, LANES, seq), lambda b, p, t: (b, p, 0))
    nsub = DIL_NEAR_SUBTILES
    assert nblk % nsub == 0
    steps = nblk // nsub
    q_spec = pl.BlockSpec((nsub * blk, LANES), lambda b, p, t: (b * steps + t, p))
    return pl.pallas_call(
        functools.partial(_dilated_near_kernel, nblk=nblk),
        grid=(batch, npair, steps),
        in_specs=[q_spec, kv_spec, kv_spec, q_spec, q_spec] + [full(t) for t in near_tabs],
        out_specs=q_spec,
        out_shape=jax.ShapeDtypeStruct((n, width), F32),
        scratch_shapes=[pltpu.VMEM(((nblk + near) * blk, LANES), BF16), pltpu.VMEM((nblk + near, LANES, blk), BF16),
                        pltpu.VMEM((nsub, near + 1, blk, blk), F32)],
        compiler_params=_cparams("parallel", "parallel", "arbitrary"),
        name="dilated_near",
    )(q, kt, vt, o_far, lse_far, *near_tabs)


RET_ROWS = 128


def _retention_tables(n_heads, chunk):
    log_g = jnp.log1p(-jnp.exp2(-5.0 - jnp.arange(n_heads, dtype=F32)))
    i = jnp.arange(chunk, dtype=F32)
    diff = i[:, None] - i[None, :]
    causal = diff >= 0
    decay = jnp.where(causal[None], jnp.exp(jnp.where(causal, diff, 0.0)[None] * log_g[:, None, None]), 0.0)
    q_dec = jnp.exp((i + 1.0)[:, None] * log_g[None, :])
    k_dec = jnp.exp((chunk - 1.0 - i)[:, None] * log_g[None, :])
    c_dec = jnp.exp(chunk * log_g)
    pad = RET_ROWS - chunk
    npair = n_heads // 2
    decay = jnp.pad(decay, ((0, 0), (0, pad), (0, pad))).reshape(npair, 2 * RET_ROWS, RET_ROWS)
    expand = lambda t: jnp.pad(jnp.repeat(t, HEAD_DIM, axis=1), ((0, pad), (0, 0))).reshape(
        RET_ROWS, npair, LANES).transpose(1, 0, 2)
    q_tab, k_tab = expand(q_dec), expand(k_dec)
    same_head = (np.arange(LANES)[:, None] // HEAD_DIM) == (np.arange(LANES)[None, :] // HEAD_DIM)
    c_tab = jnp.where(same_head[None], jnp.repeat(c_dec, HEAD_DIM).reshape(npair, LANES, 1), 0.0)
    return decay.astype(F32), q_tab.astype(F32), k_tab.astype(F32), c_tab.astype(F32)


def _retention_kernel(q_ref, k_ref, v_ref, g_ref, gn_ref, s0_ref, dec_ref, qd_ref, kd_ref, cd_ref,
                      o_ref, sout_ref, st_ref, *, chunk, nchunk):
    c = pl.program_id(2)

    @pl.when(c == 0)
    def _():
        zero = jnp.zeros((HEAD_DIM, HEAD_DIM), F32)
        st_ref[...] = jnp.concatenate([jnp.concatenate([s0_ref[0], zero], axis=1),
                                       jnp.concatenate([zero, s0_ref[1]], axis=1)], axis=0)

    pad = RET_ROWS - chunk
    lane = lax.broadcasted_iota(jnp.int32, (RET_ROWS, LANES), 1)
    head0 = lane < HEAD_DIM
    same_head = (lax.broadcasted_iota(jnp.int32, (LANES, LANES), 0) // HEAD_DIM) == (
        lax.broadcasted_iota(jnp.int32, (LANES, LANES), 1) // HEAD_DIM)

    def rows(ref, r):
        x = ref[pl.ds(r, chunk), :]
        return x if pad == 0 else jnp.concatenate([x, jnp.zeros((pad, LANES), F32)], axis=0)

    def seg_mean(x):
        s0 = jnp.sum(jnp.where(head0, x, 0.0), axis=-1, keepdims=True)
        s1 = jnp.sum(jnp.where(head0, 0.0, x), axis=-1, keepdims=True)
        return jnp.where(head0, s0, s1) * (1.0 / HEAD_DIM)

    def one_chunk(j, state):
        r = j * chunk
        q, k, v, g = rows(q_ref, r), rows(k_ref, r), rows(v_ref, r), rows(g_ref, r)
        kb, vb = k.astype(BF16), v.astype(BF16)
        q2 = jnp.concatenate([jnp.where(head0, q, 0.0), jnp.where(head0, 0.0, q)], axis=0).astype(BF16)
        attn = lax.dot_general(q2, kb, _NT, preferred_element_type=F32) * dec_ref[...]
        inn = jnp.dot(attn.astype(BF16), vb, preferred_element_type=F32)
        inner = jnp.where(head0, inn[:RET_ROWS], inn[RET_ROWS:])
        cross = jnp.dot(q.astype(BF16), state.astype(BF16), preferred_element_type=F32) * qd_ref[...]
        upd = lax.dot_general((k * kd_ref[...]).astype(BF16), vb, _TN, preferred_element_type=F32)
        o = inner + cross
        mu = seg_mean(o)
        var = seg_mean(jnp.square(o - mu))
        y = (o - mu) * lax.rsqrt(var + EPS) * gn_ref[...]
        y = y * (g * jax.nn.sigmoid(g))
        o_ref[r:r + chunk, :] = y[:chunk]
        return state * cd_ref[...] + jnp.where(same_head, upd, 0.0)

    state = st_ref[...]
    for j in range(nchunk):
        state = one_chunk(j, state)
    st_ref[...] = state

    @pl.when(c == pl.num_programs(2) - 1)
    def _():
        sout_ref[0] = st_ref[0:HEAD_DIM, 0:HEAD_DIM]
        sout_ref[1] = st_ref[HEAD_DIM:, HEAD_DIM:]


def _retention(q, k, v, g, gn_w, s0, batch, seq, chunk, nchunk):
    n, width = q.shape
    n_heads = width // HEAD_DIM
    npair = width // LANES
    rows = chunk * nchunk
    steps = seq // rows
    dec, qd, kd, cd = _retention_tables(n_heads, chunk)
    row_spec = pl.BlockSpec((rows, LANES), lambda b, p, c: (b * steps + c, p))
    st_spec = pl.BlockSpec((None, 2, HEAD_DIM, HEAD_DIM), lambda b, p, c: (b, p, 0, 0))
    tab = lambda t: pl.BlockSpec((None,) + t.shape[1:], lambda b, p, c: (p, 0, 0))
    out, s_new = pl.pallas_call(
        functools.partial(_retention_kernel, chunk=chunk, nchunk=nchunk),
        grid=(batch, npair, steps),
        in_specs=[row_spec, row_spec, row_spec, row_spec, pl.BlockSpec((1, LANES), lambda b, p, c: (0, p)),
                  st_spec, tab(dec), tab(qd), tab(kd), tab(cd)],
        out_specs=[row_spec, st_spec],
        out_shape=[jax.ShapeDtypeStruct((n, width), F32),
                   jax.ShapeDtypeStruct((batch, n_heads, HEAD_DIM, HEAD_DIM), F32)],
        scratch_shapes=[pltpu.VMEM((LANES, LANES), F32)],
        compiler_params=_cparams("parallel", "parallel", "arbitrary"),
        name="retention",
    )(q, k, v, g, gn_w.reshape(1, width), s0, dec, qd, kd, cd)
    return out, s_new


def _block_diag_queries(q, n_heads):
    t = q.shape[0]
    rep = jnp.concatenate([q] * n_heads, axis=0)
    rows = lax.broadcasted_iota(jnp.int32, rep.shape, 0) // t
    cols = lax.broadcasted_iota(jnp.int32, rep.shape, 1) // HEAD_DIM
    return jnp.where(rows == cols, rep, 0.0)


def _take_diag(o, n_heads, t):
    cols = lax.broadcasted_iota(jnp.int32, (t, o.shape[1]), 1) // HEAD_DIM
    out = jnp.zeros((t, o.shape[1]), F32)
    for h in range(n_heads):
        out = jnp.where(cols == h, o[h * t:(h + 1) * t, :], out)
    return out


def _fold_pairs(o, n_heads, t):
    rows = lax.broadcasted_iota(jnp.int32, (o.shape[0], LANES), 0) // (2 * t)
    out = jnp.zeros((o.shape[0], LANES), F32)
    for c in range(n_heads // 2):
        out = jnp.where(rows == c, o[:, c * LANES:(c + 1) * LANES], out)
    return out


def _unfold_pairs(a, n_heads, t):
    lane = lax.broadcasted_iota(jnp.int32, (t, LANES), 1)
    return jnp.concatenate([jnp.where(lane < HEAD_DIM, a[2 * c * t:(2 * c + 1) * t], a[(2 * c + 1) * t:(2 * c + 2) * t])
                            for c in range(n_heads // 2)], axis=1)


def _pad_rows(x, rows):
    return jnp.concatenate([x, jnp.zeros((rows - x.shape[0], x.shape[1]), x.dtype)], axis=0)


MOBA_PAGES_PER_STEP = 32


def _moba_decode_step(g, is_last, q_ref, kn_ref, vn_ref, kp, vp, o_ref, qbd_ref, m_ref, l_ref, sc_ref, oblk_ref,
                      *, n_heads, t_new, pages_per_blk, nblk):
    npg = len(kp)
    rows = n_heads * t_new
    lane = lax.broadcasted_iota(jnp.int32, (rows, LANES), 1)
    page = kp[0].shape[1]
    blk_keys = pages_per_blk * page
    step_blks = npg // pages_per_blk

    @pl.when(g == 0)
    def _():
        qbd_ref[...] = _scaled_bf16(_block_diag_queries(q_ref[...], n_heads))
        m_ref[...] = jnp.zeros_like(m_ref)
        l_ref[...] = jnp.zeros_like(l_ref)
        sc_ref[...] = jnp.full_like(sc_ref, -jnp.inf)

    kt_all = jnp.concatenate([kp[i][...].astype(BF16) for i in range(npg)], axis=1)
    s_all = jnp.dot(qbd_ref[...], kt_all, preferred_element_type=F32)
    m_new, l_new, sc_new = m_ref[...], l_ref[...], sc_ref[...]
    for jb in range(step_blks):
        n = g * step_blks + jb
        s = s_all[:, jb * blk_keys:(jb + 1) * blk_keys]
        mb = jnp.max(s, axis=-1, keepdims=True)
        p = jnp.exp(s - mb)
        oblk_ref[n] = _fold_pairs(sum(lax.dot_general(p[:, a * page:(a + 1) * page].astype(BF16),
                                                      vp[jb * pages_per_blk + a][...].astype(BF16),
                                                      _NT, preferred_element_type=F32)
                                      for a in range(pages_per_blk)), n_heads, t_new)
        here = lane == n
        m_new = jnp.where(here, mb, m_new)
        l_new = jnp.where(here, jnp.sum(p, axis=-1, keepdims=True), l_new)
        sc_new = jnp.where(here, jnp.sum(s, axis=-1, keepdims=True) * (1.0 / blk_keys), sc_new)
    m_ref[...], l_ref[...], sc_ref[...] = m_new, l_new, sc_new

    @pl.when(is_last)
    def _():
        s = lax.dot_general(qbd_ref[...], _pad_rows(kn_ref[...], LANES).astype(BF16), _NT,
                            preferred_element_type=F32)
        qry = lax.broadcasted_iota(jnp.int32, (rows, LANES), 0) % t_new
        s = jnp.where(lane <= qry, s, NEG_INF)
        m_own = jnp.max(s, axis=-1, keepdims=True)
        p = jnp.exp(s - m_own)
        l_own = jnp.sum(p, axis=-1, keepdims=True)
        o_own = jnp.dot(p.astype(BF16), _pad_rows(vn_ref[...], LANES).astype(BF16), preferred_element_type=F32)
        sc = sc_ref[...]
        lane_f = lane.astype(F32)
        sel = lane < 0
        for _ in range(MOBA_TOPK):
            mx = jnp.max(sc, axis=-1, keepdims=True)
            first = jnp.min(jnp.where(sc == mx, lane_f, float(LANES)), axis=-1, keepdims=True)
            pick = lane_f == first
            sel = sel | pick
            sc = jnp.where(pick, -jnp.inf, sc)
        m_all = m_ref[...]
        m_fin = jnp.maximum(jnp.max(jnp.where(sel, m_all, -jnp.inf), axis=-1, keepdims=True), m_own)
        w = jnp.where(sel, jnp.exp(m_all - m_fin), 0.0)
        w_own = jnp.exp(m_own - m_fin)
        l_fin = jnp.sum(w * l_ref[...], axis=-1, keepdims=True) + w_own * l_own

        def body(n, acc):
            col = jnp.sum(jnp.where(lane == n, w, 0.0), axis=-1, keepdims=True)
            return acc + col * oblk_ref[n]

        acc = lax.fori_loop(0, nblk, body, w_own * _fold_pairs(o_own, n_heads, t_new))
        o_ref[...] = _unfold_pairs(acc / l_fin, n_heads, t_new)


def _moba_kernel(pt_ref, q_ref, kt_ref, vt_ref, qs_ref, kn_ref, vn_ref, *rest, nblk, prompt_steps, decode_steps,
                 n_heads, t_new, pages_per_blk, dec_nblk):
    npg = MOBA_PAGES_PER_STEP
    kp, vp = rest[:npg], rest[npg:2 * npg]
    o_ref, os_ref, kb_ref, vb_ref, kmean_ref, bias_ref, causal_ref, qbd_ref, m_ref, l_ref, sc_ref, oblk_ref = rest[2 * npg:]
    i = pl.program_id(0)
    _moba_prompt_step(i % prompt_steps, q_ref, kt_ref, vt_ref, o_ref, kb_ref, vb_ref, kmean_ref, bias_ref, causal_ref,
                      nblk=nblk)
    g = i % decode_steps
    _moba_decode_step(g, g == decode_steps - 1, qs_ref, kn_ref, vn_ref, kp, vp, os_ref, qbd_ref, m_ref, l_ref, sc_ref,
                      oblk_ref, n_heads=n_heads, t_new=t_new, pages_per_blk=pages_per_blk, nblk=dec_nblk)


def _moba(q, kt, vt, batch, seq, qs, k_new, v_new, cache_kt, cache_vt, page_table, t_new):
    n, width = q.shape
    blk, grp, nsub = ATT_TILE, ATT_GROUP, MOBA_SUBTILES
    assert blk == MOBA_BLOCK and seq % (blk * grp) == 0 and grp % nsub == 0
    nblk = seq // blk
    steps_p = nblk // nsub
    npair = width // LANES
    n_heads = width // HEAD_DIM
    n_seq, n_pages = page_table.shape
    page = cache_kt.shape[2]
    pages_per_blk = MOBA_BLOCK // page
    dec_nblk = n_pages // pages_per_blk
    npg = MOBA_PAGES_PER_STEP
    steps_d = n_pages // npg
    rows = n_heads * t_new
    assert n_pages % npg == 0 and npg % pages_per_blk == 0 and MOBA_TOPK <= dec_nblk <= LANES
    assert n_pages * page == dec_nblk * MOBA_BLOCK and t_new <= page and page == LANES and rows <= LANES
    assert n_heads % 2 == 0
    total = batch * npair * steps_p
    assert total == n_seq * steps_d, "prompt and decode sides must have the same number of steps"

    kv_spec = pl.BlockSpec((None, LANES, seq), lambda i, pt: (i // (npair * steps_p), (i // steps_p) % npair, 0),
                           pipeline_mode=pl.Buffered(1))
    q_spec = pl.BlockSpec((nsub * blk, LANES),
                          lambda i, pt: ((i // (npair * steps_p)) * steps_p + i % steps_p, (i // steps_p) % npair))
    tok_spec = pl.BlockSpec((t_new, width), lambda i, pt: (i // steps_d, 0))
    page_specs = [pl.BlockSpec((None, width, page), functools.partial(
        lambda i, pt, j: (pt[i // steps_d, (i % steps_d) * npg + j], 0, 0), j=j)) for j in range(npg)]
    grid_spec = pltpu.PrefetchScalarGridSpec(
        num_scalar_prefetch=1,
        grid=(total,),
        in_specs=[q_spec, kv_spec, kv_spec, tok_spec, tok_spec, tok_spec] + page_specs + page_specs,
        out_specs=[q_spec, tok_spec],
        scratch_shapes=[pltpu.VMEM((nblk * blk, LANES), BF16), pltpu.VMEM((nblk, LANES, blk), BF16),
                        pltpu.VMEM((nblk, LANES), F32), pltpu.VMEM((nsub, nblk, 2 * blk), F32),
                        pltpu.VMEM((blk, 2 * blk), F32),
                        pltpu.VMEM((rows, width), BF16), pltpu.VMEM((rows, LANES), F32),
                        pltpu.VMEM((rows, LANES), F32), pltpu.VMEM((rows, LANES), F32),
                        pltpu.VMEM((dec_nblk, rows, LANES), F32)],
    )
    return pl.pallas_call(
        functools.partial(_moba_kernel, nblk=nblk, prompt_steps=steps_p, decode_steps=steps_d, n_heads=n_heads,
                          t_new=t_new, pages_per_blk=pages_per_blk, dec_nblk=dec_nblk),
        grid_spec=grid_spec,
        out_shape=[jax.ShapeDtypeStruct((n, width), F32), jax.ShapeDtypeStruct(qs.shape, F32)],
        compiler_params=_cparams("arbitrary"),
        name="moba",
    )(page_table, q, kt, vt, qs, k_new, v_new, *([cache_kt] * npg), *([cache_vt] * npg))


DIL_DEC_TILE = 512


def _dilated_decode_tables(n_heads, t_new, n_prev):
    qi = (np.arange(n_heads * t_new) % t_new)[:, None]
    c_old = _dilated_multiplicity(n_prev + qi - np.arange(n_prev)[None, :])
    c_new = _dilated_multiplicity(qi - np.arange(LANES)[None, :])
    c_new = np.where(np.arange(LANES)[None, :] < t_new, c_new, 0)
    f = lambda c: (np.where(c > 0, 0.0, NEG_INF).astype(np.float32), c.astype(np.float32))
    return f(c_old) + f(c_new)


def _shifted_window_tile(cur_ref, nxt_ref, new_ref, out_ref, tail_ref, is_last, t_new):
    lane = lax.broadcasted_iota(jnp.int32, (cur_ref.shape[0], LANES), 1)
    nblk = cur_ref.shape[1] // LANES

    @pl.when(is_last)
    def _():
        tail_ref[...] = _pad_rows(new_ref[...], LANES).T

    @pl.when(jnp.logical_not(is_last))
    def _():
        tail_ref[...] = nxt_ref[...]

    rolled = [pltpu.roll(cur_ref[:, c * LANES:(c + 1) * LANES], LANES - t_new, 1) for c in range(nblk)]
    rolled.append(pltpu.roll(tail_ref[...], LANES - t_new, 1))
    for c in range(nblk):
        out_ref[:, c * LANES:(c + 1) * LANES] = jnp.where(lane < LANES - t_new, rolled[c], rolled[c + 1])


def _dilated_decode_kernel(q_ref, kn_ref, vn_ref, kc_ref, vc_ref, kx_ref, vx_ref, bo_ref, mo_ref, bn_ref, mn_ref,
                           o_ref, ko_ref, vo_ref, qbd_ref, m_ref, l_ref, acc_ref, tail_ref, *, n_heads, t_new):
    kt = pl.program_id(1)
    is_last = kt == pl.num_programs(1) - 1
    _shifted_window_tile(kc_ref, kx_ref, kn_ref, ko_ref, tail_ref, is_last, t_new)
    _shifted_window_tile(vc_ref, vx_ref, vn_ref, vo_ref, tail_ref, is_last, t_new)

    @pl.when(kt == 0)
    def _():
        qbd_ref[...] = _scaled_bf16(_block_diag_queries(q_ref[...], n_heads))
        m_ref[...] = jnp.full_like(m_ref, -jnp.inf)
        l_ref[...] = jnp.zeros_like(l_ref)
        acc_ref[...] = jnp.zeros_like(acc_ref)

    def step(s, pv, bias, mult):
        s = s + bias
        m_old = m_ref[...]
        m_new = jnp.maximum(m_old, jnp.max(s, axis=-1, keepdims=True))
        alpha = jnp.exp(m_old - m_new)
        p = jnp.exp(s - m_new) * mult
        l_ref[...] = alpha * l_ref[...] + jnp.sum(p, axis=-1, keepdims=True)
        acc_ref[...] = alpha * acc_ref[...] + pv(p.astype(BF16))
        m_ref[...] = m_new

    @pl.when(kt == 0)
    def _():
        k, v = _pad_rows(kn_ref[...], LANES).astype(BF16), _pad_rows(vn_ref[...], LANES).astype(BF16)
        step(lax.dot_general(qbd_ref[...], k, _NT, preferred_element_type=F32),
             lambda p: jnp.dot(p, v, preferred_element_type=F32), bn_ref[...], mn_ref[...])

    step(jnp.dot(qbd_ref[...], kc_ref[...].astype(BF16), preferred_element_type=F32),
         lambda p: lax.dot_general(p, vc_ref[...].astype(BF16), _NT, preferred_element_type=F32),
         bo_ref[...], mo_ref[...])

    @pl.when(kt == pl.num_programs(1) - 1)
    def _():
        o_ref[...] = _take_diag(acc_ref[...] / l_ref[...], n_heads, t_new)


def _dilated_decode(q, k_new, v_new, win_kt, win_vt, t_new):
    n, width = q.shape
    n_heads = width // HEAD_DIM
    n_seq, _, n_prev = win_kt.shape
    tile = DIL_DEC_TILE
    assert n_prev == W_MAX and n_prev % tile == 0 and t_new <= LANES
    rows = n_heads * t_new
    bo, mo, bn, mn = _dilated_decode_tables(n_heads, t_new, n_prev)
    tok_spec = pl.BlockSpec((t_new, width), lambda b, t: (b, 0))
    win_spec = pl.BlockSpec((None, width, tile), lambda b, t: (b, 0, t))
    last_lane_blk = n_prev // LANES - 1
    nxt_spec = pl.BlockSpec((None, width, LANES),
                            lambda b, t: (b, 0, jnp.minimum((t + 1) * (tile // LANES), last_lane_blk)))
    old_tab = pl.BlockSpec((rows, tile), lambda b, t: (0, t))
    new_tab = pl.BlockSpec((rows, LANES), lambda b, t: (0, 0))
    win_shape = jax.ShapeDtypeStruct(win_kt.shape, F32)
    return pl.pallas_call(
        functools.partial(_dilated_decode_kernel, n_heads=n_heads, t_new=t_new),
        grid=(n_seq, n_prev // tile),
        in_specs=[tok_spec, tok_spec, tok_spec, win_spec, win_spec, nxt_spec, nxt_spec,
                  old_tab, old_tab, new_tab, new_tab],
        out_specs=[tok_spec, win_spec, win_spec],
        out_shape=[jax.ShapeDtypeStruct((n, width), F32), win_shape, win_shape],
        scratch_shapes=[pltpu.VMEM((rows, width), BF16), pltpu.VMEM((rows, 1), F32),
                        pltpu.VMEM((rows, 1), F32), pltpu.VMEM((rows, width), F32),
                        pltpu.VMEM((width, LANES), F32)],
        compiler_params=_cparams("parallel", "arbitrary"),
        name="dilated_decode",
    )(q, k_new, v_new, win_kt, win_vt, win_kt, win_vt,
      jnp.asarray(bo), jnp.asarray(mo), jnp.asarray(bn), jnp.asarray(mn))


_AB_SEGS = (("a", (0,), 0, 1.0), ("a", (1,), 0, 1.0), (None, (2,), 0, 1.0), ("b", (3,), 0, 1.0),
            ("b", (4,), 0, ATT_SCALE), (None, (5,), 0, 1.0), (None, (6,), 0, 1.0))
_C_SEGS = tuple(("a" if o < 2 else None, (o,), c, 1.0) for o in range(3) for c in (0, SEG))
_C_SEGS_PROMPT = tuple((kind, dests if dests == (0,) else dests + (dests[0] + 2,), c, post)
                       for kind, dests, c, post in _C_SEGS)
_KV_OUT = (1, 2)


def _row_tile(n):
    return 512 if n % 512 == 0 else n


def _feature_major(x):
    lead = x.shape[:-3]
    t, h, dh = x.shape[-3:]
    nl = len(lead)
    return x.transpose(*range(nl), nl + 1, nl + 2, nl).reshape(*lead, h * dh, t)


def _token_major(xt, n_heads):
    lead = xt.shape[:-2]
    t = xt.shape[-1]
    nl = len(lead)
    return xt.reshape(*lead, n_heads, HEAD_DIM, t).transpose(*range(nl), nl + 2, nl, nl + 1)


def kernel(x_prompt, x_sample, cache_k_a, cache_v_a, page_table, state_ret, cache_win_k, cache_win_v,
           norm_mix, norm_ffn, norm_final, w_in_ab, w_out_ab, ret_gn_w, w_in_c, w_out_c,
           ffn_w_gate, ffn_w_up, ffn_w_down):
    bp, tp, d = x_prompt.shape
    bs, ts, _ = x_sample.shape
    page = cache_k_a.shape[2]
    past_len = page_table.shape[1] * page
    h_a = cache_k_a.shape[3]
    wa = h_a * HEAD_DIM
    wb = w_in_ab.shape[2] - 3 * wa
    assert wb == 4 * wa and w_in_ab.shape[0] == 1 and w_in_c.shape[0] == 1 and norm_mix.shape[0] == 2
    h_b = wa // HEAD_DIM
    h_c = w_in_c.shape[2] // (3 * HEAD_DIM)

    pos_p = jnp.arange(tp, dtype=jnp.int32)
    pos_s = past_len + jnp.arange(ts, dtype=jnp.int32)
    tile_s = lambda tabs: tuple(jnp.tile(t, (bs, 1)) for t in tabs)
    tabs = {
        "p": (_rope_tables(pos_p, ROT_DIM, ROPE_THETA), _rope_tables(pos_p, HEAD_DIM, RET_THETA)),
        "s": (tile_s(_rope_tables(pos_s, ROT_DIM, ROPE_THETA)), tile_s(_rope_tables(pos_s, HEAD_DIM, RET_THETA))),
    }
    bf = lambda w: w.astype(BF16)
    w_in_ab_bf, w_out_ab_bf, w_in_c_bf, w_out_c_bf = bf(w_in_ab[0]), bf(w_out_ab[0]), bf(w_in_c[0]), bf(w_out_c[0])
    wg, wu, wd = bf(ffn_w_gate), bf(ffn_w_up), bf(ffn_w_down)

    xp = x_prompt.reshape(bp * tp, d)
    xs = x_sample.reshape(bs * ts, d)
    tm_p, tm_s = _row_tile(bp * tp), _row_tile(bs * ts)
    assert tp % tm_p == 0

    ab_widths = (wa,) * 7
    qa_p, kat_p, vat_p, qb_p, kb_p, vb_p, gb_p = _project(
        xp, norm_mix[0], w_in_ab_bf, *tabs["p"], _AB_SEGS, ab_widths, tm_p, _KV_OUT)
    qa_s, ka_s, va_s, qb_s, kb_s, vb_s, gb_s = _project(
        xs, norm_mix[0], w_in_ab_bf, *tabs["s"], _AB_SEGS, ab_widths, tm_s)

    oa_p, oa_s = _moba(qa_p, kat_p, vat_p, bp, tp, qa_s, ka_s, va_s, _feature_major(cache_k_a[0]),
                       _feature_major(cache_v_a[0]), page_table, ts)

    chunk_p = math.gcd(tp, RET_CHUNK)
    ob_p, ret_p = _retention(qb_p, kb_p, vb_p, gb_p, ret_gn_w[0],
                             jnp.zeros((bp, h_b, HEAD_DIM, HEAD_DIM), F32), bp, tp, chunk_p,
                             math.gcd(tp // chunk_p, 8))
    ob_s, ret_s = _retention(qb_s, kb_s, vb_s, gb_s, ret_gn_w[0], state_ret[0], bs, ts,
                             math.gcd(ts, RET_CHUNK), 1)

    xp = _mix_ffn([oa_p, ob_p], w_out_ab_bf, xp, norm_ffn[0], wg[0], wu[0], wd[0], norm_final, False, tm_p)
    xs = _mix_ffn([oa_s, ob_s], w_out_ab_bf, xs, norm_ffn[0], wg[0], wu[0], wd[0], norm_final, False, tm_s)

    wc = h_c * HEAD_DIM
    q_p, kt_p, vt_p, k_p, v_p = _project(xp, norm_mix[1], w_in_c_bf, *tabs["p"], _C_SEGS_PROMPT, (wc,) * 5,
                                         tm_p, _KV_OUT)
    q_s, k_s, v_s = _project(xs, norm_mix[1], w_in_c_bf, *tabs["s"], _C_SEGS, (wc,) * 3, tm_s)
    n_prev = cache_win_k.shape[2]
    o_p = _dilated_prompt(q_p, k_p, v_p, kt_p, vt_p, bp, tp)
    o_s, win_kt, win_vt = _dilated_decode(q_s, k_s, v_s, _feature_major(cache_win_k[0]),
                                          _feature_major(cache_win_v[0]), ts)
    y_p = _mix_ffn([o_p], w_out_c_bf, xp, norm_ffn[1], wg[1], wu[1], wd[1], norm_final, True, tm_p)
    y_s = _mix_ffn([o_s], w_out_c_bf, xs, norm_ffn[1], wg[1], wu[1], wd[1], norm_final, True, tm_s)

    keep_p = min(W_MAX, tp)
    return (
        y_p.reshape(bp, tp, d), y_s.reshape(bs, ts, d),
        _token_major(kat_p, h_a)[None], _token_major(vat_p, h_a)[None], ret_p[None],
        _token_major(kt_p[:, :, tp - keep_p:], h_c)[None], _token_major(vt_p[:, :, tp - keep_p:], h_c)[None],
        ka_s.reshape(1, bs, ts, h_a, HEAD_DIM), va_s.reshape(1, bs, ts, h_a, HEAD_DIM), ret_s[None],
        _token_major(win_kt, h_c)[None], _token_major(win_vt, h_c)[None],
    )
```

```python
import functools
import math

import jax
import jax.numpy as jnp
import numpy as np
from jax import lax
from jax.experimental import pallas as pl
from jax.experimental.pallas import tpu as pltpu

F32 = jnp.float32
BF16 = jnp.bfloat16

HEAD_DIM = 64
LANES = 128
ROT_DIM = HEAD_DIM // 4
ROPE_THETA = 500000.0
RET_THETA = 10000.0
MOBA_BLOCK = 256
MOBA_TOPK = 3
RET_CHUNK = 128
DILATED = ((128, 1), (512, 4), (2048, 16))
W_MAX = max(w for w, _ in DILATED)
EPS = 1e-6
NEG_INF = -1e30
ATT_SCALE = HEAD_DIM ** -0.5
VMEM_LIMIT = 56 * 1024 * 1024

_NT = (((1,), (1,)), ((), ()))
_TN = (((0,), (0,)), ((), ()))


def _cparams(*sem):
    return pltpu.CompilerParams(dimension_semantics=sem, vmem_limit_bytes=VMEM_LIMIT)


def _rms(x, w):
    ms = jnp.mean(x * x, axis=-1, keepdims=True)
    return x * lax.rsqrt(ms + EPS) * w


SEG = 512


def _proj_kernel(x_ref, nw_ref, w_ref, ca_ref, na_ref, pa_ref, cb_ref, nb_ref, pb_ref, *out_refs, segs, transposed):
    xn = _rms(x_ref[...], nw_ref[...]).astype(BF16)
    for s, (kind, dests, col, post) in enumerate(segs):
        acc = jnp.dot(xn, w_ref[:, s * SEG:(s + 1) * SEG], preferred_element_type=F32)
        for c in range(SEG // LANES):
            r = acc[:, c * LANES:(c + 1) * LANES]
            if kind is not None:
                c_ref, n_ref, p_ref, shift = (ca_ref, na_ref, pa_ref, ROT_DIM // 2) if kind == "a" else (
                    cb_ref, nb_ref, pb_ref, HEAD_DIM // 2)
                r = (r * c_ref[...] + pltpu.roll(r, LANES - shift, 1) * n_ref[...]
                     + pltpu.roll(r, shift, 1) * p_ref[...])
                if post != 1.0:
                    r = r * post
            lo = col + c * LANES
            for oi in dests:
                if oi in transposed:
                    out_refs[oi][lo:lo + LANES, :] = r.T
                else:
                    out_refs[oi][:, lo:lo + LANES] = r


def _rope_tables(pos, rot_dim, theta):
    half = rot_dim // 2
    inv = theta ** (-jnp.arange(half, dtype=F32) / half)
    ang = pos.astype(F32)[:, None] * inv[None, :]
    cos, sin = jnp.cos(ang), jnp.sin(ang)
    lane = np.arange(LANES) % HEAD_DIM
    idx = lane % half
    cos_t = jnp.where(lane < rot_dim, cos[:, idx], 1.0)
    neg_t = jnp.where(lane < half, -sin[:, idx], 0.0)
    pos_t = jnp.where((lane >= half) & (lane < rot_dim), sin[:, idx], 0.0)
    return cos_t.astype(F32), neg_t.astype(F32), pos_t.astype(F32)


def _project(x, nw, w_bf, tabs_a, tabs_b, segs, out_widths, tm, transposed=()):
    n, d = x.shape
    t_tab = tabs_a[0].shape[0]
    nt = t_tab // tm
    tab_spec = pl.BlockSpec((tm, LANES), lambda i: (i % nt, 0))
    out_specs, out_shape = [], []
    for oi, w in enumerate(out_widths):
        if oi in transposed:
            out_specs.append(pl.BlockSpec((None, w, tm), lambda i: (i // nt, 0, i % nt)))
            out_shape.append(jax.ShapeDtypeStruct((n // t_tab, w, t_tab), F32))
        else:
            out_specs.append(pl.BlockSpec((tm, w), lambda i: (i, 0)))
            out_shape.append(jax.ShapeDtypeStruct((n, w), F32))
    return pl.pallas_call(
        functools.partial(_proj_kernel, segs=segs, transposed=tuple(transposed)),
        grid=(n // tm,),
        in_specs=[pl.BlockSpec((tm, d), lambda i: (i, 0)),
                  pl.BlockSpec((1, d), lambda i: (0, 0)),
                  pl.BlockSpec(w_bf.shape, lambda i: (0, 0))] + [tab_spec] * 6,
        out_specs=out_specs,
        out_shape=out_shape,
        compiler_params=_cparams("parallel"),
        name="rms_proj_rope",
    )(x, nw.reshape(1, d), w_bf, *tabs_a, *tabs_b)


FFN_CHUNKS = 2


def _mix_ffn_kernel(*refs, nparts, final):
    parts = refs[:nparts]
    wo_ref, x_ref, nw_ref, wg_ref, wu_ref, wd_ref, fw_ref, o_ref = refs[nparts:]
    a = jnp.concatenate([p[...].astype(BF16) for p in parts], axis=1) if nparts > 1 else parts[0][...].astype(BF16)
    x = x_ref[...] + jnp.dot(a, wo_ref[...], preferred_element_type=F32)
    xn = _rms(x, nw_ref[...]).astype(BF16)
    tf = wg_ref.shape[1] // FFN_CHUNKS
    acc = x
    for c in range(FFN_CHUNKS):
        g = jnp.dot(xn, wg_ref[:, c * tf:(c + 1) * tf], preferred_element_type=F32)
        u = jnp.dot(xn, wu_ref[:, c * tf:(c + 1) * tf], preferred_element_type=F32)
        h = (g * jax.nn.sigmoid(g) * u).astype(BF16)
        acc = acc + jnp.dot(h, wd_ref[c * tf:(c + 1) * tf, :], preferred_element_type=F32)
    if final:
        acc = _rms(acc, fw_ref[...])
    o_ref[...] = acc


def _mix_ffn(parts, wo, x, nw, wg, wu, wd, fw, final, tm):
    n, d = x.shape
    dff = wg.shape[1]
    const = lambda i: (0, 0)
    row = lambda w: pl.BlockSpec((tm, w), lambda i: (i, 0))
    return pl.pallas_call(
        functools.partial(_mix_ffn_kernel, nparts=len(parts), final=final),
        grid=(n // tm,),
        in_specs=[row(p.shape[1]) for p in parts]
        + [pl.BlockSpec(wo.shape, const), row(d), pl.BlockSpec((1, d), const),
           pl.BlockSpec((d, dff), const), pl.BlockSpec((d, dff), const), pl.BlockSpec((dff, d), const),
           pl.BlockSpec((1, d), const)],
        out_specs=row(d),
        out_shape=jax.ShapeDtypeStruct((n, d), F32),
        compiler_params=_cparams("parallel"),
        name="outproj_swiglu",
    )(*parts, wo, x, nw.reshape(1, d), wg, wu, wd, fw.reshape(1, d))


ATT_TILE = 256
ATT_GROUP = 4
MOBA_SUBTILES = 4
ATT_CHUNK = 64


def _unit_scores(kb, qb, bias_fn):
    s = jnp.dot(kb, qb, preferred_element_type=F32)
    chunks = [s[r:r + ATT_CHUNK] + bias_fn(r, r + ATT_CHUNK) for r in range(0, s.shape[0], ATT_CHUNK)]
    return chunks, jnp.max(functools.reduce(jnp.maximum, chunks), axis=0, keepdims=True)


ONES_ROWS = 16


def _with_ones(vt):
    return jnp.concatenate([vt, jnp.ones((ONES_ROWS, vt.shape[1]), BF16)], axis=0)


def _unit_values(chunks, m_u, vts, mult_fn):
    ps = []
    for i, c in enumerate(chunks):
        p = jnp.exp((c - m_u).astype(BF16))
        if mult_fn is not None:
            p = p * mult_fn(i * ATT_CHUNK, (i + 1) * ATT_CHUNK).astype(BF16)
        ps.append(p)
    per = len(ps) // len(vts)
    pv = sum(jnp.dot(vt, jnp.concatenate(ps[b * per:(b + 1) * per], axis=0), preferred_element_type=F32)
             for b, vt in enumerate(vts))
    nfeat = pv.shape[0] - ONES_ROWS
    return pv[nfeat:nfeat + 1], pv[:nfeat]


def _attend_unit(kb, q2, bias_fn, vts, mult_fn=None):
    chunks, m_u = _unit_scores(kb, q2, bias_fn)
    return (m_u,) + _unit_values(chunks, m_u, vts, mult_fn)


def _merge(parts):
    m = functools.reduce(jnp.maximum, [p[0] for p in parts])
    ws = [jnp.exp(p[0] - m) for p in parts]
    return m, sum(w * p[1] for w, p in zip(ws, parts)), sum(w * p[2] for w, p in zip(ws, parts))


def _head_queries(q_t, feat):
    return [jnp.where(feat // HEAD_DIM == hh, q_t, 0.0) for hh in range(2)]


def _scaled_bf16(q):
    return (q * ATT_SCALE).astype(BF16)


def _both(x):
    return jnp.concatenate([x, x], axis=1)


def _pick_heads(x, feat):
    nq = x.shape[1] // 2
    return jnp.where(feat // HEAD_DIM == 0, x[:, :nq], x[:, nq:])


def _moba_prompt_step(step, q_ref, kt_ref, vt_ref, o_ref, kb_ref, vb_ref, kmean_ref, bias_ref, causal_ref, *, nblk):
    blk, grp = ATT_TILE, ATT_GROUP
    nsub = MOBA_SUBTILES

    @pl.when(step == 0)
    def _():
        for n in range(nblk):
            kblk = kt_ref[:, n * blk:(n + 1) * blk].T
            kmean_ref[n:n + 1, :] = jnp.sum(kblk, axis=0, keepdims=True) * (1.0 / blk)
            kb_ref[n * blk:(n + 1) * blk, :] = kblk.astype(BF16)
            vb_ref[n] = _with_ones(vt_ref[:, n * blk:(n + 1) * blk].astype(BF16))
        key_i = lax.broadcasted_iota(jnp.int32, (blk, 2 * blk), 0)
        qry_i = lax.broadcasted_iota(jnp.int32, (blk, 2 * blk), 1) % blk
        causal_ref[...] = jnp.where(key_i <= qry_i, 0.0, NEG_INF)

    feat = lax.broadcasted_iota(jnp.int32, (LANES, blk), 0)
    q2s = [jnp.concatenate(_head_queries(q_ref[sub * blk:(sub + 1) * blk, :].T, feat), axis=1) for sub in range(nsub)]
    qb2s = [_scaled_bf16(q2) for q2 in q2s]
    cols = nsub * 2 * blk
    blk_id = lax.broadcasted_iota(jnp.int32, (nblk, cols), 0)
    blk_f = blk_id.astype(F32)
    qt = step * nsub + lax.broadcasted_iota(jnp.int32, (1, cols), 1) // (2 * blk)
    sc = jnp.dot(kmean_ref[...], jnp.concatenate(q2s, axis=1), preferred_element_type=F32,
                 precision=lax.Precision.HIGHEST)
    sc = jnp.where(blk_id < qt, sc, NEG_INF)
    sel = jnp.zeros((nblk, cols), F32)
    for r in range(MOBA_TOPK):
        mx = jnp.max(sc, axis=0, keepdims=True)
        first = jnp.min(jnp.where(sc == mx, blk_f, float(nblk)), axis=0, keepdims=True)
        pick = blk_f == first
        sel = jnp.maximum(sel, jnp.where(pick & (r < qt), 1.0, 0.0))
        sc = jnp.where(pick, -jnp.inf, sc)
    bias = jnp.where((sel > 0.0) | (blk_id == qt), 0.0, NEG_INF)
    for sub in range(nsub):
        bias_ref[sub] = bias[:, sub * 2 * blk:(sub + 1) * 2 * blk]

    def group(sub, g, with_own):
        def bias_fn(r0, r1):
            n, r = g * grp + r0 // blk, r0 % blk
            row = bias_ref[sub, pl.ds(n, 1), :]
            if not with_own:
                return row
            return row + jnp.where(n == step * nsub + sub, causal_ref[r:r + (r1 - r0), :], 0.0)

        kb = kb_ref[pl.ds(pl.multiple_of(g * grp * blk, blk), grp * blk), :]
        return _attend_unit(kb, qb2s[sub], bias_fn, [vb_ref[g * grp + b] for b in range(grp)])

    g_own = step * nsub // grp

    def body(g, carry):
        return sum((_merge([carry[3 * sub:3 * sub + 3], group(sub, g, False)]) for sub in range(nsub)), ())

    carry = lax.fori_loop(0, g_own, body, sum((group(sub, g_own, True) for sub in range(nsub)), ()))
    for sub in range(nsub):
        o_ref[sub * blk:(sub + 1) * blk, :] = _pick_heads(carry[3 * sub + 2] / carry[3 * sub + 1], feat).T


def _dilated_multiplicity(delta):
    c = np.zeros(delta.shape, np.int32)
    for window, dil in DILATED:
        c += ((delta >= 0) & (delta <= window) & (delta % dil == 0)).astype(np.int32)
    return c


DIL_NEAR = 2
DIL_FAR_CLASSES_PER_STEP = 4
assert DILATED[-1][1] % DIL_FAR_CLASSES_PER_STEP == 0
DIL_NEAR_SUBTILES = 4
DIL_FAR_STRIDE = DILATED[-1][1]
DIL_FAR_REACH = DILATED[-1][0] // DIL_FAR_STRIDE
assert DILATED[-2][0] <= ATT_TILE * DIL_NEAR and ATT_TILE % DIL_FAR_STRIDE == 0 and DIL_FAR_REACH <= ATT_TILE


def _mask_bias(valid):
    return np.where(valid, 0.0, NEG_INF).astype(np.float32)


def _dilated_near_tables():
    ki = np.arange(ATT_TILE)[:, None]
    qi = np.arange(ATT_TILE)[None, :]
    c = np.stack([_dilated_multiplicity(ATT_TILE * (DIL_NEAR - i) + qi - ki) for i in range(DIL_NEAR + 1)])
    return _mask_bias(c > 0), c.astype(np.float32)


def _dilated_far_tables():
    per_tile = ATT_TILE // DIL_FAR_STRIDE

    def table(k0, nk, q0):
        ka = k0 + np.arange(nk)[:, None]
        qa = q0 + np.arange(ATT_TILE)[None, :]
        return _mask_bias((qa - ka <= DIL_FAR_REACH) & (ka // per_tile < qa // per_tile - DIL_NEAR))

    return table(0, ATT_TILE, 0), table(ATT_TILE - DIL_FAR_REACH, ATT_TILE + DIL_FAR_REACH, ATT_TILE)


def _dilated_far_kernel(q_ref, k_ref, v_ref, b0_ref, b1_ref, o_ref, lse_ref, *, seq):
    stride, blk, reach = DIL_FAR_STRIDE, ATT_TILE, DIL_FAR_REACH
    ntile = seq // stride // blk
    feat = lax.broadcasted_iota(jnp.int32, (LANES, blk), 0)

    def one_class(cls):
        for tau in range(ntile):
            k0, nk, tab = (0, blk, b0_ref) if tau == 0 else (tau * blk - reach, blk + reach, b1_ref)
            rows_q = pl.ds(cls + stride * blk * tau, blk, stride=stride)
            rows_k = pl.ds(cls + stride * k0, nk, stride=stride)
            qb2 = _scaled_bf16(jnp.concatenate(_head_queries(q_ref[rows_q, :].T, feat), axis=1))
            kb = k_ref[rows_k, :].astype(BF16)
            vt = _with_ones(v_ref[rows_k, :].T.astype(BF16))
            m_u, l_u, pv = _attend_unit(kb, qb2, lambda r0, r1: _both(tab[r0:r1, :]), [vt])
            o_ref[rows_q, :] = _pick_heads(pv / l_u, feat).T
            lse_ref[rows_q, :] = _pick_heads(jnp.broadcast_to(m_u + jnp.log(l_u), pv.shape), feat).T

    def body(i, carry):
        for c in range(DIL_FAR_CLASSES_PER_STEP):
            one_class(i * DIL_FAR_CLASSES_PER_STEP + c)
        return carry

    lax.fori_loop(0, stride // DIL_FAR_CLASSES_PER_STEP, body, 0)


def _dilated_near_kernel(q_ref, kt_ref, vt_ref, far_ref, lse_ref, bias_ref, mult_ref, o_ref,
                         kb_ref, vb_ref, tab_ref, *, nblk):
    blk, near, nsub = ATT_TILE, DIL_NEAR, DIL_NEAR_SUBTILES
    step = pl.program_id(2)

    @pl.when(step == 0)
    def _():
        kb_ref[0:near * blk, :] = jnp.zeros((near * blk, LANES), BF16)
        for n in range(near):
            vb_ref[n] = jnp.zeros((LANES + ONES_ROWS, blk), BF16)
        for n in range(nblk):
            kb_ref[(n + near) * blk:(n + near + 1) * blk, :] = kt_ref[:, n * blk:(n + 1) * blk].T.astype(BF16)
            vb_ref[n + near] = _with_ones(vt_ref[:, n * blk:(n + 1) * blk].astype(BF16))

    @pl.when(step * nsub < near + nsub)
    def _():
        for sub in range(nsub):
            for i in range(near + 1):
                tab_ref[sub, i] = jnp.where(step * nsub + sub - near + i >= 0, bias_ref[i], NEG_INF)

    feat = lax.broadcasted_iota(jnp.int32, (LANES, blk), 0)
    for sub in range(nsub):
        qt = step * nsub + sub
        rows = slice(sub * blk, (sub + 1) * blk)
        qb2 = _scaled_bf16(jnp.concatenate(_head_queries(q_ref[rows, :].T, feat), axis=1))
        m2, l2, a2 = _attend_unit(
            kb_ref[pl.ds(pl.multiple_of(qt * blk, blk), (near + 1) * blk), :], qb2,
            lambda r0, r1: _both(tab_ref[sub, r0 // blk, r0 % blk:r0 % blk + (r1 - r0), :]),
            [vb_ref[qt + i] for i in range(near + 1)],
            lambda r0, r1: _both(mult_ref[r0 // blk, r0 % blk:r0 % blk + (r1 - r0), :]))
        m_n = _pick_heads(jnp.broadcast_to(m2, a2.shape), feat)
        l_n = _pick_heads(jnp.broadcast_to(l2, a2.shape), feat)
        a_n = _pick_heads(a2, feat)
        lse_f = lse_ref[rows, :].T
        m = jnp.maximum(m_n, lse_f)
        w_n, w_f = jnp.exp(m_n - m), jnp.exp(lse_f - m)
        o_ref[rows, :] = ((a_n * w_n + far_ref[rows, :].T * w_f) / (l_n * w_n + w_f)).T


def _dilated_prompt(q, k, v, kt, vt, batch, seq):
    n, width = q.shape
    blk, near, stride = ATT_TILE, DIL_NEAR, DIL_FAR_STRIDE
    assert seq % (blk * stride) == 0
    nblk = seq // blk
    npair = width // LANES
    const = lambda nd: (lambda *_: (0,) * nd)
    full = lambda t: pl.BlockSpec(t.shape, const(t.ndim))

    far_tabs = [jnp.asarray(t) for t in _dilated_far_tables()]
    seq_spec = pl.BlockSpec((seq, LANES), lambda b, p: (b, p))
    o_far, lse_far = pl.pallas_call(
        functools.partial(_dilated_far_kernel, seq=seq),
        grid=(batch, npair),
        in_specs=[seq_spec, seq_spec, seq_spec] + [full(t) for t in far_tabs],
        out_specs=[seq_spec, seq_spec],
        out_shape=[jax.ShapeDtypeStruct((n, width), F32)] * 2,
        compiler_params=_cparams("parallel", "parallel"),
        name="dilated_far",
    )(q, k, v, *far_tabs)

    near_tabs = [jnp.asarray(t) for t in _dilated_near_tables()]
    kv_spec = pl.BlockSpec((None, LANES, seq), lambda b, p, t: (b, p, 0))
    nsub = DIL_NEAR_SUBTILES
    assert nblk % nsub == 0
    steps = nblk // nsub
    q_spec = pl.BlockSpec((nsub * blk, LANES), lambda b, p, t: (b * steps + t, p))
    return pl.pallas_call(
        functools.partial(_dilated_near_kernel, nblk=nblk),
        grid=(batch, npair, steps),
        in_specs=[q_spec, kv_spec, kv_spec, q_spec, q_spec] + [full(t) for t in near_tabs],
        out_specs=q_spec,
        out_shape=jax.ShapeDtypeStruct((n, width), F32),
        scratch_shapes=[pltpu.VMEM(((nblk + near) * blk, LANES), BF16),
                        pltpu.VMEM((nblk + near, LANES + ONES_ROWS, blk), BF16),
                        pltpu.VMEM((nsub, near + 1, blk, blk), F32)],
        compiler_params=_cparams("parallel", "parallel", "arbitrary"),
        name="dilated_near",
    )(q, kt, vt, o_far, lse_far, *near_tabs)


RET_ROWS = 128


def _retention_tables(n_heads, chunk):
    log_g = jnp.log1p(-jnp.exp2(-5.0 - jnp.arange(n_heads, dtype=F32)))
    i = jnp.arange(chunk, dtype=F32)
    diff = i[:, None] - i[None, :]
    causal = diff >= 0
    decay = jnp.where(causal[None], jnp.exp(jnp.where(causal, diff, 0.0)[None] * log_g[:, None, None]), 0.0)
    q_dec = jnp.exp((i + 1.0)[:, None] * log_g[None, :])
    k_dec = jnp.exp((chunk - 1.0 - i)[:, None] * log_g[None, :])
    c_dec = jnp.exp(chunk * log_g)
    pad = RET_ROWS - chunk
    npair = n_heads // 2
    decay = jnp.pad(decay, ((0, 0), (0, pad), (0, pad))).reshape(npair, 2 * RET_ROWS, RET_ROWS)
    expand = lambda t: jnp.pad(jnp.repeat(t, HEAD_DIM, axis=1), ((0, pad), (0, 0))).reshape(
        RET_ROWS, npair, LANES).transpose(1, 0, 2)
    q_tab, k_tab = expand(q_dec), expand(k_dec)
    same_head = (np.arange(LANES)[:, None] // HEAD_DIM) == (np.arange(LANES)[None, :] // HEAD_DIM)
    c_tab = jnp.where(same_head[None], jnp.repeat(c_dec, HEAD_DIM).reshape(npair, LANES, 1), 0.0)
    return decay.astype(F32), q_tab.astype(F32), k_tab.astype(F32), c_tab.astype(F32)


def _retention_kernel(q_ref, k_ref, v_ref, g_ref, gn_ref, s0_ref, dec_ref, qd_ref, kd_ref, cd_ref,
                      o_ref, sout_ref, st_ref, *, chunk, nchunk):
    c = pl.program_id(2)

    @pl.when(c == 0)
    def _():
        zero = jnp.zeros((HEAD_DIM, HEAD_DIM), F32)
        st_ref[...] = jnp.concatenate([jnp.concatenate([s0_ref[0], zero], axis=1),
                                       jnp.concatenate([zero, s0_ref[1]], axis=1)], axis=0)

    pad = RET_ROWS - chunk
    lane = lax.broadcasted_iota(jnp.int32, (RET_ROWS, LANES), 1)
    head0 = lane < HEAD_DIM
    same_head = (lax.broadcasted_iota(jnp.int32, (LANES, LANES), 0) // HEAD_DIM) == (
        lax.broadcasted_iota(jnp.int32, (LANES, LANES), 1) // HEAD_DIM)

    def rows(ref, r):
        x = ref[pl.ds(r, chunk), :]
        return x if pad == 0 else jnp.concatenate([x, jnp.zeros((pad, LANES), F32)], axis=0)

    def seg_mean(x):
        s0 = jnp.sum(jnp.where(head0, x, 0.0), axis=-1, keepdims=True)
        s1 = jnp.sum(jnp.where(head0, 0.0, x), axis=-1, keepdims=True)
        return jnp.where(head0, s0, s1) * (1.0 / HEAD_DIM)

    def one_chunk(j, state):
        r = j * chunk
        q, k, v, g = rows(q_ref, r), rows(k_ref, r), rows(v_ref, r), rows(g_ref, r)
        kb, vb = k.astype(BF16), v.astype(BF16)
        q2 = jnp.concatenate([jnp.where(head0, q, 0.0), jnp.where(head0, 0.0, q)], axis=0).astype(BF16)
        attn = lax.dot_general(q2, kb, _NT, preferred_element_type=F32) * dec_ref[...]
        inn = jnp.dot(attn.astype(BF16), vb, preferred_element_type=F32)
        inner = jnp.where(head0, inn[:RET_ROWS], inn[RET_ROWS:])
        cross = jnp.dot(q.astype(BF16), state.astype(BF16), preferred_element_type=F32) * qd_ref[...]
        upd = lax.dot_general((k * kd_ref[...]).astype(BF16), vb, _TN, preferred_element_type=F32)
        o = inner + cross
        mu = seg_mean(o)
        var = seg_mean(jnp.square(o - mu))
        y = (o - mu) * lax.rsqrt(var + EPS) * gn_ref[...]
        y = y * (g * jax.nn.sigmoid(g))
        o_ref[r:r + chunk, :] = y[:chunk]
        return state * cd_ref[...] + jnp.where(same_head, upd, 0.0)

    state = st_ref[...]
    for j in range(nchunk):
        state = one_chunk(j, state)
    st_ref[...] = state

    @pl.when(c == pl.num_programs(2) - 1)
    def _():
        sout_ref[0] = st_ref[0:HEAD_DIM, 0:HEAD_DIM]
        sout_ref[1] = st_ref[HEAD_DIM:, HEAD_DIM:]


def _retention(q, k, v, g, gn_w, s0, batch, seq, chunk, nchunk):
    n, width = q.shape
    n_heads = width // HEAD_DIM
    npair = width // LANES
    rows = chunk * nchunk
    steps = seq // rows
    dec, qd, kd, cd = _retention_tables(n_heads, chunk)
    row_spec = pl.BlockSpec((rows, LANES), lambda b, p, c: (b * steps + c, p))
    st_spec = pl.BlockSpec((None, 2, HEAD_DIM, HEAD_DIM), lambda b, p, c: (b, p, 0, 0))
    tab = lambda t: pl.BlockSpec((None,) + t.shape[1:], lambda b, p, c: (p, 0, 0))
    out, s_new = pl.pallas_call(
        functools.partial(_retention_kernel, chunk=chunk, nchunk=nchunk),
        grid=(batch, npair, steps),
        in_specs=[row_spec, row_spec, row_spec, row_spec, pl.BlockSpec((1, LANES), lambda b, p, c: (0, p)),
                  st_spec, tab(dec), tab(qd), tab(kd), tab(cd)],
        out_specs=[row_spec, st_spec],
        out_shape=[jax.ShapeDtypeStruct((n, width), F32),
                   jax.ShapeDtypeStruct((batch, n_heads, HEAD_DIM, HEAD_DIM), F32)],
        scratch_shapes=[pltpu.VMEM((LANES, LANES), F32)],
        compiler_params=_cparams("parallel", "parallel", "arbitrary"),
        name="retention",
    )(q, k, v, g, gn_w.reshape(1, width), s0, dec, qd, kd, cd)
    return out, s_new


def _block_diag_queries(q, n_heads):
    t = q.shape[0]
    rep = jnp.concatenate([q] * n_heads, axis=0)
    rows = lax.broadcasted_iota(jnp.int32, rep.shape, 0) // t
    cols = lax.broadcasted_iota(jnp.int32, rep.shape, 1) // HEAD_DIM
    return jnp.where(rows == cols, rep, 0.0)


def _take_diag(o, n_heads, t):
    cols = lax.broadcasted_iota(jnp.int32, (t, o.shape[1]), 1) // HEAD_DIM
    out = jnp.zeros((t, o.shape[1]), F32)
    for h in range(n_heads):
        out = jnp.where(cols == h, o[h * t:(h + 1) * t, :], out)
    return out


def _fold_pairs(o, n_heads, t):
    rows = lax.broadcasted_iota(jnp.int32, (o.shape[0], LANES), 0) // (2 * t)
    out = jnp.zeros((o.shape[0], LANES), F32)
    for c in range(n_heads // 2):
        out = jnp.where(rows == c, o[:, c * LANES:(c + 1) * LANES], out)
    return out


def _unfold_pairs(a, n_heads, t):
    lane = lax.broadcasted_iota(jnp.int32, (t, LANES), 1)
    return jnp.concatenate([jnp.where(lane < HEAD_DIM, a[2 * c * t:(2 * c + 1) * t], a[(2 * c + 1) * t:(2 * c + 2) * t])
                            for c in range(n_heads // 2)], axis=1)


def _pad_rows(x, rows):
    return jnp.concatenate([x, jnp.zeros((rows - x.shape[0], x.shape[1]), x.dtype)], axis=0)


MOBA_PAGES_PER_STEP = 32


def _moba_decode_step(g, is_last, q_ref, kn_ref, vn_ref, kp, vp, o_ref, qbd_ref, m_ref, l_ref, sc_ref, oblk_ref,
                      *, n_heads, t_new, pages_per_blk, nblk):
    npg = len(kp)
    rows = n_heads * t_new
    lane = lax.broadcasted_iota(jnp.int32, (rows, LANES), 1)
    page = kp[0].shape[1]
    blk_keys = pages_per_blk * page
    step_blks = npg // pages_per_blk

    @pl.when(g == 0)
    def _():
        qbd_ref[...] = _scaled_bf16(_block_diag_queries(q_ref[...], n_heads))
        m_ref[...] = jnp.zeros_like(m_ref)
        l_ref[...] = jnp.zeros_like(l_ref)
        sc_ref[...] = jnp.full_like(sc_ref, -jnp.inf)

    kt_all = jnp.concatenate([kp[i][...].astype(BF16) for i in range(npg)], axis=1)
    s_all = jnp.dot(qbd_ref[...], kt_all, preferred_element_type=F32)
    m_new, l_new, sc_new = m_ref[...], l_ref[...], sc_ref[...]
    for jb in range(step_blks):
        n = g * step_blks + jb
        s = s_all[:, jb * blk_keys:(jb + 1) * blk_keys]
        mb = jnp.max(s, axis=-1, keepdims=True)
        p = jnp.exp(s - mb)
        oblk_ref[n] = _fold_pairs(sum(lax.dot_general(p[:, a * page:(a + 1) * page].astype(BF16),
                                                      vp[jb * pages_per_blk + a][...].astype(BF16),
                                                      _NT, preferred_element_type=F32)
                                      for a in range(pages_per_blk)), n_heads, t_new)
        here = lane == n
        m_new = jnp.where(here, mb, m_new)
        l_new = jnp.where(here, jnp.sum(p, axis=-1, keepdims=True), l_new)
        sc_new = jnp.where(here, jnp.sum(s, axis=-1, keepdims=True) * (1.0 / blk_keys), sc_new)
    m_ref[...], l_ref[...], sc_ref[...] = m_new, l_new, sc_new

    @pl.when(is_last)
    def _():
        s = lax.dot_general(qbd_ref[...], _pad_rows(kn_ref[...], LANES).astype(BF16), _NT,
                            preferred_element_type=F32)
        qry = lax.broadcasted_iota(jnp.int32, (rows, LANES), 0) % t_new
        s = jnp.where(lane <= qry, s, NEG_INF)
        m_own = jnp.max(s, axis=-1, keepdims=True)
        p = jnp.exp(s - m_own)
        l_own = jnp.sum(p, axis=-1, keepdims=True)
        o_own = jnp.dot(p.astype(BF16), _pad_rows(vn_ref[...], LANES).astype(BF16), preferred_element_type=F32)
        sc = sc_ref[...]
        lane_f = lane.astype(F32)
        sel = lane < 0
        for _ in range(MOBA_TOPK):
            mx = jnp.max(sc, axis=-1, keepdims=True)
            first = jnp.min(jnp.where(sc == mx, lane_f, float(LANES)), axis=-1, keepdims=True)
            pick = lane_f == first
            sel = sel | pick
            sc = jnp.where(pick, -jnp.inf, sc)
        m_all = m_ref[...]
        m_fin = jnp.maximum(jnp.max(jnp.where(sel, m_all, -jnp.inf), axis=-1, keepdims=True), m_own)
        w = jnp.where(sel, jnp.exp(m_all - m_fin), 0.0)
        w_own = jnp.exp(m_own - m_fin)
        l_fin = jnp.sum(w * l_ref[...], axis=-1, keepdims=True) + w_own * l_own

        def body(n, acc):
            col = jnp.sum(jnp.where(lane == n, w, 0.0), axis=-1, keepdims=True)
            return acc + col * oblk_ref[n]

        acc = lax.fori_loop(0, nblk, body, w_own * _fold_pairs(o_own, n_heads, t_new))
        o_ref[...] = _unfold_pairs(acc / l_fin, n_heads, t_new)


def _moba_kernel(pt_ref, q_ref, kt_ref, vt_ref, qs_ref, kn_ref, vn_ref, *rest, nblk, prompt_steps, decode_steps,
                 n_heads, t_new, pages_per_blk, dec_nblk):
    npg = MOBA_PAGES_PER_STEP
    kp, vp = rest[:npg], rest[npg:2 * npg]
    o_ref, os_ref, kb_ref, vb_ref, kmean_ref, bias_ref, causal_ref, qbd_ref, m_ref, l_ref, sc_ref, oblk_ref = rest[2 * npg:]
    i = pl.program_id(0)
    _moba_prompt_step(i % prompt_steps, q_ref, kt_ref, vt_ref, o_ref, kb_ref, vb_ref, kmean_ref, bias_ref, causal_ref,
                      nblk=nblk)
    g = i % decode_steps
    _moba_decode_step(g, g == decode_steps - 1, qs_ref, kn_ref, vn_ref, kp, vp, os_ref, qbd_ref, m_ref, l_ref, sc_ref,
                      oblk_ref, n_heads=n_heads, t_new=t_new, pages_per_blk=pages_per_blk, nblk=dec_nblk)


def _moba(q, kt, vt, batch, seq, qs, k_new, v_new, cache_kt, cache_vt, page_table, t_new):
    n, width = q.shape
    blk, grp, nsub = ATT_TILE, ATT_GROUP, MOBA_SUBTILES
    assert blk == MOBA_BLOCK and seq % (blk * grp) == 0 and grp % nsub == 0
    nblk = seq // blk
    steps_p = nblk // nsub
    npair = width // LANES
    n_heads = width // HEAD_DIM
    n_seq, n_pages = page_table.shape
    page = cache_kt.shape[2]
    pages_per_blk = MOBA_BLOCK // page
    dec_nblk = n_pages // pages_per_blk
    npg = MOBA_PAGES_PER_STEP
    steps_d = n_pages // npg
    rows = n_heads * t_new
    assert n_pages % npg == 0 and npg % pages_per_blk == 0 and MOBA_TOPK <= dec_nblk <= LANES
    assert n_pages * page == dec_nblk * MOBA_BLOCK and t_new <= page and page == LANES and rows <= LANES
    assert n_heads % 2 == 0
    total = batch * npair * steps_p
    assert total == n_seq * steps_d, "prompt and decode sides must have the same number of steps"

    kv_spec = pl.BlockSpec((None, LANES, seq), lambda i, pt: (i // (npair * steps_p), (i // steps_p) % npair, 0),
                           pipeline_mode=pl.Buffered(1))
    q_spec = pl.BlockSpec((nsub * blk, LANES),
                          lambda i, pt: ((i // (npair * steps_p)) * steps_p + i % steps_p, (i // steps_p) % npair))
    tok_spec = pl.BlockSpec((t_new, width), lambda i, pt: (i // steps_d, 0))
    page_specs = [pl.BlockSpec((None, width, page), functools.partial(
        lambda i, pt, j: (pt[i // steps_d, (i % steps_d) * npg + j], 0, 0), j=j)) for j in range(npg)]
    grid_spec = pltpu.PrefetchScalarGridSpec(
        num_scalar_prefetch=1,
        grid=(total,),
        in_specs=[q_spec, kv_spec, kv_spec, tok_spec, tok_spec, tok_spec] + page_specs + page_specs,
        out_specs=[q_spec, tok_spec],
        scratch_shapes=[pltpu.VMEM((nblk * blk, LANES), BF16), pltpu.VMEM((nblk, LANES + ONES_ROWS, blk), BF16),
                        pltpu.VMEM((nblk, LANES), F32), pltpu.VMEM((nsub, nblk, 2 * blk), F32),
                        pltpu.VMEM((blk, 2 * blk), F32),
                        pltpu.VMEM((rows, width), BF16), pltpu.VMEM((rows, LANES), F32),
                        pltpu.VMEM((rows, LANES), F32), pltpu.VMEM((rows, LANES), F32),
                        pltpu.VMEM((dec_nblk, rows, LANES), F32)],
    )
    return pl.pallas_call(
        functools.partial(_moba_kernel, nblk=nblk, prompt_steps=steps_p, decode_steps=steps_d, n_heads=n_heads,
                          t_new=t_new, pages_per_blk=pages_per_blk, dec_nblk=dec_nblk),
        grid_spec=grid_spec,
        out_shape=[jax.ShapeDtypeStruct((n, width), F32), jax.ShapeDtypeStruct(qs.shape, F32)],
        compiler_params=_cparams("arbitrary"),
        name="moba",
    )(page_table, q, kt, vt, qs, k_new, v_new, *([cache_kt] * npg), *([cache_vt] * npg))


DIL_DEC_TILE = 512


def _dilated_decode_tables(n_heads, t_new, n_prev):
    qi = (np.arange(n_heads * t_new) % t_new)[:, None]
    c_old = _dilated_multiplicity(n_prev + qi - np.arange(n_prev)[None, :])
    c_new = _dilated_multiplicity(qi - np.arange(LANES)[None, :])
    c_new = np.where(np.arange(LANES)[None, :] < t_new, c_new, 0)
    f = lambda c: (np.where(c > 0, 0.0, NEG_INF).astype(np.float32), c.astype(np.float32))
    return f(c_old) + f(c_new)


def _shifted_window_tile(cur_ref, nxt_ref, new_ref, out_ref, tail_ref, is_last, t_new):
    lane = lax.broadcasted_iota(jnp.int32, (cur_ref.shape[0], LANES), 1)
    nblk = cur_ref.shape[1] // LANES

    @pl.when(is_last)
    def _():
        tail_ref[...] = _pad_rows(new_ref[...], LANES).T

    @pl.when(jnp.logical_not(is_last))
    def _():
        tail_ref[...] = nxt_ref[...]

    rolled = [pltpu.roll(cur_ref[:, c * LANES:(c + 1) * LANES], LANES - t_new, 1) for c in range(nblk)]
    rolled.append(pltpu.roll(tail_ref[...], LANES - t_new, 1))
    for c in range(nblk):
        out_ref[:, c * LANES:(c + 1) * LANES] = jnp.where(lane < LANES - t_new, rolled[c], rolled[c + 1])


def _dilated_decode_kernel(q_ref, kn_ref, vn_ref, kc_ref, vc_ref, kx_ref, vx_ref, bo_ref, mo_ref, bn_ref, mn_ref,
                           o_ref, ko_ref, vo_ref, qbd_ref, m_ref, l_ref, acc_ref, tail_ref, *, n_heads, t_new):
    kt = pl.program_id(1)
    is_last = kt == pl.num_programs(1) - 1
    _shifted_window_tile(kc_ref, kx_ref, kn_ref, ko_ref, tail_ref, is_last, t_new)
    _shifted_window_tile(vc_ref, vx_ref, vn_ref, vo_ref, tail_ref, is_last, t_new)

    @pl.when(kt == 0)
    def _():
        qbd_ref[...] = _scaled_bf16(_block_diag_queries(q_ref[...], n_heads))
        m_ref[...] = jnp.full_like(m_ref, -jnp.inf)
        l_ref[...] = jnp.zeros_like(l_ref)
        acc_ref[...] = jnp.zeros_like(acc_ref)

    def step(s, pv, bias, mult):
        s = s + bias
        m_old = m_ref[...]
        m_new = jnp.maximum(m_old, jnp.max(s, axis=-1, keepdims=True))
        alpha = jnp.exp(m_old - m_new)
        p = jnp.exp(s - m_new) * mult
        l_ref[...] = alpha * l_ref[...] + jnp.sum(p, axis=-1, keepdims=True)
        acc_ref[...] = alpha * acc_ref[...] + pv(p.astype(BF16))
        m_ref[...] = m_new

    @pl.when(kt == 0)
    def _():
        k, v = _pad_rows(kn_ref[...], LANES).astype(BF16), _pad_rows(vn_ref[...], LANES).astype(BF16)
        step(lax.dot_general(qbd_ref[...], k, _NT, preferred_element_type=F32),
             lambda p: jnp.dot(p, v, preferred_element_type=F32), bn_ref[...], mn_ref[...])

    step(jnp.dot(qbd_ref[...], kc_ref[...].astype(BF16), preferred_element_type=F32),
         lambda p: lax.dot_general(p, vc_ref[...].astype(BF16), _NT, preferred_element_type=F32),
         bo_ref[...], mo_ref[...])

    @pl.when(kt == pl.num_programs(1) - 1)
    def _():
        o_ref[...] = _take_diag(acc_ref[...] / l_ref[...], n_heads, t_new)


def _dilated_decode(q, k_new, v_new, win_kt, win_vt, t_new):
    n, width = q.shape
    n_heads = width // HEAD_DIM
    n_seq, _, n_prev = win_kt.shape
    tile = DIL_DEC_TILE
    assert n_prev == W_MAX and n_prev % tile == 0 and t_new <= LANES
    rows = n_heads * t_new
    bo, mo, bn, mn = _dilated_decode_tables(n_heads, t_new, n_prev)
    tok_spec = pl.BlockSpec((t_new, width), lambda b, t: (b, 0))
    win_spec = pl.BlockSpec((None, width, tile), lambda b, t: (b, 0, t))
    last_lane_blk = n_prev // LANES - 1
    nxt_spec = pl.BlockSpec((None, width, LANES),
                            lambda b, t: (b, 0, jnp.minimum((t + 1) * (tile // LANES), last_lane_blk)))
    old_tab = pl.BlockSpec((rows, tile), lambda b, t: (0, t))
    new_tab = pl.BlockSpec((rows, LANES), lambda b, t: (0, 0))
    win_shape = jax.ShapeDtypeStruct(win_kt.shape, F32)
    return pl.pallas_call(
        functools.partial(_dilated_decode_kernel, n_heads=n_heads, t_new=t_new),
        grid=(n_seq, n_prev // tile),
        in_specs=[tok_spec, tok_spec, tok_spec, win_spec, win_spec, nxt_spec, nxt_spec,
                  old_tab, old_tab, new_tab, new_tab],
        out_specs=[tok_spec, win_spec, win_spec],
        out_shape=[jax.ShapeDtypeStruct((n, width), F32), win_shape, win_shape],
        scratch_shapes=[pltpu.VMEM((rows, width), BF16), pltpu.VMEM((rows, 1), F32),
                        pltpu.VMEM((rows, 1), F32), pltpu.VMEM((rows, width), F32),
                        pltpu.VMEM((width, LANES), F32)],
        compiler_params=_cparams("parallel", "arbitrary"),
        name="dilated_decode",
    )(q, k_new, v_new, win_kt, win_vt, win_kt, win_vt,
      jnp.asarray(bo), jnp.asarray(mo), jnp.asarray(bn), jnp.asarray(mn))


_AB_SEGS = (("a", (0,), 0, 1.0), ("a", (1,), 0, 1.0), (None, (2,), 0, 1.0), ("b", (3,), 0, 1.0),
            ("b", (4,), 0, ATT_SCALE), (None, (5,), 0, 1.0), (None, (6,), 0, 1.0))
_C_SEGS = tuple(("a" if o < 2 else None, (o,), c, 1.0) for o in range(3) for c in (0, SEG))
_C_SEGS_PROMPT = tuple((kind, dests if dests == (0,) else dests + (dests[0] + 2,), c, post)
                       for kind, dests, c, post in _C_SEGS)
_KV_OUT = (1, 2)


def _row_tile(n):
    return 512 if n % 512 == 0 else n


def _feature_major(x):
    lead = x.shape[:-3]
    t, h, dh = x.shape[-3:]
    nl = len(lead)
    return x.transpose(*range(nl), nl + 1, nl + 2, nl).reshape(*lead, h * dh, t)


def _token_major(xt, n_heads):
    lead = xt.shape[:-2]
    t = xt.shape[-1]
    nl = len(lead)
    return xt.reshape(*lead, n_heads, HEAD_DIM, t).transpose(*range(nl), nl + 2, nl, nl + 1)


def kernel(x_prompt, x_sample, cache_k_a, cache_v_a, page_table, state_ret, cache_win_k, cache_win_v,
           norm_mix, norm_ffn, norm_final, w_in_ab, w_out_ab, ret_gn_w, w_in_c, w_out_c,
           ffn_w_gate, ffn_w_up, ffn_w_down):
    bp, tp, d = x_prompt.shape
    bs, ts, _ = x_sample.shape
    page = cache_k_a.shape[2]
    past_len = page_table.shape[1] * page
    h_a = cache_k_a.shape[3]
    wa = h_a * HEAD_DIM
    wb = w_in_ab.shape[2] - 3 * wa
    assert wb == 4 * wa and w_in_ab.shape[0] == 1 and w_in_c.shape[0] == 1 and norm_mix.shape[0] == 2
    h_b = wa // HEAD_DIM
    h_c = w_in_c.shape[2] // (3 * HEAD_DIM)

    pos_p = jnp.arange(tp, dtype=jnp.int32)
    pos_s = past_len + jnp.arange(ts, dtype=jnp.int32)
    tile_s = lambda tabs: tuple(jnp.tile(t, (bs, 1)) for t in tabs)
    tabs = {
        "p": (_rope_tables(pos_p, ROT_DIM, ROPE_THETA), _rope_tables(pos_p, HEAD_DIM, RET_THETA)),
        "s": (tile_s(_rope_tables(pos_s, ROT_DIM, ROPE_THETA)), tile_s(_rope_tables(pos_s, HEAD_DIM, RET_THETA))),
    }
    bf = lambda w: w.astype(BF16)
    w_in_ab_bf, w_out_ab_bf, w_in_c_bf, w_out_c_bf = bf(w_in_ab[0]), bf(w_out_ab[0]), bf(w_in_c[0]), bf(w_out_c[0])
    wg, wu, wd = bf(ffn_w_gate), bf(ffn_w_up), bf(ffn_w_down)

    xp = x_prompt.reshape(bp * tp, d)
    xs = x_sample.reshape(bs * ts, d)
    tm_p, tm_s = _row_tile(bp * tp), _row_tile(bs * ts)
    assert tp % tm_p == 0

    ab_widths = (wa,) * 7
    qa_p, kat_p, vat_p, qb_p, kb_p, vb_p, gb_p = _project(
        xp, norm_mix[0], w_in_ab_bf, *tabs["p"], _AB_SEGS, ab_widths, tm_p, _KV_OUT)
    qa_s, ka_s, va_s, qb_s, kb_s, vb_s, gb_s = _project(
        xs, norm_mix[0], w_in_ab_bf, *tabs["s"], _AB_SEGS, ab_widths, tm_s)

    oa_p, oa_s = _moba(qa_p, kat_p, vat_p, bp, tp, qa_s, ka_s, va_s, _feature_major(cache_k_a[0]),
                       _feature_major(cache_v_a[0]), page_table, ts)

    chunk_p = math.gcd(tp, RET_CHUNK)
    ob_p, ret_p = _retention(qb_p, kb_p, vb_p, gb_p, ret_gn_w[0],
                             jnp.zeros((bp, h_b, HEAD_DIM, HEAD_DIM), F32), bp, tp, chunk_p,
                             math.gcd(tp // chunk_p, 8))
    ob_s, ret_s = _retention(qb_s, kb_s, vb_s, gb_s, ret_gn_w[0], state_ret[0], bs, ts,
                             math.gcd(ts, RET_CHUNK), 1)

    xp = _mix_ffn([oa_p, ob_p], w_out_ab_bf, xp, norm_ffn[0], wg[0], wu[0], wd[0], norm_final, False, tm_p)
    xs = _mix_ffn([oa_s, ob_s], w_out_ab_bf, xs, norm_ffn[0], wg[0], wu[0], wd[0], norm_final, False, tm_s)

    wc = h_c * HEAD_DIM
    q_p, kt_p, vt_p, k_p, v_p = _project(xp, norm_mix[1], w_in_c_bf, *tabs["p"], _C_SEGS_PROMPT, (wc,) * 5,
                                         tm_p, _KV_OUT)
    q_s, k_s, v_s = _project(xs, norm_mix[1], w_in_c_bf, *tabs["s"], _C_SEGS, (wc,) * 3, tm_s)
    n_prev = cache_win_k.shape[2]
    o_p = _dilated_prompt(q_p, k_p, v_p, kt_p, vt_p, bp, tp)
    o_s, win_kt, win_vt = _dilated_decode(q_s, k_s, v_s, _feature_major(cache_win_k[0]),
                                          _feature_major(cache_win_v[0]), ts)
    y_p = _mix_ffn([o_p], w_out_c_bf, xp, norm_ffn[1], wg[1], wu[1], wd[1], norm_final, True, tm_p)
    y_s = _mix_ffn([o_s], w_out_c_bf, xs, norm_ffn[1], wg[1], wu[1], wd[1], norm_final, True, tm_s)

    keep_p = min(W_MAX, tp)
    return (
        y_p.reshape(bp, tp, d), y_s.reshape(bs, ts, d),
        _token_major(kat_p, h_a)[None], _token_major(vat_p, h_a)[None], ret_p[None],
        _token_major(kt_p[:, :, tp - keep_p:], h_c)[None], _token_major(vt_p[:, :, tp - keep_p:], h_c)[None],
        ka_s.reshape(1, bs, ts, h_a, HEAD_DIM), va_s.reshape(1, bs, ts, h_a, HEAD_DIM), ret_s[None],
        _token_major(win_kt, h_c)[None], _token_major(win_vt, h_c)[None],
    )
```

```python
import functools
import math

import jax
import jax.numpy as jnp
import numpy as np
from jax import lax
from jax.experimental import pallas as pl
from jax.experimental.pallas import tpu as pltpu

F32 = jnp.float32
BF16 = jnp.bfloat16

HEAD_DIM = 64
LANES = 128
ROT_DIM = HEAD_DIM // 4
ROPE_THETA = 500000.0
RET_THETA = 10000.0
MOBA_BLOCK = 256
MOBA_TOPK = 3
RET_CHUNK = 128
DILATED = ((128, 1), (512, 4), (2048, 16))
W_MAX = max(w for w, _ in DILATED)
EPS = 1e-6
NEG_INF = -1e30
ATT_SCALE = HEAD_DIM ** -0.5
VMEM_LIMIT = 56 * 1024 * 1024

_NT = (((1,), (1,)), ((), ()))
_TN = (((0,), (0,)), ((), ()))


def _cparams(*sem):
    return pltpu.CompilerParams(dimension_semantics=sem, vmem_limit_bytes=VMEM_LIMIT)


def _rms(x, w):
    ms = jnp.mean(x * x, axis=-1, keepdims=True)
    return x * lax.rsqrt(ms + EPS) * w


SEG = 512


def _proj_kernel(x_ref, nw_ref, w_ref, ca_ref, na_ref, pa_ref, cb_ref, nb_ref, pb_ref, *out_refs, segs, transposed):
    xn = _rms(x_ref[...], nw_ref[...]).astype(BF16)
    for s, (kind, dests, col, post) in enumerate(segs):
        acc = jnp.dot(xn, w_ref[:, s * SEG:(s + 1) * SEG], preferred_element_type=F32)
        for c in range(SEG // LANES):
            r = acc[:, c * LANES:(c + 1) * LANES]
            if kind is not None:
                c_ref, n_ref, p_ref, shift = (ca_ref, na_ref, pa_ref, ROT_DIM // 2) if kind == "a" else (
                    cb_ref, nb_ref, pb_ref, HEAD_DIM // 2)
                r = (r * c_ref[...] + pltpu.roll(r, LANES - shift, 1) * n_ref[...]
                     + pltpu.roll(r, shift, 1) * p_ref[...])
                if post != 1.0:
                    r = r * post
            lo = col + c * LANES
            for oi in dests:
                if oi in transposed:
                    out_refs[oi][lo:lo + LANES, :] = r.T
                else:
                    out_refs[oi][:, lo:lo + LANES] = r


def _rope_tables(pos, rot_dim, theta):
    half = rot_dim // 2
    inv = theta ** (-jnp.arange(half, dtype=F32) / half)
    ang = pos.astype(F32)[:, None] * inv[None, :]
    cos, sin = jnp.cos(ang), jnp.sin(ang)
    lane = np.arange(LANES) % HEAD_DIM
    idx = lane % half
    cos_t = jnp.where(lane < rot_dim, cos[:, idx], 1.0)
    neg_t = jnp.where(lane < half, -sin[:, idx], 0.0)
    pos_t = jnp.where((lane >= half) & (lane < rot_dim), sin[:, idx], 0.0)
    return cos_t.astype(F32), neg_t.astype(F32), pos_t.astype(F32)


def _project(x, nw, w_bf, tabs_a, tabs_b, segs, out_widths, tm, transposed=()):
    n, d = x.shape
    t_tab = tabs_a[0].shape[0]
    nt = t_tab // tm
    tab_spec = pl.BlockSpec((tm, LANES), lambda i: (i % nt, 0))
    out_specs, out_shape = [], []
    for oi, w in enumerate(out_widths):
        if oi in transposed:
            out_specs.append(pl.BlockSpec((None, w, tm), lambda i: (i // nt, 0, i % nt)))
            out_shape.append(jax.ShapeDtypeStruct((n // t_tab, w, t_tab), F32))
        else:
            out_specs.append(pl.BlockSpec((tm, w), lambda i: (i, 0)))
            out_shape.append(jax.ShapeDtypeStruct((n, w), F32))
    return pl.pallas_call(
        functools.partial(_proj_kernel, segs=segs, transposed=tuple(transposed)),
        grid=(n // tm,),
        in_specs=[pl.BlockSpec((tm, d), lambda i: (i, 0)),
                  pl.BlockSpec((1, d), lambda i: (0, 0)),
                  pl.BlockSpec(w_bf.shape, lambda i: (0, 0))] + [tab_spec] * 6,
        out_specs=out_specs,
        out_shape=out_shape,
        compiler_params=_cparams("parallel"),
        name="rms_proj_rope",
    )(x, nw.reshape(1, d), w_bf, *tabs_a, *tabs_b)


FFN_CHUNKS = 2


def _mix_ffn_kernel(*refs, nparts, final):
    parts = refs[:nparts]
    wo_ref, x_ref, nw_ref, wg_ref, wu_ref, wd_ref, fw_ref, o_ref = refs[nparts:]
    a = jnp.concatenate([p[...].astype(BF16) for p in parts], axis=1) if nparts > 1 else parts[0][...].astype(BF16)
    x = x_ref[...] + jnp.dot(a, wo_ref[...], preferred_element_type=F32)
    xn = _rms(x, nw_ref[...]).astype(BF16)
    tf = wg_ref.shape[1] // FFN_CHUNKS
    acc = x
    for c in range(FFN_CHUNKS):
        g = jnp.dot(xn, wg_ref[:, c * tf:(c + 1) * tf], preferred_element_type=F32)
        u = jnp.dot(xn, wu_ref[:, c * tf:(c + 1) * tf], preferred_element_type=F32)
        h = (g * jax.nn.sigmoid(g) * u).astype(BF16)
        acc = acc + jnp.dot(h, wd_ref[c * tf:(c + 1) * tf, :], preferred_element_type=F32)
    if final:
        acc = _rms(acc, fw_ref[...])
    o_ref[...] = acc


def _mix_ffn(parts, wo, x, nw, wg, wu, wd, fw, final, tm):
    n, d = x.shape
    dff = wg.shape[1]
    const = lambda i: (0, 0)
    row = lambda w: pl.BlockSpec((tm, w), lambda i: (i, 0))
    return pl.pallas_call(
        functools.partial(_mix_ffn_kernel, nparts=len(parts), final=final),
        grid=(n // tm,),
        in_specs=[row(p.shape[1]) for p in parts]
        + [pl.BlockSpec(wo.shape, const), row(d), pl.BlockSpec((1, d), const),
           pl.BlockSpec((d, dff), const), pl.BlockSpec((d, dff), const), pl.BlockSpec((dff, d), const),
           pl.BlockSpec((1, d), const)],
        out_specs=row(d),
        out_shape=jax.ShapeDtypeStruct((n, d), F32),
        compiler_params=_cparams("parallel"),
        name="outproj_swiglu",
    )(*parts, wo, x, nw.reshape(1, d), wg, wu, wd, fw.reshape(1, d))


ATT_TILE = 256
ATT_GROUP = 4
MOBA_SUBTILES = 4
ATT_CHUNK = 64


def _unit_scores(kb, qb, bias_fn):
    s = jnp.dot(kb, qb, preferred_element_type=F32)
    chunks = [s[r:r + ATT_CHUNK] + bias_fn(r, r + ATT_CHUNK) for r in range(0, s.shape[0], ATT_CHUNK)]
    return chunks, jnp.max(functools.reduce(jnp.maximum, chunks), axis=0, keepdims=True)


ONES_ROWS = 16


def _with_ones(vt):
    return jnp.concatenate([vt, jnp.ones((ONES_ROWS, vt.shape[1]), BF16)], axis=0)


def _unit_values(chunks, m_u, vts, mult_fn):
    ps = []
    for i, c in enumerate(chunks):
        p = jnp.exp((c - m_u).astype(BF16))
        if mult_fn is not None:
            p = p * mult_fn(i * ATT_CHUNK, (i + 1) * ATT_CHUNK).astype(BF16)
        ps.append(p)
    per = len(ps) // len(vts)
    pv = sum(jnp.dot(vt, jnp.concatenate(ps[b * per:(b + 1) * per], axis=0), preferred_element_type=F32)
             for b, vt in enumerate(vts))
    nfeat = pv.shape[0] - ONES_ROWS
    return pv[nfeat:nfeat + 1], pv[:nfeat]


def _attend_unit(kb, q2, bias_fn, vts, mult_fn=None):
    chunks, m_u = _unit_scores(kb, q2, bias_fn)
    return (m_u,) + _unit_values(chunks, m_u, vts, mult_fn)


def _merge(parts):
    m = functools.reduce(jnp.maximum, [p[0] for p in parts])
    ws = [jnp.exp(p[0] - m) for p in parts]
    return m, sum(w * p[1] for w, p in zip(ws, parts)), sum(w * p[2] for w, p in zip(ws, parts))


def _head_queries(q_t, feat):
    return [jnp.where(feat // HEAD_DIM == hh, q_t, 0.0) for hh in range(2)]


def _scaled_bf16(q):
    return (q * ATT_SCALE).astype(BF16)


def _both(x):
    return jnp.concatenate([x, x], axis=1)


def _pick_heads(x, feat):
    nq = x.shape[1] // 2
    return jnp.where(feat // HEAD_DIM == 0, x[:, :nq], x[:, nq:])


def _moba_prompt_step(step, q_ref, kt_ref, vt_ref, o_ref, kb_ref, vb_ref, kmean_ref, bias_ref, causal_ref, *, nblk):
    blk, grp = ATT_TILE, ATT_GROUP
    nsub = MOBA_SUBTILES

    @pl.when(step == 0)
    def _():
        for n in range(nblk):
            kblk = kt_ref[:, n * blk:(n + 1) * blk].T
            kmean_ref[n:n + 1, :] = jnp.sum(kblk, axis=0, keepdims=True) * (1.0 / blk)
            kb_ref[n * blk:(n + 1) * blk, :] = kblk.astype(BF16)
            vb_ref[n] = _with_ones(vt_ref[:, n * blk:(n + 1) * blk].astype(BF16))
        key_i = lax.broadcasted_iota(jnp.int32, (blk, 2 * blk), 0)
        qry_i = lax.broadcasted_iota(jnp.int32, (blk, 2 * blk), 1) % blk
        causal_ref[...] = jnp.where(key_i <= qry_i, 0.0, NEG_INF)

    feat = lax.broadcasted_iota(jnp.int32, (LANES, blk), 0)
    q2s = [jnp.concatenate(_head_queries(q_ref[sub * blk:(sub + 1) * blk, :].T, feat), axis=1) for sub in range(nsub)]
    qb2s = [_scaled_bf16(q2) for q2 in q2s]
    cols = nsub * 2 * blk
    blk_id = lax.broadcasted_iota(jnp.int32, (nblk, cols), 0)
    blk_f = blk_id.astype(F32)
    qt = step * nsub + lax.broadcasted_iota(jnp.int32, (1, cols), 1) // (2 * blk)
    sc = jnp.dot(kmean_ref[...], jnp.concatenate(q2s, axis=1), preferred_element_type=F32,
                 precision=lax.Precision.HIGHEST)
    sc = jnp.where(blk_id < qt, sc, NEG_INF)
    sel = jnp.zeros((nblk, cols), F32)
    for r in range(MOBA_TOPK):
        mx = jnp.max(sc, axis=0, keepdims=True)
        first = jnp.min(jnp.where(sc == mx, blk_f, float(nblk)), axis=0, keepdims=True)
        pick = blk_f == first
        sel = jnp.maximum(sel, jnp.where(pick & (r < qt), 1.0, 0.0))
        sc = jnp.where(pick, -jnp.inf, sc)
    bias = jnp.where((sel > 0.0) | (blk_id == qt), 0.0, NEG_INF)
    for sub in range(nsub):
        bias_ref[sub] = bias[:, sub * 2 * blk:(sub + 1) * 2 * blk]

    def group(sub, g, with_own):
        own_j = (step * nsub) % grp + sub
        nb = grp if not with_own else (sub + 1 if grp == nsub else grp)

        def bias_fn(r0, r1):
            j, r = r0 // blk, r0 % blk
            row = bias_ref[sub, pl.ds(g * grp + j, 1), :]
            if not with_own:
                return row
            return row + jnp.where(j == own_j, causal_ref[r:r + (r1 - r0), :], 0.0)

        kb = kb_ref[pl.ds(pl.multiple_of(g * grp * blk, blk), nb * blk), :]
        return _attend_unit(kb, qb2s[sub], bias_fn, [vb_ref[g * grp + b] for b in range(nb)])

    g_own = step * nsub // grp

    def body(g, carry):
        return sum((_merge([carry[3 * sub:3 * sub + 3], group(sub, g, False)]) for sub in range(nsub)), ())

    carry = lax.fori_loop(0, g_own, body, sum((group(sub, g_own, True) for sub in range(nsub)), ()))
    for sub in range(nsub):
        o_ref[sub * blk:(sub + 1) * blk, :] = _pick_heads(carry[3 * sub + 2] / carry[3 * sub + 1], feat).T


def _dilated_multiplicity(delta):
    c = np.zeros(delta.shape, np.int32)
    for window, dil in DILATED:
        c += ((delta >= 0) & (delta <= window) & (delta % dil == 0)).astype(np.int32)
    return c


DIL_NEAR = 2
DIL_FAR_CLASSES_PER_STEP = 4
assert DILATED[-1][1] % DIL_FAR_CLASSES_PER_STEP == 0
DIL_NEAR_SUBTILES = 8
DIL_FAR_STRIDE = DILATED[-1][1]
DIL_FAR_REACH = DILATED[-1][0] // DIL_FAR_STRIDE
assert DILATED[-2][0] <= ATT_TILE * DIL_NEAR and ATT_TILE % DIL_FAR_STRIDE == 0 and DIL_FAR_REACH <= ATT_TILE


def _mask_bias(valid):
    return np.where(valid, 0.0, NEG_INF).astype(np.float32)


def _dilated_near_tables():
    ki = np.arange(ATT_TILE)[:, None]
    qi = np.arange(ATT_TILE)[None, :]
    c = np.stack([_dilated_multiplicity(ATT_TILE * (DIL_NEAR - i) + qi - ki) for i in range(DIL_NEAR + 1)])
    return _mask_bias(c > 0), c.astype(np.float32)


def _dilated_far_tables():
    per_tile = ATT_TILE // DIL_FAR_STRIDE

    def table(k0, nk, q0):
        ka = k0 + np.arange(nk)[:, None]
        qa = q0 + np.arange(ATT_TILE)[None, :]
        return _mask_bias((qa - ka <= DIL_FAR_REACH) & (ka // per_tile < qa // per_tile - DIL_NEAR))

    return table(0, ATT_TILE, 0), table(ATT_TILE - DIL_FAR_REACH, ATT_TILE + DIL_FAR_REACH, ATT_TILE)


def _dilated_far_kernel(q_ref, k_ref, v_ref, b0_ref, b1_ref, o_ref, lse_ref, *, seq):
    stride, blk, reach = DIL_FAR_STRIDE, ATT_TILE, DIL_FAR_REACH
    ntile = seq // stride // blk
    feat = lax.broadcasted_iota(jnp.int32, (LANES, blk), 0)

    def one_class(cls):
        for tau in range(ntile):
            k0, nk, tab = (0, blk, b0_ref) if tau == 0 else (tau * blk - reach, blk + reach, b1_ref)
            rows_q = pl.ds(cls + stride * blk * tau, blk, stride=stride)
            rows_k = pl.ds(cls + stride * k0, nk, stride=stride)
            qb2 = _scaled_bf16(jnp.concatenate(_head_queries(q_ref[rows_q, :].T, feat), axis=1))
            kb = k_ref[rows_k, :].astype(BF16)
            vt = _with_ones(v_ref[rows_k, :].T.astype(BF16))
            m_u, l_u, pv = _attend_unit(kb, qb2, lambda r0, r1: _both(tab[r0:r1, :]), [vt])
            o_ref[rows_q, :] = _pick_heads(pv / l_u, feat).T
            lse_ref[rows_q, :] = _pick_heads(jnp.broadcast_to(m_u + jnp.log(l_u), pv.shape), feat).T

    def body(i, carry):
        for c in range(DIL_FAR_CLASSES_PER_STEP):
            one_class(i * DIL_FAR_CLASSES_PER_STEP + c)
        return carry

    lax.fori_loop(0, stride // DIL_FAR_CLASSES_PER_STEP, body, 0)


def _dilated_near_kernel(q_ref, kt_ref, vt_ref, far_ref, lse_ref, bias_ref, mult_ref, o_ref,
                         kb_ref, vb_ref, tab_ref, *, nblk):
    blk, near, nsub = ATT_TILE, DIL_NEAR, DIL_NEAR_SUBTILES
    step = pl.program_id(2)

    @pl.when(step == 0)
    def _():
        kb_ref[0:near * blk, :] = jnp.zeros((near * blk, LANES), BF16)
        for n in range(near):
            vb_ref[n] = jnp.zeros((LANES + ONES_ROWS, blk), BF16)
        for n in range(nblk):
            kb_ref[(n + near) * blk:(n + near + 1) * blk, :] = kt_ref[:, n * blk:(n + 1) * blk].T.astype(BF16)
            vb_ref[n + near] = _with_ones(vt_ref[:, n * blk:(n + 1) * blk].astype(BF16))

    @pl.when(step * nsub < near + nsub)
    def _():
        for sub in range(nsub):
            for i in range(near + 1):
                tab_ref[sub, i] = jnp.where(step * nsub + sub - near + i >= 0, bias_ref[i], NEG_INF)

    feat = lax.broadcasted_iota(jnp.int32, (LANES, blk), 0)
    for sub in range(nsub):
        qt = step * nsub + sub
        rows = slice(sub * blk, (sub + 1) * blk)
        qb2 = _scaled_bf16(jnp.concatenate(_head_queries(q_ref[rows, :].T, feat), axis=1))
        m2, l2, a2 = _attend_unit(
            kb_ref[pl.ds(pl.multiple_of(qt * blk, blk), (near + 1) * blk), :], qb2,
            lambda r0, r1: _both(tab_ref[sub, r0 // blk, r0 % blk:r0 % blk + (r1 - r0), :]),
            [vb_ref[qt + i] for i in range(near + 1)],
            lambda r0, r1: _both(mult_ref[r0 // blk, r0 % blk:r0 % blk + (r1 - r0), :]))
        m_n = _pick_heads(jnp.broadcast_to(m2, a2.shape), feat)
        l_n = _pick_heads(jnp.broadcast_to(l2, a2.shape), feat)
        a_n = _pick_heads(a2, feat)
        lse_f = lse_ref[rows, :].T
        m = jnp.maximum(m_n, lse_f)
        w_n, w_f = jnp.exp(m_n - m), jnp.exp(lse_f - m)
        o_ref[rows, :] = ((a_n * w_n + far_ref[rows, :].T * w_f) / (l_n * w_n + w_f)).T


def _dilated_prompt(q, k, v, kt, vt, batch, seq):
    n, width = q.shape
    blk, near, stride = ATT_TILE, DIL_NEAR, DIL_FAR_STRIDE
    assert seq % (blk * stride) == 0
    nblk = seq // blk
    npair = width // LANES
    const = lambda nd: (lambda *_: (0,) * nd)
    full = lambda t: pl.BlockSpec(t.shape, const(t.ndim))

    far_tabs = [jnp.asarray(t) for t in _dilated_far_tables()]
    seq_spec = pl.BlockSpec((seq, LANES), lambda b, p: (b, p))
    o_far, lse_far = pl.pallas_call(
        functools.partial(_dilated_far_kernel, seq=seq),
        grid=(batch, npair),
        in_specs=[seq_spec, seq_spec, seq_spec] + [full(t) for t in far_tabs],
        out_specs=[seq_spec, seq_spec],
        out_shape=[jax.ShapeDtypeStruct((n, width), F32)] * 2,
        compiler_params=_cparams("parallel", "parallel"),
        name="dilated_far",
    )(q, k, v, *far_tabs)

    near_tabs = [jnp.asarray(t) for t in _dilated_near_tables()]
    kv_spec = pl.BlockSpec((None, LANES, seq), lambda b, p, t: (b, p, 0))
    nsub = DIL_NEAR_SUBTILES
    assert nblk % nsub == 0
    steps = nblk // nsub
    q_spec = pl.BlockSpec((nsub * blk, LANES), lambda b, p, t: (b * steps + t, p))
    return pl.pallas_call(
        functools.partial(_dilated_near_kernel, nblk=nblk),
        grid=(batch, npair, steps),
        in_specs=[q_spec, kv_spec, kv_spec, q_spec, q_spec] + [full(t) for t in near_tabs],
        out_specs=q_spec,
        out_shape=jax.ShapeDtypeStruct((n, width), F32),
        scratch_shapes=[pltpu.VMEM(((nblk + near) * blk, LANES), BF16),
                        pltpu.VMEM((nblk + near, LANES + ONES_ROWS, blk), BF16),
                        pltpu.VMEM((nsub, near + 1, blk, blk), F32)],
        compiler_params=_cparams("parallel", "parallel", "arbitrary"),
        name="dilated_near",
    )(q, kt, vt, o_far, lse_far, *near_tabs)


RET_ROWS = 128


def _retention_tables(n_heads, chunk):
    log_g = jnp.log1p(-jnp.exp2(-5.0 - jnp.arange(n_heads, dtype=F32)))
    i = jnp.arange(chunk, dtype=F32)
    diff = i[:, None] - i[None, :]
    causal = diff >= 0
    decay = jnp.where(causal[None], jnp.exp(jnp.where(causal, diff, 0.0)[None] * log_g[:, None, None]), 0.0)
    q_dec = jnp.exp((i + 1.0)[:, None] * log_g[None, :])
    k_dec = jnp.exp((chunk - 1.0 - i)[:, None] * log_g[None, :])
    c_dec = jnp.exp(chunk * log_g)
    pad = RET_ROWS - chunk
    npair = n_heads // 2
    decay = jnp.pad(decay, ((0, 0), (0, pad), (0, pad))).reshape(npair, 2 * RET_ROWS, RET_ROWS)
    expand = lambda t: jnp.pad(jnp.repeat(t, HEAD_DIM, axis=1), ((0, pad), (0, 0))).reshape(
        RET_ROWS, npair, LANES).transpose(1, 0, 2)
    q_tab, k_tab = expand(q_dec), expand(k_dec)
    same_head = (np.arange(LANES)[:, None] // HEAD_DIM) == (np.arange(LANES)[None, :] // HEAD_DIM)
    c_tab = jnp.where(same_head[None], jnp.repeat(c_dec, HEAD_DIM).reshape(npair, LANES, 1), 0.0)
    return decay.astype(F32), q_tab.astype(F32), k_tab.astype(F32), c_tab.astype(F32)


def _retention_kernel(q_ref, k_ref, v_ref, g_ref, gn_ref, s0_ref, dec_ref, qd_ref, kd_ref, cd_ref,
                      o_ref, sout_ref, st_ref, *, chunk, nchunk):
    c = pl.program_id(2)

    @pl.when(c == 0)
    def _():
        zero = jnp.zeros((HEAD_DIM, HEAD_DIM), F32)
        st_ref[...] = jnp.concatenate([jnp.concatenate([s0_ref[0], zero], axis=1),
                                       jnp.concatenate([zero, s0_ref[1]], axis=1)], axis=0)

    pad = RET_ROWS - chunk
    lane = lax.broadcasted_iota(jnp.int32, (RET_ROWS, LANES), 1)
    head0 = lane < HEAD_DIM
    same_head = (lax.broadcasted_iota(jnp.int32, (LANES, LANES), 0) // HEAD_DIM) == (
        lax.broadcasted_iota(jnp.int32, (LANES, LANES), 1) // HEAD_DIM)

    def rows(ref, r):
        x = ref[pl.ds(r, chunk), :]
        return x if pad == 0 else jnp.concatenate([x, jnp.zeros((pad, LANES), F32)], axis=0)

    def seg_mean(x):
        s0 = jnp.sum(jnp.where(head0, x, 0.0), axis=-1, keepdims=True)
        s1 = jnp.sum(jnp.where(head0, 0.0, x), axis=-1, keepdims=True)
        return jnp.where(head0, s0, s1) * (1.0 / HEAD_DIM)

    def one_chunk(j, state):
        r = j * chunk
        q, k, v, g = rows(q_ref, r), rows(k_ref, r), rows(v_ref, r), rows(g_ref, r)
        kb, vb = k.astype(BF16), v.astype(BF16)
        q2 = jnp.concatenate([jnp.where(head0, q, 0.0), jnp.where(head0, 0.0, q)], axis=0).astype(BF16)
        attn = lax.dot_general(q2, kb, _NT, preferred_element_type=F32) * dec_ref[...]
        inn = jnp.dot(attn.astype(BF16), vb, preferred_element_type=F32)
        inner = jnp.where(head0, inn[:RET_ROWS], inn[RET_ROWS:])
        cross = jnp.dot(q.astype(BF16), state.astype(BF16), preferred_element_type=F32) * qd_ref[...]
        upd = lax.dot_general((k * kd_ref[...]).astype(BF16), vb, _TN, preferred_element_type=F32)
        o = inner + cross
        mu = seg_mean(o)
        var = seg_mean(jnp.square(o - mu))
        y = (o - mu) * lax.rsqrt(var + EPS) * gn_ref[...]
        y = y * (g * jax.nn.sigmoid(g))
        o_ref[r:r + chunk, :] = y[:chunk]
        return state * cd_ref[...] + jnp.where(same_head, upd, 0.0)

    state = st_ref[...]
    for j in range(nchunk):
        state = one_chunk(j, state)
    st_ref[...] = state

    @pl.when(c == pl.num_programs(2) - 1)
    def _():
        sout_ref[0] = st_ref[0:HEAD_DIM, 0:HEAD_DIM]
        sout_ref[1] = st_ref[HEAD_DIM:, HEAD_DIM:]


def _retention(q, k, v, g, gn_w, s0, batch, seq, chunk, nchunk):
    n, width = q.shape
    n_heads = width // HEAD_DIM
    npair = width // LANES
    rows = chunk * nchunk
    steps = seq // rows
    dec, qd, kd, cd = _retention_tables(n_heads, chunk)
    row_spec = pl.BlockSpec((rows, LANES), lambda b, p, c: (b * steps + c, p))
    st_spec = pl.BlockSpec((None, 2, HEAD_DIM, HEAD_DIM), lambda b, p, c: (b, p, 0, 0))
    tab = lambda t: pl.BlockSpec((None,) + t.shape[1:], lambda b, p, c: (p, 0, 0))
    out, s_new = pl.pallas_call(
        functools.partial(_retention_kernel, chunk=chunk, nchunk=nchunk),
        grid=(batch, npair, steps),
        in_specs=[row_spec, row_spec, row_spec, row_spec, pl.BlockSpec((1, LANES), lambda b, p, c: (0, p)),
                  st_spec, tab(dec), tab(qd), tab(kd), tab(cd)],
        out_specs=[row_spec, st_spec],
        out_shape=[jax.ShapeDtypeStruct((n, width), F32),
                   jax.ShapeDtypeStruct((batch, n_heads, HEAD_DIM, HEAD_DIM), F32)],
        scratch_shapes=[pltpu.VMEM((LANES, LANES), F32)],
        compiler_params=_cparams("parallel", "parallel", "arbitrary"),
        name="retention",
    )(q, k, v, g, gn_w.reshape(1, width), s0, dec, qd, kd, cd)
    return out, s_new


def _block_diag_queries(q, n_heads):
    t = q.shape[0]
    rep = jnp.concatenate([q] * n_heads, axis=0)
    rows = lax.broadcasted_iota(jnp.int32, rep.shape, 0) // t
    cols = lax.broadcasted_iota(jnp.int32, rep.shape, 1) // HEAD_DIM
    return jnp.where(rows == cols, rep, 0.0)


def _take_diag(o, n_heads, t):
    cols = lax.broadcasted_iota(jnp.int32, (t, o.shape[1]), 1) // HEAD_DIM
    out = jnp.zeros((t, o.shape[1]), F32)
    for h in range(n_heads):
        out = jnp.where(cols == h, o[h * t:(h + 1) * t, :], out)
    return out


def _fold_pairs(o, n_heads, t):
    rows = lax.broadcasted_iota(jnp.int32, (o.shape[0], LANES), 0) // (2 * t)
    out = jnp.zeros((o.shape[0], LANES), F32)
    for c in range(n_heads // 2):
        out = jnp.where(rows == c, o[:, c * LANES:(c + 1) * LANES], out)
    return out


def _unfold_pairs(a, n_heads, t):
    lane = lax.broadcasted_iota(jnp.int32, (t, LANES), 1)
    return jnp.concatenate([jnp.where(lane < HEAD_DIM, a[2 * c * t:(2 * c + 1) * t], a[(2 * c + 1) * t:(2 * c + 2) * t])
                            for c in range(n_heads // 2)], axis=1)


def _pad_rows(x, rows):
    return jnp.concatenate([x, jnp.zeros((rows - x.shape[0], x.shape[1]), x.dtype)], axis=0)


MOBA_PAGES_PER_STEP = 32


def _moba_decode_step(g, is_last, q_ref, kn_ref, vn_ref, kp, vp, o_ref, qbd_ref, m_ref, l_ref, sc_ref, oblk_ref,
                      *, n_heads, t_new, pages_per_blk, nblk):
    npg = len(kp)
    rows = n_heads * t_new
    lane = lax.broadcasted_iota(jnp.int32, (rows, LANES), 1)
    page = kp[0].shape[1]
    blk_keys = pages_per_blk * page
    step_blks = npg // pages_per_blk

    @pl.when(g == 0)
    def _():
        qbd_ref[...] = _scaled_bf16(_block_diag_queries(q_ref[...], n_heads))
        m_ref[...] = jnp.zeros_like(m_ref)
        l_ref[...] = jnp.zeros_like(l_ref)
        sc_ref[...] = jnp.full_like(sc_ref, -jnp.inf)

    kt_all = jnp.concatenate([kp[i][...].astype(BF16) for i in range(npg)], axis=1)
    s_all = jnp.dot(qbd_ref[...], kt_all, preferred_element_type=F32)
    m_new, l_new, sc_new = m_ref[...], l_ref[...], sc_ref[...]
    for jb in range(step_blks):
        n = g * step_blks + jb
        s = s_all[:, jb * blk_keys:(jb + 1) * blk_keys]
        mb = jnp.max(s, axis=-1, keepdims=True)
        p = jnp.exp(s - mb)
        oblk_ref[n] = _fold_pairs(sum(lax.dot_general(p[:, a * page:(a + 1) * page].astype(BF16),
                                                      vp[jb * pages_per_blk + a][...].astype(BF16),
                                                      _NT, preferred_element_type=F32)
                                      for a in range(pages_per_blk)), n_heads, t_new)
        here = lane == n
        m_new = jnp.where(here, mb, m_new)
        l_new = jnp.where(here, jnp.sum(p, axis=-1, keepdims=True), l_new)
        sc_new = jnp.where(here, jnp.sum(s, axis=-1, keepdims=True) * (1.0 / blk_keys), sc_new)
    m_ref[...], l_ref[...], sc_ref[...] = m_new, l_new, sc_new

    @pl.when(is_last)
    def _():
        s = lax.dot_general(qbd_ref[...], _pad_rows(kn_ref[...], LANES).astype(BF16), _NT,
                            preferred_element_type=F32)
        qry = lax.broadcasted_iota(jnp.int32, (rows, LANES), 0) % t_new
        s = jnp.where(lane <= qry, s, NEG_INF)
        m_own = jnp.max(s, axis=-1, keepdims=True)
        p = jnp.exp(s - m_own)
        l_own = jnp.sum(p, axis=-1, keepdims=True)
        o_own = jnp.dot(p.astype(BF16), _pad_rows(vn_ref[...], LANES).astype(BF16), preferred_element_type=F32)
        sc = sc_ref[...]
        lane_f = lane.astype(F32)
        sel = lane < 0
        for _ in range(MOBA_TOPK):
            mx = jnp.max(sc, axis=-1, keepdims=True)
            first = jnp.min(jnp.where(sc == mx, lane_f, float(LANES)), axis=-1, keepdims=True)
            pick = lane_f == first
            sel = sel | pick
            sc = jnp.where(pick, -jnp.inf, sc)
        m_all = m_ref[...]
        m_fin = jnp.maximum(jnp.max(jnp.where(sel, m_all, -jnp.inf), axis=-1, keepdims=True), m_own)
        w = jnp.where(sel, jnp.exp(m_all - m_fin), 0.0)
        w_own = jnp.exp(m_own - m_fin)
        l_fin = jnp.sum(w * l_ref[...], axis=-1, keepdims=True) + w_own * l_own

        def body(n, acc):
            col = jnp.sum(jnp.where(lane == n, w, 0.0), axis=-1, keepdims=True)
            return acc + col * oblk_ref[n]

        acc = lax.fori_loop(0, nblk, body, w_own * _fold_pairs(o_own, n_heads, t_new))
        o_ref[...] = _unfold_pairs(acc / l_fin, n_heads, t_new)


def _moba_kernel(pt_ref, q_ref, kt_ref, vt_ref, qs_ref, kn_ref, vn_ref, *rest, nblk, prompt_steps, decode_steps,
                 n_heads, t_new, pages_per_blk, dec_nblk):
    npg = MOBA_PAGES_PER_STEP
    kp, vp = rest[:npg], rest[npg:2 * npg]
    o_ref, os_ref, kb_ref, vb_ref, kmean_ref, bias_ref, causal_ref, qbd_ref, m_ref, l_ref, sc_ref, oblk_ref = rest[2 * npg:]
    i = pl.program_id(0)
    _moba_prompt_step(i % prompt_steps, q_ref, kt_ref, vt_ref, o_ref, kb_ref, vb_ref, kmean_ref, bias_ref, causal_ref,
                      nblk=nblk)
    g = i % decode_steps
    _moba_decode_step(g, g == decode_steps - 1, qs_ref, kn_ref, vn_ref, kp, vp, os_ref, qbd_ref, m_ref, l_ref, sc_ref,
                      oblk_ref, n_heads=n_heads, t_new=t_new, pages_per_blk=pages_per_blk, nblk=dec_nblk)


def _moba(q, kt, vt, batch, seq, qs, k_new, v_new, cache_kt, cache_vt, page_table, t_new):
    n, width = q.shape
    blk, grp, nsub = ATT_TILE, ATT_GROUP, MOBA_SUBTILES
    assert blk == MOBA_BLOCK and seq % (blk * grp) == 0 and grp % nsub == 0
    nblk = seq // blk
    steps_p = nblk // nsub
    npair = width // LANES
    n_heads = width // HEAD_DIM
    n_seq, n_pages = page_table.shape
    page = cache_kt.shape[2]
    pages_per_blk = MOBA_BLOCK // page
    dec_nblk = n_pages // pages_per_blk
    npg = MOBA_PAGES_PER_STEP
    steps_d = n_pages // npg
    rows = n_heads * t_new
    assert n_pages % npg == 0 and npg % pages_per_blk == 0 and MOBA_TOPK <= dec_nblk <= LANES
    assert n_pages * page == dec_nblk * MOBA_BLOCK and t_new <= page and page == LANES and rows <= LANES
    assert n_heads % 2 == 0
    total = batch * npair * steps_p
    assert total == n_seq * steps_d, "prompt and decode sides must have the same number of steps"

    kv_spec = pl.BlockSpec((None, LANES, seq), lambda i, pt: (i // (npair * steps_p), (i // steps_p) % npair, 0),
                           pipeline_mode=pl.Buffered(1))
    q_spec = pl.BlockSpec((nsub * blk, LANES),
                          lambda i, pt: ((i // (npair * steps_p)) * steps_p + i % steps_p, (i // steps_p) % npair))
    tok_spec = pl.BlockSpec((t_new, width), lambda i, pt: (i // steps_d, 0))
    page_specs = [pl.BlockSpec((None, width, page), functools.partial(
        lambda i, pt, j: (pt[i // steps_d, (i % steps_d) * npg + j], 0, 0), j=j)) for j in range(npg)]
    grid_spec = pltpu.PrefetchScalarGridSpec(
        num_scalar_prefetch=1,
        grid=(total,),
        in_specs=[q_spec, kv_spec, kv_spec, tok_spec, tok_spec, tok_spec] + page_specs + page_specs,
        out_specs=[q_spec, tok_spec],
        scratch_shapes=[pltpu.VMEM((nblk * blk, LANES), BF16), pltpu.VMEM((nblk, LANES + ONES_ROWS, blk), BF16),
                        pltpu.VMEM((nblk, LANES), F32), pltpu.VMEM((nsub, nblk, 2 * blk), F32),
                        pltpu.VMEM((blk, 2 * blk), F32),
                        pltpu.VMEM((rows, width), BF16), pltpu.VMEM((rows, LANES), F32),
                        pltpu.VMEM((rows, LANES), F32), pltpu.VMEM((rows, LANES), F32),
                        pltpu.VMEM((dec_nblk, rows, LANES), F32)],
    )
    return pl.pallas_call(
        functools.partial(_moba_kernel, nblk=nblk, prompt_steps=steps_p, decode_steps=steps_d, n_heads=n_heads,
                          t_new=t_new, pages_per_blk=pages_per_blk, dec_nblk=dec_nblk),
        grid_spec=grid_spec,
        out_shape=[jax.ShapeDtypeStruct((n, width), F32), jax.ShapeDtypeStruct(qs.shape, F32)],
        compiler_params=_cparams("arbitrary"),
        name="moba",
    )(page_table, q, kt, vt, qs, k_new, v_new, *([cache_kt] * npg), *([cache_vt] * npg))


DIL_DEC_TILE = 512


def _dilated_decode_tables(n_heads, t_new, n_prev):
    qi = (np.arange(n_heads * t_new) % t_new)[:, None]
    c_old = _dilated_multiplicity(n_prev + qi - np.arange(n_prev)[None, :])
    c_new = _dilated_multiplicity(qi - np.arange(LANES)[None, :])
    c_new = np.where(np.arange(LANES)[None, :] < t_new, c_new, 0)
    f = lambda c: (np.where(c > 0, 0.0, NEG_INF).astype(np.float32), c.astype(np.float32))
    return f(c_old) + f(c_new)


def _shifted_window_tile(cur_ref, nxt_ref, new_ref, out_ref, tail_ref, is_last, t_new):
    lane = lax.broadcasted_iota(jnp.int32, (cur_ref.shape[0], LANES), 1)
    nblk = cur_ref.shape[1] // LANES

    @pl.when(is_last)
    def _():
        tail_ref[...] = _pad_rows(new_ref[...], LANES).T

    @pl.when(jnp.logical_not(is_last))
    def _():
        tail_ref[...] = nxt_ref[...]

    rolled = [pltpu.roll(cur_ref[:, c * LANES:(c + 1) * LANES], LANES - t_new, 1) for c in range(nblk)]
    rolled.append(pltpu.roll(tail_ref[...], LANES - t_new, 1))
    for c in range(nblk):
        out_ref[:, c * LANES:(c + 1) * LANES] = jnp.where(lane < LANES - t_new, rolled[c], rolled[c + 1])


def _dilated_decode_kernel(q_ref, kn_ref, vn_ref, kc_ref, vc_ref, kx_ref, vx_ref, bo_ref, mo_ref, bn_ref, mn_ref,
                           o_ref, ko_ref, vo_ref, qbd_ref, m_ref, l_ref, acc_ref, tail_ref, *, n_heads, t_new):
    kt = pl.program_id(1)
    is_last = kt == pl.num_programs(1) - 1
    _shifted_window_tile(kc_ref, kx_ref, kn_ref, ko_ref, tail_ref, is_last, t_new)
    _shifted_window_tile(vc_ref, vx_ref, vn_ref, vo_ref, tail_ref, is_last, t_new)

    @pl.when(kt == 0)
    def _():
        qbd_ref[...] = _scaled_bf16(_block_diag_queries(q_ref[...], n_heads))
        m_ref[...] = jnp.full_like(m_ref, -jnp.inf)
        l_ref[...] = jnp.zeros_like(l_ref)
        acc_ref[...] = jnp.zeros_like(acc_ref)

    def step(s, pv, bias, mult):
        s = s + bias
        m_old = m_ref[...]
        m_new = jnp.maximum(m_old, jnp.max(s, axis=-1, keepdims=True))
        alpha = jnp.exp(m_old - m_new)
        p = jnp.exp(s - m_new) * mult
        l_ref[...] = alpha * l_ref[...] + jnp.sum(p, axis=-1, keepdims=True)
        acc_ref[...] = alpha * acc_ref[...] + pv(p.astype(BF16))
        m_ref[...] = m_new

    @pl.when(kt == 0)
    def _():
        k, v = _pad_rows(kn_ref[...], LANES).astype(BF16), _pad_rows(vn_ref[...], LANES).astype(BF16)
        step(lax.dot_general(qbd_ref[...], k, _NT, preferred_element_type=F32),
             lambda p: jnp.dot(p, v, preferred_element_type=F32), bn_ref[...], mn_ref[...])

    step(jnp.dot(qbd_ref[...], kc_ref[...].astype(BF16), preferred_element_type=F32),
         lambda p: lax.dot_general(p, vc_ref[...].astype(BF16), _NT, preferred_element_type=F32),
         bo_ref[...], mo_ref[...])

    @pl.when(kt == pl.num_programs(1) - 1)
    def _():
        o_ref[...] = _take_diag(acc_ref[...] / l_ref[...], n_heads, t_new)


def _dilated_decode(q, k_new, v_new, win_kt, win_vt, t_new):
    n, width = q.shape
    n_heads = width // HEAD_DIM
    n_seq, _, n_prev = win_kt.shape
    tile = DIL_DEC_TILE
    assert n_prev == W_MAX and n_prev % tile == 0 and t_new <= LANES
    rows = n_heads * t_new
    bo, mo, bn, mn = _dilated_decode_tables(n_heads, t_new, n_prev)
    tok_spec = pl.BlockSpec((t_new, width), lambda b, t: (b, 0))
    win_spec = pl.BlockSpec((None, width, tile), lambda b, t: (b, 0, t))
    last_lane_blk = n_prev // LANES - 1
    nxt_spec = pl.BlockSpec((None, width, LANES),
                            lambda b, t: (b, 0, jnp.minimum((t + 1) * (tile // LANES), last_lane_blk)))
    old_tab = pl.BlockSpec((rows, tile), lambda b, t: (0, t))
    new_tab = pl.BlockSpec((rows, LANES), lambda b, t: (0, 0))
    win_shape = jax.ShapeDtypeStruct(win_kt.shape, F32)
    return pl.pallas_call(
        functools.partial(_dilated_decode_kernel, n_heads=n_heads, t_new=t_new),
        grid=(n_seq, n_prev // tile),
        in_specs=[tok_spec, tok_spec, tok_spec, win_spec, win_spec, nxt_spec, nxt_spec,
                  old_tab, old_tab, new_tab, new_tab],
        out_specs=[tok_spec, win_spec, win_spec],
        out_shape=[jax.ShapeDtypeStruct((n, width), F32), win_shape, win_shape],
        scratch_shapes=[pltpu.VMEM((rows, width), BF16), pltpu.VMEM((rows, 1), F32),
                        pltpu.VMEM((rows, 1), F32), pltpu.VMEM((rows, width), F32),
                        pltpu.VMEM((width, LANES), F32)],
        compiler_params=_cparams("parallel", "arbitrary"),
        name="dilated_decode",
    )(q, k_new, v_new, win_kt, win_vt, win_kt, win_vt,
      jnp.asarray(bo), jnp.asarray(mo), jnp.asarray(bn), jnp.asarray(mn))


_AB_SEGS = (("a", (0,), 0, 1.0), ("a", (1,), 0, 1.0), (None, (2,), 0, 1.0), ("b", (3,), 0, 1.0),
            ("b", (4,), 0, ATT_SCALE), (None, (5,), 0, 1.0), (None, (6,), 0, 1.0))
_C_SEGS = tuple(("a" if o < 2 else None, (o,), c, 1.0) for o in range(3) for c in (0, SEG))
_C_SEGS_PROMPT = tuple((kind, dests if dests == (0,) else dests + (dests[0] + 2,), c, post)
                       for kind, dests, c, post in _C_SEGS)
_KV_OUT = (1, 2)


def _row_tile(n):
    return 512 if n % 512 == 0 else n


def _feature_major(x):
    lead = x.shape[:-3]
    t, h, dh = x.shape[-3:]
    nl = len(lead)
    return x.transpose(*range(nl), nl + 1, nl + 2, nl).reshape(*lead, h * dh, t)


def _token_major(xt, n_heads):
    lead = xt.shape[:-2]
    t = xt.shape[-1]
    nl = len(lead)
    return xt.reshape(*lead, n_heads, HEAD_DIM, t).transpose(*range(nl), nl + 2, nl, nl + 1)


def kernel(x_prompt, x_sample, cache_k_a, cache_v_a, page_table, state_ret, cache_win_k, cache_win_v,
           norm_mix, norm_ffn, norm_final, w_in_ab, w_out_ab, ret_gn_w, w_in_c, w_out_c,
           ffn_w_gate, ffn_w_up, ffn_w_down):
    bp, tp, d = x_prompt.shape
    bs, ts, _ = x_sample.shape
    page = cache_k_a.shape[2]
    past_len = page_table.shape[1] * page
    h_a = cache_k_a.shape[3]
    wa = h_a * HEAD_DIM
    wb = w_in_ab.shape[2] - 3 * wa
    assert wb == 4 * wa and w_in_ab.shape[0] == 1 and w_in_c.shape[0] == 1 and norm_mix.shape[0] == 2
    h_b = wa // HEAD_DIM
    h_c = w_in_c.shape[2] // (3 * HEAD_DIM)

    pos_p = jnp.arange(tp, dtype=jnp.int32)
    pos_s = past_len + jnp.arange(ts, dtype=jnp.int32)
    tile_s = lambda tabs: tuple(jnp.tile(t, (bs, 1)) for t in tabs)
    tabs = {
        "p": (_rope_tables(pos_p, ROT_DIM, ROPE_THETA), _rope_tables(pos_p, HEAD_DIM, RET_THETA)),
        "s": (tile_s(_rope_tables(pos_s, ROT_DIM, ROPE_THETA)), tile_s(_rope_tables(pos_s, HEAD_DIM, RET_THETA))),
    }
    bf = lambda w: w.astype(BF16)
    w_in_ab_bf, w_out_ab_bf, w_in_c_bf, w_out_c_bf = bf(w_in_ab[0]), bf(w_out_ab[0]), bf(w_in_c[0]), bf(w_out_c[0])
    wg, wu, wd = bf(ffn_w_gate), bf(ffn_w_up), bf(ffn_w_down)

    xp = x_prompt.reshape(bp * tp, d)
    xs = x_sample.reshape(bs * ts, d)
    tm_p, tm_s = _row_tile(bp * tp), _row_tile(bs * ts)
    assert tp % tm_p == 0

    ab_widths = (wa,) * 7
    qa_p, kat_p, vat_p, qb_p, kb_p, vb_p, gb_p = _project(
        xp, norm_mix[0], w_in_ab_bf, *tabs["p"], _AB_SEGS, ab_widths, tm_p, _KV_OUT)
    qa_s, ka_s, va_s, qb_s, kb_s, vb_s, gb_s = _project(
        xs, norm_mix[0], w_in_ab_bf, *tabs["s"], _AB_SEGS, ab_widths, tm_s)

    oa_p, oa_s = _moba(qa_p, kat_p, vat_p, bp, tp, qa_s, ka_s, va_s, _feature_major(cache_k_a[0]),
                       _feature_major(cache_v_a[0]), page_table, ts)

    chunk_p = math.gcd(tp, RET_CHUNK)
    ob_p, ret_p = _retention(qb_p, kb_p, vb_p, gb_p, ret_gn_w[0],
                             jnp.zeros((bp, h_b, HEAD_DIM, HEAD_DIM), F32), bp, tp, chunk_p,
                             math.gcd(tp // chunk_p, 8))
    ob_s, ret_s = _retention(qb_s, kb_s, vb_s, gb_s, ret_gn_w[0], state_ret[0], bs, ts,
                             math.gcd(ts, RET_CHUNK), 1)

    xp = _mix_ffn([oa_p, ob_p], w_out_ab_bf, xp, norm_ffn[0], wg[0], wu[0], wd[0], norm_final, False, tm_p)
    xs = _mix_ffn([oa_s, ob_s], w_out_ab_bf, xs, norm_ffn[0], wg[0], wu[0], wd[0], norm_final, False, tm_s)

    wc = h_c * HEAD_DIM
    q_p, kt_p, vt_p, k_p, v_p = _project(xp, norm_mix[1], w_in_c_bf, *tabs["p"], _C_SEGS_PROMPT, (wc,) * 5,
                                         tm_p, _KV_OUT)
    q_s, k_s, v_s = _project(xs, norm_mix[1], w_in_c_bf, *tabs["s"], _C_SEGS, (wc,) * 3, tm_s)
    n_prev = cache_win_k.shape[2]
    o_p = _dilated_prompt(q_p, k_p, v_p, kt_p, vt_p, bp, tp)
    o_s, win_kt, win_vt = _dilated_decode(q_s, k_s, v_s, _feature_major(cache_win_k[0]),
                                          _feature_major(cache_win_v[0]), ts)
    y_p = _mix_ffn([o_p], w_out_c_bf, xp, norm_ffn[1], wg[1], wu[1], wd[1], norm_final, True, tm_p)
    y_s = _mix_ffn([o_s], w_out_c_bf, xs, norm_ffn[1], wg[1], wu[1], wd[1], norm_final, True, tm_s)

    keep_p = min(W_MAX, tp)
    return (
        y_p.reshape(bp, tp, d), y_s.reshape(bs, ts, d),
        _token_major(kat_p, h_a)[None], _token_major(vat_p, h_a)[None], ret_p[None],
        _token_major(kt_p[:, :, tp - keep_p:], h_c)[None], _token_major(vt_p[:, :, tp - keep_p:], h_c)[None],
        ka_s.reshape(1, bs, ts, h_a, HEAD_DIM), va_s.reshape(1, bs, ts, h_a, HEAD_DIM), ret_s[None],
        _token_major(win_kt, h_c)[None], _token_major(win_vt, h_c)[None],
    )
```

```python
import functools
import math

import jax
import jax.numpy as jnp
import numpy as np
from jax import lax
from jax.experimental import pallas as pl
from jax.experimental.pallas import tpu as pltpu

F32 = jnp.float32
BF16 = jnp.bfloat16

HEAD_DIM = 64
LANES = 128
ROT_DIM = HEAD_DIM // 4
ROPE_THETA = 500000.0
RET_THETA = 10000.0
MOBA_BLOCK = 256
MOBA_TOPK = 3
RET_CHUNK = 128
DILATED = ((128, 1), (512, 4), (2048, 16))
W_MAX = max(w for w, _ in DILATED)
EPS = 1e-6
NEG_INF = -1e30
ATT_SCALE = HEAD_DIM ** -0.5
VMEM_LIMIT = 56 * 1024 * 1024

_NT = (((1,), (1,)), ((), ()))
_TN = (((0,), (0,)), ((), ()))


def _cparams(*sem):
    return pltpu.CompilerParams(dimension_semantics=sem, vmem_limit_bytes=VMEM_LIMIT)


def _rms(x, w):
    ms = jnp.mean(x * x, axis=-1, keepdims=True)
    return x * lax.rsqrt(ms + EPS) * w


SEG = 512


def _proj_kernel(x_ref, nw_ref, w_ref, ca_ref, na_ref, pa_ref, cb_ref, nb_ref, pb_ref, *out_refs, segs, transposed):
    xn = _rms(x_ref[...], nw_ref[...]).astype(BF16)
    for s, (kind, dests, col, post) in enumerate(segs):
        acc = jnp.dot(xn, w_ref[:, s * SEG:(s + 1) * SEG], preferred_element_type=F32)
        for c in range(SEG // LANES):
            r = acc[:, c * LANES:(c + 1) * LANES]
            if kind is not None:
                c_ref, n_ref, p_ref, shift = (ca_ref, na_ref, pa_ref, ROT_DIM // 2) if kind == "a" else (
                    cb_ref, nb_ref, pb_ref, HEAD_DIM // 2)
                r = (r * c_ref[...] + pltpu.roll(r, LANES - shift, 1) * n_ref[...]
                     + pltpu.roll(r, shift, 1) * p_ref[...])
                if post != 1.0:
                    r = r * post
            lo = col + c * LANES
            for oi in dests:
                if oi in transposed:
                    out_refs[oi][lo:lo + LANES, :] = r.T
                else:
                    out_refs[oi][:, lo:lo + LANES] = r


def _rope_tables(pos, rot_dim, theta):
    half = rot_dim // 2
    inv = theta ** (-jnp.arange(half, dtype=F32) / half)
    ang = pos.astype(F32)[:, None] * inv[None, :]
    cos, sin = jnp.cos(ang), jnp.sin(ang)
    lane = np.arange(LANES) % HEAD_DIM
    idx = lane % half
    cos_t = jnp.where(lane < rot_dim, cos[:, idx], 1.0)
    neg_t = jnp.where(lane < half, -sin[:, idx], 0.0)
    pos_t = jnp.where((lane >= half) & (lane < rot_dim), sin[:, idx], 0.0)
    return cos_t.astype(F32), neg_t.astype(F32), pos_t.astype(F32)


def _project(x, nw, w_bf, tabs_a, tabs_b, segs, out_widths, tm, transposed=()):
    n, d = x.shape
    t_tab = tabs_a[0].shape[0]
    nt = t_tab // tm
    tab_spec = pl.BlockSpec((tm, LANES), lambda i: (i % nt, 0))
    out_specs, out_shape = [], []
    for oi, w in enumerate(out_widths):
        if oi in transposed:
            out_specs.append(pl.BlockSpec((None, w, tm), lambda i: (i // nt, 0, i % nt)))
            out_shape.append(jax.ShapeDtypeStruct((n // t_tab, w, t_tab), F32))
        else:
            out_specs.append(pl.BlockSpec((tm, w), lambda i: (i, 0)))
            out_shape.append(jax.ShapeDtypeStruct((n, w), F32))
    return pl.pallas_call(
        functools.partial(_proj_kernel, segs=segs, transposed=tuple(transposed)),
        grid=(n // tm,),
        in_specs=[pl.BlockSpec((tm, d), lambda i: (i, 0)),
                  pl.BlockSpec((1, d), lambda i: (0, 0)),
                  pl.BlockSpec(w_bf.shape, lambda i: (0, 0))] + [tab_spec] * 6,
        out_specs=out_specs,
        out_shape=out_shape,
        compiler_params=_cparams("parallel"),
        name="rms_proj_rope",
    )(x, nw.reshape(1, d), w_bf, *tabs_a, *tabs_b)


FFN_CHUNKS = 2


def _mix_ffn_kernel(*refs, nparts, final):
    parts = refs[:nparts]
    wo_ref, x_ref, nw_ref, wg_ref, wu_ref, wd_ref, fw_ref, o_ref = refs[nparts:]
    a = jnp.concatenate([p[...].astype(BF16) for p in parts], axis=1) if nparts > 1 else parts[0][...].astype(BF16)
    x = x_ref[...] + jnp.dot(a, wo_ref[...], preferred_element_type=F32)
    xn = _rms(x, nw_ref[...]).astype(BF16)
    tf = wg_ref.shape[1] // FFN_CHUNKS
    acc = x
    for c in range(FFN_CHUNKS):
        g = jnp.dot(xn, wg_ref[:, c * tf:(c + 1) * tf], preferred_element_type=F32)
        u = jnp.dot(xn, wu_ref[:, c * tf:(c + 1) * tf], preferred_element_type=F32)
        h = (g * jax.nn.sigmoid(g) * u).astype(BF16)
        acc = acc + jnp.dot(h, wd_ref[c * tf:(c + 1) * tf, :], preferred_element_type=F32)
    if final:
        acc = _rms(acc, fw_ref[...])
    o_ref[...] = acc


def _mix_ffn(parts, wo, x, nw, wg, wu, wd, fw, final, tm):
    n, d = x.shape
    dff = wg.shape[1]
    const = lambda i: (0, 0)
    row = lambda w: pl.BlockSpec((tm, w), lambda i: (i, 0))
    return pl.pallas_call(
        functools.partial(_mix_ffn_kernel, nparts=len(parts), final=final),
        grid=(n // tm,),
        in_specs=[row(p.shape[1]) for p in parts]
        + [pl.BlockSpec(wo.shape, const), row(d), pl.BlockSpec((1, d), const),
           pl.BlockSpec((d, dff), const), pl.BlockSpec((d, dff), const), pl.BlockSpec((dff, d), const),
           pl.BlockSpec((1, d), const)],
        out_specs=row(d),
        out_shape=jax.ShapeDtypeStruct((n, d), F32),
        compiler_params=_cparams("parallel"),
        name="outproj_swiglu",
    )(*parts, wo, x, nw.reshape(1, d), wg, wu, wd, fw.reshape(1, d))


ATT_TILE = 256
ATT_GROUP = 4
MOBA_SUBTILES = 4
ATT_CHUNK = 64


def _unit_scores(kb, qb, bias_fn):
    s = jnp.dot(kb, qb, preferred_element_type=F32)
    chunks = [s[r:r + ATT_CHUNK] + bias_fn(r, r + ATT_CHUNK) for r in range(0, s.shape[0], ATT_CHUNK)]
    return chunks, jnp.max(functools.reduce(jnp.maximum, chunks), axis=0, keepdims=True)


ONES_ROWS = 16


def _with_ones(vt):
    return jnp.concatenate([vt, jnp.ones((ONES_ROWS, vt.shape[1]), BF16)], axis=0)


def _unit_values(chunks, m_u, vts, mult_fn):
    ps = []
    for i, c in enumerate(chunks):
        p = jnp.exp((c - m_u).astype(BF16))
        if mult_fn is not None:
            p = p * mult_fn(i * ATT_CHUNK, (i + 1) * ATT_CHUNK).astype(BF16)
        ps.append(p)
    per = len(ps) // len(vts)
    pv = sum(jnp.dot(vt, jnp.concatenate(ps[b * per:(b + 1) * per], axis=0), preferred_element_type=F32)
             for b, vt in enumerate(vts))
    nfeat = pv.shape[0] - ONES_ROWS
    return pv[nfeat:nfeat + 1], pv[:nfeat]


def _attend_unit(kb, q2, bias_fn, vts, mult_fn=None):
    chunks, m_u = _unit_scores(kb, q2, bias_fn)
    return (m_u,) + _unit_values(chunks, m_u, vts, mult_fn)


def _merge(parts):
    m = functools.reduce(jnp.maximum, [p[0] for p in parts])
    ws = [jnp.exp(p[0] - m) for p in parts]
    return m, sum(w * p[1] for w, p in zip(ws, parts)), sum(w * p[2] for w, p in zip(ws, parts))


def _head_queries(q_t, feat):
    return [jnp.where(feat // HEAD_DIM == hh, q_t, 0.0) for hh in range(2)]


def _scaled_bf16(q):
    return (q * ATT_SCALE).astype(BF16)


def _both(x):
    return jnp.concatenate([x, x], axis=1)


def _pick_heads(x, feat):
    nq = x.shape[1] // 2
    return jnp.where(feat // HEAD_DIM == 0, x[:, :nq], x[:, nq:])


def _moba_prompt_step(step, q_ref, kt_ref, vt_ref, o_ref, kb_ref, vb_ref, kmean_ref, bias_ref, causal_ref, *, nblk):
    blk, grp = ATT_TILE, ATT_GROUP
    nsub = MOBA_SUBTILES

    @pl.when(step == 0)
    def _():
        for n in range(nblk):
            kblk = kt_ref[:, n * blk:(n + 1) * blk].T
            kmean_ref[n:n + 1, :] = jnp.sum(kblk, axis=0, keepdims=True) * (1.0 / blk)
            kb_ref[n * blk:(n + 1) * blk, :] = kblk.astype(BF16)
            vb_ref[n] = _with_ones(vt_ref[:, n * blk:(n + 1) * blk].astype(BF16))
        key_i = lax.broadcasted_iota(jnp.int32, (blk, 2 * blk), 0)
        qry_i = lax.broadcasted_iota(jnp.int32, (blk, 2 * blk), 1) % blk
        causal_ref[...] = jnp.where(key_i <= qry_i, 0.0, NEG_INF)

    feat = lax.broadcasted_iota(jnp.int32, (LANES, blk), 0)
    q2s = [jnp.concatenate(_head_queries(q_ref[sub * blk:(sub + 1) * blk, :].T, feat), axis=1) for sub in range(nsub)]
    qb2s = [_scaled_bf16(q2) for q2 in q2s]
    cols = nsub * 2 * blk
    blk_id = lax.broadcasted_iota(jnp.int32, (nblk, cols), 0)
    blk_f = blk_id.astype(F32)
    qt = step * nsub + lax.broadcasted_iota(jnp.int32, (1, cols), 1) // (2 * blk)
    sc = jnp.dot(kmean_ref[...], jnp.concatenate(q2s, axis=1), preferred_element_type=F32,
                 precision=lax.Precision.HIGHEST)
    sc = jnp.where(blk_id < qt, sc, NEG_INF)
    sel = jnp.zeros((nblk, cols), F32)
    for r in range(MOBA_TOPK):
        mx = jnp.max(sc, axis=0, keepdims=True)
        first = jnp.min(jnp.where(sc == mx, blk_f, float(nblk)), axis=0, keepdims=True)
        pick = blk_f == first
        sel = jnp.maximum(sel, jnp.where(pick & (r < qt), 1.0, 0.0))
        sc = jnp.where(pick, -jnp.inf, sc)
    bias = jnp.where((sel > 0.0) | (blk_id == qt), 0.0, NEG_INF)
    for sub in range(nsub):
        bias_ref[sub] = bias[:, sub * 2 * blk:(sub + 1) * 2 * blk]

    def group(sub, g, with_own):
        own_j = (step * nsub) % grp + sub
        nb = grp if not with_own else (sub + 1 if grp == nsub else grp)

        def bias_fn(r0, r1):
            j, r = r0 // blk, r0 % blk
            row = bias_ref[sub, pl.ds(g * grp + j, 1), :]
            if not with_own:
                return row
            return row + jnp.where(j == own_j, causal_ref[r:r + (r1 - r0), :], 0.0)

        kb = kb_ref[pl.ds(pl.multiple_of(g * grp * blk, blk), nb * blk), :]
        return _attend_unit(kb, qb2s[sub], bias_fn, [vb_ref[g * grp + b] for b in range(nb)])

    g_own = step * nsub // grp

    def body(g, carry):
        return sum((_merge([carry[3 * sub:3 * sub + 3], group(sub, g, False)]) for sub in range(nsub)), ())

    carry = lax.fori_loop(0, g_own, body, sum((group(sub, g_own, True) for sub in range(nsub)), ()))
    for sub in range(nsub):
        o_ref[sub * blk:(sub + 1) * blk, :] = _pick_heads(carry[3 * sub + 2] / carry[3 * sub + 1], feat).T


def _dilated_multiplicity(delta):
    c = np.zeros(delta.shape, np.int32)
    for window, dil in DILATED:
        c += ((delta >= 0) & (delta <= window) & (delta % dil == 0)).astype(np.int32)
    return c


DIL_NEAR = 2
DIL_FAR_CLASSES_PER_STEP = 4
assert DILATED[-1][1] % DIL_FAR_CLASSES_PER_STEP == 0
DIL_NEAR_SUBTILES = 8
DIL_FAR_STRIDE = DILATED[-1][1]
DIL_FAR_REACH = DILATED[-1][0] // DIL_FAR_STRIDE
assert DILATED[-2][0] <= ATT_TILE * DIL_NEAR and ATT_TILE % DIL_FAR_STRIDE == 0 and DIL_FAR_REACH <= ATT_TILE


def _mask_bias(valid):
    return np.where(valid, 0.0, NEG_INF).astype(np.float32)


def _dilated_near_tables():
    ki = np.arange(ATT_TILE)[:, None]
    qi = np.arange(ATT_TILE)[None, :]
    c = np.stack([_dilated_multiplicity(ATT_TILE * (DIL_NEAR - i) + qi - ki) for i in range(DIL_NEAR + 1)])
    return _mask_bias(c > 0), c.astype(np.float32)


def _dilated_far_tables():
    per_tile = ATT_TILE // DIL_FAR_STRIDE

    def table(k0, nk, q0):
        ka = k0 + np.arange(nk)[:, None]
        qa = q0 + np.arange(ATT_TILE)[None, :]
        return _mask_bias((qa - ka <= DIL_FAR_REACH) & (ka // per_tile < qa // per_tile - DIL_NEAR))

    return table(0, ATT_TILE, 0), table(ATT_TILE - DIL_FAR_REACH, ATT_TILE + DIL_FAR_REACH, ATT_TILE)


def _dilated_far_kernel(q_ref, k_ref, v_ref, b0_ref, b1_ref, o_ref, lse_ref, *, seq):
    stride, blk, reach = DIL_FAR_STRIDE, ATT_TILE, DIL_FAR_REACH
    ntile = seq // stride // blk
    feat = lax.broadcasted_iota(jnp.int32, (LANES, blk), 0)

    def one_class(cls):
        for tau in range(ntile):
            k0, nk, tab = (0, blk, b0_ref) if tau == 0 else (tau * blk - reach, blk + reach, b1_ref)
            rows_q = pl.ds(cls + stride * blk * tau, blk, stride=stride)
            rows_k = pl.ds(cls + stride * k0, nk, stride=stride)
            qb2 = _scaled_bf16(jnp.concatenate(_head_queries(q_ref[rows_q, :].T, feat), axis=1))
            kb = k_ref[rows_k, :].astype(BF16)
            vt = _with_ones(v_ref[rows_k, :].T.astype(BF16))
            m_u, l_u, pv = _attend_unit(kb, qb2, lambda r0, r1: _both(tab[r0:r1, :]), [vt])
            o_ref[rows_q, :] = _pick_heads(pv / l_u, feat).T
            lse_ref[rows_q, :] = _pick_heads(jnp.broadcast_to(m_u + jnp.log(l_u), pv.shape), feat).T

    def body(i, carry):
        for c in range(DIL_FAR_CLASSES_PER_STEP):
            one_class(i * DIL_FAR_CLASSES_PER_STEP + c)
        return carry

    lax.fori_loop(0, stride // DIL_FAR_CLASSES_PER_STEP, body, 0)


def _dilated_near_kernel(q_ref, k_ref, v_ref, far_ref, lse_ref, bias_ref, mult_ref, o_ref, kwin_ref, vwin_ref,
                         kb_ref, vb_ref, tab_ref, *, nblk):
    blk, near, nsub = ATT_TILE, DIL_NEAR, DIL_NEAR_SUBTILES
    step = pl.program_id(2)
    keep_blks = kwin_ref.shape[1] // blk

    @pl.when(step == 0)
    def _():
        kb_ref[0:near * blk, :] = jnp.zeros((near * blk, LANES), BF16)
        for n in range(near):
            vb_ref[n] = jnp.zeros((LANES + ONES_ROWS, blk), BF16)
        for n in range(nblk):
            kblk = k_ref[n * blk:(n + 1) * blk, :]
            vblk_t = v_ref[n * blk:(n + 1) * blk, :].T
            kb_ref[(n + near) * blk:(n + near + 1) * blk, :] = kblk.astype(BF16)
            vb_ref[n + near] = _with_ones(vblk_t.astype(BF16))
            if n >= nblk - keep_blks:
                w = n - (nblk - keep_blks)
                kwin_ref[:, w * blk:(w + 1) * blk] = kblk.T
                vwin_ref[:, w * blk:(w + 1) * blk] = vblk_t

    @pl.when(step * nsub < near + nsub)
    def _():
        for sub in range(nsub):
            for i in range(near + 1):
                tab_ref[sub, i] = jnp.where(step * nsub + sub - near + i >= 0, bias_ref[i], NEG_INF)

    feat = lax.broadcasted_iota(jnp.int32, (LANES, blk), 0)
    for sub in range(nsub):
        qt = step * nsub + sub
        rows = slice(sub * blk, (sub + 1) * blk)
        qb2 = _scaled_bf16(jnp.concatenate(_head_queries(q_ref[rows, :].T, feat), axis=1))
        m2, l2, a2 = _attend_unit(
            kb_ref[pl.ds(pl.multiple_of(qt * blk, blk), (near + 1) * blk), :], qb2,
            lambda r0, r1: _both(tab_ref[sub, r0 // blk, r0 % blk:r0 % blk + (r1 - r0), :]),
            [vb_ref[qt + i] for i in range(near + 1)],
            lambda r0, r1: _both(mult_ref[r0 // blk, r0 % blk:r0 % blk + (r1 - r0), :]))
        m_n = _pick_heads(jnp.broadcast_to(m2, a2.shape), feat)
        l_n = _pick_heads(jnp.broadcast_to(l2, a2.shape), feat)
        a_n = _pick_heads(a2, feat)
        lse_f = lse_ref[rows, :].T
        m = jnp.maximum(m_n, lse_f)
        w_n, w_f = jnp.exp(m_n - m), jnp.exp(lse_f - m)
        o_ref[rows, :] = ((a_n * w_n + far_ref[rows, :].T * w_f) / (l_n * w_n + w_f)).T


def _dilated_prompt(q, k, v, batch, seq, keep):
    n, width = q.shape
    blk, near, stride = ATT_TILE, DIL_NEAR, DIL_FAR_STRIDE
    assert seq % (blk * stride) == 0
    nblk = seq // blk
    npair = width // LANES
    const = lambda nd: (lambda *_: (0,) * nd)
    full = lambda t: pl.BlockSpec(t.shape, const(t.ndim))

    far_tabs = [jnp.asarray(t) for t in _dilated_far_tables()]
    seq_spec = pl.BlockSpec((seq, LANES), lambda b, p: (b, p))
    o_far, lse_far = pl.pallas_call(
        functools.partial(_dilated_far_kernel, seq=seq),
        grid=(batch, npair),
        in_specs=[seq_spec, seq_spec, seq_spec] + [full(t) for t in far_tabs],
        out_specs=[seq_spec, seq_spec],
        out_shape=[jax.ShapeDtypeStruct((n, width), F32)] * 2,
        compiler_params=_cparams("parallel", "parallel"),
        name="dilated_far",
    )(q, k, v, *far_tabs)

    near_tabs = [jnp.asarray(t) for t in _dilated_near_tables()]
    nsub = DIL_NEAR_SUBTILES
    assert nblk % nsub == 0 and keep % blk == 0 and keep <= seq
    steps = nblk // nsub
    kv_spec = pl.BlockSpec((seq, LANES), lambda b, p, t: (b, p))
    q_spec = pl.BlockSpec((nsub * blk, LANES), lambda b, p, t: (b * steps + t, p))
    win_spec = pl.BlockSpec((None, LANES, keep), lambda b, p, t: (b, p, 0))
    win_shape = jax.ShapeDtypeStruct((batch, width, keep), F32)
    return pl.pallas_call(
        functools.partial(_dilated_near_kernel, nblk=nblk),
        grid=(batch, npair, steps),
        in_specs=[q_spec, kv_spec, kv_spec, q_spec, q_spec] + [full(t) for t in near_tabs],
        out_specs=[q_spec, win_spec, win_spec],
        out_shape=[jax.ShapeDtypeStruct((n, width), F32), win_shape, win_shape],
        scratch_shapes=[pltpu.VMEM(((nblk + near) * blk, LANES), BF16),
                        pltpu.VMEM((nblk + near, LANES + ONES_ROWS, blk), BF16),
                        pltpu.VMEM((nsub, near + 1, blk, blk), F32)],
        compiler_params=_cparams("parallel", "parallel", "arbitrary"),
        name="dilated_near",
    )(q, k, v, o_far, lse_far, *near_tabs)


RET_ROWS = 128


def _retention_tables(n_heads, chunk):
    log_g = jnp.log1p(-jnp.exp2(-5.0 - jnp.arange(n_heads, dtype=F32)))
    i = jnp.arange(chunk, dtype=F32)
    diff = i[:, None] - i[None, :]
    causal = diff >= 0
    decay = jnp.where(causal[None], jnp.exp(jnp.where(causal, diff, 0.0)[None] * log_g[:, None, None]), 0.0)
    q_dec = jnp.exp((i + 1.0)[:, None] * log_g[None, :])
    k_dec = jnp.exp((chunk - 1.0 - i)[:, None] * log_g[None, :])
    c_dec = jnp.exp(chunk * log_g)
    pad = RET_ROWS - chunk
    npair = n_heads // 2
    decay = jnp.pad(decay, ((0, 0), (0, pad), (0, pad))).reshape(npair, 2 * RET_ROWS, RET_ROWS)
    expand = lambda t: jnp.pad(jnp.repeat(t, HEAD_DIM, axis=1), ((0, pad), (0, 0))).reshape(
        RET_ROWS, npair, LANES).transpose(1, 0, 2)
    q_tab, k_tab = expand(q_dec), expand(k_dec)
    same_head = (np.arange(LANES)[:, None] // HEAD_DIM) == (np.arange(LANES)[None, :] // HEAD_DIM)
    c_tab = jnp.where(same_head[None], jnp.repeat(c_dec, HEAD_DIM).reshape(npair, LANES, 1), 0.0)
    return decay.astype(F32), q_tab.astype(F32), k_tab.astype(F32), c_tab.astype(F32)


def _retention_kernel(q_ref, k_ref, v_ref, g_ref, gn_ref, s0_ref, dec_ref, qd_ref, kd_ref, cd_ref,
                      o_ref, sout_ref, st_ref, *, chunk, nchunk):
    c = pl.program_id(2)

    @pl.when(c == 0)
    def _():
        zero = jnp.zeros((HEAD_DIM, HEAD_DIM), F32)
        st_ref[...] = jnp.concatenate([jnp.concatenate([s0_ref[0], zero], axis=1),
                                       jnp.concatenate([zero, s0_ref[1]], axis=1)], axis=0)

    pad = RET_ROWS - chunk
    lane = lax.broadcasted_iota(jnp.int32, (RET_ROWS, LANES), 1)
    head0 = lane < HEAD_DIM
    same_head = (lax.broadcasted_iota(jnp.int32, (LANES, LANES), 0) // HEAD_DIM) == (
        lax.broadcasted_iota(jnp.int32, (LANES, LANES), 1) // HEAD_DIM)

    def rows(ref, r):
        x = ref[pl.ds(r, chunk), :]
        return x if pad == 0 else jnp.concatenate([x, jnp.zeros((pad, LANES), F32)], axis=0)

    def seg_mean(x):
        s0 = jnp.sum(jnp.where(head0, x, 0.0), axis=-1, keepdims=True)
        s1 = jnp.sum(jnp.where(head0, 0.0, x), axis=-1, keepdims=True)
        return jnp.where(head0, s0, s1) * (1.0 / HEAD_DIM)

    def one_chunk(j, state):
        r = j * chunk
        q, k, v, g = rows(q_ref, r), rows(k_ref, r), rows(v_ref, r), rows(g_ref, r)
        kb, vb = k.astype(BF16), v.astype(BF16)
        q2 = jnp.concatenate([jnp.where(head0, q, 0.0), jnp.where(head0, 0.0, q)], axis=0).astype(BF16)
        attn = lax.dot_general(q2, kb, _NT, preferred_element_type=F32) * dec_ref[...]
        inn = jnp.dot(attn.astype(BF16), vb, preferred_element_type=F32)
        inner = jnp.where(head0, inn[:RET_ROWS], inn[RET_ROWS:])
        cross = jnp.dot(q.astype(BF16), state.astype(BF16), preferred_element_type=F32) * qd_ref[...]
        upd = lax.dot_general((k * kd_ref[...]).astype(BF16), vb, _TN, preferred_element_type=F32)
        o = inner + cross
        mu = seg_mean(o)
        var = seg_mean(jnp.square(o - mu))
        y = (o - mu) * lax.rsqrt(var + EPS) * gn_ref[...]
        y = y * (g * jax.nn.sigmoid(g))
        o_ref[r:r + chunk, :] = y[:chunk]
        return state * cd_ref[...] + jnp.where(same_head, upd, 0.0)

    state = st_ref[...]
    for j in range(nchunk):
        state = one_chunk(j, state)
    st_ref[...] = state

    @pl.when(c == pl.num_programs(2) - 1)
    def _():
        sout_ref[0] = st_ref[0:HEAD_DIM, 0:HEAD_DIM]
        sout_ref[1] = st_ref[HEAD_DIM:, HEAD_DIM:]


def _retention(q, k, v, g, gn_w, s0, batch, seq, chunk, nchunk):
    n, width = q.shape
    n_heads = width // HEAD_DIM
    npair = width // LANES
    rows = chunk * nchunk
    steps = seq // rows
    dec, qd, kd, cd = _retention_tables(n_heads, chunk)
    row_spec = pl.BlockSpec((rows, LANES), lambda b, p, c: (b * steps + c, p))
    st_spec = pl.BlockSpec((None, 2, HEAD_DIM, HEAD_DIM), lambda b, p, c: (b, p, 0, 0))
    tab = lambda t: pl.BlockSpec((None,) + t.shape[1:], lambda b, p, c: (p, 0, 0))
    out, s_new = pl.pallas_call(
        functools.partial(_retention_kernel, chunk=chunk, nchunk=nchunk),
        grid=(batch, npair, steps),
        in_specs=[row_spec, row_spec, row_spec, row_spec, pl.BlockSpec((1, LANES), lambda b, p, c: (0, p)),
                  st_spec, tab(dec), tab(qd), tab(kd), tab(cd)],
        out_specs=[row_spec, st_spec],
        out_shape=[jax.ShapeDtypeStruct((n, width), F32),
                   jax.ShapeDtypeStruct((batch, n_heads, HEAD_DIM, HEAD_DIM), F32)],
        scratch_shapes=[pltpu.VMEM((LANES, LANES), F32)],
        compiler_params=_cparams("parallel", "parallel", "arbitrary"),
        name="retention",
    )(q, k, v, g, gn_w.reshape(1, width), s0, dec, qd, kd, cd)
    return out, s_new


def _block_diag_queries(q, n_heads):
    t = q.shape[0]
    rep = jnp.concatenate([q] * n_heads, axis=0)
    rows = lax.broadcasted_iota(jnp.int32, rep.shape, 0) // t
    cols = lax.broadcasted_iota(jnp.int32, rep.shape, 1) // HEAD_DIM
    return jnp.where(rows == cols, rep, 0.0)


def _take_diag(o, n_heads, t):
    cols = lax.broadcasted_iota(jnp.int32, (t, o.shape[1]), 1) // HEAD_DIM
    out = jnp.zeros((t, o.shape[1]), F32)
    for h in range(n_heads):
        out = jnp.where(cols == h, o[h * t:(h + 1) * t, :], out)
    return out


def _fold_pairs(o, n_heads, t):
    rows = lax.broadcasted_iota(jnp.int32, (o.shape[0], LANES), 0) // (2 * t)
    out = jnp.zeros((o.shape[0], LANES), F32)
    for c in range(n_heads // 2):
        out = jnp.where(rows == c, o[:, c * LANES:(c + 1) * LANES], out)
    return out


def _unfold_pairs(a, n_heads, t):
    lane = lax.broadcasted_iota(jnp.int32, (t, LANES), 1)
    return jnp.concatenate([jnp.where(lane < HEAD_DIM, a[2 * c * t:(2 * c + 1) * t], a[(2 * c + 1) * t:(2 * c + 2) * t])
                            for c in range(n_heads // 2)], axis=1)


def _pad_rows(x, rows):
    return jnp.concatenate([x, jnp.zeros((rows - x.shape[0], x.shape[1]), x.dtype)], axis=0)


MOBA_PAGES_PER_STEP = 32


def _moba_decode_step(g, is_last, q_ref, kn_ref, vn_ref, kp, vp, o_ref, qbd_ref, m_ref, l_ref, sc_ref, oblk_ref,
                      *, n_heads, t_new, pages_per_blk, nblk):
    npg = len(kp)
    rows = n_heads * t_new
    lane = lax.broadcasted_iota(jnp.int32, (rows, LANES), 1)
    page = kp[0].shape[1]
    blk_keys = pages_per_blk * page
    step_blks = npg // pages_per_blk

    @pl.when(g == 0)
    def _():
        qbd_ref[...] = _scaled_bf16(_block_diag_queries(q_ref[...], n_heads))
        m_ref[...] = jnp.zeros_like(m_ref)
        l_ref[...] = jnp.zeros_like(l_ref)
        sc_ref[...] = jnp.full_like(sc_ref, -jnp.inf)

    kt_all = jnp.concatenate([kp[i][...].astype(BF16) for i in range(npg)], axis=1)
    s_all = jnp.dot(qbd_ref[...], kt_all, preferred_element_type=F32)
    m_new, l_new, sc_new = m_ref[...], l_ref[...], sc_ref[...]
    for jb in range(step_blks):
        n = g * step_blks + jb
        s = s_all[:, jb * blk_keys:(jb + 1) * blk_keys]
        mb = jnp.max(s, axis=-1, keepdims=True)
        p = jnp.exp(s - mb)
        oblk_ref[n] = _fold_pairs(sum(lax.dot_general(p[:, a * page:(a + 1) * page].astype(BF16),
                                                      vp[jb * pages_per_blk + a][...].astype(BF16),
                                                      _NT, preferred_element_type=F32)
                                      for a in range(pages_per_blk)), n_heads, t_new)
        here = lane == n
        m_new = jnp.where(here, mb, m_new)
        l_new = jnp.where(here, jnp.sum(p, axis=-1, keepdims=True), l_new)
        sc_new = jnp.where(here, jnp.sum(s, axis=-1, keepdims=True) * (1.0 / blk_keys), sc_new)
    m_ref[...], l_ref[...], sc_ref[...] = m_new, l_new, sc_new

    @pl.when(is_last)
    def _():
        s = lax.dot_general(qbd_ref[...], _pad_rows(kn_ref[...], LANES).astype(BF16), _NT,
                            preferred_element_type=F32)
        qry = lax.broadcasted_iota(jnp.int32, (rows, LANES), 0) % t_new
        s = jnp.where(lane <= qry, s, NEG_INF)
        m_own = jnp.max(s, axis=-1, keepdims=True)
        p = jnp.exp(s - m_own)
        l_own = jnp.sum(p, axis=-1, keepdims=True)
        o_own = jnp.dot(p.astype(BF16), _pad_rows(vn_ref[...], LANES).astype(BF16), preferred_element_type=F32)
        sc = sc_ref[...]
        lane_f = lane.astype(F32)
        sel = lane < 0
        for _ in range(MOBA_TOPK):
            mx = jnp.max(sc, axis=-1, keepdims=True)
            first = jnp.min(jnp.where(sc == mx, lane_f, float(LANES)), axis=-1, keepdims=True)
            pick = lane_f == first
            sel = sel | pick
            sc = jnp.where(pick, -jnp.inf, sc)
        m_all = m_ref[...]
        m_fin = jnp.maximum(jnp.max(jnp.where(sel, m_all, -jnp.inf), axis=-1, keepdims=True), m_own)
        w = jnp.where(sel, jnp.exp(m_all - m_fin), 0.0)
        w_own = jnp.exp(m_own - m_fin)
        l_fin = jnp.sum(w * l_ref[...], axis=-1, keepdims=True) + w_own * l_own

        def body(n, acc):
            col = jnp.sum(jnp.where(lane == n, w, 0.0), axis=-1, keepdims=True)
            return acc + col * oblk_ref[n]

        acc = lax.fori_loop(0, nblk, body, w_own * _fold_pairs(o_own, n_heads, t_new))
        o_ref[...] = _unfold_pairs(acc / l_fin, n_heads, t_new)


def _moba_kernel(pt_ref, q_ref, kt_ref, vt_ref, qs_ref, kn_ref, vn_ref, *rest, nblk, prompt_steps, decode_steps,
                 n_heads, t_new, pages_per_blk, dec_nblk):
    npg = MOBA_PAGES_PER_STEP
    kp, vp = rest[:npg], rest[npg:2 * npg]
    o_ref, os_ref, kb_ref, vb_ref, kmean_ref, bias_ref, causal_ref, qbd_ref, m_ref, l_ref, sc_ref, oblk_ref = rest[2 * npg:]
    i = pl.program_id(0)
    _moba_prompt_step(i % prompt_steps, q_ref, kt_ref, vt_ref, o_ref, kb_ref, vb_ref, kmean_ref, bias_ref, causal_ref,
                      nblk=nblk)
    g = i % decode_steps
    _moba_decode_step(g, g == decode_steps - 1, qs_ref, kn_ref, vn_ref, kp, vp, os_ref, qbd_ref, m_ref, l_ref, sc_ref,
                      oblk_ref, n_heads=n_heads, t_new=t_new, pages_per_blk=pages_per_blk, nblk=dec_nblk)


def _moba(q, kt, vt, batch, seq, qs, k_new, v_new, cache_kt, cache_vt, page_table, t_new):
    n, width = q.shape
    blk, grp, nsub = ATT_TILE, ATT_GROUP, MOBA_SUBTILES
    assert blk == MOBA_BLOCK and seq % (blk * grp) == 0 and grp % nsub == 0
    nblk = seq // blk
    steps_p = nblk // nsub
    npair = width // LANES
    n_heads = width // HEAD_DIM
    n_seq, n_pages = page_table.shape
    page = cache_kt.shape[2]
    pages_per_blk = MOBA_BLOCK // page
    dec_nblk = n_pages // pages_per_blk
    npg = MOBA_PAGES_PER_STEP
    steps_d = n_pages // npg
    rows = n_heads * t_new
    assert n_pages % npg == 0 and npg % pages_per_blk == 0 and MOBA_TOPK <= dec_nblk <= LANES
    assert n_pages * page == dec_nblk * MOBA_BLOCK and t_new <= page and page == LANES and rows <= LANES
    assert n_heads % 2 == 0
    total = batch * npair * steps_p
    assert total == n_seq * steps_d, "prompt and decode sides must have the same number of steps"

    kv_spec = pl.BlockSpec((None, LANES, seq), lambda i, pt: (i // (npair * steps_p), (i // steps_p) % npair, 0),
                           pipeline_mode=pl.Buffered(1))
    q_spec = pl.BlockSpec((nsub * blk, LANES),
                          lambda i, pt: ((i // (npair * steps_p)) * steps_p + i % steps_p, (i // steps_p) % npair))
    tok_spec = pl.BlockSpec((t_new, width), lambda i, pt: (i // steps_d, 0))
    page_specs = [pl.BlockSpec((None, width, page), functools.partial(
        lambda i, pt, j: (pt[i // steps_d, (i % steps_d) * npg + j], 0, 0), j=j)) for j in range(npg)]
    grid_spec = pltpu.PrefetchScalarGridSpec(
        num_scalar_prefetch=1,
        grid=(total,),
        in_specs=[q_spec, kv_spec, kv_spec, tok_spec, tok_spec, tok_spec] + page_specs + page_specs,
        out_specs=[q_spec, tok_spec],
        scratch_shapes=[pltpu.VMEM((nblk * blk, LANES), BF16), pltpu.VMEM((nblk, LANES + ONES_ROWS, blk), BF16),
                        pltpu.VMEM((nblk, LANES), F32), pltpu.VMEM((nsub, nblk, 2 * blk), F32),
                        pltpu.VMEM((blk, 2 * blk), F32),
                        pltpu.VMEM((rows, width), BF16), pltpu.VMEM((rows, LANES), F32),
                        pltpu.VMEM((rows, LANES), F32), pltpu.VMEM((rows, LANES), F32),
                        pltpu.VMEM((dec_nblk, rows, LANES), F32)],
    )
    return pl.pallas_call(
        functools.partial(_moba_kernel, nblk=nblk, prompt_steps=steps_p, decode_steps=steps_d, n_heads=n_heads,
                          t_new=t_new, pages_per_blk=pages_per_blk, dec_nblk=dec_nblk),
        grid_spec=grid_spec,
        out_shape=[jax.ShapeDtypeStruct((n, width), F32), jax.ShapeDtypeStruct(qs.shape, F32)],
        compiler_params=_cparams("arbitrary"),
        name="moba",
    )(page_table, q, kt, vt, qs, k_new, v_new, *([cache_kt] * npg), *([cache_vt] * npg))


DIL_DEC_TILE = 512


def _dilated_decode_tables(n_heads, t_new, n_prev):
    qi = (np.arange(n_heads * t_new) % t_new)[:, None]
    c_old = _dilated_multiplicity(n_prev + qi - np.arange(n_prev)[None, :])
    c_new = _dilated_multiplicity(qi - np.arange(LANES)[None, :])
    c_new = np.where(np.arange(LANES)[None, :] < t_new, c_new, 0)
    f = lambda c: (np.where(c > 0, 0.0, NEG_INF).astype(np.float32), c.astype(np.float32))
    return f(c_old) + f(c_new)


def _shifted_window_tile(cur_ref, nxt_ref, new_ref, out_ref, tail_ref, is_last, t_new):
    lane = lax.broadcasted_iota(jnp.int32, (cur_ref.shape[0], LANES), 1)
    nblk = cur_ref.shape[1] // LANES

    @pl.when(is_last)
    def _():
        tail_ref[...] = _pad_rows(new_ref[...], LANES).T

    @pl.when(jnp.logical_not(is_last))
    def _():
        tail_ref[...] = nxt_ref[...]

    rolled = [pltpu.roll(cur_ref[:, c * LANES:(c + 1) * LANES], LANES - t_new, 1) for c in range(nblk)]
    rolled.append(pltpu.roll(tail_ref[...], LANES - t_new, 1))
    for c in range(nblk):
        out_ref[:, c * LANES:(c + 1) * LANES] = jnp.where(lane < LANES - t_new, rolled[c], rolled[c + 1])


def _dilated_decode_kernel(q_ref, kn_ref, vn_ref, kc_ref, vc_ref, kx_ref, vx_ref, bo_ref, mo_ref, bn_ref, mn_ref,
                           o_ref, ko_ref, vo_ref, qbd_ref, m_ref, l_ref, acc_ref, tail_ref, *, n_heads, t_new):
    kt = pl.program_id(1)
    is_last = kt == pl.num_programs(1) - 1
    _shifted_window_tile(kc_ref, kx_ref, kn_ref, ko_ref, tail_ref, is_last, t_new)
    _shifted_window_tile(vc_ref, vx_ref, vn_ref, vo_ref, tail_ref, is_last, t_new)

    @pl.when(kt == 0)
    def _():
        qbd_ref[...] = _scaled_bf16(_block_diag_queries(q_ref[...], n_heads))
        m_ref[...] = jnp.full_like(m_ref, -jnp.inf)
        l_ref[...] = jnp.zeros_like(l_ref)
        acc_ref[...] = jnp.zeros_like(acc_ref)

    def step(s, pv, bias, mult):
        s = s + bias
        m_old = m_ref[...]
        m_new = jnp.maximum(m_old, jnp.max(s, axis=-1, keepdims=True))
        alpha = jnp.exp(m_old - m_new)
        p = jnp.exp(s - m_new) * mult
        l_ref[...] = alpha * l_ref[...] + jnp.sum(p, axis=-1, keepdims=True)
        acc_ref[...] = alpha * acc_ref[...] + pv(p.astype(BF16))
        m_ref[...] = m_new

    @pl.when(kt == 0)
    def _():
        k, v = _pad_rows(kn_ref[...], LANES).astype(BF16), _pad_rows(vn_ref[...], LANES).astype(BF16)
        step(lax.dot_general(qbd_ref[...], k, _NT, preferred_element_type=F32),
             lambda p: jnp.dot(p, v, preferred_element_type=F32), bn_ref[...], mn_ref[...])

    step(jnp.dot(qbd_ref[...], kc_ref[...].astype(BF16), preferred_element_type=F32),
         lambda p: lax.dot_general(p, vc_ref[...].astype(BF16), _NT, preferred_element_type=F32),
         bo_ref[...], mo_ref[...])

    @pl.when(kt == pl.num_programs(1) - 1)
    def _():
        o_ref[...] = _take_diag(acc_ref[...] / l_ref[...], n_heads, t_new)


def _dilated_decode(q, k_new, v_new, win_kt, win_vt, t_new):
    n, width = q.shape
    n_heads = width // HEAD_DIM
    n_seq, _, n_prev = win_kt.shape
    tile = DIL_DEC_TILE
    assert n_prev == W_MAX and n_prev % tile == 0 and t_new <= LANES
    rows = n_heads * t_new
    bo, mo, bn, mn = _dilated_decode_tables(n_heads, t_new, n_prev)
    tok_spec = pl.BlockSpec((t_new, width), lambda b, t: (b, 0))
    win_spec = pl.BlockSpec((None, width, tile), lambda b, t: (b, 0, t))
    last_lane_blk = n_prev // LANES - 1
    nxt_spec = pl.BlockSpec((None, width, LANES),
                            lambda b, t: (b, 0, jnp.minimum((t + 1) * (tile // LANES), last_lane_blk)))
    old_tab = pl.BlockSpec((rows, tile), lambda b, t: (0, t))
    new_tab = pl.BlockSpec((rows, LANES), lambda b, t: (0, 0))
    win_shape = jax.ShapeDtypeStruct(win_kt.shape, F32)
    return pl.pallas_call(
        functools.partial(_dilated_decode_kernel, n_heads=n_heads, t_new=t_new),
        grid=(n_seq, n_prev // tile),
        in_specs=[tok_spec, tok_spec, tok_spec, win_spec, win_spec, nxt_spec, nxt_spec,
                  old_tab, old_tab, new_tab, new_tab],
        out_specs=[tok_spec, win_spec, win_spec],
        out_shape=[jax.ShapeDtypeStruct((n, width), F32), win_shape, win_shape],
        scratch_shapes=[pltpu.VMEM((rows, width), BF16), pltpu.VMEM((rows, 1), F32),
                        pltpu.VMEM((rows, 1), F32), pltpu.VMEM((rows, width), F32),
                        pltpu.VMEM((width, LANES), F32)],
        compiler_params=_cparams("parallel", "arbitrary"),
        name="dilated_decode",
    )(q, k_new, v_new, win_kt, win_vt, win_kt, win_vt,
      jnp.asarray(bo), jnp.asarray(mo), jnp.asarray(bn), jnp.asarray(mn))


_AB_SEGS = (("a", (0,), 0, 1.0), ("a", (1,), 0, 1.0), (None, (2,), 0, 1.0), ("b", (3,), 0, 1.0),
            ("b", (4,), 0, ATT_SCALE), (None, (5,), 0, 1.0), (None, (6,), 0, 1.0))
_C_SEGS = tuple(("a" if o < 2 else None, (o,), c, 1.0) for o in range(3) for c in (0, SEG))
_KV_OUT = (1, 2)


def _row_tile(n):
    return 512 if n % 512 == 0 else n


def _feature_major(x):
    lead = x.shape[:-3]
    t, h, dh = x.shape[-3:]
    nl = len(lead)
    return x.transpose(*range(nl), nl + 1, nl + 2, nl).reshape(*lead, h * dh, t)


def _token_major(xt, n_heads):
    lead = xt.shape[:-2]
    t = xt.shape[-1]
    nl = len(lead)
    return xt.reshape(*lead, n_heads, HEAD_DIM, t).transpose(*range(nl), nl + 2, nl, nl + 1)


def kernel(x_prompt, x_sample, cache_k_a, cache_v_a, page_table, state_ret, cache_win_k, cache_win_v,
           norm_mix, norm_ffn, norm_final, w_in_ab, w_out_ab, ret_gn_w, w_in_c, w_out_c,
           ffn_w_gate, ffn_w_up, ffn_w_down):
    bp, tp, d = x_prompt.shape
    bs, ts, _ = x_sample.shape
    page = cache_k_a.shape[2]
    past_len = page_table.shape[1] * page
    h_a = cache_k_a.shape[3]
    wa = h_a * HEAD_DIM
    wb = w_in_ab.shape[2] - 3 * wa
    assert wb == 4 * wa and w_in_ab.shape[0] == 1 and w_in_c.shape[0] == 1 and norm_mix.shape[0] == 2
    h_b = wa // HEAD_DIM
    h_c = w_in_c.shape[2] // (3 * HEAD_DIM)

    pos_p = jnp.arange(tp, dtype=jnp.int32)
    pos_s = past_len + jnp.arange(ts, dtype=jnp.int32)
    tile_s = lambda tabs: tuple(jnp.tile(t, (bs, 1)) for t in tabs)
    tabs = {
        "p": (_rope_tables(pos_p, ROT_DIM, ROPE_THETA), _rope_tables(pos_p, HEAD_DIM, RET_THETA)),
        "s": (tile_s(_rope_tables(pos_s, ROT_DIM, ROPE_THETA)), tile_s(_rope_tables(pos_s, HEAD_DIM, RET_THETA))),
    }
    bf = lambda w: w.astype(BF16)
    w_in_ab_bf, w_out_ab_bf, w_in_c_bf, w_out_c_bf = bf(w_in_ab[0]), bf(w_out_ab[0]), bf(w_in_c[0]), bf(w_out_c[0])
    wg, wu, wd = bf(ffn_w_gate), bf(ffn_w_up), bf(ffn_w_down)

    xp = x_prompt.reshape(bp * tp, d)
    xs = x_sample.reshape(bs * ts, d)
    tm_p, tm_s = _row_tile(bp * tp), _row_tile(bs * ts)
    assert tp % tm_p == 0

    ab_widths = (wa,) * 7
    qa_p, kat_p, vat_p, qb_p, kb_p, vb_p, gb_p = _project(
        xp, norm_mix[0], w_in_ab_bf, *tabs["p"], _AB_SEGS, ab_widths, tm_p, _KV_OUT)
    qa_s, ka_s, va_s, qb_s, kb_s, vb_s, gb_s = _project(
        xs, norm_mix[0], w_in_ab_bf, *tabs["s"], _AB_SEGS, ab_widths, tm_s)

    oa_p, oa_s = _moba(qa_p, kat_p, vat_p, bp, tp, qa_s, ka_s, va_s, _feature_major(cache_k_a[0]),
                       _feature_major(cache_v_a[0]), page_table, ts)

    chunk_p = math.gcd(tp, RET_CHUNK)
    ob_p, ret_p = _retention(qb_p, kb_p, vb_p, gb_p, ret_gn_w[0],
                             jnp.zeros((bp, h_b, HEAD_DIM, HEAD_DIM), F32), bp, tp, chunk_p,
                             math.gcd(tp // chunk_p, 8))
    ob_s, ret_s = _retention(qb_s, kb_s, vb_s, gb_s, ret_gn_w[0], state_ret[0], bs, ts,
                             math.gcd(ts, RET_CHUNK), 1)

    xp = _mix_ffn([oa_p, ob_p], w_out_ab_bf, xp, norm_ffn[0], wg[0], wu[0], wd[0], norm_final, False, tm_p)
    xs = _mix_ffn([oa_s, ob_s], w_out_ab_bf, xs, norm_ffn[0], wg[0], wu[0], wd[0], norm_final, False, tm_s)

    wc = h_c * HEAD_DIM
    q_p, k_p, v_p = _project(xp, norm_mix[1], w_in_c_bf, *tabs["p"], _C_SEGS, (wc,) * 3, tm_p)
    q_s, k_s, v_s = _project(xs, norm_mix[1], w_in_c_bf, *tabs["s"], _C_SEGS, (wc,) * 3, tm_s)
    n_prev = cache_win_k.shape[2]
    o_p, kwin_p, vwin_p = _dilated_prompt(q_p, k_p, v_p, bp, tp, min(W_MAX, tp))
    o_s, win_kt, win_vt = _dilated_decode(q_s, k_s, v_s, _feature_major(cache_win_k[0]),
                                          _feature_major(cache_win_v[0]), ts)
    y_p = _mix_ffn([o_p], w_out_c_bf, xp, norm_ffn[1], wg[1], wu[1], wd[1], norm_final, True, tm_p)
    y_s = _mix_ffn([o_s], w_out_c_bf, xs, norm_ffn[1], wg[1], wu[1], wd[1], norm_final, True, tm_s)

    return (
        y_p.reshape(bp, tp, d), y_s.reshape(bs, ts, d),
        _token_major(kat_p, h_a)[None], _token_major(vat_p, h_a)[None], ret_p[None],
        _token_major(kwin_p, h_c)[None], _token_major(vwin_p, h_c)[None],
        ka_s.reshape(1, bs, ts, h_a, HEAD_DIM), va_s.reshape(1, bs, ts, h_a, HEAD_DIM), ret_s[None],
        _token_major(win_kt, h_c)[None], _token_major(win_vt, h_c)[None],
    )
```

```python
import functools
import math

import jax
import jax.numpy as jnp
import numpy as np
from jax import lax
from jax.experimental import pallas as pl
from jax.experimental.pallas import tpu as pltpu

F32 = jnp.float32
BF16 = jnp.bfloat16

HEAD_DIM = 64
LANES = 128
ROT_DIM = HEAD_DIM // 4
ROPE_THETA = 500000.0
RET_THETA = 10000.0
MOBA_BLOCK = 256
MOBA_TOPK = 3
RET_CHUNK = 128
DILATED = ((128, 1), (512, 4), (2048, 16))
W_MAX = max(w for w, _ in DILATED)
EPS = 1e-6
NEG_INF = -1e30
ATT_SCALE = HEAD_DIM ** -0.5
VMEM_LIMIT = 56 * 1024 * 1024

_NT = (((1,), (1,)), ((), ()))
_TN = (((0,), (0,)), ((), ()))


def _cparams(*sem):
    return pltpu.CompilerParams(dimension_semantics=sem, vmem_limit_bytes=VMEM_LIMIT)


def _rms(x, w):
    ms = jnp.mean(x * x, axis=-1, keepdims=True)
    return x * lax.rsqrt(ms + EPS) * w


SEG = 512


def _proj_kernel(x_ref, nw_ref, w_ref, ca_ref, na_ref, pa_ref, cb_ref, nb_ref, pb_ref, *out_refs, segs, transposed):
    xn = _rms(x_ref[...], nw_ref[...]).astype(BF16)
    for s, (kind, dests, col, post) in enumerate(segs):
        acc = jnp.dot(xn, w_ref[:, s * SEG:(s + 1) * SEG], preferred_element_type=F32)
        for c in range(SEG // LANES):
            r = acc[:, c * LANES:(c + 1) * LANES]
            if kind is not None:
                c_ref, n_ref, p_ref, shift = (ca_ref, na_ref, pa_ref, ROT_DIM // 2) if kind == "a" else (
                    cb_ref, nb_ref, pb_ref, HEAD_DIM // 2)
                r = (r * c_ref[...] + pltpu.roll(r, LANES - shift, 1) * n_ref[...]
                     + pltpu.roll(r, shift, 1) * p_ref[...])
                if post != 1.0:
                    r = r * post
            lo = col + c * LANES
            for oi in dests:
                if oi in transposed:
                    out_refs[oi][lo:lo + LANES, :] = r.T
                else:
                    out_refs[oi][:, lo:lo + LANES] = r


def _rope_tables(pos, rot_dim, theta):
    half = rot_dim // 2
    inv = theta ** (-jnp.arange(half, dtype=F32) / half)
    ang = pos.astype(F32)[:, None] * inv[None, :]
    cos, sin = jnp.cos(ang), jnp.sin(ang)
    lane = np.arange(LANES) % HEAD_DIM
    idx = lane % half
    cos_t = jnp.where(lane < rot_dim, cos[:, idx], 1.0)
    neg_t = jnp.where(lane < half, -sin[:, idx], 0.0)
    pos_t = jnp.where((lane >= half) & (lane < rot_dim), sin[:, idx], 0.0)
    return cos_t.astype(F32), neg_t.astype(F32), pos_t.astype(F32)


def _project(x, nw, w_bf, tabs_a, tabs_b, segs, out_widths, tm, transposed=()):
    n, d = x.shape
    t_tab = tabs_a[0].shape[0]
    nt = t_tab // tm
    tab_spec = pl.BlockSpec((tm, LANES), lambda i: (i % nt, 0))
    out_specs, out_shape = [], []
    for oi, w in enumerate(out_widths):
        if oi in transposed:
            out_specs.append(pl.BlockSpec((None, w, tm), lambda i: (i // nt, 0, i % nt)))
            out_shape.append(jax.ShapeDtypeStruct((n // t_tab, w, t_tab), F32))
        else:
            out_specs.append(pl.BlockSpec((tm, w), lambda i: (i, 0)))
            out_shape.append(jax.ShapeDtypeStruct((n, w), F32))
    return pl.pallas_call(
        functools.partial(_proj_kernel, segs=segs, transposed=tuple(transposed)),
        grid=(n // tm,),
        in_specs=[pl.BlockSpec((tm, d), lambda i: (i, 0)),
                  pl.BlockSpec((1, d), lambda i: (0, 0)),
                  pl.BlockSpec(w_bf.shape, lambda i: (0, 0))] + [tab_spec] * 6,
        out_specs=out_specs,
        out_shape=out_shape,
        compiler_params=_cparams("parallel"),
        name="rms_proj_rope",
    )(x, nw.reshape(1, d), w_bf, *tabs_a, *tabs_b)


FFN_CHUNKS = 11


def _mix_ffn_kernel(*refs, nparts, final):
    parts = refs[:nparts]
    wo_ref, x_ref, nw_ref, wg_ref, wu_ref, wd_ref, fw_ref, o_ref = refs[nparts:]
    a = jnp.concatenate([p[...].astype(BF16) for p in parts], axis=1) if nparts > 1 else parts[0][...].astype(BF16)
    x = x_ref[...] + jnp.dot(a, wo_ref[...], preferred_element_type=F32)
    xn = _rms(x, nw_ref[...]).astype(BF16)
    tf = wg_ref.shape[1] // FFN_CHUNKS
    acc = x
    for c in range(FFN_CHUNKS):
        g = jnp.dot(xn, wg_ref[:, c * tf:(c + 1) * tf], preferred_element_type=F32)
        u = jnp.dot(xn, wu_ref[:, c * tf:(c + 1) * tf], preferred_element_type=F32)
        h = (g * jax.nn.sigmoid(g) * u).astype(BF16)
        acc = acc + jnp.dot(h, wd_ref[c * tf:(c + 1) * tf, :], preferred_element_type=F32)
    if final:
        acc = _rms(acc, fw_ref[...])
    o_ref[...] = acc


def _mix_ffn(parts, wo, x, nw, wg, wu, wd, fw, final, tm):
    n, d = x.shape
    dff = wg.shape[1]
    const = lambda i: (0, 0)
    row = lambda w: pl.BlockSpec((tm, w), lambda i: (i, 0))
    return pl.pallas_call(
        functools.partial(_mix_ffn_kernel, nparts=len(parts), final=final),
        grid=(n // tm,),
        in_specs=[row(p.shape[1]) for p in parts]
        + [pl.BlockSpec(wo.shape, const), row(d), pl.BlockSpec((1, d), const),
           pl.BlockSpec((d, dff), const), pl.BlockSpec((d, dff), const), pl.BlockSpec((dff, d), const),
           pl.BlockSpec((1, d), const)],
        out_specs=row(d),
        out_shape=jax.ShapeDtypeStruct((n, d), F32),
        compiler_params=_cparams("parallel"),
        name="outproj_swiglu",
    )(*parts, wo, x, nw.reshape(1, d), wg, wu, wd, fw.reshape(1, d))


ATT_TILE = 256
ATT_GROUP = 4
MOBA_SUBTILES = 4
ATT_CHUNK = 64


def _unit_scores(kb, qb, bias_fn):
    s = jnp.dot(kb, qb, preferred_element_type=F32)
    chunks = [s[r:r + ATT_CHUNK] + bias_fn(r, r + ATT_CHUNK) for r in range(0, s.shape[0], ATT_CHUNK)]
    return chunks, jnp.max(functools.reduce(jnp.maximum, chunks), axis=0, keepdims=True)


ONES_ROWS = 16


def _with_ones(vt):
    return jnp.concatenate([vt, jnp.ones((ONES_ROWS, vt.shape[1]), BF16)], axis=0)


def _unit_values(chunks, m_u, vts, mult_fn):
    ps = []
    for i, c in enumerate(chunks):
        p = jnp.exp((c - m_u).astype(BF16))
        if mult_fn is not None:
            p = p * mult_fn(i * ATT_CHUNK, (i + 1) * ATT_CHUNK).astype(BF16)
        ps.append(p)
    per = len(ps) // len(vts)
    pv = sum(jnp.dot(vt, jnp.concatenate(ps[b * per:(b + 1) * per], axis=0), preferred_element_type=F32)
             for b, vt in enumerate(vts))
    nfeat = pv.shape[0] - ONES_ROWS
    return pv[nfeat:nfeat + 1], pv[:nfeat]


def _attend_unit(kb, q2, bias_fn, vts, mult_fn=None):
    chunks, m_u = _unit_scores(kb, q2, bias_fn)
    return (m_u,) + _unit_values(chunks, m_u, vts, mult_fn)


def _merge(parts):
    m = functools.reduce(jnp.maximum, [p[0] for p in parts])
    ws = [jnp.exp(p[0] - m) for p in parts]
    return m, sum(w * p[1] for w, p in zip(ws, parts)), sum(w * p[2] for w, p in zip(ws, parts))


def _head_queries(q_t, feat):
    return [jnp.where(feat // HEAD_DIM == hh, q_t, 0.0) for hh in range(2)]


def _scaled_bf16(q):
    return (q * ATT_SCALE).astype(BF16)


def _both(x):
    return jnp.concatenate([x, x], axis=1)


def _pick_heads(x, feat):
    nq = x.shape[1] // 2
    return jnp.where(feat // HEAD_DIM == 0, x[:, :nq], x[:, nq:])


def _moba_prompt_step(step, q_ref, kt_ref, vt_ref, o_ref, kb_ref, vb_ref, kmean_ref, bias_ref, causal_ref, *, nblk):
    blk, grp = ATT_TILE, ATT_GROUP
    nsub = MOBA_SUBTILES

    @pl.when(step == 0)
    def _():
        for n in range(nblk):
            kblk = kt_ref[:, n * blk:(n + 1) * blk].T
            kmean_ref[n:n + 1, :] = jnp.sum(kblk, axis=0, keepdims=True) * (1.0 / blk)
            kb_ref[n * blk:(n + 1) * blk, :] = kblk.astype(BF16)
            vb_ref[n] = _with_ones(vt_ref[:, n * blk:(n + 1) * blk].astype(BF16))
        key_i = lax.broadcasted_iota(jnp.int32, (blk, 2 * blk), 0)
        qry_i = lax.broadcasted_iota(jnp.int32, (blk, 2 * blk), 1) % blk
        causal_ref[...] = jnp.where(key_i <= qry_i, 0.0, NEG_INF)

    feat = lax.broadcasted_iota(jnp.int32, (LANES, blk), 0)
    q2s = [jnp.concatenate(_head_queries(q_ref[sub * blk:(sub + 1) * blk, :].T, feat), axis=1) for sub in range(nsub)]
    qb2s = [_scaled_bf16(q2) for q2 in q2s]
    cols = nsub * 2 * blk
    blk_id = lax.broadcasted_iota(jnp.int32, (nblk, cols), 0)
    blk_f = blk_id.astype(F32)
    qt = step * nsub + lax.broadcasted_iota(jnp.int32, (1, cols), 1) // (2 * blk)
    sc = jnp.dot(kmean_ref[...], jnp.concatenate(q2s, axis=1), preferred_element_type=F32,
                 precision=lax.Precision.HIGHEST)
    sc = jnp.where(blk_id < qt, sc, NEG_INF)
    sel = jnp.zeros((nblk, cols), F32)
    for r in range(MOBA_TOPK):
        mx = jnp.max(sc, axis=0, keepdims=True)
        first = jnp.min(jnp.where(sc == mx, blk_f, float(nblk)), axis=0, keepdims=True)
        pick = blk_f == first
        sel = jnp.maximum(sel, jnp.where(pick & (r < qt), 1.0, 0.0))
        sc = jnp.where(pick, -jnp.inf, sc)
    bias = jnp.where((sel > 0.0) | (blk_id == qt), 0.0, NEG_INF)
    for sub in range(nsub):
        bias_ref[sub] = bias[:, sub * 2 * blk:(sub + 1) * 2 * blk]

    def group(sub, g, with_own):
        own_j = (step * nsub) % grp + sub
        nb = grp if not with_own else (sub + 1 if grp == nsub else grp)

        def bias_fn(r0, r1):
            j, r = r0 // blk, r0 % blk
            row = bias_ref[sub, pl.ds(g * grp + j, 1), :]
            if not with_own:
                return row
            return row + jnp.where(j == own_j, causal_ref[r:r + (r1 - r0), :], 0.0)

        kb = kb_ref[pl.ds(pl.multiple_of(g * grp * blk, blk), nb * blk), :]
        return _attend_unit(kb, qb2s[sub], bias_fn, [vb_ref[g * grp + b] for b in range(nb)])

    g_own = step * nsub // grp

    def body(g, carry):
        return sum((_merge([carry[3 * sub:3 * sub + 3], group(sub, g, False)]) for sub in range(nsub)), ())

    carry = lax.fori_loop(0, g_own, body, sum((group(sub, g_own, True) for sub in range(nsub)), ()))
    for sub in range(nsub):
        o_ref[sub * blk:(sub + 1) * blk, :] = _pick_heads(carry[3 * sub + 2] / carry[3 * sub + 1], feat).T


def _dilated_multiplicity(delta):
    c = np.zeros(delta.shape, np.int32)
    for window, dil in DILATED:
        c += ((delta >= 0) & (delta <= window) & (delta % dil == 0)).astype(np.int32)
    return c


DIL_NEAR = 2
DIL_FAR_CLASSES_PER_STEP = 4
assert DILATED[-1][1] % DIL_FAR_CLASSES_PER_STEP == 0
DIL_NEAR_SUBTILES = 8
DIL_FAR_STRIDE = DILATED[-1][1]
DIL_FAR_REACH = DILATED[-1][0] // DIL_FAR_STRIDE
assert DILATED[-2][0] <= ATT_TILE * DIL_NEAR and ATT_TILE % DIL_FAR_STRIDE == 0 and DIL_FAR_REACH <= ATT_TILE


def _mask_bias(valid):
    return np.where(valid, 0.0, NEG_INF).astype(np.float32)


def _dilated_near_tables():
    ki = np.arange(ATT_TILE)[:, None]
    qi = np.arange(ATT_TILE)[None, :]
    c = np.stack([_dilated_multiplicity(ATT_TILE * (DIL_NEAR - i) + qi - ki) for i in range(DIL_NEAR + 1)])
    return _mask_bias(c > 0), c.astype(np.float32)


def _dilated_far_tables():
    per_tile = ATT_TILE // DIL_FAR_STRIDE

    def table(k0, nk, q0):
        ka = k0 + np.arange(nk)[:, None]
        qa = q0 + np.arange(ATT_TILE)[None, :]
        return _mask_bias((qa - ka <= DIL_FAR_REACH) & (ka // per_tile < qa // per_tile - DIL_NEAR))

    return table(0, ATT_TILE, 0), table(ATT_TILE - DIL_FAR_REACH, ATT_TILE + DIL_FAR_REACH, ATT_TILE)


def _dilated_far_kernel(q_ref, k_ref, v_ref, b0_ref, b1_ref, o_ref, lse_ref, *, seq):
    stride, blk, reach = DIL_FAR_STRIDE, ATT_TILE, DIL_FAR_REACH
    ntile = seq // stride // blk
    feat = lax.broadcasted_iota(jnp.int32, (LANES, blk), 0)

    def one_class(cls):
        for tau in range(ntile):
            k0, nk, tab = (0, blk, b0_ref) if tau == 0 else (tau * blk - reach, blk + reach, b1_ref)
            rows_q = pl.ds(cls + stride * blk * tau, blk, stride=stride)
            rows_k = pl.ds(cls + stride * k0, nk, stride=stride)
            qb2 = _scaled_bf16(jnp.concatenate(_head_queries(q_ref[rows_q, :].T, feat), axis=1))
            kb = k_ref[rows_k, :].astype(BF16)
            vt = _with_ones(v_ref[rows_k, :].T.astype(BF16))
            m_u, l_u, pv = _attend_unit(kb, qb2, lambda r0, r1: _both(tab[r0:r1, :]), [vt])
            o_ref[rows_q, :] = _pick_heads(pv / l_u, feat).T
            lse_ref[rows_q, :] = _pick_heads(jnp.broadcast_to(m_u + jnp.log(l_u), pv.shape), feat).T

    def body(i, carry):
        for c in range(DIL_FAR_CLASSES_PER_STEP):
            one_class(i * DIL_FAR_CLASSES_PER_STEP + c)
        return carry

    lax.fori_loop(0, stride // DIL_FAR_CLASSES_PER_STEP, body, 0)


def _dilated_near_kernel(q_ref, k_ref, v_ref, far_ref, lse_ref, bias_ref, mult_ref, o_ref, kwin_ref, vwin_ref,
                         kb_ref, vb_ref, tab_ref, *, nblk):
    blk, near, nsub = ATT_TILE, DIL_NEAR, DIL_NEAR_SUBTILES
    step = pl.program_id(2)
    keep_blks = kwin_ref.shape[1] // blk

    @pl.when(step == 0)
    def _():
        kb_ref[0:near * blk, :] = jnp.zeros((near * blk, LANES), BF16)
        for n in range(near):
            vb_ref[n] = jnp.zeros((LANES + ONES_ROWS, blk), BF16)
        for n in range(nblk):
            kblk = k_ref[n * blk:(n + 1) * blk, :]
            vblk_t = v_ref[n * blk:(n + 1) * blk, :].T
            kb_ref[(n + near) * blk:(n + near + 1) * blk, :] = kblk.astype(BF16)
            vb_ref[n + near] = _with_ones(vblk_t.astype(BF16))
            if n >= nblk - keep_blks:
                w = n - (nblk - keep_blks)
                kwin_ref[:, w * blk:(w + 1) * blk] = kblk.T
                vwin_ref[:, w * blk:(w + 1) * blk] = vblk_t

    @pl.when(step * nsub < near + nsub)
    def _():
        for sub in range(nsub):
            for i in range(near + 1):
                tab_ref[sub, i] = jnp.where(step * nsub + sub - near + i >= 0, bias_ref[i], NEG_INF)

    feat = lax.broadcasted_iota(jnp.int32, (LANES, blk), 0)
    for sub in range(nsub):
        qt = step * nsub + sub
        rows = slice(sub * blk, (sub + 1) * blk)
        qb2 = _scaled_bf16(jnp.concatenate(_head_queries(q_ref[rows, :].T, feat), axis=1))
        m2, l2, a2 = _attend_unit(
            kb_ref[pl.ds(pl.multiple_of(qt * blk, blk), (near + 1) * blk), :], qb2,
            lambda r0, r1: _both(tab_ref[sub, r0 // blk, r0 % blk:r0 % blk + (r1 - r0), :]),
            [vb_ref[qt + i] for i in range(near + 1)],
            lambda r0, r1: _both(mult_ref[r0 // blk, r0 % blk:r0 % blk + (r1 - r0), :]))
        m_n = _pick_heads(jnp.broadcast_to(m2, a2.shape), feat)
        l_n = _pick_heads(jnp.broadcast_to(l2, a2.shape), feat)
        a_n = _pick_heads(a2, feat)
        lse_f = lse_ref[rows, :].T
        m = jnp.maximum(m_n, lse_f)
        w_n, w_f = jnp.exp(m_n - m), jnp.exp(lse_f - m)
        o_ref[rows, :] = ((a_n * w_n + far_ref[rows, :].T * w_f) / (l_n * w_n + w_f)).T


def _dilated_prompt(q, k, v, batch, seq, keep):
    n, width = q.shape
    blk, near, stride = ATT_TILE, DIL_NEAR, DIL_FAR_STRIDE
    assert seq % (blk * stride) == 0
    nblk = seq // blk
    npair = width // LANES
    const = lambda nd: (lambda *_: (0,) * nd)
    full = lambda t: pl.BlockSpec(t.shape, const(t.ndim))

    far_tabs = [jnp.asarray(t) for t in _dilated_far_tables()]
    seq_spec = pl.BlockSpec((seq, LANES), lambda b, p: (b, p))
    o_far, lse_far = pl.pallas_call(
        functools.partial(_dilated_far_kernel, seq=seq),
        grid=(batch, npair),
        in_specs=[seq_spec, seq_spec, seq_spec] + [full(t) for t in far_tabs],
        out_specs=[seq_spec, seq_spec],
        out_shape=[jax.ShapeDtypeStruct((n, width), F32)] * 2,
        compiler_params=_cparams("parallel", "parallel"),
        name="dilated_far",
    )(q, k, v, *far_tabs)

    near_tabs = [jnp.asarray(t) for t in _dilated_near_tables()]
    nsub = DIL_NEAR_SUBTILES
    assert nblk % nsub == 0 and keep % blk == 0 and keep <= seq
    steps = nblk // nsub
    kv_spec = pl.BlockSpec((seq, LANES), lambda b, p, t: (b, p))
    q_spec = pl.BlockSpec((nsub * blk, LANES), lambda b, p, t: (b * steps + t, p))
    win_spec = pl.BlockSpec((None, LANES, keep), lambda b, p, t: (b, p, 0))
    win_shape = jax.ShapeDtypeStruct((batch, width, keep), F32)
    return pl.pallas_call(
        functools.partial(_dilated_near_kernel, nblk=nblk),
        grid=(batch, npair, steps),
        in_specs=[q_spec, kv_spec, kv_spec, q_spec, q_spec] + [full(t) for t in near_tabs],
        out_specs=[q_spec, win_spec, win_spec],
        out_shape=[jax.ShapeDtypeStruct((n, width), F32), win_shape, win_shape],
        scratch_shapes=[pltpu.VMEM(((nblk + near) * blk, LANES), BF16),
                        pltpu.VMEM((nblk + near, LANES + ONES_ROWS, blk), BF16),
                        pltpu.VMEM((nsub, near + 1, blk, blk), F32)],
        compiler_params=_cparams("parallel", "parallel", "arbitrary"),
        name="dilated_near",
    )(q, k, v, o_far, lse_far, *near_tabs)


RET_ROWS = 128


def _retention_tables(n_heads, chunk):
    log_g = jnp.log1p(-jnp.exp2(-5.0 - jnp.arange(n_heads, dtype=F32)))
    i = jnp.arange(chunk, dtype=F32)
    diff = i[:, None] - i[None, :]
    causal = diff >= 0
    decay = jnp.where(causal[None], jnp.exp(jnp.where(causal, diff, 0.0)[None] * log_g[:, None, None]), 0.0)
    q_dec = jnp.exp((i + 1.0)[:, None] * log_g[None, :])
    k_dec = jnp.exp((chunk - 1.0 - i)[:, None] * log_g[None, :])
    c_dec = jnp.exp(chunk * log_g)
    pad = RET_ROWS - chunk
    npair = n_heads // 2
    decay = jnp.pad(decay, ((0, 0), (0, pad), (0, pad))).reshape(npair, 2 * RET_ROWS, RET_ROWS)
    expand = lambda t: jnp.pad(jnp.repeat(t, HEAD_DIM, axis=1), ((0, pad), (0, 0))).reshape(
        RET_ROWS, npair, LANES).transpose(1, 0, 2)
    q_tab, k_tab = expand(q_dec), expand(k_dec)
    same_head = (np.arange(LANES)[:, None] // HEAD_DIM) == (np.arange(LANES)[None, :] // HEAD_DIM)
    c_tab = jnp.where(same_head[None], jnp.repeat(c_dec, HEAD_DIM).reshape(npair, LANES, 1), 0.0)
    return decay.astype(F32), q_tab.astype(F32), k_tab.astype(F32), c_tab.astype(F32)


def _retention_kernel(q_ref, k_ref, v_ref, g_ref, gn_ref, s0_ref, dec_ref, qd_ref, kd_ref, cd_ref,
                      o_ref, sout_ref, st_ref, *, chunk, nchunk):
    c = pl.program_id(2)

    @pl.when(c == 0)
    def _():
        zero = jnp.zeros((HEAD_DIM, HEAD_DIM), F32)
        st_ref[...] = jnp.concatenate([jnp.concatenate([s0_ref[0], zero], axis=1),
                                       jnp.concatenate([zero, s0_ref[1]], axis=1)], axis=0)

    pad = RET_ROWS - chunk
    lane = lax.broadcasted_iota(jnp.int32, (RET_ROWS, LANES), 1)
    head0 = lane < HEAD_DIM
    same_head = (lax.broadcasted_iota(jnp.int32, (LANES, LANES), 0) // HEAD_DIM) == (
        lax.broadcasted_iota(jnp.int32, (LANES, LANES), 1) // HEAD_DIM)

    def rows(ref, r):
        x = ref[pl.ds(r, chunk), :]
        return x if pad == 0 else jnp.concatenate([x, jnp.zeros((pad, LANES), F32)], axis=0)

    def seg_mean(x):
        s0 = jnp.sum(jnp.where(head0, x, 0.0), axis=-1, keepdims=True)
        s1 = jnp.sum(jnp.where(head0, 0.0, x), axis=-1, keepdims=True)
        return jnp.where(head0, s0, s1) * (1.0 / HEAD_DIM)

    def one_chunk(j, state):
        r = j * chunk
        q, k, v, g = rows(q_ref, r), rows(k_ref, r), rows(v_ref, r), rows(g_ref, r)
        kb, vb = k.astype(BF16), v.astype(BF16)
        q2 = jnp.concatenate([jnp.where(head0, q, 0.0), jnp.where(head0, 0.0, q)], axis=0).astype(BF16)
        attn = lax.dot_general(q2, kb, _NT, preferred_element_type=F32) * dec_ref[...]
        inn = jnp.dot(attn.astype(BF16), vb, preferred_element_type=F32)
        inner = jnp.where(head0, inn[:RET_ROWS], inn[RET_ROWS:])
        cross = jnp.dot(q.astype(BF16), state.astype(BF16), preferred_element_type=F32) * qd_ref[...]
        upd = lax.dot_general((k * kd_ref[...]).astype(BF16), vb, _TN, preferred_element_type=F32)
        o = inner + cross
        mu = seg_mean(o)
        var = seg_mean(jnp.square(o - mu))
        y = (o - mu) * lax.rsqrt(var + EPS) * gn_ref[...]
        y = y * (g * jax.nn.sigmoid(g))
        o_ref[r:r + chunk, :] = y[:chunk]
        return state * cd_ref[...] + jnp.where(same_head, upd, 0.0)

    state = st_ref[...]
    for j in range(nchunk):
        state = one_chunk(j, state)
    st_ref[...] = state

    @pl.when(c == pl.num_programs(2) - 1)
    def _():
        sout_ref[0] = st_ref[0:HEAD_DIM, 0:HEAD_DIM]
        sout_ref[1] = st_ref[HEAD_DIM:, HEAD_DIM:]


def _retention(q, k, v, g, gn_w, s0, batch, seq, chunk, nchunk):
    n, width = q.shape
    n_heads = width // HEAD_DIM
    npair = width // LANES
    rows = chunk * nchunk
    steps = seq // rows
    dec, qd, kd, cd = _retention_tables(n_heads, chunk)
    row_spec = pl.BlockSpec((rows, LANES), lambda b, p, c: (b * steps + c, p))
    st_spec = pl.BlockSpec((None, 2, HEAD_DIM, HEAD_DIM), lambda b, p, c: (b, p, 0, 0))
    tab = lambda t: pl.BlockSpec((None,) + t.shape[1:], lambda b, p, c: (p, 0, 0))
    out, s_new = pl.pallas_call(
        functools.partial(_retention_kernel, chunk=chunk, nchunk=nchunk),
        grid=(batch, npair, steps),
        in_specs=[row_spec, row_spec, row_spec, row_spec, pl.BlockSpec((1, LANES), lambda b, p, c: (0, p)),
                  st_spec, tab(dec), tab(qd), tab(kd), tab(cd)],
        out_specs=[row_spec, st_spec],
        out_shape=[jax.ShapeDtypeStruct((n, width), F32),
                   jax.ShapeDtypeStruct((batch, n_heads, HEAD_DIM, HEAD_DIM), F32)],
        scratch_shapes=[pltpu.VMEM((LANES, LANES), F32)],
        compiler_params=_cparams("parallel", "parallel", "arbitrary"),
        name="retention",
    )(q, k, v, g, gn_w.reshape(1, width), s0, dec, qd, kd, cd)
    return out, s_new


def _block_diag_queries(q, n_heads):
    t = q.shape[0]
    rep = jnp.concatenate([q] * n_heads, axis=0)
    rows = lax.broadcasted_iota(jnp.int32, rep.shape, 0) // t
    cols = lax.broadcasted_iota(jnp.int32, rep.shape, 1) // HEAD_DIM
    return jnp.where(rows == cols, rep, 0.0)


def _take_diag(o, n_heads, t):
    cols = lax.broadcasted_iota(jnp.int32, (t, o.shape[1]), 1) // HEAD_DIM
    out = jnp.zeros((t, o.shape[1]), F32)
    for h in range(n_heads):
        out = jnp.where(cols == h, o[h * t:(h + 1) * t, :], out)
    return out


def _fold_pairs(o, n_heads, t):
    rows = lax.broadcasted_iota(jnp.int32, (o.shape[0], LANES), 0) // (2 * t)
    out = jnp.zeros((o.shape[0], LANES), F32)
    for c in range(n_heads // 2):
        out = jnp.where(rows == c, o[:, c * LANES:(c + 1) * LANES], out)
    return out


def _unfold_pairs(a, n_heads, t):
    lane = lax.broadcasted_iota(jnp.int32, (t, LANES), 1)
    return jnp.concatenate([jnp.where(lane < HEAD_DIM, a[2 * c * t:(2 * c + 1) * t], a[(2 * c + 1) * t:(2 * c + 2) * t])
                            for c in range(n_heads // 2)], axis=1)


def _pad_rows(x, rows):
    return jnp.concatenate([x, jnp.zeros((rows - x.shape[0], x.shape[1]), x.dtype)], axis=0)


MOBA_PAGES_PER_STEP = 32


def _moba_decode_step(g, is_last, q_ref, kn_ref, vn_ref, kp, vp, o_ref, qbd_ref, m_ref, l_ref, sc_ref, oblk_ref,
                      *, n_heads, t_new, pages_per_blk, nblk):
    npg = len(kp)
    rows = n_heads * t_new
    lane = lax.broadcasted_iota(jnp.int32, (rows, LANES), 1)
    page = kp[0].shape[1]
    blk_keys = pages_per_blk * page
    step_blks = npg // pages_per_blk

    @pl.when(g == 0)
    def _():
        qbd_ref[...] = _scaled_bf16(_block_diag_queries(q_ref[...], n_heads))
        m_ref[...] = jnp.zeros_like(m_ref)
        l_ref[...] = jnp.zeros_like(l_ref)
        sc_ref[...] = jnp.full_like(sc_ref, -jnp.inf)

    kt_all = jnp.concatenate([kp[i][...].astype(BF16) for i in range(npg)], axis=1)
    s_all = jnp.dot(qbd_ref[...], kt_all, preferred_element_type=F32)
    m_new, l_new, sc_new = m_ref[...], l_ref[...], sc_ref[...]
    for jb in range(step_blks):
        n = g * step_blks + jb
        s = s_all[:, jb * blk_keys:(jb + 1) * blk_keys]
        mb = jnp.max(s, axis=-1, keepdims=True)
        p = jnp.exp(s - mb)
        v_blk = jnp.concatenate([vp[jb * pages_per_blk + a][...].astype(BF16) for a in range(pages_per_blk)], axis=1)
        oblk_ref[n] = _fold_pairs(lax.dot_general(p.astype(BF16), v_blk, _NT, preferred_element_type=F32),
                                  n_heads, t_new)
        here = lane == n
        m_new = jnp.where(here, mb, m_new)
        l_new = jnp.where(here, jnp.sum(p, axis=-1, keepdims=True), l_new)
        sc_new = jnp.where(here, jnp.sum(s, axis=-1, keepdims=True) * (1.0 / blk_keys), sc_new)
    m_ref[...], l_ref[...], sc_ref[...] = m_new, l_new, sc_new

    @pl.when(is_last)
    def _():
        s = lax.dot_general(qbd_ref[...], _pad_rows(kn_ref[...], LANES).astype(BF16), _NT,
                            preferred_element_type=F32)
        qry = lax.broadcasted_iota(jnp.int32, (rows, LANES), 0) % t_new
        s = jnp.where(lane <= qry, s, NEG_INF)
        m_own = jnp.max(s, axis=-1, keepdims=True)
        p = jnp.exp(s - m_own)
        l_own = jnp.sum(p, axis=-1, keepdims=True)
        o_own = jnp.dot(p.astype(BF16), _pad_rows(vn_ref[...], LANES).astype(BF16), preferred_element_type=F32)
        sc = sc_ref[...]
        lane_f = lane.astype(F32)
        sel = lane < 0
        for _ in range(MOBA_TOPK):
            mx = jnp.max(sc, axis=-1, keepdims=True)
            first = jnp.min(jnp.where(sc == mx, lane_f, float(LANES)), axis=-1, keepdims=True)
            pick = lane_f == first
            sel = sel | pick
            sc = jnp.where(pick, -jnp.inf, sc)
        m_all = m_ref[...]
        m_fin = jnp.maximum(jnp.max(jnp.where(sel, m_all, -jnp.inf), axis=-1, keepdims=True), m_own)
        w = jnp.where(sel, jnp.exp(m_all - m_fin), 0.0)
        w_own = jnp.exp(m_own - m_fin)
        l_fin = jnp.sum(w * l_ref[...], axis=-1, keepdims=True) + w_own * l_own

        def body(n, acc):
            col = jnp.sum(jnp.where(lane == n, w, 0.0), axis=-1, keepdims=True)
            return acc + col * oblk_ref[n]

        acc = lax.fori_loop(0, nblk, body, w_own * _fold_pairs(o_own, n_heads, t_new))
        o_ref[...] = _unfold_pairs(acc / l_fin, n_heads, t_new)


def _moba_kernel(pt_ref, q_ref, kt_ref, vt_ref, qs_ref, kn_ref, vn_ref, *rest, nblk, prompt_steps, decode_steps,
                 n_heads, t_new, pages_per_blk, dec_nblk):
    npg = MOBA_PAGES_PER_STEP
    kp, vp = rest[:npg], rest[npg:2 * npg]
    o_ref, os_ref, kb_ref, vb_ref, kmean_ref, bias_ref, causal_ref, qbd_ref, m_ref, l_ref, sc_ref, oblk_ref = rest[2 * npg:]
    i = pl.program_id(0)
    _moba_prompt_step(i % prompt_steps, q_ref, kt_ref, vt_ref, o_ref, kb_ref, vb_ref, kmean_ref, bias_ref, causal_ref,
                      nblk=nblk)
    g = i % decode_steps
    _moba_decode_step(g, g == decode_steps - 1, qs_ref, kn_ref, vn_ref, kp, vp, os_ref, qbd_ref, m_ref, l_ref, sc_ref,
                      oblk_ref, n_heads=n_heads, t_new=t_new, pages_per_blk=pages_per_blk, nblk=dec_nblk)


def _moba(q, kt, vt, batch, seq, qs, k_new, v_new, cache_kt, cache_vt, page_table, t_new):
    n, width = q.shape
    blk, grp, nsub = ATT_TILE, ATT_GROUP, MOBA_SUBTILES
    assert blk == MOBA_BLOCK and seq % (blk * grp) == 0 and grp % nsub == 0
    nblk = seq // blk
    steps_p = nblk // nsub
    npair = width // LANES
    n_heads = width // HEAD_DIM
    n_seq, n_pages = page_table.shape
    page = cache_kt.shape[2]
    pages_per_blk = MOBA_BLOCK // page
    dec_nblk = n_pages // pages_per_blk
    npg = MOBA_PAGES_PER_STEP
    steps_d = n_pages // npg
    rows = n_heads * t_new
    assert n_pages % npg == 0 and npg % pages_per_blk == 0 and MOBA_TOPK <= dec_nblk <= LANES
    assert n_pages * page == dec_nblk * MOBA_BLOCK and t_new <= page and page == LANES and rows <= LANES
    assert n_heads % 2 == 0
    total = batch * npair * steps_p
    assert total == n_seq * steps_d, "prompt and decode sides must have the same number of steps"

    kv_spec = pl.BlockSpec((None, LANES, seq), lambda i, pt: (i // (npair * steps_p), (i // steps_p) % npair, 0),
                           pipeline_mode=pl.Buffered(1))
    q_spec = pl.BlockSpec((nsub * blk, LANES),
                          lambda i, pt: ((i // (npair * steps_p)) * steps_p + i % steps_p, (i // steps_p) % npair))
    tok_spec = pl.BlockSpec((t_new, width), lambda i, pt: (i // steps_d, 0))
    page_specs = [pl.BlockSpec((None, width, page), functools.partial(
        lambda i, pt, j: (pt[i // steps_d, (i % steps_d) * npg + j], 0, 0), j=j)) for j in range(npg)]
    grid_spec = pltpu.PrefetchScalarGridSpec(
        num_scalar_prefetch=1,
        grid=(total,),
        in_specs=[q_spec, kv_spec, kv_spec, tok_spec, tok_spec, tok_spec] + page_specs + page_specs,
        out_specs=[q_spec, tok_spec],
        scratch_shapes=[pltpu.VMEM((nblk * blk, LANES), BF16), pltpu.VMEM((nblk, LANES + ONES_ROWS, blk), BF16),
                        pltpu.VMEM((nblk, LANES), F32), pltpu.VMEM((nsub, nblk, 2 * blk), F32),
                        pltpu.VMEM((blk, 2 * blk), F32),
                        pltpu.VMEM((rows, width), BF16), pltpu.VMEM((rows, LANES), F32),
                        pltpu.VMEM((rows, LANES), F32), pltpu.VMEM((rows, LANES), F32),
                        pltpu.VMEM((dec_nblk, rows, LANES), F32)],
    )
    return pl.pallas_call(
        functools.partial(_moba_kernel, nblk=nblk, prompt_steps=steps_p, decode_steps=steps_d, n_heads=n_heads,
                          t_new=t_new, pages_per_blk=pages_per_blk, dec_nblk=dec_nblk),
        grid_spec=grid_spec,
        out_shape=[jax.ShapeDtypeStruct((n, width), F32), jax.ShapeDtypeStruct(qs.shape, F32)],
        compiler_params=_cparams("arbitrary"),
        name="moba",
    )(page_table, q, kt, vt, qs, k_new, v_new, *([cache_kt] * npg), *([cache_vt] * npg))


DIL_DEC_TILE = 512


def _dilated_decode_tables(n_heads, t_new, n_prev):
    qi = (np.arange(n_heads * t_new) % t_new)[:, None]
    c_old = _dilated_multiplicity(n_prev + qi - np.arange(n_prev)[None, :])
    c_new = _dilated_multiplicity(qi - np.arange(LANES)[None, :])
    c_new = np.where(np.arange(LANES)[None, :] < t_new, c_new, 0)
    f = lambda c: (np.where(c > 0, 0.0, NEG_INF).astype(np.float32), c.astype(np.float32))
    return f(c_old) + f(c_new)


def _shifted_window_tile(cur_ref, nxt_ref, new_ref, out_ref, tail_ref, is_last, t_new):
    lane = lax.broadcasted_iota(jnp.int32, (cur_ref.shape[0], LANES), 1)
    nblk = cur_ref.shape[1] // LANES

    @pl.when(is_last)
    def _():
        tail_ref[...] = _pad_rows(new_ref[...], LANES).T

    @pl.when(jnp.logical_not(is_last))
    def _():
        tail_ref[...] = nxt_ref[...]

    rolled = [pltpu.roll(cur_ref[:, c * LANES:(c + 1) * LANES], LANES - t_new, 1) for c in range(nblk)]
    rolled.append(pltpu.roll(tail_ref[...], LANES - t_new, 1))
    for c in range(nblk):
        out_ref[:, c * LANES:(c + 1) * LANES] = jnp.where(lane < LANES - t_new, rolled[c], rolled[c + 1])


def _dilated_decode_kernel(q_ref, kn_ref, vn_ref, kc_ref, vc_ref, kx_ref, vx_ref, bo_ref, mo_ref, bn_ref, mn_ref,
                           o_ref, ko_ref, vo_ref, qbd_ref, m_ref, l_ref, acc_ref, tail_ref, *, n_heads, t_new):
    kt = pl.program_id(1)
    is_last = kt == pl.num_programs(1) - 1
    _shifted_window_tile(kc_ref, kx_ref, kn_ref, ko_ref, tail_ref, is_last, t_new)
    _shifted_window_tile(vc_ref, vx_ref, vn_ref, vo_ref, tail_ref, is_last, t_new)

    @pl.when(kt == 0)
    def _():
        qbd_ref[...] = _scaled_bf16(_block_diag_queries(q_ref[...], n_heads))
        m_ref[...] = jnp.full_like(m_ref, -jnp.inf)
        l_ref[...] = jnp.zeros_like(l_ref)
        acc_ref[...] = jnp.zeros_like(acc_ref)

    def step(s, pv, bias, mult):
        s = s + bias
        m_old = m_ref[...]
        m_new = jnp.maximum(m_old, jnp.max(s, axis=-1, keepdims=True))
        alpha = jnp.exp(m_old - m_new)
        p = jnp.exp(s - m_new) * mult
        l_ref[...] = alpha * l_ref[...] + jnp.sum(p, axis=-1, keepdims=True)
        acc_ref[...] = alpha * acc_ref[...] + pv(p.astype(BF16))
        m_ref[...] = m_new

    @pl.when(kt == 0)
    def _():
        k, v = _pad_rows(kn_ref[...], LANES).astype(BF16), _pad_rows(vn_ref[...], LANES).astype(BF16)
        step(lax.dot_general(qbd_ref[...], k, _NT, preferred_element_type=F32),
             lambda p: jnp.dot(p, v, preferred_element_type=F32), bn_ref[...], mn_ref[...])

    step(jnp.dot(qbd_ref[...], kc_ref[...].astype(BF16), preferred_element_type=F32),
         lambda p: lax.dot_general(p, vc_ref[...].astype(BF16), _NT, preferred_element_type=F32),
         bo_ref[...], mo_ref[...])

    @pl.when(kt == pl.num_programs(1) - 1)
    def _():
        o_ref[...] = _take_diag(acc_ref[...] / l_ref[...], n_heads, t_new)


def _dilated_decode(q, k_new, v_new, win_kt, win_vt, t_new):
    n, width = q.shape
    n_heads = width // HEAD_DIM
    n_seq, _, n_prev = win_kt.shape
    tile = DIL_DEC_TILE
    assert n_prev == W_MAX and n_prev % tile == 0 and t_new <= LANES
    rows = n_heads * t_new
    bo, mo, bn, mn = _dilated_decode_tables(n_heads, t_new, n_prev)
    tok_spec = pl.BlockSpec((t_new, width), lambda b, t: (b, 0))
    win_spec = pl.BlockSpec((None, width, tile), lambda b, t: (b, 0, t))
    last_lane_blk = n_prev // LANES - 1
    nxt_spec = pl.BlockSpec((None, width, LANES),
                            lambda b, t: (b, 0, jnp.minimum((t + 1) * (tile // LANES), last_lane_blk)))
    old_tab = pl.BlockSpec((rows, tile), lambda b, t: (0, t))
    new_tab = pl.BlockSpec((rows, LANES), lambda b, t: (0, 0))
    win_shape = jax.ShapeDtypeStruct(win_kt.shape, F32)
    return pl.pallas_call(
        functools.partial(_dilated_decode_kernel, n_heads=n_heads, t_new=t_new),
        grid=(n_seq, n_prev // tile),
        in_specs=[tok_spec, tok_spec, tok_spec, win_spec, win_spec, nxt_spec, nxt_spec,
                  old_tab, old_tab, new_tab, new_tab],
        out_specs=[tok_spec, win_spec, win_spec],
        out_shape=[jax.ShapeDtypeStruct((n, width), F32), win_shape, win_shape],
        scratch_shapes=[pltpu.VMEM((rows, width), BF16), pltpu.VMEM((rows, 1), F32),
                        pltpu.VMEM((rows, 1), F32), pltpu.VMEM((rows, width), F32),
                        pltpu.VMEM((width, LANES), F32)],
        compiler_params=_cparams("parallel", "arbitrary"),
        name="dilated_decode",
    )(q, k_new, v_new, win_kt, win_vt, win_kt, win_vt,
      jnp.asarray(bo), jnp.asarray(mo), jnp.asarray(bn), jnp.asarray(mn))


_AB_SEGS = (("a", (0,), 0, 1.0), ("a", (1,), 0, 1.0), (None, (2,), 0, 1.0), ("b", (3,), 0, 1.0),
            ("b", (4,), 0, ATT_SCALE), (None, (5,), 0, 1.0), (None, (6,), 0, 1.0))
_C_SEGS = tuple(("a" if o < 2 else None, (o,), c, 1.0) for o in range(3) for c in (0, SEG))
_KV_OUT = (1, 2)


def _row_tile(n):
    return 512 if n % 512 == 0 else n


def _ffn_row_tile(n):
    return 1024 if n % 1024 == 0 else n


def _feature_major(x):
    lead = x.shape[:-3]
    t, h, dh = x.shape[-3:]
    nl = len(lead)
    return x.transpose(*range(nl), nl + 1, nl + 2, nl).reshape(*lead, h * dh, t)


def _token_major(xt, n_heads):
    lead = xt.shape[:-2]
    t = xt.shape[-1]
    nl = len(lead)
    return xt.reshape(*lead, n_heads, HEAD_DIM, t).transpose(*range(nl), nl + 2, nl, nl + 1)


def kernel(x_prompt, x_sample, cache_k_a, cache_v_a, page_table, state_ret, cache_win_k, cache_win_v,
           norm_mix, norm_ffn, norm_final, w_in_ab, w_out_ab, ret_gn_w, w_in_c, w_out_c,
           ffn_w_gate, ffn_w_up, ffn_w_down):
    bp, tp, d = x_prompt.shape
    bs, ts, _ = x_sample.shape
    page = cache_k_a.shape[2]
    past_len = page_table.shape[1] * page
    h_a = cache_k_a.shape[3]
    wa = h_a * HEAD_DIM
    wb = w_in_ab.shape[2] - 3 * wa
    assert wb == 4 * wa and w_in_ab.shape[0] == 1 and w_in_c.shape[0] == 1 and norm_mix.shape[0] == 2
    h_b = wa // HEAD_DIM
    h_c = w_in_c.shape[2] // (3 * HEAD_DIM)

    pos_p = jnp.arange(tp, dtype=jnp.int32)
    pos_s = past_len + jnp.arange(ts, dtype=jnp.int32)
    tile_s = lambda tabs: tuple(jnp.tile(t, (bs, 1)) for t in tabs)
    tabs = {
        "p": (_rope_tables(pos_p, ROT_DIM, ROPE_THETA), _rope_tables(pos_p, HEAD_DIM, RET_THETA)),
        "s": (tile_s(_rope_tables(pos_s, ROT_DIM, ROPE_THETA)), tile_s(_rope_tables(pos_s, HEAD_DIM, RET_THETA))),
    }
    bf = lambda w: w.astype(BF16)
    w_in_ab_bf, w_out_ab_bf, w_in_c_bf, w_out_c_bf = bf(w_in_ab[0]), bf(w_out_ab[0]), bf(w_in_c[0]), bf(w_out_c[0])
    wg, wu, wd = bf(ffn_w_gate), bf(ffn_w_up), bf(ffn_w_down)

    xp = x_prompt.reshape(bp * tp, d)
    xs = x_sample.reshape(bs * ts, d)
    tm_p, tm_s = _row_tile(bp * tp), _row_tile(bs * ts)
    tf_p = _ffn_row_tile(bp * tp)
    assert tp % tm_p == 0

    ab_widths = (wa,) * 7
    qa_p, kat_p, vat_p, qb_p, kb_p, vb_p, gb_p = _project(
        xp, norm_mix[0], w_in_ab_bf, *tabs["p"], _AB_SEGS, ab_widths, tm_p, _KV_OUT)
    qa_s, ka_s, va_s, qb_s, kb_s, vb_s, gb_s = _project(
        xs, norm_mix[0], w_in_ab_bf, *tabs["s"], _AB_SEGS, ab_widths, tm_s)

    oa_p, oa_s = _moba(qa_p, kat_p, vat_p, bp, tp, qa_s, ka_s, va_s, _feature_major(cache_k_a[0]),
                       _feature_major(cache_v_a[0]), page_table, ts)

    chunk_p = math.gcd(tp, RET_CHUNK)
    ob_p, ret_p = _retention(qb_p, kb_p, vb_p, gb_p, ret_gn_w[0],
                             jnp.zeros((bp, h_b, HEAD_DIM, HEAD_DIM), F32), bp, tp, chunk_p,
                             math.gcd(tp // chunk_p, 8))
    ob_s, ret_s = _retention(qb_s, kb_s, vb_s, gb_s, ret_gn_w[0], state_ret[0], bs, ts,
                             math.gcd(ts, RET_CHUNK), 1)

    xp = _mix_ffn([oa_p, ob_p], w_out_ab_bf, xp, norm_ffn[0], wg[0], wu[0], wd[0], norm_final, False, tf_p)
    xs = _mix_ffn([oa_s, ob_s], w_out_ab_bf, xs, norm_ffn[0], wg[0], wu[0], wd[0], norm_final, False, tm_s)

    wc = h_c * HEAD_DIM
    q_p, k_p, v_p = _project(xp, norm_mix[1], w_in_c_bf, *tabs["p"], _C_SEGS, (wc,) * 3, tm_p)
    q_s, k_s, v_s = _project(xs, norm_mix[1], w_in_c_bf, *tabs["s"], _C_SEGS, (wc,) * 3, tm_s)
    n_prev = cache_win_k.shape[2]
    o_p, kwin_p, vwin_p = _dilated_prompt(q_p, k_p, v_p, bp, tp, min(W_MAX, tp))
    o_s, win_kt, win_vt = _dilated_decode(q_s, k_s, v_s, _feature_major(cache_win_k[0]),
                                          _feature_major(cache_win_v[0]), ts)
    y_p = _mix_ffn([o_p], w_out_c_bf, xp, norm_ffn[1], wg[1], wu[1], wd[1], norm_final, True, tf_p)
    y_s = _mix_ffn([o_s], w_out_c_bf, xs, norm_ffn[1], wg[1], wu[1], wd[1], norm_final, True, tm_s)

    return (
        y_p.reshape(bp, tp, d), y_s.reshape(bs, ts, d),
        _token_major(kat_p, h_a)[None], _token_major(vat_p, h_a)[None], ret_p[None],
        _token_major(kwin_p, h_c)[None], _token_major(vwin_p, h_c)[None],
        ka_s.reshape(1, bs, ts, h_a, HEAD_DIM), va_s.reshape(1, bs, ts, h_a, HEAD_DIM), ret_s[None],
        _token_major(win_kt, h_c)[None], _token_major(win_vt, h_c)[None],
    )
```

```python
import functools
import math

import jax
import jax.numpy as jnp
import numpy as np
from jax import lax
from jax.experimental import pallas as pl
from jax.experimental.pallas import tpu as pltpu

F32 = jnp.float32
BF16 = jnp.bfloat16

HEAD_DIM = 64
LANES = 128
ROT_DIM = HEAD_DIM // 4
ROPE_THETA = 500000.0
RET_THETA = 10000.0
MOBA_BLOCK = 256
MOBA_TOPK = 3
RET_CHUNK = 128
DILATED = ((128, 1), (512, 4), (2048, 16))
W_MAX = max(w for w, _ in DILATED)
EPS = 1e-6
NEG_INF = -1e30
ATT_SCALE = HEAD_DIM ** -0.5
VMEM_LIMIT = 60 * 1024 * 1024

_NT = (((1,), (1,)), ((), ()))
_TN = (((0,), (0,)), ((), ()))


def _cparams(*sem):
    return pltpu.CompilerParams(dimension_semantics=sem, vmem_limit_bytes=VMEM_LIMIT)


def _rms(x, w):
    ms = jnp.mean(x * x, axis=-1, keepdims=True)
    return x * lax.rsqrt(ms + EPS) * w


SEG = 512


def _proj_kernel(x_ref, nw_ref, w_ref, ca_ref, na_ref, pa_ref, cb_ref, nb_ref, pb_ref, *out_refs, segs, transposed):
    xn = _rms(x_ref[...], nw_ref[...]).astype(BF16)
    for s, (kind, dests, col, post) in enumerate(segs):
        acc = jnp.dot(xn, w_ref[:, s * SEG:(s + 1) * SEG], preferred_element_type=F32)
        for c in range(SEG // LANES):
            r = acc[:, c * LANES:(c + 1) * LANES]
            if kind is not None:
                c_ref, n_ref, p_ref, shift = (ca_ref, na_ref, pa_ref, ROT_DIM // 2) if kind == "a" else (
                    cb_ref, nb_ref, pb_ref, HEAD_DIM // 2)
                r = (r * c_ref[...] + pltpu.roll(r, LANES - shift, 1) * n_ref[...]
                     + pltpu.roll(r, shift, 1) * p_ref[...])
                if post != 1.0:
                    r = r * post
            lo = col + c * LANES
            for oi in dests:
                if oi in transposed:
                    out_refs[oi][lo:lo + LANES, :] = r.T
                else:
                    out_refs[oi][:, lo:lo + LANES] = r


def _rope_tables(pos, rot_dim, theta):
    half = rot_dim // 2
    inv = theta ** (-jnp.arange(half, dtype=F32) / half)
    ang = pos.astype(F32)[:, None] * inv[None, :]
    cos, sin = jnp.cos(ang), jnp.sin(ang)
    lane = np.arange(LANES) % HEAD_DIM
    idx = lane % half
    cos_t = jnp.where(lane < rot_dim, cos[:, idx], 1.0)
    neg_t = jnp.where(lane < half, -sin[:, idx], 0.0)
    pos_t = jnp.where((lane >= half) & (lane < rot_dim), sin[:, idx], 0.0)
    return cos_t.astype(F32), neg_t.astype(F32), pos_t.astype(F32)


def _project(x, nw, w_bf, tabs_a, tabs_b, segs, out_widths, tm, transposed=()):
    n, d = x.shape
    t_tab = tabs_a[0].shape[0]
    nt = t_tab // tm
    tab_spec = pl.BlockSpec((tm, LANES), lambda i: (i % nt, 0))
    out_specs, out_shape = [], []
    for oi, w in enumerate(out_widths):
        if oi in transposed:
            out_specs.append(pl.BlockSpec((None, w, tm), lambda i: (i // nt, 0, i % nt)))
            out_shape.append(jax.ShapeDtypeStruct((n // t_tab, w, t_tab), F32))
        else:
            out_specs.append(pl.BlockSpec((tm, w), lambda i: (i, 0)))
            out_shape.append(jax.ShapeDtypeStruct((n, w), F32))
    return pl.pallas_call(
        functools.partial(_proj_kernel, segs=segs, transposed=tuple(transposed)),
        grid=(n // tm,),
        in_specs=[pl.BlockSpec((tm, d), lambda i: (i, 0)),
                  pl.BlockSpec((1, d), lambda i: (0, 0)),
                  pl.BlockSpec(w_bf.shape, lambda i: (0, 0))] + [tab_spec] * 6,
        out_specs=out_specs,
        out_shape=out_shape,
        compiler_params=_cparams("parallel"),
        name="rms_proj_rope",
    )(x, nw.reshape(1, d), w_bf, *tabs_a, *tabs_b)


FFN_CHUNKS = 11


def _mix_ffn_kernel(*refs, nparts, final):
    parts = refs[:nparts]
    wo_ref, x_ref, nw_ref, wg_ref, wu_ref, wd_ref, fw_ref, o_ref = refs[nparts:]
    a = jnp.concatenate([p[...].astype(BF16) for p in parts], axis=1) if nparts > 1 else parts[0][...].astype(BF16)
    x = x_ref[...] + jnp.dot(a, wo_ref[...], preferred_element_type=F32)
    xn = _rms(x, nw_ref[...]).astype(BF16)
    tf = wg_ref.shape[1] // FFN_CHUNKS
    acc = x
    for c in range(FFN_CHUNKS):
        g = jnp.dot(xn, wg_ref[:, c * tf:(c + 1) * tf], preferred_element_type=F32)
        u = jnp.dot(xn, wu_ref[:, c * tf:(c + 1) * tf], preferred_element_type=F32)
        h = (g * jax.nn.sigmoid(g) * u).astype(BF16)
        acc = acc + jnp.dot(h, wd_ref[c * tf:(c + 1) * tf, :], preferred_element_type=F32)
    if final:
        acc = _rms(acc, fw_ref[...])
    o_ref[...] = acc


def _mix_ffn(parts, wo, x, nw, wg, wu, wd, fw, final, tm):
    n, d = x.shape
    dff = wg.shape[1]
    const = lambda i: (0, 0)
    row = lambda w: pl.BlockSpec((tm, w), lambda i: (i, 0))
    return pl.pallas_call(
        functools.partial(_mix_ffn_kernel, nparts=len(parts), final=final),
        grid=(n // tm,),
        in_specs=[row(p.shape[1]) for p in parts]
        + [pl.BlockSpec(wo.shape, const), row(d), pl.BlockSpec((1, d), const),
           pl.BlockSpec((d, dff), const), pl.BlockSpec((d, dff), const), pl.BlockSpec((dff, d), const),
           pl.BlockSpec((1, d), const)],
        out_specs=row(d),
        out_shape=jax.ShapeDtypeStruct((n, d), F32),
        compiler_params=_cparams("parallel"),
        name="outproj_swiglu",
    )(*parts, wo, x, nw.reshape(1, d), wg, wu, wd, fw.reshape(1, d))


ATT_TILE = 256
ATT_GROUP = 4
MOBA_SUBTILES = 4
ATT_CHUNK = 64


def _unit_scores(kb, qb, bias_fn):
    s = jnp.dot(kb, qb, preferred_element_type=F32)
    chunks = [s[r:r + ATT_CHUNK] + bias_fn(r, r + ATT_CHUNK) for r in range(0, s.shape[0], ATT_CHUNK)]
    return chunks, jnp.max(functools.reduce(jnp.maximum, chunks), axis=0, keepdims=True)


ONES_ROWS = 16


def _with_ones(vt):
    ones = jnp.ones((ONES_ROWS, vt.shape[1]), BF16)
    return jnp.stack([jnp.concatenate([vt[h * HEAD_DIM:(h + 1) * HEAD_DIM], ones], axis=0) for h in range(2)])


def _unit_values(chunks, m_u, vts, mult_fn):
    ps = []
    for i, c in enumerate(chunks):
        p = jnp.exp((c - m_u).astype(BF16))
        if mult_fn is not None:
            p = p * mult_fn(i * ATT_CHUNK, (i + 1) * ATT_CHUNK).astype(BF16)
        ps.append(p)
    per = len(ps) // len(vts)
    pcat = [jnp.concatenate(ps[b * per:(b + 1) * per], axis=0) for b in range(len(vts))]
    nq = pcat[0].shape[1] // 2
    pv = [sum(jnp.dot(vt[h], pc[:, h * nq:(h + 1) * nq], preferred_element_type=F32) for vt, pc in zip(vts, pcat))
          for h in range(2)]
    return (jnp.concatenate([x[HEAD_DIM:HEAD_DIM + 1] for x in pv], axis=1),
            jnp.concatenate([x[:HEAD_DIM] for x in pv], axis=1))


def _attend_unit(kb, q2, bias_fn, vts, mult_fn=None):
    chunks, m_u = _unit_scores(kb, q2, bias_fn)
    return (m_u,) + _unit_values(chunks, m_u, vts, mult_fn)


def _merge(parts):
    m = functools.reduce(jnp.maximum, [p[0] for p in parts])
    ws = [jnp.exp(p[0] - m) for p in parts]
    return m, sum(w * p[1] for w, p in zip(ws, parts)), sum(w * p[2] for w, p in zip(ws, parts))


def _head_queries(q_t, feat):
    return [jnp.where(feat // HEAD_DIM == hh, q_t, 0.0) for hh in range(2)]


def _scaled_bf16(q):
    return (q * ATT_SCALE).astype(BF16)


def _both(x):
    return jnp.concatenate([x, x], axis=1)


def _pick_heads(x, feat):
    nq = x.shape[1] // 2
    return jnp.concatenate([x[:, :nq], x[:, nq:]], axis=0)


def _moba_prompt_step(step, q_ref, kt_ref, vt_ref, o_ref, kb_ref, vb_ref, kmean_ref, bias_ref, causal_ref, *, nblk):
    blk, grp = ATT_TILE, ATT_GROUP
    nsub = MOBA_SUBTILES

    @pl.when(step == 0)
    def _():
        for n in range(nblk):
            kblk = kt_ref[:, n * blk:(n + 1) * blk].T
            kmean_ref[n:n + 1, :] = jnp.sum(kblk, axis=0, keepdims=True) * (1.0 / blk)
            kb_ref[n * blk:(n + 1) * blk, :] = kblk.astype(BF16)
            vb_ref[n] = _with_ones(vt_ref[:, n * blk:(n + 1) * blk].astype(BF16))
        key_i = lax.broadcasted_iota(jnp.int32, (blk, 2 * blk), 0)
        qry_i = lax.broadcasted_iota(jnp.int32, (blk, 2 * blk), 1) % blk
        causal_ref[...] = jnp.where(key_i <= qry_i, 0.0, NEG_INF)

    feat = lax.broadcasted_iota(jnp.int32, (LANES, blk), 0)
    q2s = [jnp.concatenate(_head_queries(q_ref[sub * blk:(sub + 1) * blk, :].T, feat), axis=1) for sub in range(nsub)]
    qb2s = [_scaled_bf16(q2) for q2 in q2s]
    cols = nsub * 2 * blk
    blk_id = lax.broadcasted_iota(jnp.int32, (nblk, cols), 0)
    blk_f = blk_id.astype(F32)
    qt = step * nsub + lax.broadcasted_iota(jnp.int32, (1, cols), 1) // (2 * blk)
    sc = jnp.dot(kmean_ref[...], jnp.concatenate(q2s, axis=1), preferred_element_type=F32,
                 precision=lax.Precision.HIGHEST)
    sc = jnp.where(blk_id < qt, sc, NEG_INF)
    sel = jnp.zeros((nblk, cols), F32)
    for r in range(MOBA_TOPK):
        mx = jnp.max(sc, axis=0, keepdims=True)
        first = jnp.min(jnp.where(sc == mx, blk_f, float(nblk)), axis=0, keepdims=True)
        pick = blk_f == first
        sel = jnp.maximum(sel, jnp.where(pick & (r < qt), 1.0, 0.0))
        sc = jnp.where(pick, -jnp.inf, sc)
    bias = jnp.where((sel > 0.0) | (blk_id == qt), 0.0, NEG_INF)
    for sub in range(nsub):
        bias_ref[sub] = bias[:, sub * 2 * blk:(sub + 1) * 2 * blk]

    def group(sub, g, with_own):
        own_j = (step * nsub) % grp + sub
        nb = grp if not with_own else (sub + 1 if grp == nsub else grp)

        def bias_fn(r0, r1):
            j, r = r0 // blk, r0 % blk
            row = bias_ref[sub, pl.ds(g * grp + j, 1), :]
            if not with_own:
                return row
            return row + jnp.where(j == own_j, causal_ref[r:r + (r1 - r0), :], 0.0)

        kb = kb_ref[pl.ds(pl.multiple_of(g * grp * blk, blk), nb * blk), :]
        return _attend_unit(kb, qb2s[sub], bias_fn, [vb_ref[g * grp + b] for b in range(nb)])

    g_own = step * nsub // grp

    def body(g, carry):
        return sum((_merge([carry[3 * sub:3 * sub + 3], group(sub, g, False)]) for sub in range(nsub)), ())

    carry = lax.fori_loop(0, g_own, body, sum((group(sub, g_own, True) for sub in range(nsub)), ()))
    for sub in range(nsub):
        o_ref[sub * blk:(sub + 1) * blk, :] = _pick_heads(carry[3 * sub + 2] / carry[3 * sub + 1], feat).T


def _dilated_multiplicity(delta):
    c = np.zeros(delta.shape, np.int32)
    for window, dil in DILATED:
        c += ((delta >= 0) & (delta <= window) & (delta % dil == 0)).astype(np.int32)
    return c


DIL_NEAR = 2
DIL_FAR_CLASSES_PER_STEP = 4
assert DILATED[-1][1] % DIL_FAR_CLASSES_PER_STEP == 0
DIL_NEAR_SUBTILES = 8
DIL_FAR_STRIDE = DILATED[-1][1]
DIL_FAR_REACH = DILATED[-1][0] // DIL_FAR_STRIDE
assert DILATED[-2][0] <= ATT_TILE * DIL_NEAR and ATT_TILE % DIL_FAR_STRIDE == 0 and DIL_FAR_REACH <= ATT_TILE


def _mask_bias(valid):
    return np.where(valid, 0.0, NEG_INF).astype(np.float32)


def _dilated_near_tables():
    ki = np.arange(ATT_TILE)[:, None]
    qi = np.arange(ATT_TILE)[None, :]
    c = np.stack([_dilated_multiplicity(ATT_TILE * (DIL_NEAR - i) + qi - ki) for i in range(DIL_NEAR + 1)])
    return _mask_bias(c > 0), c.astype(np.float32)


def _dilated_far_tables():
    per_tile = ATT_TILE // DIL_FAR_STRIDE

    def table(k0, nk, q0):
        ka = k0 + np.arange(nk)[:, None]
        qa = q0 + np.arange(ATT_TILE)[None, :]
        return _mask_bias((qa - ka <= DIL_FAR_REACH) & (ka // per_tile < qa // per_tile - DIL_NEAR))

    return table(0, ATT_TILE, 0), table(ATT_TILE - DIL_FAR_REACH, ATT_TILE + DIL_FAR_REACH, ATT_TILE)


def _dilated_far_kernel(q_ref, k_ref, v_ref, b0_ref, b1_ref, o_ref, lse_ref, *, seq):
    stride, blk, reach = DIL_FAR_STRIDE, ATT_TILE, DIL_FAR_REACH
    ntile = seq // stride // blk
    feat = lax.broadcasted_iota(jnp.int32, (LANES, blk), 0)

    def one_class(cls):
        for tau in range(ntile):
            k0, nk, tab = (0, blk, b0_ref) if tau == 0 else (tau * blk - reach, blk + reach, b1_ref)
            rows_q = pl.ds(cls + stride * blk * tau, blk, stride=stride)
            rows_k = pl.ds(cls + stride * k0, nk, stride=stride)
            qb2 = _scaled_bf16(jnp.concatenate(_head_queries(q_ref[rows_q, :].T, feat), axis=1))
            kb = k_ref[rows_k, :].astype(BF16)
            vt = _with_ones(v_ref[rows_k, :].T.astype(BF16))
            m_u, l_u, pv = _attend_unit(kb, qb2, lambda r0, r1: _both(tab[r0:r1, :]), [vt])
            o_ref[rows_q, :] = _pick_heads(pv / l_u, feat).T
            lse_ref[rows_q, :] = _pick_heads(jnp.broadcast_to(m_u + jnp.log(l_u), pv.shape), feat).T

    def body(i, carry):
        for c in range(DIL_FAR_CLASSES_PER_STEP):
            one_class(i * DIL_FAR_CLASSES_PER_STEP + c)
        return carry

    lax.fori_loop(0, stride // DIL_FAR_CLASSES_PER_STEP, body, 0)


def _dilated_near_kernel(q_ref, k_ref, v_ref, far_ref, lse_ref, bias_ref, mult_ref, o_ref, kwin_ref, vwin_ref,
                         kb_ref, vb_ref, tab_ref, *, nblk):
    blk, near, nsub = ATT_TILE, DIL_NEAR, DIL_NEAR_SUBTILES
    step = pl.program_id(2)
    keep_blks = kwin_ref.shape[1] // blk

    @pl.when(step == 0)
    def _():
        kb_ref[0:near * blk, :] = jnp.zeros((near * blk, LANES), BF16)
        for n in range(near):
            vb_ref[n] = jnp.zeros((2, HEAD_DIM + ONES_ROWS, blk), BF16)
        for n in range(nblk):
            kblk = k_ref[n * blk:(n + 1) * blk, :]
            vblk_t = v_ref[n * blk:(n + 1) * blk, :].T
            kb_ref[(n + near) * blk:(n + near + 1) * blk, :] = kblk.astype(BF16)
            vb_ref[n + near] = _with_ones(vblk_t.astype(BF16))
            if n >= nblk - keep_blks:
                w = n - (nblk - keep_blks)
                kwin_ref[:, w * blk:(w + 1) * blk] = kblk.T
                vwin_ref[:, w * blk:(w + 1) * blk] = vblk_t

    @pl.when(step * nsub < near + nsub)
    def _():
        for sub in range(nsub):
            for i in range(near + 1):
                tab_ref[sub, i] = jnp.where(step * nsub + sub - near + i >= 0, bias_ref[i], NEG_INF)

    feat = lax.broadcasted_iota(jnp.int32, (LANES, blk), 0)
    for sub in range(nsub):
        qt = step * nsub + sub
        rows = slice(sub * blk, (sub + 1) * blk)
        qb2 = _scaled_bf16(jnp.concatenate(_head_queries(q_ref[rows, :].T, feat), axis=1))
        m2, l2, a2 = _attend_unit(
            kb_ref[pl.ds(pl.multiple_of(qt * blk, blk), (near + 1) * blk), :], qb2,
            lambda r0, r1: _both(tab_ref[sub, r0 // blk, r0 % blk:r0 % blk + (r1 - r0), :]),
            [vb_ref[qt + i] for i in range(near + 1)],
            lambda r0, r1: _both(mult_ref[r0 // blk, r0 % blk:r0 % blk + (r1 - r0), :]))
        m_n = _pick_heads(jnp.broadcast_to(m2, a2.shape), feat)
        l_n = _pick_heads(jnp.broadcast_to(l2, a2.shape), feat)
        a_n = _pick_heads(a2, feat)
        lse_f = lse_ref[rows, :].T
        m = jnp.maximum(m_n, lse_f)
        w_n, w_f = jnp.exp(m_n - m), jnp.exp(lse_f - m)
        o_ref[rows, :] = ((a_n * w_n + far_ref[rows, :].T * w_f) / (l_n * w_n + w_f)).T


def _dilated_prompt(q, k, v, batch, seq, keep):
    n, width = q.shape
    blk, near, stride = ATT_TILE, DIL_NEAR, DIL_FAR_STRIDE
    assert seq % (blk * stride) == 0
    nblk = seq // blk
    npair = width // LANES
    const = lambda nd: (lambda *_: (0,) * nd)
    full = lambda t: pl.BlockSpec(t.shape, const(t.ndim))

    far_tabs = [jnp.asarray(t) for t in _dilated_far_tables()]
    seq_spec = pl.BlockSpec((seq, LANES), lambda b, p: (b, p))
    o_far, lse_far = pl.pallas_call(
        functools.partial(_dilated_far_kernel, seq=seq),
        grid=(batch, npair),
        in_specs=[seq_spec, seq_spec, seq_spec] + [full(t) for t in far_tabs],
        out_specs=[seq_spec, seq_spec],
        out_shape=[jax.ShapeDtypeStruct((n, width), F32)] * 2,
        compiler_params=_cparams("parallel", "parallel"),
        name="dilated_far",
    )(q, k, v, *far_tabs)

    near_tabs = [jnp.asarray(t) for t in _dilated_near_tables()]
    nsub = DIL_NEAR_SUBTILES
    assert nblk % nsub == 0 and keep % blk == 0 and keep <= seq
    steps = nblk // nsub
    kv_spec = pl.BlockSpec((seq, LANES), lambda b, p, t: (b, p))
    q_spec = pl.BlockSpec((nsub * blk, LANES), lambda b, p, t: (b * steps + t, p))
    win_spec = pl.BlockSpec((None, LANES, keep), lambda b, p, t: (b, p, 0))
    win_shape = jax.ShapeDtypeStruct((batch, width, keep), F32)
    return pl.pallas_call(
        functools.partial(_dilated_near_kernel, nblk=nblk),
        grid=(batch, npair, steps),
        in_specs=[q_spec, kv_spec, kv_spec, q_spec, q_spec] + [full(t) for t in near_tabs],
        out_specs=[q_spec, win_spec, win_spec],
        out_shape=[jax.ShapeDtypeStruct((n, width), F32), win_shape, win_shape],
        scratch_shapes=[pltpu.VMEM(((nblk + near) * blk, LANES), BF16),
                        pltpu.VMEM((nblk + near, 2, HEAD_DIM + ONES_ROWS, blk), BF16),
                        pltpu.VMEM((nsub, near + 1, blk, blk), F32)],
        compiler_params=_cparams("parallel", "parallel", "arbitrary"),
        name="dilated_near",
    )(q, k, v, o_far, lse_far, *near_tabs)


RET_ROWS = 128


def _retention_tables(n_heads, chunk):
    log_g = jnp.log1p(-jnp.exp2(-5.0 - jnp.arange(n_heads, dtype=F32)))
    i = jnp.arange(chunk, dtype=F32)
    diff = i[:, None] - i[None, :]
    causal = diff >= 0
    decay = jnp.where(causal[None], jnp.exp(jnp.where(causal, diff, 0.0)[None] * log_g[:, None, None]), 0.0)
    q_dec = jnp.exp((i + 1.0)[:, None] * log_g[None, :])
    k_dec = jnp.exp((chunk - 1.0 - i)[:, None] * log_g[None, :])
    c_dec = jnp.exp(chunk * log_g)
    pad = RET_ROWS - chunk
    npair = n_heads // 2
    decay = jnp.pad(decay, ((0, 0), (0, pad), (0, pad))).reshape(npair, 2 * RET_ROWS, RET_ROWS)
    expand = lambda t: jnp.pad(jnp.repeat(t, HEAD_DIM, axis=1), ((0, pad), (0, 0))).reshape(
        RET_ROWS, npair, LANES).transpose(1, 0, 2)
    q_tab, k_tab = expand(q_dec), expand(k_dec)
    same_head = (np.arange(LANES)[:, None] // HEAD_DIM) == (np.arange(LANES)[None, :] // HEAD_DIM)
    c_tab = jnp.where(same_head[None], jnp.repeat(c_dec, HEAD_DIM).reshape(npair, LANES, 1), 0.0)
    return decay.astype(F32), q_tab.astype(F32), k_tab.astype(F32), c_tab.astype(F32)


def _retention_kernel(q_ref, k_ref, v_ref, g_ref, gn_ref, s0_ref, dec_ref, qd_ref, kd_ref, cd_ref,
                      o_ref, sout_ref, st_ref, *, chunk, nchunk):
    c = pl.program_id(2)

    @pl.when(c == 0)
    def _():
        zero = jnp.zeros((HEAD_DIM, HEAD_DIM), F32)
        st_ref[...] = jnp.concatenate([jnp.concatenate([s0_ref[0], zero], axis=1),
                                       jnp.concatenate([zero, s0_ref[1]], axis=1)], axis=0)

    pad = RET_ROWS - chunk
    lane = lax.broadcasted_iota(jnp.int32, (RET_ROWS, LANES), 1)
    head0 = lane < HEAD_DIM
    same_head = (lax.broadcasted_iota(jnp.int32, (LANES, LANES), 0) // HEAD_DIM) == (
        lax.broadcasted_iota(jnp.int32, (LANES, LANES), 1) // HEAD_DIM)

    def rows(ref, r):
        x = ref[pl.ds(r, chunk), :]
        return x if pad == 0 else jnp.concatenate([x, jnp.zeros((pad, LANES), F32)], axis=0)

    def seg_mean(x):
        s0 = jnp.sum(jnp.where(head0, x, 0.0), axis=-1, keepdims=True)
        s1 = jnp.sum(jnp.where(head0, 0.0, x), axis=-1, keepdims=True)
        return jnp.where(head0, s0, s1) * (1.0 / HEAD_DIM)

    def one_chunk(j, state):
        r = j * chunk
        q, k, v, g = rows(q_ref, r), rows(k_ref, r), rows(v_ref, r), rows(g_ref, r)
        kb, vb = k.astype(BF16), v.astype(BF16)
        q2 = jnp.concatenate([jnp.where(head0, q, 0.0), jnp.where(head0, 0.0, q)], axis=0).astype(BF16)
        attn = lax.dot_general(q2, kb, _NT, preferred_element_type=F32) * dec_ref[...]
        inn = jnp.dot(attn.astype(BF16), vb, preferred_element_type=F32)
        inner = jnp.where(head0, inn[:RET_ROWS], inn[RET_ROWS:])
        cross = jnp.dot(q.astype(BF16), state.astype(BF16), preferred_element_type=F32) * qd_ref[...]
        upd = lax.dot_general((k * kd_ref[...]).astype(BF16), vb, _TN, preferred_element_type=F32)
        o = inner + cross
        mu = seg_mean(o)
        var = seg_mean(jnp.square(o - mu))
        y = (o - mu) * lax.rsqrt(var + EPS) * gn_ref[...]
        y = y * (g * jax.nn.sigmoid(g))
        o_ref[r:r + chunk, :] = y[:chunk]
        return state * cd_ref[...] + jnp.where(same_head, upd, 0.0)

    state = st_ref[...]
    for j in range(nchunk):
        state = one_chunk(j, state)
    st_ref[...] = state

    @pl.when(c == pl.num_programs(2) - 1)
    def _():
        sout_ref[0] = st_ref[0:HEAD_DIM, 0:HEAD_DIM]
        sout_ref[1] = st_ref[HEAD_DIM:, HEAD_DIM:]


def _retention(q, k, v, g, gn_w, s0, batch, seq, chunk, nchunk):
    n, width = q.shape
    n_heads = width // HEAD_DIM
    npair = width // LANES
    rows = chunk * nchunk
    steps = seq // rows
    dec, qd, kd, cd = _retention_tables(n_heads, chunk)
    row_spec = pl.BlockSpec((rows, LANES), lambda b, p, c: (b * steps + c, p))
    st_spec = pl.BlockSpec((None, 2, HEAD_DIM, HEAD_DIM), lambda b, p, c: (b, p, 0, 0))
    tab = lambda t: pl.BlockSpec((None,) + t.shape[1:], lambda b, p, c: (p, 0, 0))
    out, s_new = pl.pallas_call(
        functools.partial(_retention_kernel, chunk=chunk, nchunk=nchunk),
        grid=(batch, npair, steps),
        in_specs=[row_spec, row_spec, row_spec, row_spec, pl.BlockSpec((1, LANES), lambda b, p, c: (0, p)),
                  st_spec, tab(dec), tab(qd), tab(kd), tab(cd)],
        out_specs=[row_spec, st_spec],
        out_shape=[jax.ShapeDtypeStruct((n, width), F32),
                   jax.ShapeDtypeStruct((batch, n_heads, HEAD_DIM, HEAD_DIM), F32)],
        scratch_shapes=[pltpu.VMEM((LANES, LANES), F32)],
        compiler_params=_cparams("parallel", "parallel", "arbitrary"),
        name="retention",
    )(q, k, v, g, gn_w.reshape(1, width), s0, dec, qd, kd, cd)
    return out, s_new


def _block_diag_queries(q, n_heads):
    t = q.shape[0]
    rep = jnp.concatenate([q] * n_heads, axis=0)
    rows = lax.broadcasted_iota(jnp.int32, rep.shape, 0) // t
    cols = lax.broadcasted_iota(jnp.int32, rep.shape, 1) // HEAD_DIM
    return jnp.where(rows == cols, rep, 0.0)


def _take_diag(o, n_heads, t):
    cols = lax.broadcasted_iota(jnp.int32, (t, o.shape[1]), 1) // HEAD_DIM
    out = jnp.zeros((t, o.shape[1]), F32)
    for h in range(n_heads):
        out = jnp.where(cols == h, o[h * t:(h + 1) * t, :], out)
    return out


def _fold_pairs(o, n_heads, t):
    rows = lax.broadcasted_iota(jnp.int32, (o.shape[0], LANES), 0) // (2 * t)
    out = jnp.zeros((o.shape[0], LANES), F32)
    for c in range(n_heads // 2):
        out = jnp.where(rows == c, o[:, c * LANES:(c + 1) * LANES], out)
    return out


def _unfold_pairs(a, n_heads, t):
    lane = lax.broadcasted_iota(jnp.int32, (t, LANES), 1)
    return jnp.concatenate([jnp.where(lane < HEAD_DIM, a[2 * c * t:(2 * c + 1) * t], a[(2 * c + 1) * t:(2 * c + 2) * t])
                            for c in range(n_heads // 2)], axis=1)


def _pad_rows(x, rows):
    return jnp.concatenate([x, jnp.zeros((rows - x.shape[0], x.shape[1]), x.dtype)], axis=0)


MOBA_PAGES_PER_STEP = 32


def _moba_decode_step(g, is_last, q_ref, kn_ref, vn_ref, kp, vp, o_ref, qbd_ref, m_ref, l_ref, sc_ref, oblk_ref,
                      *, n_heads, t_new, pages_per_blk, nblk):
    npg = len(kp)
    rows = n_heads * t_new
    lane = lax.broadcasted_iota(jnp.int32, (rows, LANES), 1)
    page = kp[0].shape[1]
    blk_keys = pages_per_blk * page
    step_blks = npg // pages_per_blk

    @pl.when(g == 0)
    def _():
        qbd_ref[...] = _scaled_bf16(_block_diag_queries(q_ref[...], n_heads))
        m_ref[...] = jnp.zeros_like(m_ref)
        l_ref[...] = jnp.zeros_like(l_ref)
        sc_ref[...] = jnp.full_like(sc_ref, -jnp.inf)

    kt_all = jnp.concatenate([kp[i][...].astype(BF16) for i in range(npg)], axis=1)
    s_all = jnp.dot(qbd_ref[...], kt_all, preferred_element_type=F32)
    m_new, l_new, sc_new = m_ref[...], l_ref[...], sc_ref[...]
    for jb in range(step_blks):
        n = g * step_blks + jb
        s = s_all[:, jb * blk_keys:(jb + 1) * blk_keys]
        mb = jnp.max(s, axis=-1, keepdims=True)
        p = jnp.exp(s - mb)
        v_blk = jnp.concatenate([vp[jb * pages_per_blk + a][...].astype(BF16) for a in range(pages_per_blk)], axis=1)
        oblk_ref[n] = _fold_pairs(lax.dot_general(p.astype(BF16), v_blk, _NT, preferred_element_type=F32),
                                  n_heads, t_new)
        here = lane == n
        m_new = jnp.where(here, mb, m_new)
        l_new = jnp.where(here, jnp.sum(p, axis=-1, keepdims=True), l_new)
        sc_new = jnp.where(here, jnp.sum(s, axis=-1, keepdims=True) * (1.0 / blk_keys), sc_new)
    m_ref[...], l_ref[...], sc_ref[...] = m_new, l_new, sc_new

    @pl.when(is_last)
    def _():
        s = lax.dot_general(qbd_ref[...], _pad_rows(kn_ref[...], LANES).astype(BF16), _NT,
                            preferred_element_type=F32)
        qry = lax.broadcasted_iota(jnp.int32, (rows, LANES), 0) % t_new
        s = jnp.where(lane <= qry, s, NEG_INF)
        m_own = jnp.max(s, axis=-1, keepdims=True)
        p = jnp.exp(s - m_own)
        l_own = jnp.sum(p, axis=-1, keepdims=True)
        o_own = jnp.dot(p.astype(BF16), _pad_rows(vn_ref[...], LANES).astype(BF16), preferred_element_type=F32)
        sc = sc_ref[...]
        lane_f = lane.astype(F32)
        sel = lane < 0
        for _ in range(MOBA_TOPK):
            mx = jnp.max(sc, axis=-1, keepdims=True)
            first = jnp.min(jnp.where(sc == mx, lane_f, float(LANES)), axis=-1, keepdims=True)
            pick = lane_f == first
            sel = sel | pick
            sc = jnp.where(pick, -jnp.inf, sc)
        m_all = m_ref[...]
        m_fin = jnp.maximum(jnp.max(jnp.where(sel, m_all, -jnp.inf), axis=-1, keepdims=True), m_own)
        w = jnp.where(sel, jnp.exp(m_all - m_fin), 0.0)
        w_own = jnp.exp(m_own - m_fin)
        l_fin = jnp.sum(w * l_ref[...], axis=-1, keepdims=True) + w_own * l_own

        def body(n, acc):
            col = jnp.sum(jnp.where(lane == n, w, 0.0), axis=-1, keepdims=True)
            return acc + col * oblk_ref[n]

        acc = lax.fori_loop(0, nblk, body, w_own * _fold_pairs(o_own, n_heads, t_new))
        o_ref[...] = _unfold_pairs(acc / l_fin, n_heads, t_new)


def _moba_kernel(pt_ref, q_ref, kt_ref, vt_ref, qs_ref, kn_ref, vn_ref, *rest, nblk, prompt_steps, decode_steps,
                 n_heads, t_new, pages_per_blk, dec_nblk):
    npg = MOBA_PAGES_PER_STEP
    kp, vp = rest[:npg], rest[npg:2 * npg]
    o_ref, os_ref, kb_ref, vb_ref, kmean_ref, bias_ref, causal_ref, qbd_ref, m_ref, l_ref, sc_ref, oblk_ref = rest[2 * npg:]
    i = pl.program_id(0)
    _moba_prompt_step(i % prompt_steps, q_ref, kt_ref, vt_ref, o_ref, kb_ref, vb_ref, kmean_ref, bias_ref, causal_ref,
                      nblk=nblk)
    g = i % decode_steps
    _moba_decode_step(g, g == decode_steps - 1, qs_ref, kn_ref, vn_ref, kp, vp, os_ref, qbd_ref, m_ref, l_ref, sc_ref,
                      oblk_ref, n_heads=n_heads, t_new=t_new, pages_per_blk=pages_per_blk, nblk=dec_nblk)


def _moba(q, kt, vt, batch, seq, qs, k_new, v_new, cache_kt, cache_vt, page_table, t_new):
    n, width = q.shape
    blk, grp, nsub = ATT_TILE, ATT_GROUP, MOBA_SUBTILES
    assert blk == MOBA_BLOCK and seq % (blk * grp) == 0 and grp % nsub == 0
    nblk = seq // blk
    steps_p = nblk // nsub
    npair = width // LANES
    n_heads = width // HEAD_DIM
    n_seq, n_pages = page_table.shape
    page = cache_kt.shape[2]
    pages_per_blk = MOBA_BLOCK // page
    dec_nblk = n_pages // pages_per_blk
    npg = MOBA_PAGES_PER_STEP
    steps_d = n_pages // npg
    rows = n_heads * t_new
    assert n_pages % npg == 0 and npg % pages_per_blk == 0 and MOBA_TOPK <= dec_nblk <= LANES
    assert n_pages * page == dec_nblk * MOBA_BLOCK and t_new <= page and page == LANES and rows <= LANES
    assert n_heads % 2 == 0
    total = batch * npair * steps_p
    assert total == n_seq * steps_d, "prompt and decode sides must have the same number of steps"

    kv_spec = pl.BlockSpec((None, LANES, seq), lambda i, pt: (i // (npair * steps_p), (i // steps_p) % npair, 0),
                           pipeline_mode=pl.Buffered(1))
    q_spec = pl.BlockSpec((nsub * blk, LANES),
                          lambda i, pt: ((i // (npair * steps_p)) * steps_p + i % steps_p, (i // steps_p) % npair))
    tok_spec = pl.BlockSpec((t_new, width), lambda i, pt: (i // steps_d, 0))
    page_specs = [pl.BlockSpec((None, width, page), functools.partial(
        lambda i, pt, j: (pt[i // steps_d, (i % steps_d) * npg + j], 0, 0), j=j)) for j in range(npg)]
    grid_spec = pltpu.PrefetchScalarGridSpec(
        num_scalar_prefetch=1,
        grid=(total,),
        in_specs=[q_spec, kv_spec, kv_spec, tok_spec, tok_spec, tok_spec] + page_specs + page_specs,
        out_specs=[q_spec, tok_spec],
        scratch_shapes=[pltpu.VMEM((nblk * blk, LANES), BF16), pltpu.VMEM((nblk, 2, HEAD_DIM + ONES_ROWS, blk), BF16),
                        pltpu.VMEM((nblk, LANES), F32), pltpu.VMEM((nsub, nblk, 2 * blk), F32),
                        pltpu.VMEM((blk, 2 * blk), F32),
                        pltpu.VMEM((rows, width), BF16), pltpu.VMEM((rows, LANES), F32),
                        pltpu.VMEM((rows, LANES), F32), pltpu.VMEM((rows, LANES), F32),
                        pltpu.VMEM((dec_nblk, rows, LANES), F32)],
    )
    return pl.pallas_call(
        functools.partial(_moba_kernel, nblk=nblk, prompt_steps=steps_p, decode_steps=steps_d, n_heads=n_heads,
                          t_new=t_new, pages_per_blk=pages_per_blk, dec_nblk=dec_nblk),
        grid_spec=grid_spec,
        out_shape=[jax.ShapeDtypeStruct((n, width), F32), jax.ShapeDtypeStruct(qs.shape, F32)],
        compiler_params=_cparams("arbitrary"),
        name="moba",
    )(page_table, q, kt, vt, qs, k_new, v_new, *([cache_kt] * npg), *([cache_vt] * npg))


DIL_DEC_TILE = 512


def _dilated_decode_tables(n_heads, t_new, n_prev):
    qi = (np.arange(n_heads * t_new) % t_new)[:, None]
    c_old = _dilated_multiplicity(n_prev + qi - np.arange(n_prev)[None, :])
    c_new = _dilated_multiplicity(qi - np.arange(LANES)[None, :])
    c_new = np.where(np.arange(LANES)[None, :] < t_new, c_new, 0)
    f = lambda c: (np.where(c > 0, 0.0, NEG_INF).astype(np.float32), c.astype(np.float32))
    return f(c_old) + f(c_new)


def _shifted_window_tile(cur_ref, nxt_ref, new_ref, out_ref, tail_ref, is_last, t_new):
    lane = lax.broadcasted_iota(jnp.int32, (cur_ref.shape[0], LANES), 1)
    nblk = cur_ref.shape[1] // LANES

    @pl.when(is_last)
    def _():
        tail_ref[...] = _pad_rows(new_ref[...], LANES).T

    @pl.when(jnp.logical_not(is_last))
    def _():
        tail_ref[...] = nxt_ref[...]

    rolled = [pltpu.roll(cur_ref[:, c * LANES:(c + 1) * LANES], LANES - t_new, 1) for c in range(nblk)]
    rolled.append(pltpu.roll(tail_ref[...], LANES - t_new, 1))
    for c in range(nblk):
        out_ref[:, c * LANES:(c + 1) * LANES] = jnp.where(lane < LANES - t_new, rolled[c], rolled[c + 1])


def _dilated_decode_kernel(q_ref, kn_ref, vn_ref, kc_ref, vc_ref, kx_ref, vx_ref, bo_ref, mo_ref, bn_ref, mn_ref,
                           o_ref, ko_ref, vo_ref, qbd_ref, m_ref, l_ref, acc_ref, tail_ref, *, n_heads, t_new):
    kt = pl.program_id(1)
    is_last = kt == pl.num_programs(1) - 1
    _shifted_window_tile(kc_ref, kx_ref, kn_ref, ko_ref, tail_ref, is_last, t_new)
    _shifted_window_tile(vc_ref, vx_ref, vn_ref, vo_ref, tail_ref, is_last, t_new)

    @pl.when(kt == 0)
    def _():
        qbd_ref[...] = _scaled_bf16(_block_diag_queries(q_ref[...], n_heads))
        m_ref[...] = jnp.full_like(m_ref, -jnp.inf)
        l_ref[...] = jnp.zeros_like(l_ref)
        acc_ref[...] = jnp.zeros_like(acc_ref)

    def step(s, pv, bias, mult):
        s = s + bias
        m_old = m_ref[...]
        m_new = jnp.maximum(m_old, jnp.max(s, axis=-1, keepdims=True))
        alpha = jnp.exp(m_old - m_new)
        p = jnp.exp(s - m_new) * mult
        l_ref[...] = alpha * l_ref[...] + jnp.sum(p, axis=-1, keepdims=True)
        acc_ref[...] = alpha * acc_ref[...] + pv(p.astype(BF16))
        m_ref[...] = m_new

    @pl.when(kt == 0)
    def _():
        k, v = _pad_rows(kn_ref[...], LANES).astype(BF16), _pad_rows(vn_ref[...], LANES).astype(BF16)
        step(lax.dot_general(qbd_ref[...], k, _NT, preferred_element_type=F32),
             lambda p: jnp.dot(p, v, preferred_element_type=F32), bn_ref[...], mn_ref[...])

    step(jnp.dot(qbd_ref[...], kc_ref[...].astype(BF16), preferred_element_type=F32),
         lambda p: lax.dot_general(p, vc_ref[...].astype(BF16), _NT, preferred_element_type=F32),
         bo_ref[...], mo_ref[...])

    @pl.when(kt == pl.num_programs(1) - 1)
    def _():
        o_ref[...] = _take_diag(acc_ref[...] / l_ref[...], n_heads, t_new)


def _dilated_decode(q, k_new, v_new, win_kt, win_vt, t_new):
    n, width = q.shape
    n_heads = width // HEAD_DIM
    n_seq, _, n_prev = win_kt.shape
    tile = DIL_DEC_TILE
    assert n_prev == W_MAX and n_prev % tile == 0 and t_new <= LANES
    rows = n_heads * t_new
    bo, mo, bn, mn = _dilated_decode_tables(n_heads, t_new, n_prev)
    tok_spec = pl.BlockSpec((t_new, width), lambda b, t: (b, 0))
    win_spec = pl.BlockSpec((None, width, tile), lambda b, t: (b, 0, t))
    last_lane_blk = n_prev // LANES - 1
    nxt_spec = pl.BlockSpec((None, width, LANES),
                            lambda b, t: (b, 0, jnp.minimum((t + 1) * (tile // LANES), last_lane_blk)))
    old_tab = pl.BlockSpec((rows, tile), lambda b, t: (0, t))
    new_tab = pl.BlockSpec((rows, LANES), lambda b, t: (0, 0))
    win_shape = jax.ShapeDtypeStruct(win_kt.shape, F32)
    return pl.pallas_call(
        functools.partial(_dilated_decode_kernel, n_heads=n_heads, t_new=t_new),
        grid=(n_seq, n_prev // tile),
        in_specs=[tok_spec, tok_spec, tok_spec, win_spec, win_spec, nxt_spec, nxt_spec,
                  old_tab, old_tab, new_tab, new_tab],
        out_specs=[tok_spec, win_spec, win_spec],
        out_shape=[jax.ShapeDtypeStruct((n, width), F32), win_shape, win_shape],
        scratch_shapes=[pltpu.VMEM((rows, width), BF16), pltpu.VMEM((rows, 1), F32),
                        pltpu.VMEM((rows, 1), F32), pltpu.VMEM((rows, width), F32),
                        pltpu.VMEM((width, LANES), F32)],
        compiler_params=_cparams("parallel", "arbitrary"),
        name="dilated_decode",
    )(q, k_new, v_new, win_kt, win_vt, win_kt, win_vt,
      jnp.asarray(bo), jnp.asarray(mo), jnp.asarray(bn), jnp.asarray(mn))


_AB_SEGS = (("a", (0,), 0, 1.0), ("a", (1,), 0, 1.0), (None, (2,), 0, 1.0), ("b", (3,), 0, 1.0),
            ("b", (4,), 0, ATT_SCALE), (None, (5,), 0, 1.0), (None, (6,), 0, 1.0))
_C_SEGS = tuple(("a" if o < 2 else None, (o,), c, 1.0) for o in range(3) for c in (0, SEG))
_KV_OUT = (1, 2)


def _row_tile(n):
    return 512 if n % 512 == 0 else n


def _ffn_row_tile(n):
    return 1024 if n % 1024 == 0 else n


def _feature_major(x):
    lead = x.shape[:-3]
    t, h, dh = x.shape[-3:]
    nl = len(lead)
    return x.transpose(*range(nl), nl + 1, nl + 2, nl).reshape(*lead, h * dh, t)


def _token_major(xt, n_heads):
    lead = xt.shape[:-2]
    t = xt.shape[-1]
    nl = len(lead)
    return xt.reshape(*lead, n_heads, HEAD_DIM, t).transpose(*range(nl), nl + 2, nl, nl + 1)


def kernel(x_prompt, x_sample, cache_k_a, cache_v_a, page_table, state_ret, cache_win_k, cache_win_v,
           norm_mix, norm_ffn, norm_final, w_in_ab, w_out_ab, ret_gn_w, w_in_c, w_out_c,
           ffn_w_gate, ffn_w_up, ffn_w_down):
    bp, tp, d = x_prompt.shape
    bs, ts, _ = x_sample.shape
    page = cache_k_a.shape[2]
    past_len = page_table.shape[1] * page
    h_a = cache_k_a.shape[3]
    wa = h_a * HEAD_DIM
    wb = w_in_ab.shape[2] - 3 * wa
    assert wb == 4 * wa and w_in_ab.shape[0] == 1 and w_in_c.shape[0] == 1 and norm_mix.shape[0] == 2
    h_b = wa // HEAD_DIM
    h_c = w_in_c.shape[2] // (3 * HEAD_DIM)

    pos_p = jnp.arange(tp, dtype=jnp.int32)
    pos_s = past_len + jnp.arange(ts, dtype=jnp.int32)
    tile_s = lambda tabs: tuple(jnp.tile(t, (bs, 1)) for t in tabs)
    tabs = {
        "p": (_rope_tables(pos_p, ROT_DIM, ROPE_THETA), _rope_tables(pos_p, HEAD_DIM, RET_THETA)),
        "s": (tile_s(_rope_tables(pos_s, ROT_DIM, ROPE_THETA)), tile_s(_rope_tables(pos_s, HEAD_DIM, RET_THETA))),
    }
    bf = lambda w: w.astype(BF16)
    w_in_ab_bf, w_out_ab_bf, w_in_c_bf, w_out_c_bf = bf(w_in_ab[0]), bf(w_out_ab[0]), bf(w_in_c[0]), bf(w_out_c[0])
    wg, wu, wd = bf(ffn_w_gate), bf(ffn_w_up), bf(ffn_w_down)

    xp = x_prompt.reshape(bp * tp, d)
    xs = x_sample.reshape(bs * ts, d)
    tm_p, tm_s = _row_tile(bp * tp), _row_tile(bs * ts)
    tf_p = _ffn_row_tile(bp * tp)
    assert tp % tm_p == 0

    ab_widths = (wa,) * 7
    qa_p, kat_p, vat_p, qb_p, kb_p, vb_p, gb_p = _project(
        xp, norm_mix[0], w_in_ab_bf, *tabs["p"], _AB_SEGS, ab_widths, tm_p, _KV_OUT)
    qa_s, ka_s, va_s, qb_s, kb_s, vb_s, gb_s = _project(
        xs, norm_mix[0], w_in_ab_bf, *tabs["s"], _AB_SEGS, ab_widths, tm_s)

    oa_p, oa_s = _moba(qa_p, kat_p, vat_p, bp, tp, qa_s, ka_s, va_s, _feature_major(cache_k_a[0]),
                       _feature_major(cache_v_a[0]), page_table, ts)

    chunk_p = math.gcd(tp, RET_CHUNK)
    ob_p, ret_p = _retention(qb_p, kb_p, vb_p, gb_p, ret_gn_w[0],
                             jnp.zeros((bp, h_b, HEAD_DIM, HEAD_DIM), F32), bp, tp, chunk_p,
                             math.gcd(tp // chunk_p, 8))
    ob_s, ret_s = _retention(qb_s, kb_s, vb_s, gb_s, ret_gn_w[0], state_ret[0], bs, ts,
                             math.gcd(ts, RET_CHUNK), 1)

    xp = _mix_ffn([oa_p, ob_p], w_out_ab_bf, xp, norm_ffn[0], wg[0], wu[0], wd[0], norm_final, False, tf_p)
    xs = _mix_ffn([oa_s, ob_s], w_out_ab_bf, xs, norm_ffn[0], wg[0], wu[0], wd[0], norm_final, False, tm_s)

    wc = h_c * HEAD_DIM
    q_p, k_p, v_p = _project(xp, norm_mix[1], w_in_c_bf, *tabs["p"], _C_SEGS, (wc,) * 3, tm_p)
    q_s, k_s, v_s = _project(xs, norm_mix[1], w_in_c_bf, *tabs["s"], _C_SEGS, (wc,) * 3, tm_s)
    n_prev = cache_win_k.shape[2]
    o_p, kwin_p, vwin_p = _dilated_prompt(q_p, k_p, v_p, bp, tp, min(W_MAX, tp))
    o_s, win_kt, win_vt = _dilated_decode(q_s, k_s, v_s, _feature_major(cache_win_k[0]),
                                          _feature_major(cache_win_v[0]), ts)
    y_p = _mix_ffn([o_p], w_out_c_bf, xp, norm_ffn[1], wg[1], wu[1], wd[1], norm_final, True, tf_p)
    y_s = _mix_ffn([o_s], w_out_c_bf, xs, norm_ffn[1], wg[1], wu[1], wd[1], norm_final, True, tm_s)

    return (
        y_p.reshape(bp, tp, d), y_s.reshape(bs, ts, d),
        _token_major(kat_p, h_a)[None], _token_major(vat_p, h_a)[None], ret_p[None],
        _token_major(kwin_p, h_c)[None], _token_major(vwin_p, h_c)[None],
        ka_s.reshape(1, bs, ts, h_a, HEAD_DIM), va_s.reshape(1, bs, ts, h_a, HEAD_DIM), ret_s[None],
        _token_major(win_kt, h_c)[None], _token_major(win_vt, h_c)[None],
    )
```

```python
import functools
import math

import jax
import jax.numpy as jnp
import numpy as np
from jax import lax
from jax.experimental import pallas as pl
from jax.experimental.pallas import tpu as pltpu

F32 = jnp.float32
BF16 = jnp.bfloat16

HEAD_DIM = 64
LANES = 128
ROT_DIM = HEAD_DIM // 4
ROPE_THETA = 500000.0
RET_THETA = 10000.0
MOBA_BLOCK = 256
MOBA_TOPK = 3
RET_CHUNK = 128
DILATED = ((128, 1), (512, 4), (2048, 16))
W_MAX = max(w for w, _ in DILATED)
EPS = 1e-6
NEG_INF = -1e30
ATT_SCALE = HEAD_DIM ** -0.5
VMEM_LIMIT = 60 * 1024 * 1024

_NT = (((1,), (1,)), ((), ()))
_TN = (((0,), (0,)), ((), ()))


def _cparams(*sem):
    return pltpu.CompilerParams(dimension_semantics=sem, vmem_limit_bytes=VMEM_LIMIT)


def _rms(x, w):
    ms = jnp.mean(x * x, axis=-1, keepdims=True)
    return x * lax.rsqrt(ms + EPS) * w


SEG = 512


def _proj_kernel(x_ref, nw_ref, w_ref, ca_ref, na_ref, pa_ref, cb_ref, nb_ref, pb_ref, *out_refs, segs, transposed):
    xn = _rms(x_ref[...], nw_ref[...]).astype(BF16)
    for s, (kind, dests, col, post) in enumerate(segs):
        acc = jnp.dot(xn, w_ref[:, s * SEG:(s + 1) * SEG], preferred_element_type=F32)
        for c in range(SEG // LANES):
            r = acc[:, c * LANES:(c + 1) * LANES]
            if kind is not None:
                c_ref, n_ref, p_ref, shift = (ca_ref, na_ref, pa_ref, ROT_DIM // 2) if kind == "a" else (
                    cb_ref, nb_ref, pb_ref, HEAD_DIM // 2)
                r = (r * c_ref[...] + pltpu.roll(r, LANES - shift, 1) * n_ref[...]
                     + pltpu.roll(r, shift, 1) * p_ref[...])
                if post != 1.0:
                    r = r * post
            lo = col + c * LANES
            for oi in dests:
                if oi in transposed:
                    out_refs[oi][lo:lo + LANES, :] = r.T
                else:
                    out_refs[oi][:, lo:lo + LANES] = r


def _rope_tables(pos, rot_dim, theta):
    half = rot_dim // 2
    inv = theta ** (-jnp.arange(half, dtype=F32) / half)
    ang = pos.astype(F32)[:, None] * inv[None, :]
    cos, sin = jnp.cos(ang), jnp.sin(ang)
    lane = np.arange(LANES) % HEAD_DIM
    idx = lane % half
    cos_t = jnp.where(lane < rot_dim, cos[:, idx], 1.0)
    neg_t = jnp.where(lane < half, -sin[:, idx], 0.0)
    pos_t = jnp.where((lane >= half) & (lane < rot_dim), sin[:, idx], 0.0)
    return cos_t.astype(F32), neg_t.astype(F32), pos_t.astype(F32)


def _project(x, nw, w_bf, tabs_a, tabs_b, segs, out_widths, tm, transposed=()):
    n, d = x.shape
    t_tab = tabs_a[0].shape[0]
    nt = t_tab // tm
    tab_spec = pl.BlockSpec((tm, LANES), lambda i: (i % nt, 0))
    out_specs, out_shape = [], []
    for oi, w in enumerate(out_widths):
        if oi in transposed:
            out_specs.append(pl.BlockSpec((None, w, tm), lambda i: (i // nt, 0, i % nt)))
            out_shape.append(jax.ShapeDtypeStruct((n // t_tab, w, t_tab), F32))
        else:
            out_specs.append(pl.BlockSpec((tm, w), lambda i: (i, 0)))
            out_shape.append(jax.ShapeDtypeStruct((n, w), F32))
    return pl.pallas_call(
        functools.partial(_proj_kernel, segs=segs, transposed=tuple(transposed)),
        grid=(n // tm,),
        in_specs=[pl.BlockSpec((tm, d), lambda i: (i, 0)),
                  pl.BlockSpec((1, d), lambda i: (0, 0)),
                  pl.BlockSpec(w_bf.shape, lambda i: (0, 0))] + [tab_spec] * 6,
        out_specs=out_specs,
        out_shape=out_shape,
        compiler_params=_cparams("parallel"),
        name="rms_proj_rope",
    )(x, nw.reshape(1, d), w_bf, *tabs_a, *tabs_b)


FFN_CHUNKS = 11


def _mix_ffn_kernel(*refs, nparts, final):
    parts = refs[:nparts]
    wo_ref, x_ref, nw_ref, wg_ref, wu_ref, wd_ref, fw_ref, o_ref = refs[nparts:]
    a = jnp.concatenate([p[...].astype(BF16) for p in parts], axis=1) if nparts > 1 else parts[0][...].astype(BF16)
    x = x_ref[...] + jnp.dot(a, wo_ref[...], preferred_element_type=F32)
    xn = _rms(x, nw_ref[...]).astype(BF16)
    tf = wg_ref.shape[1] // FFN_CHUNKS
    acc = x
    for c in range(FFN_CHUNKS):
        g = jnp.dot(xn, wg_ref[:, c * tf:(c + 1) * tf], preferred_element_type=F32)
        u = jnp.dot(xn, wu_ref[:, c * tf:(c + 1) * tf], preferred_element_type=F32)
        h = (g * jax.nn.sigmoid(g) * u).astype(BF16)
        acc = acc + jnp.dot(h, wd_ref[c * tf:(c + 1) * tf, :], preferred_element_type=F32)
    if final:
        acc = _rms(acc, fw_ref[...])
    o_ref[...] = acc


def _mix_ffn(parts, wo, x, nw, wg, wu, wd, fw, final, tm):
    n, d = x.shape
    dff = wg.shape[1]
    const = lambda i: (0, 0)
    row = lambda w: pl.BlockSpec((tm, w), lambda i: (i, 0))
    return pl.pallas_call(
        functools.partial(_mix_ffn_kernel, nparts=len(parts), final=final),
        grid=(n // tm,),
        in_specs=[row(p.shape[1]) for p in parts]
        + [pl.BlockSpec(wo.shape, const), row(d), pl.BlockSpec((1, d), const),
           pl.BlockSpec((d, dff), const), pl.BlockSpec((d, dff), const), pl.BlockSpec((dff, d), const),
           pl.BlockSpec((1, d), const)],
        out_specs=row(d),
        out_shape=jax.ShapeDtypeStruct((n, d), F32),
        compiler_params=_cparams("parallel"),
        name="outproj_swiglu",
    )(*parts, wo, x, nw.reshape(1, d), wg, wu, wd, fw.reshape(1, d))


ATT_TILE = 256
ATT_GROUP = 4
MOBA_SUBTILES = 4
ATT_CHUNK = 64


def _unit_scores(kb, qb, bias_fn):
    s = jnp.dot(kb, qb, preferred_element_type=F32)
    chunks = [s[r:r + ATT_CHUNK] + bias_fn(r, r + ATT_CHUNK) for r in range(0, s.shape[0], ATT_CHUNK)]
    return chunks, jnp.max(functools.reduce(jnp.maximum, chunks), axis=0, keepdims=True)


ONES_ROWS = 16


def _with_ones(vt):
    ones = jnp.ones((ONES_ROWS, vt.shape[1]), BF16)
    return jnp.stack([jnp.concatenate([vt[h * HEAD_DIM:(h + 1) * HEAD_DIM], ones], axis=0) for h in range(2)])


def _unit_values(chunks, m_u, vts, mult_fn):
    ps = []
    for i, c in enumerate(chunks):
        p = jnp.exp((c - m_u).astype(BF16))
        if mult_fn is not None:
            p = p * mult_fn(i * ATT_CHUNK, (i + 1) * ATT_CHUNK).astype(BF16)
        ps.append(p)
    per = len(ps) // len(vts)
    pcat = [jnp.concatenate(ps[b * per:(b + 1) * per], axis=0) for b in range(len(vts))]
    nq = pcat[0].shape[1] // 2
    pv = [sum(jnp.dot(vt[h], pc[:, h * nq:(h + 1) * nq], preferred_element_type=F32) for vt, pc in zip(vts, pcat))
          for h in range(2)]
    return (jnp.concatenate([x[HEAD_DIM:HEAD_DIM + 1] for x in pv], axis=1),
            jnp.concatenate([x[:HEAD_DIM] for x in pv], axis=1))


def _attend_unit(kb, q2, bias_fn, vts, mult_fn=None):
    chunks, m_u = _unit_scores(kb, q2, bias_fn)
    return (m_u,) + _unit_values(chunks, m_u, vts, mult_fn)


def _merge(parts):
    m = functools.reduce(jnp.maximum, [p[0] for p in parts])
    ws = [jnp.exp(p[0] - m) for p in parts]
    return m, sum(w * p[1] for w, p in zip(ws, parts)), sum(w * p[2] for w, p in zip(ws, parts))


def _head_queries(q_t, feat):
    return [jnp.where(feat // HEAD_DIM == hh, q_t, 0.0) for hh in range(2)]


def _scaled_bf16(q):
    return (q * ATT_SCALE).astype(BF16)


def _both(x):
    return jnp.concatenate([x, x], axis=1)


def _pick_heads(x, feat):
    nq = x.shape[1] // 2
    return jnp.concatenate([x[:, :nq], x[:, nq:]], axis=0)


def _moba_prompt_step(step, q_ref, kt_ref, vt_ref, o_ref, kb_ref, vb_ref, kmean_ref, bias_ref, causal_ref, *, nblk):
    blk, grp = ATT_TILE, ATT_GROUP
    nsub = MOBA_SUBTILES

    @pl.when(step == 0)
    def _():
        for n in range(nblk):
            kblk = kt_ref[:, n * blk:(n + 1) * blk].T
            kmean_ref[n:n + 1, :] = jnp.sum(kblk, axis=0, keepdims=True) * (1.0 / blk)
            kb_ref[n * blk:(n + 1) * blk, :] = kblk.astype(BF16)
            vb_ref[n] = _with_ones(vt_ref[:, n * blk:(n + 1) * blk].astype(BF16))
        key_i = lax.broadcasted_iota(jnp.int32, (blk, 2 * blk), 0)
        qry_i = lax.broadcasted_iota(jnp.int32, (blk, 2 * blk), 1) % blk
        causal_ref[...] = jnp.where(key_i <= qry_i, 0.0, NEG_INF)

    feat = lax.broadcasted_iota(jnp.int32, (LANES, blk), 0)
    q2s = [jnp.concatenate(_head_queries(q_ref[sub * blk:(sub + 1) * blk, :].T, feat), axis=1) for sub in range(nsub)]
    qb2s = [_scaled_bf16(q2) for q2 in q2s]
    cols = nsub * 2 * blk
    blk_id = lax.broadcasted_iota(jnp.int32, (nblk, cols), 0)
    blk_f = blk_id.astype(F32)
    qt = step * nsub + lax.broadcasted_iota(jnp.int32, (1, cols), 1) // (2 * blk)
    sc = jnp.dot(kmean_ref[...], jnp.concatenate(q2s, axis=1), preferred_element_type=F32,
                 precision=lax.Precision.HIGHEST)
    sc = jnp.where(blk_id < qt, sc, NEG_INF)
    sel = jnp.zeros((nblk, cols), F32)
    for r in range(MOBA_TOPK):
        mx = jnp.max(sc, axis=0, keepdims=True)
        first = jnp.min(jnp.where(sc == mx, blk_f, float(nblk)), axis=0, keepdims=True)
        pick = blk_f == first
        sel = jnp.maximum(sel, jnp.where(pick & (r < qt), 1.0, 0.0))
        sc = jnp.where(pick, -jnp.inf, sc)
    bias = jnp.where((sel > 0.0) | (blk_id == qt), 0.0, NEG_INF)
    for sub in range(nsub):
        bias_ref[sub] = bias[:, sub * 2 * blk:(sub + 1) * 2 * blk]

    def group(sub, g, with_own):
        own_j = (step * nsub) % grp + sub
        nb = grp if not with_own else (sub + 1 if grp == nsub else grp)

        def bias_fn(r0, r1):
            j, r = r0 // blk, r0 % blk
            row = bias_ref[sub, pl.ds(g * grp + j, 1), :]
            if not with_own:
                return row
            return row + jnp.where(j == own_j, causal_ref[r:r + (r1 - r0), :], 0.0)

        kb = kb_ref[pl.ds(pl.multiple_of(g * grp * blk, blk), nb * blk), :]
        return _attend_unit(kb, qb2s[sub], bias_fn, [vb_ref[g * grp + b] for b in range(nb)])

    g_own = step * nsub // grp

    def body(g, carry):
        return sum((_merge([carry[3 * sub:3 * sub + 3], group(sub, g, False)]) for sub in range(nsub)), ())

    carry = lax.fori_loop(0, g_own, body, sum((group(sub, g_own, True) for sub in range(nsub)), ()))
    for sub in range(nsub):
        o_ref[sub * blk:(sub + 1) * blk, :] = _pick_heads(carry[3 * sub + 2] / carry[3 * sub + 1], feat).T


def _dilated_multiplicity(delta):
    c = np.zeros(delta.shape, np.int32)
    for window, dil in DILATED:
        c += ((delta >= 0) & (delta <= window) & (delta % dil == 0)).astype(np.int32)
    return c


DIL_NEAR = 2
DIL_FAR_CLASSES_PER_STEP = 8
assert DILATED[-1][1] % DIL_FAR_CLASSES_PER_STEP == 0
DIL_NEAR_SUBTILES = 8
DIL_FAR_STRIDE = DILATED[-1][1]
DIL_FAR_REACH = DILATED[-1][0] // DIL_FAR_STRIDE
assert DILATED[-2][0] <= ATT_TILE * DIL_NEAR and ATT_TILE % DIL_FAR_STRIDE == 0 and DIL_FAR_REACH <= ATT_TILE


def _mask_bias(valid):
    return np.where(valid, 0.0, NEG_INF).astype(np.float32)


def _dilated_near_tables():
    ki = np.arange(ATT_TILE)[:, None]
    qi = np.arange(ATT_TILE)[None, :]
    c = np.stack([_dilated_multiplicity(ATT_TILE * (DIL_NEAR - i) + qi - ki) for i in range(DIL_NEAR + 1)])
    return _mask_bias(c > 0), c.astype(np.float32)


def _dilated_far_tables():
    per_tile = ATT_TILE // DIL_FAR_STRIDE

    def table(k0, nk, q0):
        ka = k0 + np.arange(nk)[:, None]
        qa = q0 + np.arange(ATT_TILE)[None, :]
        return _mask_bias((qa - ka <= DIL_FAR_REACH) & (ka // per_tile < qa // per_tile - DIL_NEAR))

    return table(0, ATT_TILE, 0), table(ATT_TILE - DIL_FAR_REACH, ATT_TILE + DIL_FAR_REACH, ATT_TILE)


def _dilated_far_kernel(q_ref, k_ref, v_ref, b0_ref, b1_ref, o_ref, lse_ref, *, seq):
    stride, blk, reach = DIL_FAR_STRIDE, ATT_TILE, DIL_FAR_REACH
    ntile = seq // stride // blk
    feat = lax.broadcasted_iota(jnp.int32, (LANES, blk), 0)

    def one_class(cls):
        for tau in range(ntile):
            k0, nk, tab = (0, blk, b0_ref) if tau == 0 else (tau * blk - reach, blk + reach, b1_ref)
            rows_q = pl.ds(cls + stride * blk * tau, blk, stride=stride)
            rows_k = pl.ds(cls + stride * k0, nk, stride=stride)
            qb2 = _scaled_bf16(jnp.concatenate(_head_queries(q_ref[rows_q, :].T, feat), axis=1))
            kb = k_ref[rows_k, :].astype(BF16)
            vt = _with_ones(v_ref[rows_k, :].T.astype(BF16))
            m_u, l_u, pv = _attend_unit(kb, qb2, lambda r0, r1: _both(tab[r0:r1, :]), [vt])
            o_ref[rows_q, :] = _pick_heads(pv / l_u, feat).T
            lse_ref[rows_q, :] = _pick_heads(jnp.broadcast_to(m_u + jnp.log(l_u), pv.shape), feat).T

    def body(i, carry):
        for c in range(DIL_FAR_CLASSES_PER_STEP):
            one_class(i * DIL_FAR_CLASSES_PER_STEP + c)
        return carry

    lax.fori_loop(0, stride // DIL_FAR_CLASSES_PER_STEP, body, 0)


def _dilated_near_kernel(q_ref, k_ref, v_ref, far_ref, lse_ref, bias_ref, mult_ref, o_ref, kwin_ref, vwin_ref,
                         kb_ref, vb_ref, tab_ref, *, nblk):
    blk, near, nsub = ATT_TILE, DIL_NEAR, DIL_NEAR_SUBTILES
    step = pl.program_id(2)
    keep_blks = kwin_ref.shape[1] // blk

    @pl.when(step == 0)
    def _():
        kb_ref[0:near * blk, :] = jnp.zeros((near * blk, LANES), BF16)
        for n in range(near):
            vb_ref[n] = jnp.zeros((2, HEAD_DIM + ONES_ROWS, blk), BF16)
        for n in range(nblk):
            kblk = k_ref[n * blk:(n + 1) * blk, :]
            vblk_t = v_ref[n * blk:(n + 1) * blk, :].T
            kb_ref[(n + near) * blk:(n + near + 1) * blk, :] = kblk.astype(BF16)
            vb_ref[n + near] = _with_ones(vblk_t.astype(BF16))
            if n >= nblk - keep_blks:
                w = n - (nblk - keep_blks)
                kwin_ref[:, w * blk:(w + 1) * blk] = kblk.T
                vwin_ref[:, w * blk:(w + 1) * blk] = vblk_t

    @pl.when(step * nsub < near + nsub)
    def _():
        for sub in range(nsub):
            for i in range(near + 1):
                tab_ref[sub, i] = jnp.where(step * nsub + sub - near + i >= 0, bias_ref[i], NEG_INF)

    feat = lax.broadcasted_iota(jnp.int32, (LANES, blk), 0)
    for sub in range(nsub):
        qt = step * nsub + sub
        rows = slice(sub * blk, (sub + 1) * blk)
        qb2 = _scaled_bf16(jnp.concatenate(_head_queries(q_ref[rows, :].T, feat), axis=1))
        m2, l2, a2 = _attend_unit(
            kb_ref[pl.ds(pl.multiple_of(qt * blk, blk), (near + 1) * blk), :], qb2,
            lambda r0, r1: _both(tab_ref[sub, r0 // blk, r0 % blk:r0 % blk + (r1 - r0), :]),
            [vb_ref[qt + i] for i in range(near + 1)],
            lambda r0, r1: _both(mult_ref[r0 // blk, r0 % blk:r0 % blk + (r1 - r0), :]))
        m_n = _pick_heads(jnp.broadcast_to(m2, a2.shape), feat)
        l_n = _pick_heads(jnp.broadcast_to(l2, a2.shape), feat)
        a_n = _pick_heads(a2, feat)
        lse_f = lse_ref[rows, :].T
        m = jnp.maximum(m_n, lse_f)
        w_n, w_f = jnp.exp(m_n - m), jnp.exp(lse_f - m)
        o_ref[rows, :] = ((a_n * w_n + far_ref[rows, :].T * w_f) / (l_n * w_n + w_f)).T


def _dilated_prompt(q, k, v, batch, seq, keep):
    n, width = q.shape
    blk, near, stride = ATT_TILE, DIL_NEAR, DIL_FAR_STRIDE
    assert seq % (blk * stride) == 0
    nblk = seq // blk
    npair = width // LANES
    const = lambda nd: (lambda *_: (0,) * nd)
    full = lambda t: pl.BlockSpec(t.shape, const(t.ndim))

    far_tabs = [jnp.asarray(t) for t in _dilated_far_tables()]
    seq_spec = pl.BlockSpec((seq, LANES), lambda b, p: (b, p))
    o_far, lse_far = pl.pallas_call(
        functools.partial(_dilated_far_kernel, seq=seq),
        grid=(batch, npair),
        in_specs=[seq_spec, seq_spec, seq_spec] + [full(t) for t in far_tabs],
        out_specs=[seq_spec, seq_spec],
        out_shape=[jax.ShapeDtypeStruct((n, width), F32)] * 2,
        compiler_params=_cparams("parallel", "parallel"),
        name="dilated_far",
    )(q, k, v, *far_tabs)

    near_tabs = [jnp.asarray(t) for t in _dilated_near_tables()]
    nsub = DIL_NEAR_SUBTILES
    assert nblk % nsub == 0 and keep % blk == 0 and keep <= seq
    steps = nblk // nsub
    kv_spec = pl.BlockSpec((seq, LANES), lambda b, p, t: (b, p))
    q_spec = pl.BlockSpec((nsub * blk, LANES), lambda b, p, t: (b * steps + t, p))
    win_spec = pl.BlockSpec((None, LANES, keep), lambda b, p, t: (b, p, 0))
    win_shape = jax.ShapeDtypeStruct((batch, width, keep), F32)
    return pl.pallas_call(
        functools.partial(_dilated_near_kernel, nblk=nblk),
        grid=(batch, npair, steps),
        in_specs=[q_spec, kv_spec, kv_spec, q_spec, q_spec] + [full(t) for t in near_tabs],
        out_specs=[q_spec, win_spec, win_spec],
        out_shape=[jax.ShapeDtypeStruct((n, width), F32), win_shape, win_shape],
        scratch_shapes=[pltpu.VMEM(((nblk + near) * blk, LANES), BF16),
                        pltpu.VMEM((nblk + near, 2, HEAD_DIM + ONES_ROWS, blk), BF16),
                        pltpu.VMEM((nsub, near + 1, blk, blk), F32)],
        compiler_params=_cparams("parallel", "parallel", "arbitrary"),
        name="dilated_near",
    )(q, k, v, o_far, lse_far, *near_tabs)


RET_ROWS = 128


def _retention_tables(n_heads, chunk):
    log_g = jnp.log1p(-jnp.exp2(-5.0 - jnp.arange(n_heads, dtype=F32)))
    i = jnp.arange(chunk, dtype=F32)
    diff = i[:, None] - i[None, :]
    causal = diff >= 0
    decay = jnp.where(causal[None], jnp.exp(jnp.where(causal, diff, 0.0)[None] * log_g[:, None, None]), 0.0)
    q_dec = jnp.exp((i + 1.0)[:, None] * log_g[None, :])
    k_dec = jnp.exp((chunk - 1.0 - i)[:, None] * log_g[None, :])
    c_dec = jnp.exp(chunk * log_g)
    pad = RET_ROWS - chunk
    npair = n_heads // 2
    decay = jnp.pad(decay, ((0, 0), (0, pad), (0, pad))).reshape(npair, 2 * RET_ROWS, RET_ROWS)
    expand = lambda t: jnp.pad(jnp.repeat(t, HEAD_DIM, axis=1), ((0, pad), (0, 0))).reshape(
        RET_ROWS, npair, LANES).transpose(1, 0, 2)
    q_tab, k_tab = expand(q_dec), expand(k_dec)
    same_head = (np.arange(LANES)[:, None] // HEAD_DIM) == (np.arange(LANES)[None, :] // HEAD_DIM)
    c_tab = jnp.where(same_head[None], jnp.repeat(c_dec, HEAD_DIM).reshape(npair, LANES, 1), 0.0)
    return decay.astype(F32), q_tab.astype(F32), k_tab.astype(F32), c_tab.astype(F32)


def _retention_kernel(q_ref, k_ref, v_ref, g_ref, gn_ref, s0_ref, dec_ref, qd_ref, kd_ref, cd_ref,
                      o_ref, sout_ref, st_ref, *, chunk, nchunk):
    c = pl.program_id(2)

    @pl.when(c == 0)
    def _():
        zero = jnp.zeros((HEAD_DIM, HEAD_DIM), F32)
        st_ref[...] = jnp.concatenate([jnp.concatenate([s0_ref[0], zero], axis=1),
                                       jnp.concatenate([zero, s0_ref[1]], axis=1)], axis=0)

    pad = RET_ROWS - chunk
    lane = lax.broadcasted_iota(jnp.int32, (RET_ROWS, LANES), 1)
    head0 = lane < HEAD_DIM
    same_head = (lax.broadcasted_iota(jnp.int32, (LANES, LANES), 0) // HEAD_DIM) == (
        lax.broadcasted_iota(jnp.int32, (LANES, LANES), 1) // HEAD_DIM)

    def rows(ref, r):
        x = ref[pl.ds(r, chunk), :]
        return x if pad == 0 else jnp.concatenate([x, jnp.zeros((pad, LANES), F32)], axis=0)

    def seg_mean(x):
        s0 = jnp.sum(jnp.where(head0, x, 0.0), axis=-1, keepdims=True)
        s1 = jnp.sum(jnp.where(head0, 0.0, x), axis=-1, keepdims=True)
        return jnp.where(head0, s0, s1) * (1.0 / HEAD_DIM)

    def one_chunk(j, state):
        r = j * chunk
        q, k, v, g = rows(q_ref, r), rows(k_ref, r), rows(v_ref, r), rows(g_ref, r)
        kb, vb = k.astype(BF16), v.astype(BF16)
        q2 = jnp.concatenate([jnp.where(head0, q, 0.0), jnp.where(head0, 0.0, q)], axis=0).astype(BF16)
        attn = lax.dot_general(q2, kb, _NT, preferred_element_type=F32) * dec_ref[...]
        inn = jnp.dot(attn.astype(BF16), vb, preferred_element_type=F32)
        inner = jnp.where(head0, inn[:RET_ROWS], inn[RET_ROWS:])
        cross = jnp.dot(q.astype(BF16), state.astype(BF16), preferred_element_type=F32) * qd_ref[...]
        upd = lax.dot_general((k * kd_ref[...]).astype(BF16), vb, _TN, preferred_element_type=F32)
        o = inner + cross
        mu = seg_mean(o)
        var = seg_mean(jnp.square(o - mu))
        y = (o - mu) * lax.rsqrt(var + EPS) * gn_ref[...]
        y = y * (g * jax.nn.sigmoid(g))
        o_ref[r:r + chunk, :] = y[:chunk]
        return state * cd_ref[...] + jnp.where(same_head, upd, 0.0)

    state = st_ref[...]
    for j in range(nchunk):
        state = one_chunk(j, state)
    st_ref[...] = state

    @pl.when(c == pl.num_programs(2) - 1)
    def _():
        sout_ref[0] = st_ref[0:HEAD_DIM, 0:HEAD_DIM]
        sout_ref[1] = st_ref[HEAD_DIM:, HEAD_DIM:]


def _retention(q, k, v, g, gn_w, s0, batch, seq, chunk, nchunk):
    n, width = q.shape
    n_heads = width // HEAD_DIM
    npair = width // LANES
    rows = chunk * nchunk
    steps = seq // rows
    dec, qd, kd, cd = _retention_tables(n_heads, chunk)
    row_spec = pl.BlockSpec((rows, LANES), lambda b, p, c: (b * steps + c, p))
    st_spec = pl.BlockSpec((None, 2, HEAD_DIM, HEAD_DIM), lambda b, p, c: (b, p, 0, 0))
    tab = lambda t: pl.BlockSpec((None,) + t.shape[1:], lambda b, p, c: (p, 0, 0))
    out, s_new = pl.pallas_call(
        functools.partial(_retention_kernel, chunk=chunk, nchunk=nchunk),
        grid=(batch, npair, steps),
        in_specs=[row_spec, row_spec, row_spec, row_spec, pl.BlockSpec((1, LANES), lambda b, p, c: (0, p)),
                  st_spec, tab(dec), tab(qd), tab(kd), tab(cd)],
        out_specs=[row_spec, st_spec],
        out_shape=[jax.ShapeDtypeStruct((n, width), F32),
                   jax.ShapeDtypeStruct((batch, n_heads, HEAD_DIM, HEAD_DIM), F32)],
        scratch_shapes=[pltpu.VMEM((LANES, LANES), F32)],
        compiler_params=_cparams("parallel", "parallel", "arbitrary"),
        name="retention",
    )(q, k, v, g, gn_w.reshape(1, width), s0, dec, qd, kd, cd)
    return out, s_new


def _block_diag_queries(q, n_heads):
    t = q.shape[0]
    rep = jnp.concatenate([q] * n_heads, axis=0)
    rows = lax.broadcasted_iota(jnp.int32, rep.shape, 0) // t
    cols = lax.broadcasted_iota(jnp.int32, rep.shape, 1) // HEAD_DIM
    return jnp.where(rows == cols, rep, 0.0)


def _take_diag(o, n_heads, t):
    cols = lax.broadcasted_iota(jnp.int32, (t, o.shape[1]), 1) // HEAD_DIM
    out = jnp.zeros((t, o.shape[1]), F32)
    for h in range(n_heads):
        out = jnp.where(cols == h, o[h * t:(h + 1) * t, :], out)
    return out


def _fold_pairs(o, n_heads, t):
    rows = lax.broadcasted_iota(jnp.int32, (o.shape[0], LANES), 0) // (2 * t)
    out = jnp.zeros((o.shape[0], LANES), F32)
    for c in range(n_heads // 2):
        out = jnp.where(rows == c, o[:, c * LANES:(c + 1) * LANES], out)
    return out


def _unfold_pairs(a, n_heads, t):
    lane = lax.broadcasted_iota(jnp.int32, (t, LANES), 1)
    return jnp.concatenate([jnp.where(lane < HEAD_DIM, a[2 * c * t:(2 * c + 1) * t], a[(2 * c + 1) * t:(2 * c + 2) * t])
                            for c in range(n_heads // 2)], axis=1)


def _pad_rows(x, rows):
    return jnp.concatenate([x, jnp.zeros((rows - x.shape[0], x.shape[1]), x.dtype)], axis=0)


MOBA_PAGES_PER_STEP = 32


def _moba_decode_step(g, is_last, q_ref, kn_ref, vn_ref, kp, vp, o_ref, qbd_ref, m_ref, l_ref, sc_ref, oblk_ref,
                      *, n_heads, t_new, pages_per_blk, nblk):
    npg = len(kp)
    rows = n_heads * t_new
    lane = lax.broadcasted_iota(jnp.int32, (rows, LANES), 1)
    page = kp[0].shape[1]
    blk_keys = pages_per_blk * page
    step_blks = npg // pages_per_blk

    @pl.when(g == 0)
    def _():
        qbd_ref[...] = _scaled_bf16(_block_diag_queries(q_ref[...], n_heads))
        m_ref[...] = jnp.zeros_like(m_ref)
        l_ref[...] = jnp.zeros_like(l_ref)
        sc_ref[...] = jnp.full_like(sc_ref, -jnp.inf)

    kt_all = jnp.concatenate([kp[i][...].astype(BF16) for i in range(npg)], axis=1)
    s_all = jnp.dot(qbd_ref[...], kt_all, preferred_element_type=F32)
    m_new, l_new, sc_new = m_ref[...], l_ref[...], sc_ref[...]
    for jb in range(step_blks):
        n = g * step_blks + jb
        s = s_all[:, jb * blk_keys:(jb + 1) * blk_keys]
        mb = jnp.max(s, axis=-1, keepdims=True)
        p = jnp.exp(s - mb)
        v_blk = jnp.concatenate([vp[jb * pages_per_blk + a][...].astype(BF16) for a in range(pages_per_blk)], axis=1)
        oblk_ref[n] = _fold_pairs(lax.dot_general(p.astype(BF16), v_blk, _NT, preferred_element_type=F32),
                                  n_heads, t_new)
        here = lane == n
        m_new = jnp.where(here, mb, m_new)
        l_new = jnp.where(here, jnp.sum(p, axis=-1, keepdims=True), l_new)
        sc_new = jnp.where(here, jnp.sum(s, axis=-1, keepdims=True) * (1.0 / blk_keys), sc_new)
    m_ref[...], l_ref[...], sc_ref[...] = m_new, l_new, sc_new

    @pl.when(is_last)
    def _():
        s = lax.dot_general(qbd_ref[...], _pad_rows(kn_ref[...], LANES).astype(BF16), _NT,
                            preferred_element_type=F32)
        qry = lax.broadcasted_iota(jnp.int32, (rows, LANES), 0) % t_new
        s = jnp.where(lane <= qry, s, NEG_INF)
        m_own = jnp.max(s, axis=-1, keepdims=True)
        p = jnp.exp(s - m_own)
        l_own = jnp.sum(p, axis=-1, keepdims=True)
        o_own = jnp.dot(p.astype(BF16), _pad_rows(vn_ref[...], LANES).astype(BF16), preferred_element_type=F32)
        sc = sc_ref[...]
        lane_f = lane.astype(F32)
        sel = lane < 0
        for _ in range(MOBA_TOPK):
            mx = jnp.max(sc, axis=-1, keepdims=True)
            first = jnp.min(jnp.where(sc == mx, lane_f, float(LANES)), axis=-1, keepdims=True)
            pick = lane_f == first
            sel = sel | pick
            sc = jnp.where(pick, -jnp.inf, sc)
        m_all = m_ref[...]
        m_fin = jnp.maximum(jnp.max(jnp.where(sel, m_all, -jnp.inf), axis=-1, keepdims=True), m_own)
        w = jnp.where(sel, jnp.exp(m_all - m_fin), 0.0)
        w_own = jnp.exp(m_own - m_fin)
        l_fin = jnp.sum(w * l_ref[...], axis=-1, keepdims=True) + w_own * l_own

        def body(n, acc):
            col = jnp.sum(jnp.where(lane == n, w, 0.0), axis=-1, keepdims=True)
            return acc + col * oblk_ref[n]

        acc = lax.fori_loop(0, nblk, body, w_own * _fold_pairs(o_own, n_heads, t_new))
        o_ref[...] = _unfold_pairs(acc / l_fin, n_heads, t_new)


def _moba_kernel(pt_ref, q_ref, kt_ref, vt_ref, qs_ref, kn_ref, vn_ref, *rest, nblk, prompt_steps, decode_steps,
                 n_heads, t_new, pages_per_blk, dec_nblk):
    npg = MOBA_PAGES_PER_STEP
    kp, vp = rest[:npg], rest[npg:2 * npg]
    o_ref, os_ref, kb_ref, vb_ref, kmean_ref, bias_ref, causal_ref, qbd_ref, m_ref, l_ref, sc_ref, oblk_ref = rest[2 * npg:]
    i = pl.program_id(0)
    _moba_prompt_step(i % prompt_steps, q_ref, kt_ref, vt_ref, o_ref, kb_ref, vb_ref, kmean_ref, bias_ref, causal_ref,
                      nblk=nblk)
    g = i % decode_steps
    _moba_decode_step(g, g == decode_steps - 1, qs_ref, kn_ref, vn_ref, kp, vp, os_ref, qbd_ref, m_ref, l_ref, sc_ref,
                      oblk_ref, n_heads=n_heads, t_new=t_new, pages_per_blk=pages_per_blk, nblk=dec_nblk)


def _moba(q, kt, vt, batch, seq, qs, k_new, v_new, cache_kt, cache_vt, page_table, t_new):
    n, width = q.shape
    blk, grp, nsub = ATT_TILE, ATT_GROUP, MOBA_SUBTILES
    assert blk == MOBA_BLOCK and seq % (blk * grp) == 0 and grp % nsub == 0
    nblk = seq // blk
    steps_p = nblk // nsub
    npair = width // LANES
    n_heads = width // HEAD_DIM
    n_seq, n_pages = page_table.shape
    page = cache_kt.shape[2]
    pages_per_blk = MOBA_BLOCK // page
    dec_nblk = n_pages // pages_per_blk
    npg = MOBA_PAGES_PER_STEP
    steps_d = n_pages // npg
    rows = n_heads * t_new
    assert n_pages % npg == 0 and npg % pages_per_blk == 0 and MOBA_TOPK <= dec_nblk <= LANES
    assert n_pages * page == dec_nblk * MOBA_BLOCK and t_new <= page and page == LANES and rows <= LANES
    assert n_heads % 2 == 0
    total = batch * npair * steps_p
    assert total == n_seq * steps_d, "prompt and decode sides must have the same number of steps"

    kv_spec = pl.BlockSpec((None, LANES, seq), lambda i, pt: (i // (npair * steps_p), (i // steps_p) % npair, 0),
                           pipeline_mode=pl.Buffered(1))
    q_spec = pl.BlockSpec((nsub * blk, LANES),
                          lambda i, pt: ((i // (npair * steps_p)) * steps_p + i % steps_p, (i // steps_p) % npair))
    tok_spec = pl.BlockSpec((t_new, width), lambda i, pt: (i // steps_d, 0))
    page_specs = [pl.BlockSpec((None, width, page), functools.partial(
        lambda i, pt, j: (pt[i // steps_d, (i % steps_d) * npg + j], 0, 0), j=j)) for j in range(npg)]
    grid_spec = pltpu.PrefetchScalarGridSpec(
        num_scalar_prefetch=1,
        grid=(total,),
        in_specs=[q_spec, kv_spec, kv_spec, tok_spec, tok_spec, tok_spec] + page_specs + page_specs,
        out_specs=[q_spec, tok_spec],
        scratch_shapes=[pltpu.VMEM((nblk * blk, LANES), BF16), pltpu.VMEM((nblk, 2, HEAD_DIM + ONES_ROWS, blk), BF16),
                        pltpu.VMEM((nblk, LANES), F32), pltpu.VMEM((nsub, nblk, 2 * blk), F32),
                        pltpu.VMEM((blk, 2 * blk), F32),
                        pltpu.VMEM((rows, width), BF16), pltpu.VMEM((rows, LANES), F32),
                        pltpu.VMEM((rows, LANES), F32), pltpu.VMEM((rows, LANES), F32),
                        pltpu.VMEM((dec_nblk, rows, LANES), F32)],
    )
    return pl.pallas_call(
        functools.partial(_moba_kernel, nblk=nblk, prompt_steps=steps_p, decode_steps=steps_d, n_heads=n_heads,
                          t_new=t_new, pages_per_blk=pages_per_blk, dec_nblk=dec_nblk),
        grid_spec=grid_spec,
        out_shape=[jax.ShapeDtypeStruct((n, width), F32), jax.ShapeDtypeStruct(qs.shape, F32)],
        compiler_params=_cparams("arbitrary"),
        name="moba",
    )(page_table, q, kt, vt, qs, k_new, v_new, *([cache_kt] * npg), *([cache_vt] * npg))


DIL_DEC_TILE = 1024


def _dilated_decode_tables(n_heads, t_new, n_prev):
    qi = (np.arange(n_heads * t_new) % t_new)[:, None]
    c_old = _dilated_multiplicity(n_prev + qi - np.arange(n_prev)[None, :])
    c_new = _dilated_multiplicity(qi - np.arange(LANES)[None, :])
    c_new = np.where(np.arange(LANES)[None, :] < t_new, c_new, 0)
    f = lambda c: (np.where(c > 0, 0.0, NEG_INF).astype(np.float32), c.astype(np.float32))
    return f(c_old) + f(c_new)


def _shifted_window_tile(cur_ref, nxt_ref, new_ref, out_ref, tail_ref, is_last, t_new):
    lane = lax.broadcasted_iota(jnp.int32, (cur_ref.shape[0], LANES), 1)
    nblk = cur_ref.shape[1] // LANES

    @pl.when(is_last)
    def _():
        tail_ref[...] = _pad_rows(new_ref[...], LANES).T

    @pl.when(jnp.logical_not(is_last))
    def _():
        tail_ref[...] = nxt_ref[...]

    rolled = [pltpu.roll(cur_ref[:, c * LANES:(c + 1) * LANES], LANES - t_new, 1) for c in range(nblk)]
    rolled.append(pltpu.roll(tail_ref[...], LANES - t_new, 1))
    for c in range(nblk):
        out_ref[:, c * LANES:(c + 1) * LANES] = jnp.where(lane < LANES - t_new, rolled[c], rolled[c + 1])


def _dilated_decode_kernel(q_ref, kn_ref, vn_ref, kc_ref, vc_ref, kx_ref, vx_ref, bo_ref, mo_ref, bn_ref, mn_ref,
                           o_ref, ko_ref, vo_ref, qbd_ref, m_ref, l_ref, acc_ref, tail_ref, *, n_heads, t_new):
    kt = pl.program_id(1)
    is_last = kt == pl.num_programs(1) - 1
    _shifted_window_tile(kc_ref, kx_ref, kn_ref, ko_ref, tail_ref, is_last, t_new)
    _shifted_window_tile(vc_ref, vx_ref, vn_ref, vo_ref, tail_ref, is_last, t_new)

    @pl.when(kt == 0)
    def _():
        qbd_ref[...] = _scaled_bf16(_block_diag_queries(q_ref[...], n_heads))
        m_ref[...] = jnp.full_like(m_ref, -jnp.inf)
        l_ref[...] = jnp.zeros_like(l_ref)
        acc_ref[...] = jnp.zeros_like(acc_ref)

    def step(s, pv, bias, mult):
        s = s + bias
        m_old = m_ref[...]
        m_new = jnp.maximum(m_old, jnp.max(s, axis=-1, keepdims=True))
        alpha = jnp.exp(m_old - m_new)
        p = jnp.exp(s - m_new) * mult
        l_ref[...] = alpha * l_ref[...] + jnp.sum(p, axis=-1, keepdims=True)
        acc_ref[...] = alpha * acc_ref[...] + pv(p.astype(BF16))
        m_ref[...] = m_new

    @pl.when(kt == 0)
    def _():
        k, v = _pad_rows(kn_ref[...], LANES).astype(BF16), _pad_rows(vn_ref[...], LANES).astype(BF16)
        step(lax.dot_general(qbd_ref[...], k, _NT, preferred_element_type=F32),
             lambda p: jnp.dot(p, v, preferred_element_type=F32), bn_ref[...], mn_ref[...])

    step(jnp.dot(qbd_ref[...], kc_ref[...].astype(BF16), preferred_element_type=F32),
         lambda p: lax.dot_general(p, vc_ref[...].astype(BF16), _NT, preferred_element_type=F32),
         bo_ref[...], mo_ref[...])

    @pl.when(kt == pl.num_programs(1) - 1)
    def _():
        o_ref[...] = _take_diag(acc_ref[...] / l_ref[...], n_heads, t_new)


def _dilated_decode(q, k_new, v_new, win_kt, win_vt, t_new):
    n, width = q.shape
    n_heads = width // HEAD_DIM
    n_seq, _, n_prev = win_kt.shape
    tile = DIL_DEC_TILE
    assert n_prev == W_MAX and n_prev % tile == 0 and t_new <= LANES
    rows = n_heads * t_new
    bo, mo, bn, mn = _dilated_decode_tables(n_heads, t_new, n_prev)
    tok_spec = pl.BlockSpec((t_new, width), lambda b, t: (b, 0))
    win_spec = pl.BlockSpec((None, width, tile), lambda b, t: (b, 0, t))
    last_lane_blk = n_prev // LANES - 1
    nxt_spec = pl.BlockSpec((None, width, LANES),
                            lambda b, t: (b, 0, jnp.minimum((t + 1) * (tile // LANES), last_lane_blk)))
    old_tab = pl.BlockSpec((rows, tile), lambda b, t: (0, t))
    new_tab = pl.BlockSpec((rows, LANES), lambda b, t: (0, 0))
    win_shape = jax.ShapeDtypeStruct(win_kt.shape, F32)
    return pl.pallas_call(
        functools.partial(_dilated_decode_kernel, n_heads=n_heads, t_new=t_new),
        grid=(n_seq, n_prev // tile),
        in_specs=[tok_spec, tok_spec, tok_spec, win_spec, win_spec, nxt_spec, nxt_spec,
                  old_tab, old_tab, new_tab, new_tab],
        out_specs=[tok_spec, win_spec, win_spec],
        out_shape=[jax.ShapeDtypeStruct((n, width), F32), win_shape, win_shape],
        scratch_shapes=[pltpu.VMEM((rows, width), BF16), pltpu.VMEM((rows, 1), F32),
                        pltpu.VMEM((rows, 1), F32), pltpu.VMEM((rows, width), F32),
                        pltpu.VMEM((width, LANES), F32)],
        compiler_params=_cparams("parallel", "arbitrary"),
        name="dilated_decode",
    )(q, k_new, v_new, win_kt, win_vt, win_kt, win_vt,
      jnp.asarray(bo), jnp.asarray(mo), jnp.asarray(bn), jnp.asarray(mn))


_AB_SEGS = (("a", (0,), 0, 1.0), ("a", (1,), 0, 1.0), (None, (2,), 0, 1.0), ("b", (3,), 0, 1.0),
            ("b", (4,), 0, ATT_SCALE), (None, (5,), 0, 1.0), (None, (6,), 0, 1.0))
_C_SEGS = tuple(("a" if o < 2 else None, (o,), c, 1.0) for o in range(3) for c in (0, SEG))
_KV_OUT = (1, 2)


def _row_tile(n):
    return 512 if n % 512 == 0 else n


def _ffn_row_tile(n):
    return 1024 if n % 1024 == 0 else n


def _feature_major(x):
    lead = x.shape[:-3]
    t, h, dh = x.shape[-3:]
    nl = len(lead)
    return x.transpose(*range(nl), nl + 1, nl + 2, nl).reshape(*lead, h * dh, t)


def _token_major(xt, n_heads):
    lead = xt.shape[:-2]
    t = xt.shape[-1]
    nl = len(lead)
    return xt.reshape(*lead, n_heads, HEAD_DIM, t).transpose(*range(nl), nl + 2, nl, nl + 1)


def kernel(x_prompt, x_sample, cache_k_a, cache_v_a, page_table, state_ret, cache_win_k, cache_win_v,
           norm_mix, norm_ffn, norm_final, w_in_ab, w_out_ab, ret_gn_w, w_in_c, w_out_c,
           ffn_w_gate, ffn_w_up, ffn_w_down):
    bp, tp, d = x_prompt.shape
    bs, ts, _ = x_sample.shape
    page = cache_k_a.shape[2]
    past_len = page_table.shape[1] * page
    h_a = cache_k_a.shape[3]
    wa = h_a * HEAD_DIM
    wb = w_in_ab.shape[2] - 3 * wa
    assert wb == 4 * wa and w_in_ab.shape[0] == 1 and w_in_c.shape[0] == 1 and norm_mix.shape[0] == 2
    h_b = wa // HEAD_DIM
    h_c = w_in_c.shape[2] // (3 * HEAD_DIM)

    pos_p = jnp.arange(tp, dtype=jnp.int32)
    pos_s = past_len + jnp.arange(ts, dtype=jnp.int32)
    tile_s = lambda tabs: tuple(jnp.tile(t, (bs, 1)) for t in tabs)
    tabs = {
        "p": (_rope_tables(pos_p, ROT_DIM, ROPE_THETA), _rope_tables(pos_p, HEAD_DIM, RET_THETA)),
        "s": (tile_s(_rope_tables(pos_s, ROT_DIM, ROPE_THETA)), tile_s(_rope_tables(pos_s, HEAD_DIM, RET_THETA))),
    }
    bf = lambda w: w.astype(BF16)
    w_in_ab_bf, w_out_ab_bf, w_in_c_bf, w_out_c_bf = bf(w_in_ab[0]), bf(w_out_ab[0]), bf(w_in_c[0]), bf(w_out_c[0])
    wg, wu, wd = bf(ffn_w_gate), bf(ffn_w_up), bf(ffn_w_down)

    xp = x_prompt.reshape(bp * tp, d)
    xs = x_sample.reshape(bs * ts, d)
    tm_p, tm_s = _row_tile(bp * tp), _row_tile(bs * ts)
    tf_p = _ffn_row_tile(bp * tp)
    assert tp % tm_p == 0

    ab_widths = (wa,) * 7
    qa_p, kat_p, vat_p, qb_p, kb_p, vb_p, gb_p = _project(
        xp, norm_mix[0], w_in_ab_bf, *tabs["p"], _AB_SEGS, ab_widths, tm_p, _KV_OUT)
    qa_s, ka_s, va_s, qb_s, kb_s, vb_s, gb_s = _project(
        xs, norm_mix[0], w_in_ab_bf, *tabs["s"], _AB_SEGS, ab_widths, tm_s)

    oa_p, oa_s = _moba(qa_p, kat_p, vat_p, bp, tp, qa_s, ka_s, va_s, _feature_major(cache_k_a[0]),
                       _feature_major(cache_v_a[0]), page_table, ts)

    chunk_p = math.gcd(tp, RET_CHUNK)
    ob_p, ret_p = _retention(qb_p, kb_p, vb_p, gb_p, ret_gn_w[0],
                             jnp.zeros((bp, h_b, HEAD_DIM, HEAD_DIM), F32), bp, tp, chunk_p,
                             math.gcd(tp // chunk_p, 8))
    ob_s, ret_s = _retention(qb_s, kb_s, vb_s, gb_s, ret_gn_w[0], state_ret[0], bs, ts,
                             math.gcd(ts, RET_CHUNK), 1)

    xp = _mix_ffn([oa_p, ob_p], w_out_ab_bf, xp, norm_ffn[0], wg[0], wu[0], wd[0], norm_final, False, tf_p)
    xs = _mix_ffn([oa_s, ob_s], w_out_ab_bf, xs, norm_ffn[0], wg[0], wu[0], wd[0], norm_final, False, tm_s)

    wc = h_c * HEAD_DIM
    q_p, k_p, v_p = _project(xp, norm_mix[1], w_in_c_bf, *tabs["p"], _C_SEGS, (wc,) * 3, tm_p)
    q_s, k_s, v_s = _project(xs, norm_mix[1], w_in_c_bf, *tabs["s"], _C_SEGS, (wc,) * 3, tm_s)
    n_prev = cache_win_k.shape[2]
    o_p, kwin_p, vwin_p = _dilated_prompt(q_p, k_p, v_p, bp, tp, min(W_MAX, tp))
    o_s, win_kt, win_vt = _dilated_decode(q_s, k_s, v_s, _feature_major(cache_win_k[0]),
                                          _feature_major(cache_win_v[0]), ts)
    y_p = _mix_ffn([o_p], w_out_c_bf, xp, norm_ffn[1], wg[1], wu[1], wd[1], norm_final, True, tf_p)
    y_s = _mix_ffn([o_s], w_out_c_bf, xs, norm_ffn[1], wg[1], wu[1], wd[1], norm_final, True, tm_s)

    return (
        y_p.reshape(bp, tp, d), y_s.reshape(bs, ts, d),
        _token_major(kat_p, h_a)[None], _token_major(vat_p, h_a)[None], ret_p[None],
        _token_major(kwin_p, h_c)[None], _token_major(vwin_p, h_c)[None],
        ka_s.reshape(1, bs, ts, h_a, HEAD_DIM), va_s.reshape(1, bs, ts, h_a, HEAD_DIM), ret_s[None],
        _token_major(win_kt, h_c)[None], _token_major(win_vt, h_c)[None],
    )
```

```python
import functools
import math

import jax
import jax.numpy as jnp
import numpy as np
from jax import lax
from jax.experimental import pallas as pl
from jax.experimental.pallas import tpu as pltpu

F32 = jnp.float32
BF16 = jnp.bfloat16

HEAD_DIM = 64
LANES = 128
ROT_DIM = HEAD_DIM // 4
ROPE_THETA = 500000.0
RET_THETA = 10000.0
MOBA_BLOCK = 256
MOBA_TOPK = 3
RET_CHUNK = 128
DILATED = ((128, 1), (512, 4), (2048, 16))
W_MAX = max(w for w, _ in DILATED)
EPS = 1e-6
NEG_INF = -1e30
ATT_SCALE = HEAD_DIM ** -0.5
VMEM_LIMIT = 60 * 1024 * 1024

_NT = (((1,), (1,)), ((), ()))
_TN = (((0,), (0,)), ((), ()))


def _cparams(*sem):
    return pltpu.CompilerParams(dimension_semantics=sem, vmem_limit_bytes=VMEM_LIMIT)


def _rms(x, w):
    ms = jnp.mean(x * x, axis=-1, keepdims=True)
    return x * lax.rsqrt(ms + EPS) * w


SEG = 512


def _proj_kernel(x_ref, nw_ref, w_ref, ca_ref, na_ref, pa_ref, cb_ref, nb_ref, pb_ref, *out_refs, segs, transposed):
    xn = _rms(x_ref[...], nw_ref[...]).astype(BF16)
    for s, (kind, dests, col, post) in enumerate(segs):
        acc = jnp.dot(xn, w_ref[:, s * SEG:(s + 1) * SEG], preferred_element_type=F32)
        for c in range(SEG // LANES):
            r = acc[:, c * LANES:(c + 1) * LANES]
            if kind is not None:
                c_ref, n_ref, p_ref, shift = (ca_ref, na_ref, pa_ref, ROT_DIM // 2) if kind == "a" else (
                    cb_ref, nb_ref, pb_ref, HEAD_DIM // 2)
                r = (r * c_ref[...] + pltpu.roll(r, LANES - shift, 1) * n_ref[...]
                     + pltpu.roll(r, shift, 1) * p_ref[...])
                if post != 1.0:
                    r = r * post
            lo = col + c * LANES
            for oi in dests:
                if oi in transposed:
                    out_refs[oi][lo:lo + LANES, :] = r.T
                else:
                    out_refs[oi][:, lo:lo + LANES] = r


def _rope_tables(pos, rot_dim, theta):
    half = rot_dim // 2
    inv = theta ** (-jnp.arange(half, dtype=F32) / half)
    ang = pos.astype(F32)[:, None] * inv[None, :]
    cos, sin = jnp.cos(ang), jnp.sin(ang)
    lane = np.arange(LANES) % HEAD_DIM
    idx = lane % half
    cos_t = jnp.where(lane < rot_dim, cos[:, idx], 1.0)
    neg_t = jnp.where(lane < half, -sin[:, idx], 0.0)
    pos_t = jnp.where((lane >= half) & (lane < rot_dim), sin[:, idx], 0.0)
    return cos_t.astype(F32), neg_t.astype(F32), pos_t.astype(F32)


def _project(x, nw, w_bf, tabs_a, tabs_b, segs, out_widths, tm, transposed=()):
    n, d = x.shape
    t_tab = tabs_a[0].shape[0]
    nt = t_tab // tm
    tab_spec = pl.BlockSpec((tm, LANES), lambda i: (i % nt, 0))
    out_specs, out_shape = [], []
    for oi, w in enumerate(out_widths):
        if oi in transposed:
            out_specs.append(pl.BlockSpec((None, w, tm), lambda i: (i // nt, 0, i % nt)))
            out_shape.append(jax.ShapeDtypeStruct((n // t_tab, w, t_tab), F32))
        else:
            out_specs.append(pl.BlockSpec((tm, w), lambda i: (i, 0)))
            out_shape.append(jax.ShapeDtypeStruct((n, w), F32))
    return pl.pallas_call(
        functools.partial(_proj_kernel, segs=segs, transposed=tuple(transposed)),
        grid=(n // tm,),
        in_specs=[pl.BlockSpec((tm, d), lambda i: (i, 0)),
                  pl.BlockSpec((1, d), lambda i: (0, 0)),
                  pl.BlockSpec(w_bf.shape, lambda i: (0, 0))] + [tab_spec] * 6,
        out_specs=out_specs,
        out_shape=out_shape,
        compiler_params=_cparams("parallel"),
        name="rms_proj_rope",
    )(x, nw.reshape(1, d), w_bf, *tabs_a, *tabs_b)


FFN_CHUNKS = 11


def _mix_ffn_kernel(*refs, nparts, final):
    parts = refs[:nparts]
    wo_ref, x_ref, nw_ref, wg_ref, wu_ref, wd_ref, fw_ref, o_ref = refs[nparts:]
    a = jnp.concatenate([p[...].astype(BF16) for p in parts], axis=1) if nparts > 1 else parts[0][...].astype(BF16)
    x = x_ref[...] + jnp.dot(a, wo_ref[...], preferred_element_type=F32)
    xn = _rms(x, nw_ref[...]).astype(BF16)
    tf = wg_ref.shape[1] // FFN_CHUNKS
    acc = x
    for c in range(FFN_CHUNKS):
        g = jnp.dot(xn, wg_ref[:, c * tf:(c + 1) * tf], preferred_element_type=F32)
        u = jnp.dot(xn, wu_ref[:, c * tf:(c + 1) * tf], preferred_element_type=F32)
        h = (g * jax.nn.sigmoid(g) * u).astype(BF16)
        acc = acc + jnp.dot(h, wd_ref[c * tf:(c + 1) * tf, :], preferred_element_type=F32)
    if final:
        acc = _rms(acc, fw_ref[...])
    o_ref[...] = acc


def _mix_ffn(parts, wo, x, nw, wg, wu, wd, fw, final, tm):
    n, d = x.shape
    dff = wg.shape[1]
    const = lambda i: (0, 0)
    row = lambda w: pl.BlockSpec((tm, w), lambda i: (i, 0))
    return pl.pallas_call(
        functools.partial(_mix_ffn_kernel, nparts=len(parts), final=final),
        grid=(n // tm,),
        in_specs=[row(p.shape[1]) for p in parts]
        + [pl.BlockSpec(wo.shape, const), row(d), pl.BlockSpec((1, d), const),
           pl.BlockSpec((d, dff), const), pl.BlockSpec((d, dff), const), pl.BlockSpec((dff, d), const),
           pl.BlockSpec((1, d), const)],
        out_specs=row(d),
        out_shape=jax.ShapeDtypeStruct((n, d), F32),
        compiler_params=_cparams("parallel"),
        name="outproj_swiglu",
    )(*parts, wo, x, nw.reshape(1, d), wg, wu, wd, fw.reshape(1, d))


ATT_TILE = 256
ATT_GROUP = 4
MOBA_SUBTILES = 4
ATT_CHUNK = 64


def _unit_scores(kb, qb, bias_fn):
    s = jnp.dot(kb, qb, preferred_element_type=F32)
    chunks = [s[r:r + ATT_CHUNK] + bias_fn(r, r + ATT_CHUNK) for r in range(0, s.shape[0], ATT_CHUNK)]
    return chunks, jnp.max(functools.reduce(jnp.maximum, chunks), axis=0, keepdims=True)


ONES_ROWS = 16


def _with_ones(vt):
    ones = jnp.ones((ONES_ROWS, vt.shape[1]), BF16)
    return jnp.stack([jnp.concatenate([vt[h * HEAD_DIM:(h + 1) * HEAD_DIM], ones], axis=0) for h in range(2)])


def _unit_values(chunks, m_u, vts, mult_fn):
    ps = []
    for i, c in enumerate(chunks):
        p = jnp.exp((c - m_u).astype(BF16))
        if mult_fn is not None:
            p = p * mult_fn(i * ATT_CHUNK, (i + 1) * ATT_CHUNK).astype(BF16)
        ps.append(p)
    per = len(ps) // len(vts)
    pcat = [jnp.concatenate(ps[b * per:(b + 1) * per], axis=0) for b in range(len(vts))]
    nq = pcat[0].shape[1] // 2
    pv = [sum(jnp.dot(vt[h], pc[:, h * nq:(h + 1) * nq], preferred_element_type=F32) for vt, pc in zip(vts, pcat))
          for h in range(2)]
    return (jnp.concatenate([x[HEAD_DIM:HEAD_DIM + 1] for x in pv], axis=1),
            jnp.concatenate([x[:HEAD_DIM] for x in pv], axis=1))


def _attend_unit(kb, q2, bias_fn, vts, mult_fn=None):
    chunks, m_u = _unit_scores(kb, q2, bias_fn)
    return (m_u,) + _unit_values(chunks, m_u, vts, mult_fn)


def _merge(parts):
    m = functools.reduce(jnp.maximum, [p[0] for p in parts])
    ws = [jnp.exp(p[0] - m) for p in parts]
    return m, sum(w * p[1] for w, p in zip(ws, parts)), sum(w * p[2] for w, p in zip(ws, parts))


def _head_queries(q_t, feat):
    return [jnp.where(feat // HEAD_DIM == hh, q_t, 0.0) for hh in range(2)]


def _scaled_bf16(q):
    return (q * ATT_SCALE).astype(BF16)


def _both(x):
    return jnp.concatenate([x, x], axis=1)


def _pick_heads(x, feat):
    nq = x.shape[1] // 2
    return jnp.concatenate([x[:, :nq], x[:, nq:]], axis=0)


def _moba_prompt_step(step, q_ref, kt_ref, vt_ref, o_ref, kb_ref, vb_ref, kmean_ref, bias_ref, causal_ref, *, nblk):
    blk, grp = ATT_TILE, ATT_GROUP
    nsub = MOBA_SUBTILES

    @pl.when(step == 0)
    def _():
        for n in range(nblk):
            kblk = kt_ref[:, n * blk:(n + 1) * blk].T
            kmean_ref[n:n + 1, :] = jnp.sum(kblk, axis=0, keepdims=True) * (1.0 / blk)
            kb_ref[n * blk:(n + 1) * blk, :] = kblk.astype(BF16)
            vb_ref[n] = _with_ones(vt_ref[:, n * blk:(n + 1) * blk].astype(BF16))
        key_i = lax.broadcasted_iota(jnp.int32, (blk, 2 * blk), 0)
        qry_i = lax.broadcasted_iota(jnp.int32, (blk, 2 * blk), 1) % blk
        causal_ref[...] = jnp.where(key_i <= qry_i, 0.0, NEG_INF)

    feat = lax.broadcasted_iota(jnp.int32, (LANES, blk), 0)
    q2s = [jnp.concatenate(_head_queries(q_ref[sub * blk:(sub + 1) * blk, :].T, feat), axis=1) for sub in range(nsub)]
    qb2s = [_scaled_bf16(q2) for q2 in q2s]
    cols = nsub * 2 * blk
    blk_id = lax.broadcasted_iota(jnp.int32, (nblk, cols), 0)
    blk_f = blk_id.astype(F32)
    qt = step * nsub + lax.broadcasted_iota(jnp.int32, (1, cols), 1) // (2 * blk)
    sc = jnp.dot(kmean_ref[...], jnp.concatenate(q2s, axis=1), preferred_element_type=F32,
                 precision=lax.Precision.HIGHEST)
    sc = jnp.where(blk_id < qt, sc, NEG_INF)
    sel = jnp.zeros((nblk, cols), F32)
    for r in range(MOBA_TOPK):
        mx = jnp.max(sc, axis=0, keepdims=True)
        first = jnp.min(jnp.where(sc == mx, blk_f, float(nblk)), axis=0, keepdims=True)
        pick = blk_f == first
        sel = jnp.maximum(sel, jnp.where(pick & (r < qt), 1.0, 0.0))
        sc = jnp.where(pick, -jnp.inf, sc)
    bias = jnp.where((sel > 0.0) | (blk_id == qt), 0.0, NEG_INF)
    for sub in range(nsub):
        bias_ref[sub] = bias[:, sub * 2 * blk:(sub + 1) * 2 * blk]

    def group(sub, g, with_own):
        own_j = (step * nsub) % grp + sub
        nb = grp if not with_own else (sub + 1 if grp == nsub else grp)

        def bias_fn(r0, r1):
            j, r = r0 // blk, r0 % blk
            row = bias_ref[sub, pl.ds(g * grp + j, 1), :]
            if not with_own:
                return row
            return row + jnp.where(j == own_j, causal_ref[r:r + (r1 - r0), :], 0.0)

        kb = kb_ref[pl.ds(pl.multiple_of(g * grp * blk, blk), nb * blk), :]
        return _attend_unit(kb, qb2s[sub], bias_fn, [vb_ref[g * grp + b] for b in range(nb)])

    g_own = step * nsub // grp

    def body(g, carry):
        return sum((_merge([carry[3 * sub:3 * sub + 3], group(sub, g, False)]) for sub in range(nsub)), ())

    carry = lax.fori_loop(0, g_own, body, sum((group(sub, g_own, True) for sub in range(nsub)), ()))
    for sub in range(nsub):
        o_ref[sub * blk:(sub + 1) * blk, :] = _pick_heads(carry[3 * sub + 2] / carry[3 * sub + 1], feat).T


def _dilated_multiplicity(delta):
    c = np.zeros(delta.shape, np.int32)
    for window, dil in DILATED:
        c += ((delta >= 0) & (delta <= window) & (delta % dil == 0)).astype(np.int32)
    return c


DIL_NEAR = 2
DIL_FAR_CLASSES_PER_STEP = 8
assert DILATED[-1][1] % DIL_FAR_CLASSES_PER_STEP == 0
DIL_NEAR_SUBTILES = 8
DIL_FAR_STRIDE = DILATED[-1][1]
DIL_FAR_REACH = DILATED[-1][0] // DIL_FAR_STRIDE
assert DILATED[-2][0] <= ATT_TILE * DIL_NEAR and ATT_TILE % DIL_FAR_STRIDE == 0 and DIL_FAR_REACH <= ATT_TILE


def _mask_bias(valid):
    return np.where(valid, 0.0, NEG_INF).astype(np.float32)


def _dilated_near_tables():
    ki = np.arange(ATT_TILE)[:, None]
    qi = np.arange(ATT_TILE)[None, :]
    c = np.stack([_dilated_multiplicity(ATT_TILE * (DIL_NEAR - i) + qi - ki) for i in range(DIL_NEAR + 1)])
    return _mask_bias(c > 0), c.astype(np.float32)


def _dilated_far_tables():
    per_tile = ATT_TILE // DIL_FAR_STRIDE

    def table(k0, nk, q0):
        ka = k0 + np.arange(nk)[:, None]
        qa = q0 + np.arange(ATT_TILE)[None, :]
        return _mask_bias((qa - ka <= DIL_FAR_REACH) & (ka // per_tile < qa // per_tile - DIL_NEAR))

    return table(0, ATT_TILE, 0), table(ATT_TILE - DIL_FAR_REACH, ATT_TILE + DIL_FAR_REACH, ATT_TILE)


def _dilated_far_kernel(q_ref, k_ref, v_ref, b0_ref, b1_ref, o_ref, lse_ref, *, seq):
    stride, blk, reach = DIL_FAR_STRIDE, ATT_TILE, DIL_FAR_REACH
    ntile = seq // stride // blk
    feat = lax.broadcasted_iota(jnp.int32, (LANES, blk), 0)

    def one_class(cls):
        for tau in range(ntile):
            k0, nk, tab = (0, blk, b0_ref) if tau == 0 else (tau * blk - reach, blk + reach, b1_ref)
            rows_q = pl.ds(cls + stride * blk * tau, blk, stride=stride)
            rows_k = pl.ds(cls + stride * k0, nk, stride=stride)
            qb2 = _scaled_bf16(jnp.concatenate(_head_queries(q_ref[rows_q, :].T, feat), axis=1))
            kb = k_ref[rows_k, :].astype(BF16)
            vt = _with_ones(v_ref[rows_k, :].T.astype(BF16))
            m_u, l_u, pv = _attend_unit(kb, qb2, lambda r0, r1: _both(tab[r0:r1, :]), [vt])
            o_ref[rows_q, :] = _pick_heads(pv / l_u, feat).T
            lse_ref[rows_q, :] = _pick_heads(jnp.broadcast_to(m_u + jnp.log(l_u), pv.shape), feat).T

    def body(i, carry):
        for c in range(DIL_FAR_CLASSES_PER_STEP):
            one_class(i * DIL_FAR_CLASSES_PER_STEP + c)
        return carry

    lax.fori_loop(0, stride // DIL_FAR_CLASSES_PER_STEP, body, 0)


def _dilated_near_kernel(q_ref, k_ref, v_ref, far_ref, lse_ref, bias_ref, mult_ref, o_ref, kwin_ref, vwin_ref,
                         kb_ref, vb_ref, tab_ref, *, nblk):
    blk, near, nsub = ATT_TILE, DIL_NEAR, DIL_NEAR_SUBTILES
    step = pl.program_id(2)
    keep_blks = kwin_ref.shape[1] // blk

    @pl.when(step == 0)
    def _():
        kb_ref[0:near * blk, :] = jnp.zeros((near * blk, LANES), BF16)
        for n in range(near):
            vb_ref[n] = jnp.zeros((2, HEAD_DIM + ONES_ROWS, blk), BF16)
        for n in range(nblk):
            kblk = k_ref[n * blk:(n + 1) * blk, :]
            vblk_t = v_ref[n * blk:(n + 1) * blk, :].T
            kb_ref[(n + near) * blk:(n + near + 1) * blk, :] = kblk.astype(BF16)
            vb_ref[n + near] = _with_ones(vblk_t.astype(BF16))
            if n >= nblk - keep_blks:
                w = n - (nblk - keep_blks)
                kwin_ref[:, w * blk:(w + 1) * blk] = kblk.T
                vwin_ref[:, w * blk:(w + 1) * blk] = vblk_t

    @pl.when(step * nsub < near + nsub)
    def _():
        for sub in range(nsub):
            for i in range(near + 1):
                tab_ref[sub, i] = jnp.where(step * nsub + sub - near + i >= 0, bias_ref[i], NEG_INF)

    feat = lax.broadcasted_iota(jnp.int32, (LANES, blk), 0)
    for sub in range(nsub):
        qt = step * nsub + sub
        rows = slice(sub * blk, (sub + 1) * blk)
        qb2 = _scaled_bf16(jnp.concatenate(_head_queries(q_ref[rows, :].T, feat), axis=1))
        m2, l2, a2 = _attend_unit(
            kb_ref[pl.ds(pl.multiple_of(qt * blk, blk), (near + 1) * blk), :], qb2,
            lambda r0, r1: _both(tab_ref[sub, r0 // blk, r0 % blk:r0 % blk + (r1 - r0), :]),
            [vb_ref[qt + i] for i in range(near + 1)],
            lambda r0, r1: _both(mult_ref[r0 // blk, r0 % blk:r0 % blk + (r1 - r0), :]))
        m_n = _pick_heads(jnp.broadcast_to(m2, a2.shape), feat)
        l_n = _pick_heads(jnp.broadcast_to(l2, a2.shape), feat)
        a_n = _pick_heads(a2, feat)
        lse_f = lse_ref[rows, :].T
        m = jnp.maximum(m_n, lse_f)
        w_n, w_f = jnp.exp(m_n - m), jnp.exp(lse_f - m)
        o_ref[rows, :] = ((a_n * w_n + far_ref[rows, :].T * w_f) / (l_n * w_n + w_f)).T


def _dilated_prompt(q, k, v, batch, seq, keep):
    n, width = q.shape
    blk, near, stride = ATT_TILE, DIL_NEAR, DIL_FAR_STRIDE
    assert seq % (blk * stride) == 0
    nblk = seq // blk
    npair = width // LANES
    const = lambda nd: (lambda *_: (0,) * nd)
    full = lambda t: pl.BlockSpec(t.shape, const(t.ndim))

    far_tabs = [jnp.asarray(t) for t in _dilated_far_tables()]
    seq_spec = pl.BlockSpec((seq, LANES), lambda b, p: (b, p))
    o_far, lse_far = pl.pallas_call(
        functools.partial(_dilated_far_kernel, seq=seq),
        grid=(batch, npair),
        in_specs=[seq_spec, seq_spec, seq_spec] + [full(t) for t in far_tabs],
        out_specs=[seq_spec, seq_spec],
        out_shape=[jax.ShapeDtypeStruct((n, width), F32)] * 2,
        compiler_params=_cparams("parallel", "parallel"),
        name="dilated_far",
    )(q, k, v, *far_tabs)

    near_tabs = [jnp.asarray(t) for t in _dilated_near_tables()]
    nsub = DIL_NEAR_SUBTILES
    assert nblk % nsub == 0 and keep % blk == 0 and keep <= seq
    steps = nblk // nsub
    kv_spec = pl.BlockSpec((seq, LANES), lambda b, p, t: (b, p))
    q_spec = pl.BlockSpec((nsub * blk, LANES), lambda b, p, t: (b * steps + t, p))
    win_spec = pl.BlockSpec((None, LANES, keep), lambda b, p, t: (b, p, 0))
    win_shape = jax.ShapeDtypeStruct((batch, width, keep), F32)
    return pl.pallas_call(
        functools.partial(_dilated_near_kernel, nblk=nblk),
        grid=(batch, npair, steps),
        in_specs=[q_spec, kv_spec, kv_spec, q_spec, q_spec] + [full(t) for t in near_tabs],
        out_specs=[q_spec, win_spec, win_spec],
        out_shape=[jax.ShapeDtypeStruct((n, width), F32), win_shape, win_shape],
        scratch_shapes=[pltpu.VMEM(((nblk + near) * blk, LANES), BF16),
                        pltpu.VMEM((nblk + near, 2, HEAD_DIM + ONES_ROWS, blk), BF16),
                        pltpu.VMEM((nsub, near + 1, blk, blk), F32)],
        compiler_params=_cparams("parallel", "parallel", "arbitrary"),
        name="dilated_near",
    )(q, k, v, o_far, lse_far, *near_tabs)


RET_ROWS = 128


def _retention_tables(n_heads, chunk):
    log_g = jnp.log1p(-jnp.exp2(-5.0 - jnp.arange(n_heads, dtype=F32)))
    i = jnp.arange(chunk, dtype=F32)
    diff = i[:, None] - i[None, :]
    causal = diff >= 0
    decay = jnp.where(causal[None], jnp.exp(jnp.where(causal, diff, 0.0)[None] * log_g[:, None, None]), 0.0)
    q_dec = jnp.exp((i + 1.0)[:, None] * log_g[None, :])
    k_dec = jnp.exp((chunk - 1.0 - i)[:, None] * log_g[None, :])
    c_dec = jnp.exp(chunk * log_g)
    pad = RET_ROWS - chunk
    npair = n_heads // 2
    decay = jnp.pad(decay, ((0, 0), (0, pad), (0, pad))).reshape(npair, 2 * RET_ROWS, RET_ROWS)
    expand = lambda t: jnp.pad(jnp.repeat(t, HEAD_DIM, axis=1), ((0, pad), (0, 0))).reshape(
        RET_ROWS, npair, LANES).transpose(1, 0, 2)
    q_tab, k_tab = expand(q_dec), expand(k_dec)
    same_head = (np.arange(LANES)[:, None] // HEAD_DIM) == (np.arange(LANES)[None, :] // HEAD_DIM)
    c_tab = jnp.where(same_head[None], jnp.repeat(c_dec, HEAD_DIM).reshape(npair, LANES, 1), 0.0)
    return decay.astype(F32), q_tab.astype(F32), k_tab.astype(F32), c_tab.astype(F32)


def _retention_kernel(q_ref, k_ref, v_ref, g_ref, gn_ref, s0_ref, dec_ref, qd_ref, kd_ref, cd_ref,
                      o_ref, sout_ref, st_ref, *, chunk, nchunk):
    c = pl.program_id(2)

    @pl.when(c == 0)
    def _():
        zero = jnp.zeros((HEAD_DIM, HEAD_DIM), F32)
        st_ref[...] = jnp.concatenate([jnp.concatenate([s0_ref[0], zero], axis=1),
                                       jnp.concatenate([zero, s0_ref[1]], axis=1)], axis=0)

    pad = RET_ROWS - chunk
    lane = lax.broadcasted_iota(jnp.int32, (RET_ROWS, LANES), 1)
    head0 = lane < HEAD_DIM
    same_head = (lax.broadcasted_iota(jnp.int32, (LANES, LANES), 0) // HEAD_DIM) == (
        lax.broadcasted_iota(jnp.int32, (LANES, LANES), 1) // HEAD_DIM)

    def rows(ref, r):
        x = ref[pl.ds(r, chunk), :]
        return x if pad == 0 else jnp.concatenate([x, jnp.zeros((pad, LANES), F32)], axis=0)

    def seg_mean(x):
        s0 = jnp.sum(jnp.where(head0, x, 0.0), axis=-1, keepdims=True)
        s1 = jnp.sum(jnp.where(head0, 0.0, x), axis=-1, keepdims=True)
        return jnp.where(head0, s0, s1) * (1.0 / HEAD_DIM)

    def one_chunk(j, state):
        r = j * chunk
        q, k, v, g = rows(q_ref, r), rows(k_ref, r), rows(v_ref, r), rows(g_ref, r)
        kb, vb = k.astype(BF16), v.astype(BF16)
        q2 = jnp.concatenate([jnp.where(head0, q, 0.0), jnp.where(head0, 0.0, q)], axis=0).astype(BF16)
        attn = lax.dot_general(q2, kb, _NT, preferred_element_type=F32) * dec_ref[...]
        inn = jnp.dot(attn.astype(BF16), vb, preferred_element_type=F32)
        inner = jnp.where(head0, inn[:RET_ROWS], inn[RET_ROWS:])
        cross = jnp.dot(q.astype(BF16), state.astype(BF16), preferred_element_type=F32) * qd_ref[...]
        upd = lax.dot_general((k * kd_ref[...]).astype(BF16), vb, _TN, preferred_element_type=F32)
        o = inner + cross
        mu = seg_mean(o)
        var = seg_mean(jnp.square(o - mu))
        y = (o - mu) * lax.rsqrt(var + EPS) * gn_ref[...]
        y = y * (g * jax.nn.sigmoid(g))
        o_ref[r:r + chunk, :] = y[:chunk]
        return state * cd_ref[...] + jnp.where(same_head, upd, 0.0)

    state = st_ref[...]
    for j in range(nchunk):
        state = one_chunk(j, state)
    st_ref[...] = state

    @pl.when(c == pl.num_programs(2) - 1)
    def _():
        sout_ref[0] = st_ref[0:HEAD_DIM, 0:HEAD_DIM]
        sout_ref[1] = st_ref[HEAD_DIM:, HEAD_DIM:]


def _retention(q, k, v, g, gn_w, s0, batch, seq, chunk, nchunk):
    n, width = q.shape
    n_heads = width // HEAD_DIM
    npair = width // LANES
    rows = chunk * nchunk
    steps = seq // rows
    dec, qd, kd, cd = _retention_tables(n_heads, chunk)
    row_spec = pl.BlockSpec((rows, LANES), lambda b, p, c: (b * steps + c, p))
    st_spec = pl.BlockSpec((None, 2, HEAD_DIM, HEAD_DIM), lambda b, p, c: (b, p, 0, 0))
    tab = lambda t: pl.BlockSpec((None,) + t.shape[1:], lambda b, p, c: (p, 0, 0))
    out, s_new = pl.pallas_call(
        functools.partial(_retention_kernel, chunk=chunk, nchunk=nchunk),
        grid=(batch, npair, steps),
        in_specs=[row_spec, row_spec, row_spec, row_spec, pl.BlockSpec((1, LANES), lambda b, p, c: (0, p)),
                  st_spec, tab(dec), tab(qd), tab(kd), tab(cd)],
        out_specs=[row_spec, st_spec],
        out_shape=[jax.ShapeDtypeStruct((n, width), F32),
                   jax.ShapeDtypeStruct((batch, n_heads, HEAD_DIM, HEAD_DIM), F32)],
        scratch_shapes=[pltpu.VMEM((LANES, LANES), F32)],
        compiler_params=_cparams("parallel", "parallel", "arbitrary"),
        name="retention",
    )(q, k, v, g, gn_w.reshape(1, width), s0, dec, qd, kd, cd)
    return out, s_new


def _block_diag_queries(q, n_heads):
    t = q.shape[0]
    rep = jnp.concatenate([q] * n_heads, axis=0)
    rows = lax.broadcasted_iota(jnp.int32, rep.shape, 0) // t
    cols = lax.broadcasted_iota(jnp.int32, rep.shape, 1) // HEAD_DIM
    return jnp.where(rows == cols, rep, 0.0)


def _take_diag(o, n_heads, t):
    cols = lax.broadcasted_iota(jnp.int32, (t, o.shape[1]), 1) // HEAD_DIM
    out = jnp.zeros((t, o.shape[1]), F32)
    for h in range(n_heads):
        out = jnp.where(cols == h, o[h * t:(h + 1) * t, :], out)
    return out


def _fold_pairs(o, n_heads, t):
    rows = lax.broadcasted_iota(jnp.int32, (o.shape[0], LANES), 0) // (2 * t)
    out = jnp.zeros((o.shape[0], LANES), F32)
    for c in range(n_heads // 2):
        out = jnp.where(rows == c, o[:, c * LANES:(c + 1) * LANES], out)
    return out


def _unfold_pairs(a, n_heads, t):
    lane = lax.broadcasted_iota(jnp.int32, (t, LANES), 1)
    return jnp.concatenate([jnp.where(lane < HEAD_DIM, a[2 * c * t:(2 * c + 1) * t], a[(2 * c + 1) * t:(2 * c + 2) * t])
                            for c in range(n_heads // 2)], axis=1)


def _pad_rows(x, rows):
    return jnp.concatenate([x, jnp.zeros((rows - x.shape[0], x.shape[1]), x.dtype)], axis=0)


MOBA_PAGES_PER_STEP = 32


def _moba_decode_step(g, is_last, q_ref, kn_ref, vn_ref, kp, vp, o_ref, qbd_ref, m_ref, l_ref, sc_ref, oblk_ref,
                      *, n_heads, t_new, pages_per_blk, nblk):
    npg = len(kp)
    rows = n_heads * t_new
    lane = lax.broadcasted_iota(jnp.int32, (rows, LANES), 1)
    page = kp[0].shape[1]
    blk_keys = pages_per_blk * page
    step_blks = npg // pages_per_blk

    @pl.when(g == 0)
    def _():
        qbd_ref[...] = _scaled_bf16(_block_diag_queries(q_ref[...], n_heads))
        m_ref[...] = jnp.zeros_like(m_ref)
        l_ref[...] = jnp.zeros_like(l_ref)
        sc_ref[...] = jnp.full_like(sc_ref, -jnp.inf)

    kt_all = jnp.concatenate([kp[i][...].astype(BF16) for i in range(npg)], axis=1)
    s_all = jnp.dot(qbd_ref[...], kt_all, preferred_element_type=F32)
    m_new, l_new, sc_new = m_ref[...], l_ref[...], sc_ref[...]
    for jb in range(step_blks):
        n = g * step_blks + jb
        s = s_all[:, jb * blk_keys:(jb + 1) * blk_keys]
        mb = jnp.max(s, axis=-1, keepdims=True)
        p = jnp.exp(s - mb)
        v_blk = jnp.concatenate([vp[jb * pages_per_blk + a][...].astype(BF16) for a in range(pages_per_blk)], axis=1)
        oblk_ref[n] = _fold_pairs(lax.dot_general(p.astype(BF16), v_blk, _NT, preferred_element_type=F32),
                                  n_heads, t_new)
        here = lane == n
        m_new = jnp.where(here, mb, m_new)
        l_new = jnp.where(here, jnp.sum(p, axis=-1, keepdims=True), l_new)
        sc_new = jnp.where(here, jnp.sum(s, axis=-1, keepdims=True) * (1.0 / blk_keys), sc_new)
    m_ref[...], l_ref[...], sc_ref[...] = m_new, l_new, sc_new

    @pl.when(is_last)
    def _():
        s = lax.dot_general(qbd_ref[...], _pad_rows(kn_ref[...], LANES).astype(BF16), _NT,
                            preferred_element_type=F32)
        qry = lax.broadcasted_iota(jnp.int32, (rows, LANES), 0) % t_new
        s = jnp.where(lane <= qry, s, NEG_INF)
        m_own = jnp.max(s, axis=-1, keepdims=True)
        p = jnp.exp(s - m_own)
        l_own = jnp.sum(p, axis=-1, keepdims=True)
        o_own = jnp.dot(p.astype(BF16), _pad_rows(vn_ref[...], LANES).astype(BF16), preferred_element_type=F32)
        sc = sc_ref[...]
        lane_f = lane.astype(F32)
        sel = lane < 0
        for _ in range(MOBA_TOPK):
            mx = jnp.max(sc, axis=-1, keepdims=True)
            first = jnp.min(jnp.where(sc == mx, lane_f, float(LANES)), axis=-1, keepdims=True)
            pick = lane_f == first
            sel = sel | pick
            sc = jnp.where(pick, -jnp.inf, sc)
        m_all = m_ref[...]
        m_fin = jnp.maximum(jnp.max(jnp.where(sel, m_all, -jnp.inf), axis=-1, keepdims=True), m_own)
        w = jnp.where(sel, jnp.exp(m_all - m_fin), 0.0)
        w_own = jnp.exp(m_own - m_fin)
        l_fin = jnp.sum(w * l_ref[...], axis=-1, keepdims=True) + w_own * l_own

        def body(n, acc):
            col = jnp.sum(jnp.where(lane == n, w, 0.0), axis=-1, keepdims=True)
            return acc + col * oblk_ref[n]

        acc = lax.fori_loop(0, nblk, body, w_own * _fold_pairs(o_own, n_heads, t_new))
        o_ref[...] = _unfold_pairs(acc / l_fin, n_heads, t_new)


def _moba_kernel(pt_ref, q_ref, kt_ref, vt_ref, qs_ref, kn_ref, vn_ref, *rest, nblk, prompt_steps, decode_steps,
                 n_heads, t_new, pages_per_blk, dec_nblk):
    npg = MOBA_PAGES_PER_STEP
    kp, vp = rest[:npg], rest[npg:2 * npg]
    o_ref, os_ref, kb_ref, vb_ref, kmean_ref, bias_ref, causal_ref, qbd_ref, m_ref, l_ref, sc_ref, oblk_ref = rest[2 * npg:]
    i = pl.program_id(0)
    _moba_prompt_step(i % prompt_steps, q_ref, kt_ref, vt_ref, o_ref, kb_ref, vb_ref, kmean_ref, bias_ref, causal_ref,
                      nblk=nblk)
    g = i % decode_steps
    _moba_decode_step(g, g == decode_steps - 1, qs_ref, kn_ref, vn_ref, kp, vp, os_ref, qbd_ref, m_ref, l_ref, sc_ref,
                      oblk_ref, n_heads=n_heads, t_new=t_new, pages_per_blk=pages_per_blk, nblk=dec_nblk)


def _moba(q, kt, vt, batch, seq, qs, k_new, v_new, cache_kt, cache_vt, page_table, t_new):
    n, width = q.shape
    blk, grp, nsub = ATT_TILE, ATT_GROUP, MOBA_SUBTILES
    assert blk == MOBA_BLOCK and seq % (blk * grp) == 0 and grp % nsub == 0
    nblk = seq // blk
    steps_p = nblk // nsub
    npair = width // LANES
    n_heads = width // HEAD_DIM
    n_seq, n_pages = page_table.shape
    page = cache_kt.shape[2]
    pages_per_blk = MOBA_BLOCK // page
    dec_nblk = n_pages // pages_per_blk
    npg = MOBA_PAGES_PER_STEP
    steps_d = n_pages // npg
    rows = n_heads * t_new
    assert n_pages % npg == 0 and npg % pages_per_blk == 0 and MOBA_TOPK <= dec_nblk <= LANES
    assert n_pages * page == dec_nblk * MOBA_BLOCK and t_new <= page and page == LANES and rows <= LANES
    assert n_heads % 2 == 0
    total = batch * npair * steps_p
    assert total == n_seq * steps_d, "prompt and decode sides must have the same number of steps"

    kv_spec = pl.BlockSpec((None, LANES, seq), lambda i, pt: (i // (npair * steps_p), (i // steps_p) % npair, 0),
                           pipeline_mode=pl.Buffered(1))
    q_spec = pl.BlockSpec((nsub * blk, LANES),
                          lambda i, pt: ((i // (npair * steps_p)) * steps_p + i % steps_p, (i // steps_p) % npair))
    tok_spec = pl.BlockSpec((t_new, width), lambda i, pt: (i // steps_d, 0))
    page_specs = [pl.BlockSpec((None, width, page), functools.partial(
        lambda i, pt, j: (pt[i // steps_d, (i % steps_d) * npg + j], 0, 0), j=j)) for j in range(npg)]
    grid_spec = pltpu.PrefetchScalarGridSpec(
        num_scalar_prefetch=1,
        grid=(total,),
        in_specs=[q_spec, kv_spec, kv_spec, tok_spec, tok_spec, tok_spec] + page_specs + page_specs,
        out_specs=[q_spec, tok_spec],
        scratch_shapes=[pltpu.VMEM((nblk * blk, LANES), BF16), pltpu.VMEM((nblk, 2, HEAD_DIM + ONES_ROWS, blk), BF16),
                        pltpu.VMEM((nblk, LANES), F32), pltpu.VMEM((nsub, nblk, 2 * blk), F32),
                        pltpu.VMEM((blk, 2 * blk), F32),
                        pltpu.VMEM((rows, width), BF16), pltpu.VMEM((rows, LANES), F32),
                        pltpu.VMEM((rows, LANES), F32), pltpu.VMEM((rows, LANES), F32),
                        pltpu.VMEM((dec_nblk, rows, LANES), F32)],
    )
    return pl.pallas_call(
        functools.partial(_moba_kernel, nblk=nblk, prompt_steps=steps_p, decode_steps=steps_d, n_heads=n_heads,
                          t_new=t_new, pages_per_blk=pages_per_blk, dec_nblk=dec_nblk),
        grid_spec=grid_spec,
        out_shape=[jax.ShapeDtypeStruct((n, width), F32), jax.ShapeDtypeStruct(qs.shape, F32)],
        compiler_params=_cparams("arbitrary"),
        name="moba",
    )(page_table, q, kt, vt, qs, k_new, v_new, *([cache_kt] * npg), *([cache_vt] * npg))


DIL_DEC_TILE = 1024


def _dilated_decode_tables(n_heads, t_new, n_prev):
    qi = (np.arange(n_heads * t_new) % t_new)[:, None]
    c_old = _dilated_multiplicity(n_prev + qi - np.arange(n_prev)[None, :])
    c_new = _dilated_multiplicity(qi - np.arange(LANES)[None, :])
    c_new = np.where(np.arange(LANES)[None, :] < t_new, c_new, 0)
    f = lambda c: (np.where(c > 0, 0.0, NEG_INF).astype(np.float32), c.astype(np.float32))
    return f(c_old) + f(c_new)


def _shifted_window_tile(cur_ref, nxt_ref, new_ref, out_ref, tail_ref, is_last, t_new):
    lane = lax.broadcasted_iota(jnp.int32, (cur_ref.shape[0], LANES), 1)
    nblk = cur_ref.shape[1] // LANES

    @pl.when(is_last)
    def _():
        tail_ref[...] = _pad_rows(new_ref[...], LANES).T

    @pl.when(jnp.logical_not(is_last))
    def _():
        tail_ref[...] = nxt_ref[...]

    rolled = [pltpu.roll(cur_ref[:, c * LANES:(c + 1) * LANES], LANES - t_new, 1) for c in range(nblk)]
    rolled.append(pltpu.roll(tail_ref[...], LANES - t_new, 1))
    for c in range(nblk):
        out_ref[:, c * LANES:(c + 1) * LANES] = jnp.where(lane < LANES - t_new, rolled[c], rolled[c + 1])


WIN_SLOTS = 3


def _dilated_decode_kernel(q_ref, kn_ref, vn_ref, kwin_hbm, vwin_hbm, kx_ref, vx_ref, bo_ref, mo_ref, bn_ref, mn_ref,
                           o_ref, ko_ref, vo_ref, qbd_ref, m_ref, l_ref, acc_ref, tail_ref, kbuf_ref, vbuf_ref, sem_ref,
                           *, n_heads, t_new):
    kt = pl.program_id(1)
    ntile = pl.num_programs(1)
    is_last = kt == ntile - 1
    s = pl.program_id(0) * ntile + kt
    total = pl.num_programs(0) * ntile
    tile = kbuf_ref.shape[2]

    def copies(step):
        seq, tok = step // ntile, pl.ds(pl.multiple_of((step % ntile) * tile, tile), tile)
        slot = step % WIN_SLOTS
        return [pltpu.make_async_copy(hbm.at[seq, :, tok], buf.at[slot], sem_ref.at[a, slot])
                for a, (hbm, buf) in enumerate(((kwin_hbm, kbuf_ref), (vwin_hbm, vbuf_ref)))]

    @pl.when(s == 0)
    def _():
        for first in range(WIN_SLOTS - 1):
            for c in copies(first):
                c.start()

    @pl.when(s + WIN_SLOTS - 1 < total)
    def _():
        for c in copies(s + WIN_SLOTS - 1):
            c.start()

    for c in copies(s):
        c.wait()
    kc_ref, vc_ref = kbuf_ref.at[s % WIN_SLOTS], vbuf_ref.at[s % WIN_SLOTS]
    _shifted_window_tile(kc_ref, kx_ref, kn_ref, ko_ref, tail_ref, is_last, t_new)
    _shifted_window_tile(vc_ref, vx_ref, vn_ref, vo_ref, tail_ref, is_last, t_new)

    @pl.when(kt == 0)
    def _():
        qbd_ref[...] = _scaled_bf16(_block_diag_queries(q_ref[...], n_heads))
        m_ref[...] = jnp.full_like(m_ref, -jnp.inf)
        l_ref[...] = jnp.zeros_like(l_ref)
        acc_ref[...] = jnp.zeros_like(acc_ref)

    def step(s, pv, bias, mult):
        s = s + bias
        m_old = m_ref[...]
        m_new = jnp.maximum(m_old, jnp.max(s, axis=-1, keepdims=True))
        alpha = jnp.exp(m_old - m_new)
        p = jnp.exp(s - m_new) * mult
        l_ref[...] = alpha * l_ref[...] + jnp.sum(p, axis=-1, keepdims=True)
        acc_ref[...] = alpha * acc_ref[...] + pv(p.astype(BF16))
        m_ref[...] = m_new

    @pl.when(kt == 0)
    def _():
        k, v = _pad_rows(kn_ref[...], LANES).astype(BF16), _pad_rows(vn_ref[...], LANES).astype(BF16)
        step(lax.dot_general(qbd_ref[...], k, _NT, preferred_element_type=F32),
             lambda p: jnp.dot(p, v, preferred_element_type=F32), bn_ref[...], mn_ref[...])

    step(jnp.dot(qbd_ref[...], kc_ref[...].astype(BF16), preferred_element_type=F32),
         lambda p: lax.dot_general(p, vc_ref[...].astype(BF16), _NT, preferred_element_type=F32),
         bo_ref[...], mo_ref[...])

    @pl.when(kt == pl.num_programs(1) - 1)
    def _():
        o_ref[...] = _take_diag(acc_ref[...] / l_ref[...], n_heads, t_new)


def _dilated_decode(q, k_new, v_new, win_kt, win_vt, t_new):
    n, width = q.shape
    n_heads = width // HEAD_DIM
    n_seq, _, n_prev = win_kt.shape
    tile = DIL_DEC_TILE
    assert n_prev == W_MAX and n_prev % tile == 0 and t_new <= LANES
    rows = n_heads * t_new
    bo, mo, bn, mn = _dilated_decode_tables(n_heads, t_new, n_prev)
    tok_spec = pl.BlockSpec((t_new, width), lambda b, t: (b, 0))
    win_spec = pl.BlockSpec((None, width, tile), lambda b, t: (b, 0, t))
    last_lane_blk = n_prev // LANES - 1
    nxt_spec = pl.BlockSpec((None, width, LANES),
                            lambda b, t: (b, 0, jnp.minimum((t + 1) * (tile // LANES), last_lane_blk)))
    old_tab = pl.BlockSpec((rows, tile), lambda b, t: (0, t))
    new_tab = pl.BlockSpec((rows, LANES), lambda b, t: (0, 0))
    win_shape = jax.ShapeDtypeStruct(win_kt.shape, F32)
    assert n_seq * (n_prev // tile) >= WIN_SLOTS
    hbm_spec = pl.BlockSpec(memory_space=pl.ANY)
    return pl.pallas_call(
        functools.partial(_dilated_decode_kernel, n_heads=n_heads, t_new=t_new),
        grid=(n_seq, n_prev // tile),
        in_specs=[tok_spec, tok_spec, tok_spec, hbm_spec, hbm_spec, nxt_spec, nxt_spec,
                  old_tab, old_tab, new_tab, new_tab],
        out_specs=[tok_spec, win_spec, win_spec],
        out_shape=[jax.ShapeDtypeStruct((n, width), F32), win_shape, win_shape],
        scratch_shapes=[pltpu.VMEM((rows, width), BF16), pltpu.VMEM((rows, 1), F32),
                        pltpu.VMEM((rows, 1), F32), pltpu.VMEM((rows, width), F32),
                        pltpu.VMEM((width, LANES), F32),
                        pltpu.VMEM((WIN_SLOTS, width, tile), F32), pltpu.VMEM((WIN_SLOTS, width, tile), F32),
                        pltpu.SemaphoreType.DMA((2, WIN_SLOTS))],
        compiler_params=_cparams("arbitrary", "arbitrary"),
        name="dilated_decode",
    )(q, k_new, v_new, win_kt, win_vt, win_kt, win_vt,
      jnp.asarray(bo), jnp.asarray(mo), jnp.asarray(bn), jnp.asarray(mn))


_AB_SEGS = (("a", (0,), 0, 1.0), ("a", (1,), 0, 1.0), (None, (2,), 0, 1.0), ("b", (3,), 0, 1.0),
            ("b", (4,), 0, ATT_SCALE), (None, (5,), 0, 1.0), (None, (6,), 0, 1.0))
_C_SEGS = tuple(("a" if o < 2 else None, (o,), c, 1.0) for o in range(3) for c in (0, SEG))
_KV_OUT = (1, 2)


def _row_tile(n):
    return 512 if n % 512 == 0 else n


def _ffn_row_tile(n):
    return 1024 if n % 1024 == 0 else n


def _feature_major(x):
    lead = x.shape[:-3]
    t, h, dh = x.shape[-3:]
    nl = len(lead)
    return x.transpose(*range(nl), nl + 1, nl + 2, nl).reshape(*lead, h * dh, t)


def _token_major(xt, n_heads):
    lead = xt.shape[:-2]
    t = xt.shape[-1]
    nl = len(lead)
    return xt.reshape(*lead, n_heads, HEAD_DIM, t).transpose(*range(nl), nl + 2, nl, nl + 1)


def kernel(x_prompt, x_sample, cache_k_a, cache_v_a, page_table, state_ret, cache_win_k, cache_win_v,
           norm_mix, norm_ffn, norm_final, w_in_ab, w_out_ab, ret_gn_w, w_in_c, w_out_c,
           ffn_w_gate, ffn_w_up, ffn_w_down):
    bp, tp, d = x_prompt.shape
    bs, ts, _ = x_sample.shape
    page = cache_k_a.shape[2]
    past_len = page_table.shape[1] * page
    h_a = cache_k_a.shape[3]
    wa = h_a * HEAD_DIM
    wb = w_in_ab.shape[2] - 3 * wa
    assert wb == 4 * wa and w_in_ab.shape[0] == 1 and w_in_c.shape[0] == 1 and norm_mix.shape[0] == 2
    h_b = wa // HEAD_DIM
    h_c = w_in_c.shape[2] // (3 * HEAD_DIM)

    pos_p = jnp.arange(tp, dtype=jnp.int32)
    pos_s = past_len + jnp.arange(ts, dtype=jnp.int32)
    tile_s = lambda tabs: tuple(jnp.tile(t, (bs, 1)) for t in tabs)
    tabs = {
        "p": (_rope_tables(pos_p, ROT_DIM, ROPE_THETA), _rope_tables(pos_p, HEAD_DIM, RET_THETA)),
        "s": (tile_s(_rope_tables(pos_s, ROT_DIM, ROPE_THETA)), tile_s(_rope_tables(pos_s, HEAD_DIM, RET_THETA))),
    }
    bf = lambda w: w.astype(BF16)
    w_in_ab_bf, w_out_ab_bf, w_in_c_bf, w_out_c_bf = bf(w_in_ab[0]), bf(w_out_ab[0]), bf(w_in_c[0]), bf(w_out_c[0])
    wg, wu, wd = bf(ffn_w_gate), bf(ffn_w_up), bf(ffn_w_down)

    xp = x_prompt.reshape(bp * tp, d)
    xs = x_sample.reshape(bs * ts, d)
    tm_p, tm_s = _row_tile(bp * tp), _row_tile(bs * ts)
    tf_p = _ffn_row_tile(bp * tp)
    assert tp % tm_p == 0

    ab_widths = (wa,) * 7
    qa_p, kat_p, vat_p, qb_p, kb_p, vb_p, gb_p = _project(
        xp, norm_mix[0], w_in_ab_bf, *tabs["p"], _AB_SEGS, ab_widths, tm_p, _KV_OUT)
    qa_s, ka_s, va_s, qb_s, kb_s, vb_s, gb_s = _project(
        xs, norm_mix[0], w_in_ab_bf, *tabs["s"], _AB_SEGS, ab_widths, tm_s)

    oa_p, oa_s = _moba(qa_p, kat_p, vat_p, bp, tp, qa_s, ka_s, va_s, _feature_major(cache_k_a[0]),
                       _feature_major(cache_v_a[0]), page_table, ts)

    chunk_p = math.gcd(tp, RET_CHUNK)
    ob_p, ret_p = _retention(qb_p, kb_p, vb_p, gb_p, ret_gn_w[0],
                             jnp.zeros((bp, h_b, HEAD_DIM, HEAD_DIM), F32), bp, tp, chunk_p,
                             math.gcd(tp // chunk_p, 8))
    ob_s, ret_s = _retention(qb_s, kb_s, vb_s, gb_s, ret_gn_w[0], state_ret[0], bs, ts,
                             math.gcd(ts, RET_CHUNK), 1)

    xp = _mix_ffn([oa_p, ob_p], w_out_ab_bf, xp, norm_ffn[0], wg[0], wu[0], wd[0], norm_final, False, tf_p)
    xs = _mix_ffn([oa_s, ob_s], w_out_ab_bf, xs, norm_ffn[0], wg[0], wu[0], wd[0], norm_final, False, tm_s)

    wc = h_c * HEAD_DIM
    q_p, k_p, v_p = _project(xp, norm_mix[1], w_in_c_bf, *tabs["p"], _C_SEGS, (wc,) * 3, tm_p)
    q_s, k_s, v_s = _project(xs, norm_mix[1], w_in_c_bf, *tabs["s"], _C_SEGS, (wc,) * 3, tm_s)
    n_prev = cache_win_k.shape[2]
    o_p, kwin_p, vwin_p = _dilated_prompt(q_p, k_p, v_p, bp, tp, min(W_MAX, tp))
    o_s, win_kt, win_vt = _dilated_decode(q_s, k_s, v_s, _feature_major(cache_win_k[0]),
                                          _feature_major(cache_win_v[0]), ts)
    y_p = _mix_ffn([o_p], w_out_c_bf, xp, norm_ffn[1], wg[1], wu[1], wd[1], norm_final, True, tf_p)
    y_s = _mix_ffn([o_s], w_out_c_bf, xs, norm_ffn[1], wg[1], wu[1], wd[1], norm_final, True, tm_s)

    return (
        y_p.reshape(bp, tp, d), y_s.reshape(bs, ts, d),
        _token_major(kat_p, h_a)[None], _token_major(vat_p, h_a)[None], ret_p[None],
        _token_major(kwin_p, h_c)[None], _token_major(vwin_p, h_c)[None],
        ka_s.reshape(1, bs, ts, h_a, HEAD_DIM), va_s.reshape(1, bs, ts, h_a, HEAD_DIM), ret_s[None],
        _token_major(win_kt, h_c)[None], _token_major(win_vt, h_c)[None],
    )
```
